```python
import math
import jax, jax.numpy as jnp
from jax import lax
import numpy as np

D_MODEL = 1024
BATCH = 8
SEQ = 2048
DEPTH = 1

MLSTM_HEADS = 4
MLSTM_HEAD_DIM = 256
MLSTM_WIDTH = MLSTM_HEADS * MLSTM_HEAD_DIM
MLSTM_CONV = 4
MLSTM_CHUNK = 64
DIFF_HEADS = 8
DIFF_HEAD_DIM = 64
DIFF_V_DIM = 2 * DIFF_HEAD_DIM
DIFF_QK_WIDTH = 2 * DIFF_HEADS * DIFF_HEAD_DIM
DIFF_WIDTH = DIFF_HEADS * DIFF_V_DIM
Q_BLOCK = 128
REL_BUCKETS = 32
REL_MAX_DIST = 128
D_FF = 2816
FFN_CONV = 3
N_BRANCHES = 2
IN_SPLITS = (MLSTM_WIDTH, MLSTM_WIDTH, MLSTM_WIDTH, MLSTM_WIDTH, MLSTM_HEADS, MLSTM_HEADS, DIFF_QK_WIDTH, DIFF_QK_WIDTH, DIFF_WIDTH, N_BRANCHES * D_MODEL)
N_IN = 4 * MLSTM_WIDTH + 2 * MLSTM_HEADS + 2 * DIFF_QK_WIDTH + DIFF_WIDTH + N_BRANCHES * D_MODEL
DEEPNORM_ALPHA = (2.0 * DEPTH) ** 0.25
DEEPNORM_BETA = (8.0 * DEPTH) ** -0.25
LN_EPS = 1e-5

kernel_name = 'hybrid_mlstm_diffattn_convffn_deepnorm'

F32 = jnp.float32


def layer_norm(t, g, b):
    tf = t.astype(F32)
    mu = tf.mean(-1, keepdims=True)
    var = jnp.square(tf - mu).mean(-1, keepdims=True)
    return ((tf - mu) * lax.rsqrt(var + LN_EPS) * g.astype(F32) + b.astype(F32)).astype(t.dtype)


def causal_depthwise_conv(u, w, b):
    K, C = w.shape
    y = lax.conv_general_dilated(u, w[:, None, :].astype(u.dtype), window_strides=(1,), padding=[(K - 1, 0)], dimension_numbers=('NWC', 'WIO', 'NWC'), feature_group_count=C)
    return y + b.astype(u.dtype)


def t5_bucket(dist):
    n = jnp.maximum(dist, 0)
    max_exact = REL_BUCKETS // 2
    nf = jnp.maximum(n, 1).astype(F32)
    large = max_exact + (jnp.log(nf / max_exact) / math.log(REL_MAX_DIST / max_exact) * (REL_BUCKETS - max_exact)).astype(jnp.int32)
    large = jnp.minimum(large, REL_BUCKETS - 1)
    return jnp.where(n < max_exact, n, large)


def mlstm_chunkwise(q, k, v, i_pre, f_pre):
    B, H, S, d = q.shape
    L = MLSTM_CHUNK
    nc = S // L
    k = k * (d ** -0.5)
    log_f = jax.nn.log_sigmoid(f_pre)
    to_c = lambda t: t.reshape(B, H, nc, L, d).transpose(2, 0, 1, 3, 4)
    to_cg = lambda t: t.reshape(B, H, nc, L).transpose(2, 0, 1, 3)
    tril = jnp.tril(jnp.ones((L, L), dtype=bool))

    def step(carry, inp):
        C, n, m = carry
        qc, kc, vc, ic, lfc = inp
        b = jnp.cumsum(lfc, axis=-1)
        D = jnp.where(tril, b[..., :, None] - b[..., None, :] + ic[..., None, :], -jnp.inf)
        a = b + m[..., None]
        m_row = jnp.maximum(a, D.max(-1))
        W = jnp.exp(D - m_row[..., None])
        inter = jnp.exp(a - m_row)
        sqk = jnp.einsum('bhjd,bhsd->bhjs', qc, kc) * W
        num = inter[..., None] * jnp.einsum('bhjd,bhde->bhje', qc, C) + jnp.einsum('bhjs,bhse->bhje', sqk, vc)
        den = inter * jnp.einsum('bhjd,bhd->bhj', qc, n) + sqk.sum(-1)
        h = num / jnp.maximum(jnp.abs(den), jnp.exp(-m_row))[..., None]
        bL = b[..., -1]
        g = bL[..., None] - b + ic
        m_new = jnp.maximum(bL + m, g.max(-1))
        decay = jnp.exp(bL + m - m_new)
        ws = jnp.exp(g - m_new[..., None])
        C_new = decay[..., None, None] * C + jnp.einsum('bhsd,bhse->bhde', kc * ws[..., None], vc)
        n_new = decay[..., None] * n + jnp.einsum('bhs,bhsd->bhd', ws, kc)
        return (C_new, n_new, m_new), h

    init = (jnp.zeros((B, H, d, d), F32), jnp.zeros((B, H, d), F32), jnp.zeros((B, H), F32))
    _, hs = lax.scan(step, init, (to_c(q), to_c(k), to_c(v), to_cg(i_pre), to_cg(log_f)))
    return hs.transpose(1, 2, 0, 3, 4).reshape(B, H, S, d)


def diff_attention(q1, q2, k1, k2, v, lam, rel_bias):
    B, H, S, dh = q1.shape
    dv = v.shape[-1]
    nb = S // Q_BLOCK
    scale = dh ** -0.5
    k_pos = jnp.arange(S)
    table = rel_bias.astype(F32)

    def block(args):
        qb1, qb2, blk = args
        q_pos = blk * Q_BLOCK + jnp.arange(Q_BLOCK)
        dist = q_pos[:, None] - k_pos[None, :]
        bias = table[t5_bucket(dist)].transpose(2, 0, 1)
        causal = dist >= 0

        def probs(qb, kk):
            s = jnp.einsum('bhqd,bhkd->bhqk', qb, kk) * scale + bias
            return jax.nn.softmax(jnp.where(causal, s, -jnp.inf), axis=-1)

        attn = probs(qb1, k1) - lam * probs(qb2, k2)
        return jnp.einsum('bhqk,bhkv->bhqv', attn, v)

    to_b = lambda t: t.reshape(B, H, nb, Q_BLOCK, dh).transpose(2, 0, 1, 3, 4)
    out = lax.map(block, (to_b(q1), to_b(q2), jnp.arange(nb)))
    return out.transpose(1, 0, 3, 2, 4).reshape(B, S, H, dv)


def token_mixer(x, w_in, b_in, conv_w, conv_b, m_norm_w, lq1, lk1, lq2, lk2, d_norm_w, rel_bias, w_bm, w_bd, w_out, lam_init):
    B, S, _ = x.shape
    proj = x @ w_in + b_in
    splits = np.cumsum(IN_SPLITS)[:-1].tolist()
    mq, mk, mv, mo, mi, mf, dq, dk, dv, gates = jnp.split(proj, splits, axis=-1)

    qk = jax.nn.silu(causal_depthwise_conv(jnp.concatenate([mq, mk], axis=-1), conv_w, conv_b))
    mq, mk = jnp.split(qk, 2, axis=-1)
    mh = lambda t: t.reshape(B, S, MLSTM_HEADS, MLSTM_HEAD_DIM).transpose(0, 2, 1, 3).astype(F32)
    hm = mlstm_chunkwise(mh(mq), mh(mk), mh(mv), mi.transpose(0, 2, 1).astype(F32), mf.transpose(0, 2, 1).astype(F32))
    hm = hm.transpose(0, 2, 1, 3)
    hm = jax.nn.sigmoid(mo.astype(F32)).reshape(B, S, MLSTM_HEADS, MLSTM_HEAD_DIM) * hm
    mu = hm.mean(-1, keepdims=True)
    var = jnp.square(hm - mu).mean(-1, keepdims=True)
    hm = (hm - mu) * lax.rsqrt(var + LN_EPS) * m_norm_w.astype(F32).reshape(MLSTM_HEADS, MLSTM_HEAD_DIM)
    hm = hm.reshape(B, S, MLSTM_WIDTH).astype(x.dtype)

    qq = dq.reshape(B, S, DIFF_HEADS, 2, DIFF_HEAD_DIM).transpose(3, 0, 2, 1, 4).astype(F32)
    kk = dk.reshape(B, S, DIFF_HEADS, 2, DIFF_HEAD_DIM).transpose(3, 0, 2, 1, 4).astype(F32)
    vv = dv.reshape(B, S, DIFF_HEADS, DIFF_V_DIM).transpose(0, 2, 1, 3).astype(F32)
    lam = jnp.exp(jnp.sum(lq1.astype(F32) * lk1.astype(F32))) - jnp.exp(jnp.sum(lq2.astype(F32) * lk2.astype(F32))) + lam_init
    hd = diff_attention(qq[0], qq[1], kk[0], kk[1], vv, lam, rel_bias)
    hd = hd * lax.rsqrt(jnp.square(hd).mean(-1, keepdims=True) + LN_EPS) * d_norm_w.astype(F32)
    hd = (hd * (1.0 - lam_init)).reshape(B, S, DIFF_WIDTH).astype(x.dtype)

    g_m, g_d = jnp.split(jax.nn.sigmoid(gates), N_BRANCHES, axis=-1)
    merged = g_m * (hm @ w_bm) + g_d * (hd @ w_bd)
    return merged @ w_out


def conv_ffn(x, w_up, conv_w, conv_b, w_down):
    a, b = jnp.split(x @ w_up, 2, axis=-1)
    a = causal_depthwise_conv(a, conv_w, conv_b)
    return (jax.nn.silu(a) * b) @ w_down


def setup_inputs(seed: int = 0) -> dict:
    key = jax.random.key(seed)
    ks = jax.random.split(key, 24)

    def nrm(k, shape, scale):
        return jax.random.normal(k, shape, F32) * scale

    beta = DEEPNORM_BETA
    x = nrm(ks[0], (BATCH, SEQ, D_MODEL), 1.0)
    col_scale = jnp.concatenate([
        jnp.ones((2 * MLSTM_WIDTH,), F32),
        jnp.full((MLSTM_WIDTH,), beta, F32),
        jnp.ones((MLSTM_WIDTH + 2 * MLSTM_HEADS + 2 * DIFF_QK_WIDTH,), F32),
        jnp.full((DIFF_WIDTH,), beta, F32),
        jnp.ones((N_BRANCHES * D_MODEL,), F32)])
    w_in = nrm(ks[1], (DEPTH, D_MODEL, N_IN), D_MODEL ** -0.5) * col_scale
    b_in = nrm(ks[2], (DEPTH, N_IN), 0.02)
    f_off = 4 * MLSTM_WIDTH + MLSTM_HEADS
    b_in = b_in.at[:, f_off:f_off + MLSTM_HEADS].add(jnp.linspace(3.0, 6.0, MLSTM_HEADS, dtype=F32))
    mlstm_conv_w = nrm(ks[3], (DEPTH, MLSTM_CONV, 2 * MLSTM_WIDTH), MLSTM_CONV ** -0.5)
    mlstm_conv_b = nrm(ks[4], (DEPTH, 2 * MLSTM_WIDTH), 0.02)
    mlstm_norm_w = 1.0 + nrm(ks[5], (DEPTH, MLSTM_WIDTH), 0.02)
    lambda_q1 = nrm(ks[6], (DEPTH, DIFF_HEAD_DIM), 0.1)
    lambda_k1 = nrm(ks[7], (DEPTH, DIFF_HEAD_DIM), 0.1)
    lambda_q2 = nrm(ks[8], (DEPTH, DIFF_HEAD_DIM), 0.1)
    lambda_k2 = nrm(ks[9], (DEPTH, DIFF_HEAD_DIM), 0.1)
    diff_norm_w = 1.0 + nrm(ks[10], (DEPTH, DIFF_V_DIM), 0.02)
    rel_bias = nrm(ks[11], (REL_BUCKETS, DIFF_HEADS), 0.2)
    w_branch_mlstm = nrm(ks[12], (DEPTH, MLSTM_WIDTH, D_MODEL), MLSTM_WIDTH ** -0.5 * beta)
    w_branch_diff = nrm(ks[13], (DEPTH, DIFF_WIDTH, D_MODEL), DIFF_WIDTH ** -0.5 * beta)
    w_out = nrm(ks[14], (DEPTH, D_MODEL, D_MODEL), D_MODEL ** -0.5 * beta)
    ln1_g = 1.0 + nrm(ks[15], (DEPTH, D_MODEL), 0.02)
    ln1_b = nrm(ks[16], (DEPTH, D_MODEL), 0.02)
    w_ffn_up = nrm(ks[17], (DEPTH, D_MODEL, 2 * D_FF), D_MODEL ** -0.5 * beta)
    ffn_conv_w = nrm(ks[18], (DEPTH, FFN_CONV, D_FF), FFN_CONV ** -0.5)
    ffn_conv_b = nrm(ks[19], (DEPTH, D_FF), 0.02)
    w_ffn_down = nrm(ks[20], (DEPTH, D_FF, D_MODEL), D_FF ** -0.5 * beta)
    ln2_g = 1.0 + nrm(ks[21], (DEPTH, D_MODEL), 0.02)
    ln2_b = nrm(ks[22], (DEPTH, D_MODEL), 0.02)
    return {'x': x, 'w_in': w_in, 'b_in': b_in, 'mlstm_conv_w': mlstm_conv_w, 'mlstm_conv_b': mlstm_conv_b,
            'mlstm_norm_w': mlstm_norm_w, 'lambda_q1': lambda_q1, 'lambda_k1': lambda_k1, 'lambda_q2': lambda_q2,
            'lambda_k2': lambda_k2, 'diff_norm_w': diff_norm_w, 'rel_bias': rel_bias, 'w_branch_mlstm': w_branch_mlstm,
            'w_branch_diff': w_branch_diff, 'w_out': w_out, 'ln1_g': ln1_g, 'ln1_b': ln1_b, 'w_ffn_up': w_ffn_up,
            'ffn_conv_w': ffn_conv_w, 'ffn_conv_b': ffn_conv_b, 'w_ffn_down': w_ffn_down, 'ln2_g': ln2_g, 'ln2_b': ln2_b}


def reference(x, w_in, b_in, mlstm_conv_w, mlstm_conv_b, mlstm_norm_w, lambda_q1, lambda_k1, lambda_q2, lambda_k2,
              diff_norm_w, rel_bias, w_branch_mlstm, w_branch_diff, w_out, ln1_g, ln1_b, w_ffn_up, ffn_conv_w,
              ffn_conv_b, w_ffn_down, ln2_g, ln2_b):
    h = x
    for l in range(DEPTH):
        lam_init = 0.8 - 0.6 * math.exp(-0.3 * l)
        mix = token_mixer(h, w_in[l], b_in[l], mlstm_conv_w[l], mlstm_conv_b[l], mlstm_norm_w[l], lambda_q1[l],
                          lambda_k1[l], lambda_q2[l], lambda_k2[l], diff_norm_w[l], rel_bias, w_branch_mlstm[l],
                          w_branch_diff[l], w_out[l], lam_init)
        h = layer_norm(DEEPNORM_ALPHA * h + mix, ln1_g[l], ln1_b[l])
        ffn = conv_ffn(h, w_ffn_up[l], ffn_conv_w[l], ffn_conv_b[l], w_ffn_down[l])
        h = layer_norm(DEEPNORM_ALPHA * h + ffn, ln2_g[l], ln2_b[l])
    return h
```

```python
import functools
import math

import jax
import jax.numpy as jnp
from jax import lax
from jax.experimental import pallas as pl
from jax.experimental.pallas import tpu as pltpu

F32 = jnp.float32
BF16 = jnp.bfloat16

D_MODEL = 1024
MLSTM_HEADS = 4
MLSTM_HEAD_DIM = 256
MLSTM_WIDTH = MLSTM_HEADS * MLSTM_HEAD_DIM
MLSTM_CONV = 4
DIFF_HEADS = 8
DIFF_HEAD_DIM = 64
DIFF_V_DIM = 2 * DIFF_HEAD_DIM
DIFF_WIDTH = DIFF_HEADS * DIFF_V_DIM
REL_BUCKETS = 32
REL_MAX_DIST = 128
D_FF = 2816
FFN_CONV = 3
DEPTH = 1
DEEPNORM_ALPHA = (2.0 * DEPTH) ** 0.25
LN_EPS = 1e-5

COL_MQ = 0
COL_MK = 1024
COL_MV = 2048
COL_MO = 3072
COL_DQ = 4096
COL_DK = 5120
COL_DV = 6144
COL_GM = 7168
COL_GD = 8192
N_MAIN = 9216
GATE_PAD = 128
GATE_ROWS = 16

HALO = 8
PROJ_TM = 1024
PROJ_TN = 512
MLSTM_L = 256
ATT_T = 512
MERGE_TM = 512
FFN_TM = 1024
FFN_TC = 256
VMEM_LIMIT = 56 * 1024 * 1024


def _sigmoid(v):
    return 1.0 / (1.0 + jnp.exp(-v))


def _log_sigmoid(v):
    return jnp.minimum(v, 0.0) - jnp.log(1.0 + jnp.exp(-jnp.abs(v)))


def _rel_bias_kernel(table_ref, out_ref):
    h = pl.program_id(0)
    T = out_ref.shape[-1]
    r = lax.broadcasted_iota(jnp.int32, (T, T), 0)
    c = lax.broadcasted_iota(jnp.int32, (T, T), 1)
    max_exact = REL_BUCKETS // 2
    far = table_ref[REL_BUCKETS - 1, h]
    for t in range(2):
        dist = r - c + (1 - t) * T
        n = jnp.maximum(dist, 0)
        nf = jnp.maximum(n, 1).astype(F32)
        large = max_exact + (jnp.log(nf / max_exact) / math.log(REL_MAX_DIST / max_exact)
                             * (REL_BUCKETS - max_exact)).astype(jnp.int32)
        large = jnp.minimum(large, REL_BUCKETS - 1)
        bucket = jnp.where(n < max_exact, n, large)
        bias = jnp.zeros((T, T), F32)
        for kk in range(REL_BUCKETS):
            bias = jnp.where(bucket == kk, table_ref[kk, h], bias)
        bias = bias - far
        out_ref[0, t] = jnp.where(dist >= 0, bias, -jnp.inf)


def _rel_bias_tiles(rel_bias):
    T = ATT_T
    return pl.pallas_call(
        _rel_bias_kernel,
        grid=(DIFF_HEADS,),
        in_specs=[pl.BlockSpec(memory_space=pltpu.SMEM)],
        out_specs=pl.BlockSpec((1, 2, T, T), lambda h: (h, 0, 0, 0)),
        out_shape=jax.ShapeDtypeStruct((DIFF_HEADS, 2, T, T), F32),
        name="rel_bias_tiles",
    )(rel_bias.astype(F32))


def _in_proj_kernel(tiles_per_seq, x_ref, w_ref, b_ref, wg_ref, bg_ref, wgt_ref, bgt_ref, cw_ref, cb_ref,
                    o_ref, gcol_ref, grow_ref, xb_ref, conv_ref, halo_ref):
    i = pl.program_id(0)
    j = pl.program_id(1)
    tm = x_ref.shape[0]
    tn = w_ref.shape[1]
    n_conv = (2 * MLSTM_WIDTH) // tn

    @pl.when(j == 0)
    def _():
        xb = x_ref[...].astype(BF16)
        xb_ref[...] = xb
        gcol_ref[...] = jnp.dot(xb, wg_ref[...], preferred_element_type=F32) + bg_ref[...]
        grow_ref[...] = lax.dot_general(wgt_ref[...], xb, (((1,), (1,)), ((), ())),
                                        preferred_element_type=F32) + bgt_ref[...]

    acc = jnp.dot(xb_ref[...], w_ref[...], preferred_element_type=F32) + b_ref[...]

    @pl.when(j < n_conv)
    def _():
        @pl.when(i % tiles_per_seq == 0)
        def _():
            halo_ref[j] = jnp.zeros((HALO, tn), F32)

        conv_ref[0:HALO, :] = halo_ref[j]
        conv_ref[HALO:HALO + tm, :] = acc
        y = cb_ref[...] + cw_ref[3:4, :] * acc
        for kk in range(MLSTM_CONV - 1):
            off = HALO - (MLSTM_CONV - 1) + kk
            y = y + cw_ref[kk:kk + 1, :] * conv_ref[off:off + tm, :]
        halo_ref[j] = acc[tm - HALO:tm, :]
        y = y * _sigmoid(y)
        scale = jnp.where(j >= n_conv // 2, MLSTM_HEAD_DIM ** -0.5, 1.0)
        o_ref[...] = (y * scale).astype(BF16)

    @pl.when(jnp.logical_and(j >= COL_MV // tn, j < COL_MO // tn))
    def _():
        o_ref[...] = acc.astype(BF16)

    @pl.when(jnp.logical_and(j >= COL_MO // tn, j < COL_DQ // tn))
    def _():
        o_ref[...] = _sigmoid(acc).astype(BF16)

    @pl.when(jnp.logical_and(j >= COL_DQ // tn, j < COL_DK // tn))
    def _():
        o_ref[...] = (acc * (DIFF_HEAD_DIM ** -0.5)).astype(BF16)

    @pl.when(jnp.logical_and(j >= COL_DK // tn, j < COL_GM // tn))
    def _():
        o_ref[...] = acc.astype(BF16)

    @pl.when(j >= COL_GM // tn)
    def _():
        o_ref[...] = _sigmoid(acc).astype(BF16)


def _in_proj(x2, w_main, b_main, w_g, b_g, w_gt, b_gt, conv_w, conv_b, seq):
    T = x2.shape[0]
    tm, tn = PROJ_TM, PROJ_TN
    n_conv = (2 * MLSTM_WIDTH) // tn
    conv_idx = lambda i, j: (0, jnp.minimum(j, n_conv - 1))
    return pl.pallas_call(
        functools.partial(_in_proj_kernel, seq // tm),
        grid=(T // tm, N_MAIN // tn),
        in_specs=[
            pl.BlockSpec((tm, D_MODEL), lambda i, j: (i, 0)),
            pl.BlockSpec((D_MODEL, tn), lambda i, j: (0, j)),
            pl.BlockSpec((1, tn), lambda i, j: (0, j)),
            pl.BlockSpec((D_MODEL, GATE_PAD), lambda i, j: (0, 0)),
            pl.BlockSpec((1, GATE_PAD), lambda i, j: (0, 0)),
            pl.BlockSpec((GATE_ROWS, D_MODEL), lambda i, j: (0, 0)),
            pl.BlockSpec((GATE_ROWS, 1), lambda i, j: (0, 0)),
            pl.BlockSpec((MLSTM_CONV, tn), conv_idx),
            pl.BlockSpec((1, tn), conv_idx),
        ],
        out_specs=[
            pl.BlockSpec((tm, tn), lambda i, j: (i, j)),
            pl.BlockSpec((tm, GATE_PAD), lambda i, j: (i, 0)),
            pl.BlockSpec((GATE_ROWS, tm), lambda i, j: (0, i)),
        ],
        out_shape=[
            jax.ShapeDtypeStruct((T, N_MAIN), BF16),
            jax.ShapeDtypeStruct((T, GATE_PAD), F32),
            jax.ShapeDtypeStruct((GATE_ROWS, T), F32),
        ],
        scratch_shapes=[
            pltpu.VMEM((tm, D_MODEL), BF16),
            pltpu.VMEM((HALO + tm, tn), F32),
            pltpu.VMEM((n_conv, HALO, tn), F32),
        ],
        compiler_params=pltpu.CompilerParams(
            dimension_semantics=("arbitrary", "arbitrary"), vmem_limit_bytes=VMEM_LIMIT),
        name="in_proj",
    )(x2, w_main, b_main, w_g, b_g, w_gt, b_gt, conv_w, conv_b)


def _mlstm_kernel(q_ref, k_ref, v_ref, o_ref, gcol_ref, grow_ref, nw_ref, out_ref, c_ref, n_ref, m_ref):
    c = pl.program_id(1)
    L = q_ref.shape[0]
    d = MLSTM_HEAD_DIM
    H = MLSTM_HEADS

    @pl.when(c == 0)
    def _():
        c_ref[...] = jnp.zeros_like(c_ref)
        n_ref[...] = jnp.zeros_like(n_ref)
        m_ref[...] = jnp.zeros_like(m_ref)

    rr = lax.broadcasted_iota(jnp.int32, (L, L), 0)
    cc = lax.broadcasted_iota(jnp.int32, (L, L), 1)
    causal = rr >= cc
    tri = causal.astype(F32)

    gcol = gcol_ref[...]
    grow = grow_ref[...]
    bcol = jnp.dot(tri, _log_sigmoid(gcol), precision=lax.Precision.HIGHEST, preferred_element_type=F32)
    brow = lax.dot_general(_log_sigmoid(grow), tri, (((1,), (1,)), ((), ())),
                           precision=lax.Precision.HIGHEST, preferred_element_type=F32)

    for h in range(H):
        cols = slice(h * d, (h + 1) * d)
        q = q_ref[:, cols]
        k = k_ref[:, cols]
        v = v_ref[:, cols]
        b_c = bcol[:, H + h:H + h + 1]
        i_c = gcol[:, h:h + 1]
        b_r = brow[H + h:H + h + 1, :]
        i_r = grow[h:h + 1, :]
        m_prev = m_ref[h]
        n_prev = n_ref[h]
        c_prev = c_ref[h]

        dmat = jnp.where(causal, b_c - b_r + i_r, -jnp.inf)
        a = b_c + m_prev
        m_row = jnp.maximum(a, jnp.max(dmat, axis=-1, keepdims=True))
        w = jnp.exp(dmat - m_row)
        inter = jnp.exp(a - m_row)
        qk = lax.dot_general(q, k, (((1,), (1,)), ((), ())), preferred_element_type=F32)
        sqk = qk * w
        num = inter * jnp.dot(q, c_prev.astype(BF16), preferred_element_type=F32) \
            + jnp.dot(sqk.astype(BF16), v, preferred_element_type=F32)
        den = inter * jnp.sum(q.astype(F32) * n_prev, axis=-1, keepdims=True) \
            + jnp.sum(sqk, axis=-1, keepdims=True)
        hid = num / jnp.maximum(jnp.abs(den), jnp.exp(-m_row))

        b_last = b_r[:, L - 1:L]
        g_max = jnp.max(b_last - b_r + i_r, axis=-1, keepdims=True)
        m_new = jnp.maximum(b_last + m_prev, g_max)
        decay = jnp.exp(b_last + m_prev - m_new)
        ws = jnp.exp(b_last - b_c + i_c - m_new)
        kw = k.astype(F32) * ws
        c_ref[h] = decay * c_prev + lax.dot_general(kw.astype(BF16), v, (((0,), (0,)), ((), ())),
                                                    preferred_element_type=F32)
        n_ref[h] = decay * n_prev + jnp.sum(kw, axis=0, keepdims=True)
        m_ref[h] = m_new

        hid = o_ref[:, cols].astype(F32) * hid
        mu = jnp.mean(hid, axis=-1, keepdims=True)
        cen = hid - mu
        var = jnp.mean(cen * cen, axis=-1, keepdims=True)
        out_ref[:, cols] = (cen * lax.rsqrt(var + LN_EPS) * nw_ref[:, cols]).astype(BF16)


def _mlstm(proj, gcol, grow, norm_w, batch, seq):
    L = MLSTM_L
    nc = seq // L
    W = MLSTM_WIDTH
    row = lambda b, c: b * nc + c
    return pl.pallas_call(
        _mlstm_kernel,
        grid=(batch, nc),
        in_specs=[
            pl.BlockSpec((L, W), lambda b, c: (row(b, c), COL_MQ // W)),
            pl.BlockSpec((L, W), lambda b, c: (row(b, c), COL_MK // W)),
            pl.BlockSpec((L, W), lambda b, c: (row(b, c), COL_MV // W)),
            pl.BlockSpec((L, W), lambda b, c: (row(b, c), COL_MO // W)),
            pl.BlockSpec((L, GATE_PAD), lambda b, c: (row(b, c), 0)),
            pl.BlockSpec((GATE_ROWS, L), lambda b, c: (0, row(b, c))),
            pl.BlockSpec((1, W), lambda b, c: (0, 0)),
        ],
        out_specs=pl.BlockSpec((L, W), lambda b, c: (row(b, c), 0)),
        out_shape=jax.ShapeDtypeStruct((batch * seq, W), BF16),
        scratch_shapes=[
            pltpu.VMEM((MLSTM_HEADS, MLSTM_HEAD_DIM, MLSTM_HEAD_DIM), F32),
            pltpu.VMEM((MLSTM_HEADS, 1, MLSTM_HEAD_DIM), F32),
            pltpu.VMEM((MLSTM_HEADS, 1, 1), F32),
        ],
        compiler_params=pltpu.CompilerParams(
            dimension_semantics=("arbitrary", "arbitrary"), vmem_limit_bytes=VMEM_LIMIT),
        name="mlstm",
    )(proj, proj, proj, proj, gcol, grow, norm_w)


def _diff_attn_kernel(lam_init, q_ref, k_ref, v_ref, bias_ref, lq1_ref, lk1_ref, lq2_ref, lk2_ref, nw_ref,
                      out_ref, qs_ref, m_ref, l_ref, acc_ref):
    i = pl.program_id(2)
    T = q_ref.shape[0]
    dh = DIFF_HEAD_DIM

    qt = q_ref[...]
    lane = lax.broadcasted_iota(jnp.int32, qt.shape, 1)
    zero = jnp.zeros_like(qt)
    qs_ref[0:T, :] = jnp.where(lane < dh, qt, zero)
    qs_ref[T:2 * T, :] = jnp.where(lane >= dh, qt, zero)
    m_ref[...] = jnp.full(m_ref.shape, -jnp.inf, F32)
    l_ref[...] = jnp.zeros(l_ref.shape, F32)
    acc_ref[...] = jnp.zeros(acc_ref.shape, F32)

    def tile(j, bias):
        start = pl.multiple_of(j * T, T)
        k = k_ref[pl.ds(start, T), :]
        v = v_ref[pl.ds(start, T), :]
        s = lax.dot_general(qs_ref[...], k, (((1,), (1,)), ((), ())), preferred_element_type=F32)
        if bias is not None:
            s = s + jnp.concatenate([bias, bias], axis=0)
        m_prev = m_ref[...]
        m_new = jnp.maximum(m_prev, jnp.max(s, axis=-1, keepdims=True))
        p = jnp.exp(s - m_new)
        alpha = jnp.exp(m_prev - m_new)
        l_ref[...] = alpha * l_ref[...] + jnp.sum(p, axis=-1, keepdims=True)
        acc_ref[...] = alpha * acc_ref[...] + jnp.dot(p.astype(BF16), v, preferred_element_type=F32)
        m_ref[...] = m_new

    def far_body(j, carry):
        tile(j, None)
        return carry

    lax.fori_loop(0, jnp.maximum(i - 1, 0), far_body, 0)

    @pl.when(i >= 1)
    def _():
        tile(i - 1, bias_ref[0, 0])

    tile(i, bias_ref[0, 1])

    lam = (jnp.exp(jnp.sum(lq1_ref[...] * lk1_ref[...], axis=-1, keepdims=True))
           - jnp.exp(jnp.sum(lq2_ref[...] * lk2_ref[...], axis=-1, keepdims=True)) + lam_init)
    out = acc_ref[...] / l_ref[...]
    hd = out[0:T, :] - lam * out[T:2 * T, :]
    hd = hd * lax.rsqrt(jnp.mean(hd * hd, axis=-1, keepdims=True) + LN_EPS) * nw_ref[...]
    out_ref[...] = (hd * (1.0 - lam_init)).astype(BF16)


def _diff_attn(proj, bias_tiles, lq1, lk1, lq2, lk2, norm_w, lam_init, batch, seq):
    T = ATT_T
    nq = seq // T
    dv = DIFF_V_DIM
    small = lambda shape: pl.BlockSpec(shape, lambda b, h, i: (0, 0))
    return pl.pallas_call(
        functools.partial(_diff_attn_kernel, lam_init),
        grid=(batch, DIFF_HEADS, nq),
        in_specs=[
            pl.BlockSpec((T, dv), lambda b, h, i: (b * nq + i, COL_DQ // dv + h)),
            pl.BlockSpec((seq, dv), lambda b, h, i: (b, COL_DK // dv + h)),
            pl.BlockSpec((seq, dv), lambda b, h, i: (b, COL_DV // dv + h)),
            pl.BlockSpec((1, 2, T, T), lambda b, h, i: (h, 0, 0, 0)),
            small((1, DIFF_HEAD_DIM)), small((1, DIFF_HEAD_DIM)),
            small((1, DIFF_HEAD_DIM)), small((1, DIFF_HEAD_DIM)),
            small((1, dv)),
        ],
        out_specs=pl.BlockSpec((T, dv), lambda b, h, i: (b * nq + i, h)),
        out_shape=jax.ShapeDtypeStruct((batch * seq, DIFF_WIDTH), BF16),
        scratch_shapes=[
            pltpu.VMEM((2 * T, dv), BF16),
            pltpu.VMEM((2 * T, 1), F32),
            pltpu.VMEM((2 * T, 1), F32),
            pltpu.VMEM((2 * T, dv), F32),
        ],
        compiler_params=pltpu.CompilerParams(
            dimension_semantics=("arbitrary", "arbitrary", "arbitrary"), vmem_limit_bytes=VMEM_LIMIT),
        name="diff_attn",
    )(proj, proj, proj, bias_tiles, lq1, lk1, lq2, lk2, norm_w)


def _layer_norm(y, g, b):
    mu = jnp.mean(y, axis=-1, keepdims=True)
    cen = y - mu
    var = jnp.mean(cen * cen, axis=-1, keepdims=True)
    return cen * lax.rsqrt(var + LN_EPS) * g + b


def _merge_kernel(hm_ref, hd_ref, gm_ref, gd_ref, x_ref, wbm_ref, wbd_ref, wo_ref, g_ref, b_ref, out_ref):
    pm = jnp.dot(hm_ref[...], wbm_ref[...], preferred_element_type=F32)
    pd = jnp.dot(hd_ref[...], wbd_ref[...], preferred_element_type=F32)
    merged = gm_ref[...].astype(F32) * pm + gd_ref[...].astype(F32) * pd
    mix = jnp.dot(merged.astype(BF16), wo_ref[...], preferred_element_type=F32)
    out_ref[...] = _layer_norm(DEEPNORM_ALPHA * x_ref[...] + mix, g_ref[...], b_ref[...])


def _merge(hm, hd, proj, x2, w_bm, w_bd, w_o, ln_g, ln_b):
    T = x2.shape[0]
    tm = MERGE_TM
    D = D_MODEL
    rows = lambda col: pl.BlockSpec((tm, D), lambda i: (i, col))
    full = lambda shape: pl.BlockSpec(shape, lambda i: (0, 0))
    return pl.pallas_call(
        _merge_kernel,
        grid=(T // tm,),
        in_specs=[rows(0), rows(0), rows(COL_GM // D), rows(COL_GD // D), rows(0),
                  full((MLSTM_WIDTH, D)), full((DIFF_WIDTH, D)), full((D, D)), full((1, D)), full((1, D))],
        out_specs=rows(0),
        out_shape=jax.ShapeDtypeStruct((T, D), F32),
        compiler_params=pltpu.CompilerParams(
            dimension_semantics=("arbitrary",), vmem_limit_bytes=VMEM_LIMIT),
        name="merge_ln",
    )(hm, hd, proj, proj, x2, w_bm, w_bd, w_o, ln_g, ln_b)


def _ffn_kernel(tiles_per_seq, h_ref, wa_ref, wb_ref, cw_ref, cb_ref, wd_ref, g_ref, b_ref, out_ref,
                hb_ref, conv_ref, halo_ref, acc_ref):
    i = pl.program_id(0)
    c = pl.program_id(1)
    tm = h_ref.shape[0]
    tc = wa_ref.shape[1]

    @pl.when(c == 0)
    def _():
        hb_ref[...] = h_ref[...].astype(BF16)
        acc_ref[...] = jnp.zeros_like(acc_ref)

    hb = hb_ref[...]
    a = jnp.dot(hb, wa_ref[...], preferred_element_type=F32)
    gate = jnp.dot(hb, wb_ref[...], preferred_element_type=F32)

    @pl.when(i % tiles_per_seq == 0)
    def _():
        halo_ref[c] = jnp.zeros((HALO, tc), F32)

    conv_ref[0:HALO, :] = halo_ref[c]
    conv_ref[HALO:HALO + tm, :] = a
    y = cb_ref[...] + cw_ref[FFN_CONV - 1:FFN_CONV, :] * a
    for kk in range(FFN_CONV - 1):
        off = HALO - (FFN_CONV - 1) + kk
        y = y + cw_ref[kk:kk + 1, :] * conv_ref[off:off + tm, :]
    halo_ref[c] = a[tm - HALO:tm, :]
    act = (y * _sigmoid(y) * gate).astype(BF16)
    acc_ref[...] += jnp.dot(act, wd_ref[...], preferred_element_type=F32)

    @pl.when(c == pl.num_programs(1) - 1)
    def _():
        out_ref[...] = _layer_norm(DEEPNORM_ALPHA * h_ref[...] + acc_ref[...], g_ref[...], b_ref[...])


def _ffn(h1, w_up, conv_w, conv_b, w_down, ln_g, ln_b, seq):
    T = h1.shape[0]
    tm, tc = FFN_TM, FFN_TC
    nchunk = D_FF // tc
    D = D_MODEL
    return pl.pallas_call(
        functools.partial(_ffn_kernel, seq // tm),
        grid=(T // tm, nchunk),
        in_specs=[
            pl.BlockSpec((tm, D), lambda i, c: (i, 0)),
            pl.BlockSpec((D, tc), lambda i, c: (0, c)),
            pl.BlockSpec((D, tc), lambda i, c: (0, nchunk + c)),
            pl.BlockSpec((FFN_CONV, tc), lambda i, c: (0, c)),
            pl.BlockSpec((1, tc), lambda i, c: (0, c)),
            pl.BlockSpec((tc, D), lambda i, c: (c, 0)),
            pl.BlockSpec((1, D), lambda i, c: (0, 0)),
            pl.BlockSpec((1, D), lambda i, c: (0, 0)),
        ],
        out_specs=pl.BlockSpec((tm, D), lambda i, c: (i, 0)),
        out_shape=jax.ShapeDtypeStruct((T, D), F32),
        scratch_shapes=[
            pltpu.VMEM((tm, D), BF16),
            pltpu.VMEM((HALO + tm, tc), F32),
            pltpu.VMEM((nchunk, HALO, tc), F32),
            pltpu.VMEM((tm, D), F32),
        ],
        compiler_params=pltpu.CompilerParams(
            dimension_semantics=("arbitrary", "arbitrary"), vmem_limit_bytes=VMEM_LIMIT),
        name="conv_ffn",
    )(h1, w_up, w_up, conv_w, conv_b, w_down, ln_g, ln_b)


def _layer(h2, batch, seq, l, w_in, b_in, mconv_w, mconv_b, mnorm_w, lq1, lk1, lq2, lk2, dnorm_w, bias_tiles,
           w_bm, w_bd, w_o, ln1_g, ln1_b, w_up, fconv_w, fconv_b, w_down, ln2_g, ln2_b):
    lam_init = 0.8 - 0.6 * math.exp(-0.3 * l)
    n_m = 4 * MLSTM_WIDTH
    n_gate = 2 * MLSTM_HEADS
    w_main = jnp.concatenate([w_in[:, :n_m], w_in[:, n_m + n_gate:]], axis=1).astype(BF16)
    b_main = jnp.concatenate([b_in[:n_m], b_in[n_m + n_gate:]])[None, :].astype(F32)
    w_gate = w_in[:, n_m:n_m + n_gate]
    b_gate = b_in[n_m:n_m + n_gate].astype(F32)
    w_g = jnp.pad(w_gate, ((0, 0), (0, GATE_PAD - n_gate))).astype(BF16)
    b_g = jnp.pad(b_gate, (0, GATE_PAD - n_gate))[None, :]
    w_gt = jnp.pad(w_gate.T, ((0, GATE_ROWS - n_gate), (0, 0))).astype(BF16)
    b_gt = jnp.pad(b_gate, (0, GATE_ROWS - n_gate))[:, None]

    proj, gcol, grow = _in_proj(h2, w_main, b_main, w_g, b_g, w_gt, b_gt,
                                mconv_w.astype(F32), mconv_b[None, :].astype(F32), seq)
    hm = _mlstm(proj, gcol, grow, mnorm_w[None, :].astype(F32), batch, seq)
    hd = _diff_attn(proj, bias_tiles, lq1[None, :].astype(F32), lk1[None, :].astype(F32),
                    lq2[None, :].astype(F32), lk2[None, :].astype(F32), dnorm_w[None, :].astype(F32),
                    lam_init, batch, seq)
    h1 = _merge(hm, hd, proj, h2, w_bm.astype(BF16), w_bd.astype(BF16), w_o.astype(BF16),
                ln1_g[None, :].astype(F32), ln1_b[None, :].astype(F32))
    return _ffn(h1, w_up.astype(BF16), fconv_w.astype(F32), fconv_b[None, :].astype(F32),
                w_down.astype(BF16), ln2_g[None, :].astype(F32), ln2_b[None, :].astype(F32), seq)


def kernel(x, w_in, b_in, mlstm_conv_w, mlstm_conv_b, mlstm_norm_w, lambda_q1, lambda_k1, lambda_q2, lambda_k2,
           diff_norm_w, rel_bias, w_branch_mlstm, w_branch_diff, w_out, ln1_g, ln1_b, w_ffn_up, ffn_conv_w,
           ffn_conv_b, w_ffn_down, ln2_g, ln2_b):
    batch, seq, d_model = x.shape
    assert d_model == D_MODEL and seq % max(PROJ_TM, FFN_TM, ATT_T, MLSTM_L) == 0
    bias_tiles = _rel_bias_tiles(rel_bias)
    h2 = x.reshape(batch * seq, d_model)
    for l in range(w_in.shape[0]):
        h2 = _layer(h2, batch, seq, l, w_in[l], b_in[l], mlstm_conv_w[l], mlstm_conv_b[l], mlstm_norm_w[l],
                    lambda_q1[l], lambda_k1[l], lambda_q2[l], lambda_k2[l], diff_norm_w[l], bias_tiles,
                    w_branch_mlstm[l], w_branch_diff[l], w_out[l], ln1_g[l], ln1_b[l], w_ffn_up[l],
                    ffn_conv_w[l], ffn_conv_b[l], w_ffn_down[l], ln2_g[l], ln2_b[l])
    return h2.reshape(batch, seq, d_model).astype(x.dtype)
```

```python
import functools
import math

import jax
import jax.numpy as jnp
from jax import lax
from jax.experimental import pallas as pl
from jax.experimental.pallas import tpu as pltpu

F32 = jnp.float32
BF16 = jnp.bfloat16

D_MODEL = 1024
MLSTM_HEADS = 4
MLSTM_HEAD_DIM = 256
MLSTM_WIDTH = MLSTM_HEADS * MLSTM_HEAD_DIM
MLSTM_CONV = 4
DIFF_HEADS = 8
DIFF_HEAD_DIM = 64
DIFF_V_DIM = 2 * DIFF_HEAD_DIM
DIFF_WIDTH = DIFF_HEADS * DIFF_V_DIM
REL_BUCKETS = 32
REL_MAX_DIST = 128
D_FF = 2816
FFN_CONV = 3
DEPTH = 1
DEEPNORM_ALPHA = (2.0 * DEPTH) ** 0.25
LN_EPS = 1e-5
LOG2E = math.log2(math.e)

COL_MQ = 0
COL_MK = 1024
COL_MV = 2048
COL_MO = 3072
COL_DQ = 4096
COL_DK = 5120
COL_DV = 6144
COL_GM = 7168
COL_GD = 8192
N_MAIN = 9216
GATE_PAD = 128
GATE_ROWS = 16

HALO = 8
PROJ_TM = 1024
PROJ_TN = 512
MLSTM_L = 256
ATT_T = 512
ATT_SUM_ROWS = 16
MERGE_TM = 512
FFN_TM = 1024
FFN_TC = 256
VMEM_LIMIT = 56 * 1024 * 1024


def _sigmoid(v):
    return 1.0 / (1.0 + jnp.exp(-v))


def _log_sigmoid(v):
    return jnp.minimum(v, 0.0) - jnp.log(1.0 + jnp.exp(-jnp.abs(v)))


def _rel_bias_kernel(table_ref, out_ref):
    h = pl.program_id(0)
    T = out_ref.shape[-1]
    kpos = lax.broadcasted_iota(jnp.int32, (T, T), 0)
    qpos = lax.broadcasted_iota(jnp.int32, (T, T), 1)
    max_exact = REL_BUCKETS // 2
    far = table_ref[REL_BUCKETS - 1, h]
    for t in range(2):
        dist = qpos - kpos + (1 - t) * T
        n = jnp.maximum(dist, 0)
        nf = jnp.maximum(n, 1).astype(F32)
        large = max_exact + (jnp.log(nf / max_exact) / math.log(REL_MAX_DIST / max_exact)
                             * (REL_BUCKETS - max_exact)).astype(jnp.int32)
        large = jnp.minimum(large, REL_BUCKETS - 1)
        bucket = jnp.where(n < max_exact, n, large)
        bias = jnp.zeros((T, T), F32)
        for kk in range(REL_BUCKETS):
            bias = jnp.where(bucket == kk, table_ref[kk, h], bias)
        bias = (bias - far) * LOG2E
        out_ref[0, t] = jnp.where(dist >= 0, bias, -jnp.inf)


def _rel_bias_tiles(rel_bias):
    T = ATT_T
    return pl.pallas_call(
        _rel_bias_kernel,
        grid=(DIFF_HEADS,),
        in_specs=[pl.BlockSpec(memory_space=pltpu.SMEM)],
        out_specs=pl.BlockSpec((1, 2, T, T), lambda h: (h, 0, 0, 0)),
        out_shape=jax.ShapeDtypeStruct((DIFF_HEADS, 2, T, T), F32),
        name="rel_bias_tiles",
    )(rel_bias.astype(F32))


def _in_proj_kernel(tiles_per_seq, x_ref, w_ref, b_ref, wg_ref, bg_ref, wgt_ref, bgt_ref, cw_ref, cb_ref,
                    o_ref, gcol_ref, grow_ref, xb_ref, conv_ref, halo_ref):
    i = pl.program_id(0)
    j = pl.program_id(1)
    tm = x_ref.shape[0]
    tn = w_ref.shape[1]
    n_conv = (2 * MLSTM_WIDTH) // tn

    @pl.when(j == 0)
    def _():
        xb = x_ref[...].astype(BF16)
        xb_ref[...] = xb
        gcol_ref[...] = jnp.dot(xb, wg_ref[...], preferred_element_type=F32) + bg_ref[...]
        grow_ref[...] = lax.dot_general(wgt_ref[...], xb, (((1,), (1,)), ((), ())),
                                        preferred_element_type=F32) + bgt_ref[...]

    acc = jnp.dot(xb_ref[...], w_ref[...], preferred_element_type=F32) + b_ref[...]

    @pl.when(j < n_conv)
    def _():
        @pl.when(i % tiles_per_seq == 0)
        def _():
            halo_ref[j] = jnp.zeros((HALO, tn), F32)

        conv_ref[0:HALO, :] = halo_ref[j]
        conv_ref[HALO:HALO + tm, :] = acc
        y = cb_ref[...] + cw_ref[3:4, :] * acc
        for kk in range(MLSTM_CONV - 1):
            off = HALO - (MLSTM_CONV - 1) + kk
            y = y + cw_ref[kk:kk + 1, :] * conv_ref[off:off + tm, :]
        halo_ref[j] = acc[tm - HALO:tm, :]
        y = y * _sigmoid(y)
        scale = jnp.where(j >= n_conv // 2, MLSTM_HEAD_DIM ** -0.5, 1.0)
        o_ref[...] = (y * scale).astype(BF16)

    @pl.when(jnp.logical_and(j >= COL_MV // tn, j < COL_MO // tn))
    def _():
        o_ref[...] = acc.astype(BF16)

    @pl.when(jnp.logical_and(j >= COL_MO // tn, j < COL_DQ // tn))
    def _():
        o_ref[...] = _sigmoid(acc).astype(BF16)

    @pl.when(jnp.logical_and(j >= COL_DQ // tn, j < COL_DK // tn))
    def _():
        o_ref[...] = (acc * (DIFF_HEAD_DIM ** -0.5 * LOG2E)).astype(BF16)

    @pl.when(jnp.logical_and(j >= COL_DK // tn, j < COL_GM // tn))
    def _():
        o_ref[...] = acc.astype(BF16)

    @pl.when(j >= COL_GM // tn)
    def _():
        o_ref[...] = _sigmoid(acc).astype(BF16)


def _in_proj(x2, w_main, b_main, w_g, b_g, w_gt, b_gt, conv_w, conv_b, seq):
    T = x2.shape[0]
    tm, tn = PROJ_TM, PROJ_TN
    n_conv = (2 * MLSTM_WIDTH) // tn
    conv_idx = lambda i, j: (0, jnp.minimum(j, n_conv - 1))
    return pl.pallas_call(
        functools.partial(_in_proj_kernel, seq // tm),
        grid=(T // tm, N_MAIN // tn),
        in_specs=[
            pl.BlockSpec((tm, D_MODEL), lambda i, j: (i, 0)),
            pl.BlockSpec((D_MODEL, tn), lambda i, j: (0, j)),
            pl.BlockSpec((1, tn), lambda i, j: (0, j)),
            pl.BlockSpec((D_MODEL, GATE_PAD), lambda i, j: (0, 0)),
            pl.BlockSpec((1, GATE_PAD), lambda i, j: (0, 0)),
            pl.BlockSpec((GATE_ROWS, D_MODEL), lambda i, j: (0, 0)),
            pl.BlockSpec((GATE_ROWS, 1), lambda i, j: (0, 0)),
            pl.BlockSpec((MLSTM_CONV, tn), conv_idx),
            pl.BlockSpec((1, tn), conv_idx),
        ],
        out_specs=[
            pl.BlockSpec((tm, tn), lambda i, j: (i, j)),
            pl.BlockSpec((tm, GATE_PAD), lambda i, j: (i, 0)),
            pl.BlockSpec((GATE_ROWS, tm), lambda i, j: (0, i)),
        ],
        out_shape=[
            jax.ShapeDtypeStruct((T, N_MAIN), BF16),
            jax.ShapeDtypeStruct((T, GATE_PAD), F32),
            jax.ShapeDtypeStruct((GATE_ROWS, T), F32),
        ],
        scratch_shapes=[
            pltpu.VMEM((tm, D_MODEL), BF16),
            pltpu.VMEM((HALO + tm, tn), F32),
            pltpu.VMEM((n_conv, HALO, tn), F32),
        ],
        compiler_params=pltpu.CompilerParams(
            dimension_semantics=("arbitrary", "arbitrary"), vmem_limit_bytes=VMEM_LIMIT),
        name="in_proj",
    )(x2, w_main, b_main, w_g, b_g, w_gt, b_gt, conv_w, conv_b)


def _mlstm_kernel(q_ref, k_ref, v_ref, o_ref, gcol_ref, grow_ref, nw_ref, out_ref, c_ref, n_ref, m_ref):
    c = pl.program_id(1)
    L = q_ref.shape[0]
    d = MLSTM_HEAD_DIM
    H = MLSTM_HEADS

    @pl.when(c == 0)
    def _():
        c_ref[...] = jnp.zeros_like(c_ref)
        n_ref[...] = jnp.zeros_like(n_ref)
        m_ref[...] = jnp.zeros_like(m_ref)

    rr = lax.broadcasted_iota(jnp.int32, (L, L), 0)
    cc = lax.broadcasted_iota(jnp.int32, (L, L), 1)
    causal = rr >= cc
    tri = causal.astype(F32)

    gcol = gcol_ref[...]
    grow = grow_ref[...]
    bcol = jnp.dot(tri, _log_sigmoid(gcol), precision=lax.Precision.HIGHEST, preferred_element_type=F32)
    brow = lax.dot_general(_log_sigmoid(grow), tri, (((1,), (1,)), ((), ())),
                           precision=lax.Precision.HIGHEST, preferred_element_type=F32)

    for h in range(H):
        cols = slice(h * d, (h + 1) * d)
        q = q_ref[:, cols]
        k = k_ref[:, cols]
        v = v_ref[:, cols]
        b_c = bcol[:, H + h:H + h + 1]
        i_c = gcol[:, h:h + 1]
        b_r = brow[H + h:H + h + 1, :]
        i_r = grow[h:h + 1, :]
        m_prev = m_ref[h]
        n_prev = n_ref[h]
        c_prev = c_ref[h]

        dmat = jnp.where(causal, b_c - b_r + i_r, -jnp.inf)
        a = b_c + m_prev
        m_row = jnp.maximum(a, jnp.max(dmat, axis=-1, keepdims=True))
        w = jnp.exp(dmat - m_row)
        inter = jnp.exp(a - m_row)
        qk = lax.dot_general(q, k, (((1,), (1,)), ((), ())), preferred_element_type=F32)
        sqk = qk * w
        num = inter * jnp.dot(q, c_prev.astype(BF16), preferred_element_type=F32) \
            + jnp.dot(sqk.astype(BF16), v, preferred_element_type=F32)
        den = inter * jnp.sum(q.astype(F32) * n_prev, axis=-1, keepdims=True) \
            + jnp.sum(sqk, axis=-1, keepdims=True)
        hid = num / jnp.maximum(jnp.abs(den), jnp.exp(-m_row))

        b_last = b_r[:, L - 1:L]
        g_max = jnp.max(b_last - b_r + i_r, axis=-1, keepdims=True)
        m_new = jnp.maximum(b_last + m_prev, g_max)
        decay = jnp.exp(b_last + m_prev - m_new)
        ws = jnp.exp(b_last - b_c + i_c - m_new)
        kw = k.astype(F32) * ws
        c_ref[h] = decay * c_prev + lax.dot_general(kw.astype(BF16), v, (((0,), (0,)), ((), ())),
                                                    preferred_element_type=F32)
        n_ref[h] = decay * n_prev + jnp.sum(kw, axis=0, keepdims=True)
        m_ref[h] = m_new

        hid = o_ref[:, cols].astype(F32) * hid
        mu = jnp.mean(hid, axis=-1, keepdims=True)
        cen = hid - mu
        var = jnp.mean(cen * cen, axis=-1, keepdims=True)
        out_ref[:, cols] = (cen * lax.rsqrt(var + LN_EPS) * nw_ref[:, cols]).astype(BF16)


def _mlstm(proj, gcol, grow, norm_w, batch, seq):
    L = MLSTM_L
    nc = seq // L
    W = MLSTM_WIDTH
    row = lambda b, c: b * nc + c
    return pl.pallas_call(
        _mlstm_kernel,
        grid=(batch, nc),
        in_specs=[
            pl.BlockSpec((L, W), lambda b, c: (row(b, c), COL_MQ // W)),
            pl.BlockSpec((L, W), lambda b, c: (row(b, c), COL_MK // W)),
            pl.BlockSpec((L, W), lambda b, c: (row(b, c), COL_MV // W)),
            pl.BlockSpec((L, W), lambda b, c: (row(b, c), COL_MO // W)),
            pl.BlockSpec((L, GATE_PAD), lambda b, c: (row(b, c), 0)),
            pl.BlockSpec((GATE_ROWS, L), lambda b, c: (0, row(b, c))),
            pl.BlockSpec((1, W), lambda b, c: (0, 0)),
        ],
        out_specs=pl.BlockSpec((L, W), lambda b, c: (row(b, c), 0)),
        out_shape=jax.ShapeDtypeStruct((batch * seq, W), BF16),
        scratch_shapes=[
            pltpu.VMEM((MLSTM_HEADS, MLSTM_HEAD_DIM, MLSTM_HEAD_DIM), F32),
            pltpu.VMEM((MLSTM_HEADS, 1, MLSTM_HEAD_DIM), F32),
            pltpu.VMEM((MLSTM_HEADS, 1, 1), F32),
        ],
        compiler_params=pltpu.CompilerParams(
            dimension_semantics=("arbitrary", "arbitrary"), vmem_limit_bytes=VMEM_LIMIT),
        name="mlstm",
    )(proj, proj, proj, proj, gcol, grow, norm_w)


def _diff_attn_kernel(lam_init, q_ref, k_ref, v_ref, bias_ref, lq1_ref, lk1_ref, lq2_ref, lk2_ref, nw_ref,
                      out_ref, qs_ref, vt_ref, s_ref, p_ref, acc_ref):
    T = ATT_T
    dh = DIFF_HEAD_DIM
    dv = DIFF_V_DIM
    nq = q_ref.shape[0] // T
    R = 2 * T

    ones = jnp.ones((ATT_SUM_ROWS, T), BF16)
    for jj in range(nq):
        vt_ref[jj, 0:dv, :] = v_ref[jj * T:(jj + 1) * T, :].astype(F32).T.astype(BF16)
        vt_ref[jj, dv:dv + ATT_SUM_ROWS, :] = ones

    lane = lax.broadcasted_iota(jnp.int32, (T, dv), 1)
    for ii in range(nq):
        qt = q_ref[ii * T:(ii + 1) * T, :]
        zero = jnp.zeros_like(qt)
        qs_ref[ii, 0:T, :] = jnp.where(lane < dh, qt, zero)
        qs_ref[ii, T:R, :] = jnp.where(lane >= dh, qt, zero)

    lam = (jnp.exp(jnp.sum(lq1_ref[...] * lk1_ref[...], axis=-1, keepdims=True))
           - jnp.exp(jnp.sum(lq2_ref[...] * lk2_ref[...], axis=-1, keepdims=True)) + lam_init)

    pairs = [(i, j) for i in range(nq) for j in range(i + 1)]

    def scores(n):
        i, j = pairs[n]
        s_ref[n % 2] = lax.dot_general(k_ref[j * T:(j + 1) * T, :], qs_ref[i], (((1,), (1,)), ((), ())),
                                       preferred_element_type=F32)

    def softmax(n, m_prev):
        i, j = pairs[n]
        s = s_ref[n % 2]
        if j >= i - 1:
            bias = bias_ref[0, j - i + 1]
            s = s + jnp.concatenate([bias, bias], axis=1)
        m_tile = jnp.max(s, axis=0, keepdims=True)
        m_new = m_tile if j == 0 else jnp.maximum(m_prev, m_tile)
        p_ref[n % 2] = jnp.exp2(s - m_new).astype(BF16)
        alpha = None if j == 0 else jnp.exp2(m_prev - m_new)
        return m_new, alpha

    def values(n, alpha):
        i, j = pairs[n]
        pv = jnp.dot(vt_ref[j], p_ref[n % 2], preferred_element_type=F32)
        if j == 0:
            acc_ref[...] = pv
        else:
            acc_ref[...] = alpha * acc_ref[...] + pv
        if j == i:
            acc = acc_ref[...]
            out = acc[0:dv, :] / acc[dv:dv + 1, :]
            hd = out[:, 0:T] - lam * out[:, T:R]
            hd = hd * lax.rsqrt(jnp.mean(hd * hd, axis=0, keepdims=True) + LN_EPS) * nw_ref[...]
            out_ref[i * T:(i + 1) * T, :] = (hd * (1.0 - lam_init)).T.astype(BF16)

    scores(0)
    m_run = None
    alphas = {}
    for n in range(len(pairs)):
        if n + 1 < len(pairs):
            scores(n + 1)
        m_run, alphas[n] = softmax(n, m_run)
        if n >= 1:
            values(n - 1, alphas.pop(n - 1))
    values(len(pairs) - 1, alphas.pop(len(pairs) - 1))


def _diff_attn(proj, bias_tiles, lq1, lk1, lq2, lk2, norm_w, lam_init, batch, seq):
    T = ATT_T
    nq = seq // T
    dv = DIFF_V_DIM
    small = lambda shape: pl.BlockSpec(shape, lambda b, h: (0, 0))
    return pl.pallas_call(
        functools.partial(_diff_attn_kernel, lam_init),
        grid=(batch, DIFF_HEADS),
        in_specs=[
            pl.BlockSpec((seq, dv), lambda b, h: (b, COL_DQ // dv + h)),
            pl.BlockSpec((seq, dv), lambda b, h: (b, COL_DK // dv + h)),
            pl.BlockSpec((seq, dv), lambda b, h: (b, COL_DV // dv + h)),
            pl.BlockSpec((1, 2, T, T), lambda b, h: (h, 0, 0, 0)),
            small((1, DIFF_HEAD_DIM)), small((1, DIFF_HEAD_DIM)),
            small((1, DIFF_HEAD_DIM)), small((1, DIFF_HEAD_DIM)),
            small((dv, 1)),
        ],
        out_specs=pl.BlockSpec((seq, dv), lambda b, h: (b, h)),
        out_shape=jax.ShapeDtypeStruct((batch * seq, DIFF_WIDTH), BF16),
        scratch_shapes=[
            pltpu.VMEM((nq, 2 * T, dv), BF16),
            pltpu.VMEM((nq, dv + ATT_SUM_ROWS, T), BF16),
            pltpu.VMEM((2, T, 2 * T), F32),
            pltpu.VMEM((2, T, 2 * T), BF16),
            pltpu.VMEM((dv + ATT_SUM_ROWS, 2 * T), F32),
        ],
        compiler_params=pltpu.CompilerParams(
            dimension_semantics=("arbitrary", "arbitrary"), vmem_limit_bytes=VMEM_LIMIT),
        name="diff_attn",
    )(proj, proj, proj, bias_tiles, lq1, lk1, lq2, lk2, norm_w)


def _layer_norm(y, g, b):
    mu = jnp.mean(y, axis=-1, keepdims=True)
    cen = y - mu
    var = jnp.mean(cen * cen, axis=-1, keepdims=True)
    return cen * lax.rsqrt(var + LN_EPS) * g + b


def _merge_kernel(hm_ref, hd_ref, gm_ref, gd_ref, x_ref, wbm_ref, wbd_ref, wo_ref, g_ref, b_ref, out_ref):
    pm = jnp.dot(hm_ref[...], wbm_ref[...], preferred_element_type=F32)
    pd = jnp.dot(hd_ref[...], wbd_ref[...], preferred_element_type=F32)
    merged = gm_ref[...].astype(F32) * pm + gd_ref[...].astype(F32) * pd
    mix = jnp.dot(merged.astype(BF16), wo_ref[...], preferred_element_type=F32)
    out_ref[...] = _layer_norm(DEEPNORM_ALPHA * x_ref[...] + mix, g_ref[...], b_ref[...])


def _merge(hm, hd, proj, x2, w_bm, w_bd, w_o, ln_g, ln_b):
    T = x2.shape[0]
    tm = MERGE_TM
    D = D_MODEL
    rows = lambda col: pl.BlockSpec((tm, D), lambda i: (i, col))
    full = lambda shape: pl.BlockSpec(shape, lambda i: (0, 0))
    return pl.pallas_call(
        _merge_kernel,
        grid=(T // tm,),
        in_specs=[rows(0), rows(0), rows(COL_GM // D), rows(COL_GD // D), rows(0),
                  full((MLSTM_WIDTH, D)), full((DIFF_WIDTH, D)), full((D, D)), full((1, D)), full((1, D))],
        out_specs=rows(0),
        out_shape=jax.ShapeDtypeStruct((T, D), F32),
        compiler_params=pltpu.CompilerParams(
            dimension_semantics=("arbitrary",), vmem_limit_bytes=VMEM_LIMIT),
        name="merge_ln",
    )(hm, hd, proj, proj, x2, w_bm, w_bd, w_o, ln_g, ln_b)


def _ffn_kernel(tiles_per_seq, h_ref, wa_ref, wb_ref, cw_ref, cb_ref, wd_ref, g_ref, b_ref, out_ref,
                hb_ref, conv_ref, halo_ref, acc_ref):
    i = pl.program_id(0)
    c = pl.program_id(1)
    tm = h_ref.shape[0]
    tc = wa_ref.shape[1]

    @pl.when(c == 0)
    def _():
        hb_ref[...] = h_ref[...].astype(BF16)
        acc_ref[...] = jnp.zeros_like(acc_ref)

    hb = hb_ref[...]
    a = jnp.dot(hb, wa_ref[...], preferred_element_type=F32)
    gate = jnp.dot(hb, wb_ref[...], preferred_element_type=F32)

    @pl.when(i % tiles_per_seq == 0)
    def _():
        halo_ref[c] = jnp.zeros((HALO, tc), F32)

    conv_ref[0:HALO, :] = halo_ref[c]
    conv_ref[HALO:HALO + tm, :] = a
    y = cb_ref[...] + cw_ref[FFN_CONV - 1:FFN_CONV, :] * a
    for kk in range(FFN_CONV - 1):
        off = HALO - (FFN_CONV - 1) + kk
        y = y + cw_ref[kk:kk + 1, :] * conv_ref[off:off + tm, :]
    halo_ref[c] = a[tm - HALO:tm, :]
    act = (y * _sigmoid(y) * gate).astype(BF16)
    acc_ref[...] += jnp.dot(act, wd_ref[...], preferred_element_type=F32)

    @pl.when(c == pl.num_programs(1) - 1)
    def _():
        out_ref[...] = _layer_norm(DEEPNORM_ALPHA * h_ref[...] + acc_ref[...], g_ref[...], b_ref[...])


def _ffn(h1, w_up, conv_w, conv_b, w_down, ln_g, ln_b, seq):
    T = h1.shape[0]
    tm, tc = FFN_TM, FFN_TC
    nchunk = D_FF // tc
    D = D_MODEL
    return pl.pallas_call(
        functools.partial(_ffn_kernel, seq // tm),
        grid=(T // tm, nchunk),
        in_specs=[
            pl.BlockSpec((tm, D), lambda i, c: (i, 0)),
            pl.BlockSpec((D, tc), lambda i, c: (0, c)),
            pl.BlockSpec((D, tc), lambda i, c: (0, nchunk + c)),
            pl.BlockSpec((FFN_CONV, tc), lambda i, c: (0, c)),
            pl.BlockSpec((1, tc), lambda i, c: (0, c)),
            pl.BlockSpec((tc, D), lambda i, c: (c, 0)),
            pl.BlockSpec((1, D), lambda i, c: (0, 0)),
            pl.BlockSpec((1, D), lambda i, c: (0, 0)),
        ],
        out_specs=pl.BlockSpec((tm, D), lambda i, c: (i, 0)),
        out_shape=jax.ShapeDtypeStruct((T, D), F32),
        scratch_shapes=[
            pltpu.VMEM((tm, D), BF16),
            pltpu.VMEM((HALO + tm, tc), F32),
            pltpu.VMEM((nchunk, HALO, tc), F32),
            pltpu.VMEM((tm, D), F32),
        ],
        compiler_params=pltpu.CompilerParams(
            dimension_semantics=("arbitrary", "arbitrary"), vmem_limit_bytes=VMEM_LIMIT),
        name="conv_ffn",
    )(h1, w_up, w_up, conv_w, conv_b, w_down, ln_g, ln_b)


def _layer(h2, batch, seq, l, w_in, b_in, mconv_w, mconv_b, mnorm_w, lq1, lk1, lq2, lk2, dnorm_w, bias_tiles,
           w_bm, w_bd, w_o, ln1_g, ln1_b, w_up, fconv_w, fconv_b, w_down, ln2_g, ln2_b):
    lam_init = 0.8 - 0.6 * math.exp(-0.3 * l)
    n_m = 4 * MLSTM_WIDTH
    n_gate = 2 * MLSTM_HEADS
    w_main = jnp.concatenate([w_in[:, :n_m], w_in[:, n_m + n_gate:]], axis=1).astype(BF16)
    b_main = jnp.concatenate([b_in[:n_m], b_in[n_m + n_gate:]])[None, :].astype(F32)
    w_gate = w_in[:, n_m:n_m + n_gate]
    b_gate = b_in[n_m:n_m + n_gate].astype(F32)
    w_g = jnp.pad(w_gate, ((0, 0), (0, GATE_PAD - n_gate))).astype(BF16)
    b_g = jnp.pad(b_gate, (0, GATE_PAD - n_gate))[None, :]
    w_gt = jnp.pad(w_gate.T, ((0, GATE_ROWS - n_gate), (0, 0))).astype(BF16)
    b_gt = jnp.pad(b_gate, (0, GATE_ROWS - n_gate))[:, None]

    proj, gcol, grow = _in_proj(h2, w_main, b_main, w_g, b_g, w_gt, b_gt,
                                mconv_w.astype(F32), mconv_b[None, :].astype(F32), seq)
    hm = _mlstm(proj, gcol, grow, mnorm_w[None, :].astype(F32), batch, seq)
    hd = _diff_attn(proj, bias_tiles, lq1[None, :].astype(F32), lk1[None, :].astype(F32),
                    lq2[None, :].astype(F32), lk2[None, :].astype(F32), dnorm_w[:, None].astype(F32),
                    lam_init, batch, seq)
    h1 = _merge(hm, hd, proj, h2, w_bm.astype(BF16), w_bd.astype(BF16), w_o.astype(BF16),
                ln1_g[None, :].astype(F32), ln1_b[None, :].astype(F32))
    return _ffn(h1, w_up.astype(BF16), fconv_w.astype(F32), fconv_b[None, :].astype(F32),
                w_down.astype(BF16), ln2_g[None, :].astype(F32), ln2_b[None, :].astype(F32), seq)


def kernel(x, w_in, b_in, mlstm_conv_w, mlstm_conv_b, mlstm_norm_w, lambda_q1, lambda_k1, lambda_q2, lambda_k2,
           diff_norm_w, rel_bias, w_branch_mlstm, w_branch_diff, w_out, ln1_g, ln1_b, w_ffn_up, ffn_conv_w,
           ffn_conv_b, w_ffn_down, ln2_g, ln2_b):
    batch, seq, d_model = x.shape
    assert d_model == D_MODEL and seq % max(PROJ_TM, FFN_TM, ATT_T, MLSTM_L) == 0
    bias_tiles = _rel_bias_tiles(rel_bias)
    h2 = x.reshape(batch * seq, d_model)
    for l in range(w_in.shape[0]):
        h2 = _layer(h2, batch, seq, l, w_in[l], b_in[l], mlstm_conv_w[l], mlstm_conv_b[l], mlstm_norm_w[l],
                    lambda_q1[l], lambda_k1[l], lambda_q2[l], lambda_k2[l], diff_norm_w[l], bias_tiles,
                    w_branch_mlstm[l], w_branch_diff[l], w_out[l], ln1_g[l], ln1_b[l], w_ffn_up[l],
                    ffn_conv_w[l], ffn_conv_b[l], w_ffn_down[l], ln2_g[l], ln2_b[l])
    return h2.reshape(batch, seq, d_model).astype(x.dtype)
```

```python
import functools
import math

import jax
import jax.numpy as jnp
from jax import lax
from jax.experimental import pallas as pl
from jax.experimental.pallas import tpu as pltpu

F32 = jnp.float32
BF16 = jnp.bfloat16

D_MODEL = 1024
MLSTM_HEADS = 4
MLSTM_HEAD_DIM = 256
MLSTM_WIDTH = MLSTM_HEADS * MLSTM_HEAD_DIM
MLSTM_CONV = 4
DIFF_HEADS = 8
DIFF_HEAD_DIM = 64
DIFF_V_DIM = 2 * DIFF_HEAD_DIM
DIFF_WIDTH = DIFF_HEADS * DIFF_V_DIM
REL_BUCKETS = 32
REL_MAX_DIST = 128
D_FF = 2816
FFN_CONV = 3
DEPTH = 1
DEEPNORM_ALPHA = (2.0 * DEPTH) ** 0.25
LN_EPS = 1e-5
LOG2E = math.log2(math.e)

COL_MQ = 0
COL_MK = 1024
COL_MV = 2048
COL_MO = 3072
COL_DQ = 4096
COL_DK = 5120
COL_DV = 6144
COL_GM = 7168
COL_GD = 8192
N_MAIN = 9216
GATE_PAD = 128
GATE_ROWS = 16

HALO = 8
PROJ_TM = 512
PROJ_TN = 512
MLSTM_L = 256
ATT_T = 512
ATT_SUM_ROWS = 16
MERGE_TM = 512
FFN_TM = 512
FFN_TC = 256
VMEM_LIMIT = 56 * 1024 * 1024


def _sigmoid(v):
    return 1.0 / (1.0 + jnp.exp(-v))


def _log_sigmoid(v):
    return jnp.minimum(v, 0.0) - jnp.log(1.0 + jnp.exp(-jnp.abs(v)))


def _rel_bias_kernel(table_ref, out_ref):
    h = pl.program_id(0)
    T = out_ref.shape[-1]
    kpos = lax.broadcasted_iota(jnp.int32, (T, T), 0)
    qpos = lax.broadcasted_iota(jnp.int32, (T, T), 1)
    max_exact = REL_BUCKETS // 2
    far = table_ref[REL_BUCKETS - 1, h]
    for t in range(2):
        dist = qpos - kpos + (1 - t) * T
        n = jnp.maximum(dist, 0)
        nf = jnp.maximum(n, 1).astype(F32)
        large = max_exact + (jnp.log(nf / max_exact) / math.log(REL_MAX_DIST / max_exact)
                             * (REL_BUCKETS - max_exact)).astype(jnp.int32)
        large = jnp.minimum(large, REL_BUCKETS - 1)
        bucket = jnp.where(n < max_exact, n, large)
        bias = jnp.zeros((T, T), F32)
        for kk in range(REL_BUCKETS):
            bias = jnp.where(bucket == kk, table_ref[kk, h], bias)
        bias = (bias - far) * LOG2E
        out_ref[0, t] = jnp.where(dist >= 0, bias, -jnp.inf)


def _rel_bias_tiles(rel_bias):
    T = ATT_T
    return pl.pallas_call(
        _rel_bias_kernel,
        grid=(DIFF_HEADS,),
        in_specs=[pl.BlockSpec(memory_space=pltpu.SMEM)],
        out_specs=pl.BlockSpec((1, 2, T, T), lambda h: (h, 0, 0, 0)),
        out_shape=jax.ShapeDtypeStruct((DIFF_HEADS, 2, T, T), F32),
        name="rel_bias_tiles",
    )(rel_bias.astype(F32))


def _in_proj_kernel(tiles_per_seq, x_ref, w_ref, b_ref, wg_ref, bg_ref, wgt_ref, bgt_ref, cw_ref, cb_ref,
                    o_ref, gcol_ref, grow_ref, conv_ref, halo_ref):
    i = pl.program_id(0)
    tm = x_ref.shape[0]
    tn = PROJ_TN
    n_conv_cols = 2 * MLSTM_WIDTH

    @pl.when(i % tiles_per_seq == 0)
    def _():
        halo_ref[...] = jnp.zeros_like(halo_ref)

    xb = x_ref[...].astype(BF16)
    gcol_ref[...] = jnp.dot(xb, wg_ref[...], preferred_element_type=F32) + bg_ref[...]
    grow_ref[...] = lax.dot_general(wgt_ref[...], xb, (((1,), (1,)), ((), ())),
                                    preferred_element_type=F32) + bgt_ref[...]

    conv_chunks = list(range(0, n_conv_cols, tn))
    plain_chunks = list(range(n_conv_cols, N_MAIN, tn))
    per_conv = -(-len(plain_chunks) // len(conv_chunks))
    order = []
    for idx, c0 in enumerate(conv_chunks):
        order.append(c0)
        order.extend(plain_chunks[idx * per_conv:(idx + 1) * per_conv])

    for c0 in order:
        cols = slice(c0, c0 + tn)
        acc = jnp.dot(xb, w_ref[:, cols], preferred_element_type=F32) + b_ref[:, cols]
        if c0 < n_conv_cols:
            conv_ref[0:HALO, cols] = halo_ref[:, cols]
            conv_ref[HALO:HALO + tm, cols] = acc
            y = cb_ref[:, cols] + cw_ref[MLSTM_CONV - 1:MLSTM_CONV, cols] * acc
            for kk in range(MLSTM_CONV - 1):
                off = HALO - (MLSTM_CONV - 1) + kk
                y = y + cw_ref[kk:kk + 1, cols] * conv_ref[off:off + tm, cols]
            halo_ref[:, cols] = acc[tm - HALO:tm, :]
            y = y * _sigmoid(y)
            if c0 >= COL_MK:
                y = y * (MLSTM_HEAD_DIM ** -0.5)
            o_ref[:, cols] = y.astype(BF16)
        elif COL_MO <= c0 < COL_DQ or c0 >= COL_GM:
            o_ref[:, cols] = _sigmoid(acc).astype(BF16)
        elif COL_DQ <= c0 < COL_DK:
            o_ref[:, cols] = (acc * (DIFF_HEAD_DIM ** -0.5 * LOG2E)).astype(BF16)
        else:
            o_ref[:, cols] = acc.astype(BF16)


def _in_proj(x2, w_main, b_main, w_g, b_g, w_gt, b_gt, conv_w, conv_b, seq):
    T = x2.shape[0]
    tm = PROJ_TM
    n_conv_cols = 2 * MLSTM_WIDTH
    const = lambda shape: pl.BlockSpec(shape, lambda i: (0, 0), pipeline_mode=pl.Buffered(1))
    return pl.pallas_call(
        functools.partial(_in_proj_kernel, seq // tm),
        grid=(T // tm,),
        in_specs=[
            pl.BlockSpec((tm, D_MODEL), lambda i: (i, 0)),
            const((D_MODEL, N_MAIN)),
            const((1, N_MAIN)),
            const((D_MODEL, GATE_PAD)),
            const((1, GATE_PAD)),
            const((GATE_ROWS, D_MODEL)),
            const((GATE_ROWS, 1)),
            const((MLSTM_CONV, n_conv_cols)),
            const((1, n_conv_cols)),
        ],
        out_specs=[
            pl.BlockSpec((tm, N_MAIN), lambda i: (i, 0)),
            pl.BlockSpec((tm, GATE_PAD), lambda i: (i, 0)),
            pl.BlockSpec((GATE_ROWS, tm), lambda i: (0, i)),
        ],
        out_shape=[
            jax.ShapeDtypeStruct((T, N_MAIN), BF16),
            jax.ShapeDtypeStruct((T, GATE_PAD), F32),
            jax.ShapeDtypeStruct((GATE_ROWS, T), F32),
        ],
        scratch_shapes=[
            pltpu.VMEM((HALO + tm, n_conv_cols), F32),
            pltpu.VMEM((HALO, n_conv_cols), F32),
        ],
        compiler_params=pltpu.CompilerParams(
            dimension_semantics=("arbitrary",), vmem_limit_bytes=VMEM_LIMIT),
        name="in_proj",
    )(x2, w_main, b_main, w_g, b_g, w_gt, b_gt, conv_w, conv_b)


def _mlstm_kernel(q_ref, k_ref, v_ref, o_ref, gcol_ref, grow_ref, nw_ref, out_ref, c_ref, n_ref, m_ref):
    c = pl.program_id(1)
    L = q_ref.shape[0]
    d = MLSTM_HEAD_DIM
    H = MLSTM_HEADS

    @pl.when(c == 0)
    def _():
        c_ref[...] = jnp.zeros_like(c_ref)
        n_ref[...] = jnp.zeros_like(n_ref)
        m_ref[...] = jnp.zeros_like(m_ref)

    rr = lax.broadcasted_iota(jnp.int32, (L, L), 0)
    cc = lax.broadcasted_iota(jnp.int32, (L, L), 1)
    causal = rr >= cc
    tri = causal.astype(F32)

    gcol = gcol_ref[...]
    grow = grow_ref[...]
    bcol = jnp.dot(tri, _log_sigmoid(gcol), precision=lax.Precision.HIGHEST, preferred_element_type=F32)
    brow = lax.dot_general(_log_sigmoid(grow), tri, (((1,), (1,)), ((), ())),
                           precision=lax.Precision.HIGHEST, preferred_element_type=F32)

    for h in range(H):
        cols = slice(h * d, (h + 1) * d)
        q = q_ref[:, cols]
        k = k_ref[:, cols]
        v = v_ref[:, cols]
        b_c = bcol[:, H + h:H + h + 1]
        i_c = gcol[:, h:h + 1]
        b_r = brow[H + h:H + h + 1, :]
        i_r = grow[h:h + 1, :]
        m_prev = m_ref[h]
        n_prev = n_ref[h]
        c_prev = c_ref[h]

        dmat = jnp.where(causal, b_c - b_r + i_r, -jnp.inf)
        a = b_c + m_prev
        m_row = jnp.maximum(a, jnp.max(dmat, axis=-1, keepdims=True))
        w = jnp.exp(dmat - m_row)
        inter = jnp.exp(a - m_row)
        qk = lax.dot_general(q, k, (((1,), (1,)), ((), ())), preferred_element_type=F32)
        sqk = qk * w
        num = inter * jnp.dot(q, c_prev.astype(BF16), preferred_element_type=F32) \
            + jnp.dot(sqk.astype(BF16), v, preferred_element_type=F32)
        den = inter * jnp.sum(q.astype(F32) * n_prev, axis=-1, keepdims=True) \
            + jnp.sum(sqk, axis=-1, keepdims=True)
        hid = num / jnp.maximum(jnp.abs(den), jnp.exp(-m_row))

        b_last = b_r[:, L - 1:L]
        g_max = jnp.max(b_last - b_r + i_r, axis=-1, keepdims=True)
        m_new = jnp.maximum(b_last + m_prev, g_max)
        decay = jnp.exp(b_last + m_prev - m_new)
        ws = jnp.exp(b_last - b_c + i_c - m_new)
        kw = k.astype(F32) * ws
        c_ref[h] = decay * c_prev + lax.dot_general(kw.astype(BF16), v, (((0,), (0,)), ((), ())),
                                                    preferred_element_type=F32)
        n_ref[h] = decay * n_prev + jnp.sum(kw, axis=0, keepdims=True)
        m_ref[h] = m_new

        hid = o_ref[:, cols].astype(F32) * hid
        mu = jnp.mean(hid, axis=-1, keepdims=True)
        cen = hid - mu
        var = jnp.mean(cen * cen, axis=-1, keepdims=True)
        out_ref[:, cols] = (cen * lax.rsqrt(var + LN_EPS) * nw_ref[:, cols]).astype(BF16)


def _mlstm(proj, gcol, grow, norm_w, batch, seq):
    L = MLSTM_L
    nc = seq // L
    W = MLSTM_WIDTH
    row = lambda b, c: b * nc + c
    return pl.pallas_call(
        _mlstm_kernel,
        grid=(batch, nc),
        in_specs=[
            pl.BlockSpec((L, W), lambda b, c: (row(b, c), COL_MQ // W)),
            pl.BlockSpec((L, W), lambda b, c: (row(b, c), COL_MK // W)),
            pl.BlockSpec((L, W), lambda b, c: (row(b, c), COL_MV // W)),
            pl.BlockSpec((L, W), lambda b, c: (row(b, c), COL_MO // W)),
            pl.BlockSpec((L, GATE_PAD), lambda b, c: (row(b, c), 0)),
            pl.BlockSpec((GATE_ROWS, L), lambda b, c: (0, row(b, c))),
            pl.BlockSpec((1, W), lambda b, c: (0, 0)),
        ],
        out_specs=pl.BlockSpec((L, W), lambda b, c: (row(b, c), 0)),
        out_shape=jax.ShapeDtypeStruct((batch * seq, W), BF16),
        scratch_shapes=[
            pltpu.VMEM((MLSTM_HEADS, MLSTM_HEAD_DIM, MLSTM_HEAD_DIM), F32),
            pltpu.VMEM((MLSTM_HEADS, 1, MLSTM_HEAD_DIM), F32),
            pltpu.VMEM((MLSTM_HEADS, 1, 1), F32),
        ],
        compiler_params=pltpu.CompilerParams(
            dimension_semantics=("arbitrary", "arbitrary"), vmem_limit_bytes=VMEM_LIMIT),
        name="mlstm",
    )(proj, proj, proj, proj, gcol, grow, norm_w)


def _diff_attn_kernel(lam_init, q_ref, k_ref, v_ref, bias_ref, lq1_ref, lk1_ref, lq2_ref, lk2_ref, nw_ref,
                      out_ref, qs_ref, vt_ref, s_ref, p_ref, acc_ref):
    T = ATT_T
    dh = DIFF_HEAD_DIM
    dv = DIFF_V_DIM
    nq = q_ref.shape[0] // T
    R = 2 * T

    ones = jnp.ones((ATT_SUM_ROWS, T), BF16)
    for jj in range(nq):
        vt_ref[jj, 0:dv, :] = v_ref[jj * T:(jj + 1) * T, :].astype(F32).T.astype(BF16)
        vt_ref[jj, dv:dv + ATT_SUM_ROWS, :] = ones

    lane = lax.broadcasted_iota(jnp.int32, (T, dv), 1)
    for ii in range(nq):
        qt = q_ref[ii * T:(ii + 1) * T, :]
        zero = jnp.zeros_like(qt)
        qs_ref[ii, 0:T, :] = jnp.where(lane < dh, qt, zero)
        qs_ref[ii, T:R, :] = jnp.where(lane >= dh, qt, zero)

    lam = (jnp.exp(jnp.sum(lq1_ref[...] * lk1_ref[...], axis=-1, keepdims=True))
           - jnp.exp(jnp.sum(lq2_ref[...] * lk2_ref[...], axis=-1, keepdims=True)) + lam_init)

    pairs = [(i, j) for i in range(nq) for j in range(i + 1)]

    def scores(n):
        i, j = pairs[n]
        s_ref[n % 2] = lax.dot_general(k_ref[j * T:(j + 1) * T, :], qs_ref[i], (((1,), (1,)), ((), ())),
                                       preferred_element_type=F32)

    def softmax(n, m_prev):
        i, j = pairs[n]
        s = s_ref[n % 2]
        if j >= i - 1:
            bias = bias_ref[0, j - i + 1]
            s = s + jnp.concatenate([bias, bias], axis=1)
        m_tile = jnp.max(s, axis=0, keepdims=True)
        m_new = m_tile if j == 0 else jnp.maximum(m_prev, m_tile)
        p_ref[n % 2] = jnp.exp2(s - m_new).astype(BF16)
        alpha = None if j == 0 else jnp.exp2(m_prev - m_new)
        return m_new, alpha

    def values(n, alpha):
        i, j = pairs[n]
        pv = jnp.dot(vt_ref[j], p_ref[n % 2], preferred_element_type=F32)
        if j == 0:
            acc_ref[...] = pv
        else:
            acc_ref[...] = alpha * acc_ref[...] + pv
        if j == i:
            acc = acc_ref[...]
            out = acc[0:dv, :] / acc[dv:dv + 1, :]
            hd = out[:, 0:T] - lam * out[:, T:R]
            hd = hd * lax.rsqrt(jnp.mean(hd * hd, axis=0, keepdims=True) + LN_EPS) * nw_ref[...]
            out_ref[i * T:(i + 1) * T, :] = (hd * (1.0 - lam_init)).T.astype(BF16)

    scores(0)
    m_run = None
    alphas = {}
    for n in range(len(pairs)):
        if n + 1 < len(pairs):
            scores(n + 1)
        m_run, alphas[n] = softmax(n, m_run)
        if n >= 1:
            values(n - 1, alphas.pop(n - 1))
    values(len(pairs) - 1, alphas.pop(len(pairs) - 1))


def _diff_attn(proj, bias_tiles, lq1, lk1, lq2, lk2, norm_w, lam_init, batch, seq):
    T = ATT_T
    nq = seq // T
    dv = DIFF_V_DIM
    small = lambda shape: pl.BlockSpec(shape, lambda b, h: (0, 0))
    return pl.pallas_call(
        functools.partial(_diff_attn_kernel, lam_init),
        grid=(batch, DIFF_HEADS),
        in_specs=[
            pl.BlockSpec((seq, dv), lambda b, h: (b, COL_DQ // dv + h)),
            pl.BlockSpec((seq, dv), lambda b, h: (b, COL_DK // dv + h)),
            pl.BlockSpec((seq, dv), lambda b, h: (b, COL_DV // dv + h)),
            pl.BlockSpec((1, 2, T, T), lambda b, h: (h, 0, 0, 0)),
            small((1, DIFF_HEAD_DIM)), small((1, DIFF_HEAD_DIM)),
            small((1, DIFF_HEAD_DIM)), small((1, DIFF_HEAD_DIM)),
            small((dv, 1)),
        ],
        out_specs=pl.BlockSpec((seq, dv), lambda b, h: (b, h)),
        out_shape=jax.ShapeDtypeStruct((batch * seq, DIFF_WIDTH), BF16),
        scratch_shapes=[
            pltpu.VMEM((nq, 2 * T, dv), BF16),
            pltpu.VMEM((nq, dv + ATT_SUM_ROWS, T), BF16),
            pltpu.VMEM((2, T, 2 * T), F32),
            pltpu.VMEM((2, T, 2 * T), BF16),
            pltpu.VMEM((dv + ATT_SUM_ROWS, 2 * T), F32),
        ],
        compiler_params=pltpu.CompilerParams(
            dimension_semantics=("arbitrary", "arbitrary"), vmem_limit_bytes=VMEM_LIMIT),
        name="diff_attn",
    )(proj, proj, proj, bias_tiles, lq1, lk1, lq2, lk2, norm_w)


def _layer_norm(y, g, b):
    mu = jnp.mean(y, axis=-1, keepdims=True)
    cen = y - mu
    var = jnp.mean(cen * cen, axis=-1, keepdims=True)
    return cen * lax.rsqrt(var + LN_EPS) * g + b


def _merge_kernel(hm_ref, hd_ref, gm_ref, gd_ref, x_ref, wbm_ref, wbd_ref, wo_ref, g_ref, b_ref, out_ref):
    pm = jnp.dot(hm_ref[...], wbm_ref[...], preferred_element_type=F32)
    pd = jnp.dot(hd_ref[...], wbd_ref[...], preferred_element_type=F32)
    merged = gm_ref[...].astype(F32) * pm + gd_ref[...].astype(F32) * pd
    mix = jnp.dot(merged.astype(BF16), wo_ref[...], preferred_element_type=F32)
    out_ref[...] = _layer_norm(DEEPNORM_ALPHA * x_ref[...] + mix, g_ref[...], b_ref[...])


def _merge(hm, hd, proj, x2, w_bm, w_bd, w_o, ln_g, ln_b):
    T = x2.shape[0]
    tm = MERGE_TM
    D = D_MODEL
    rows = lambda col: pl.BlockSpec((tm, D), lambda i: (i, col))
    full = lambda shape: pl.BlockSpec(shape, lambda i: (0, 0))
    return pl.pallas_call(
        _merge_kernel,
        grid=(T // tm,),
        in_specs=[rows(0), rows(0), rows(COL_GM // D), rows(COL_GD // D), rows(0),
                  full((MLSTM_WIDTH, D)), full((DIFF_WIDTH, D)), full((D, D)), full((1, D)), full((1, D))],
        out_specs=rows(0),
        out_shape=jax.ShapeDtypeStruct((T, D), F32),
        compiler_params=pltpu.CompilerParams(
            dimension_semantics=("arbitrary",), vmem_limit_bytes=VMEM_LIMIT),
        name="merge_ln",
    )(hm, hd, proj, proj, x2, w_bm, w_bd, w_o, ln_g, ln_b)


def _ffn_kernel(tiles_per_seq, h_ref, wu_ref, cw_ref, cb_ref, wd_ref, g_ref, b_ref, out_ref,
                conv_ref, halo_ref, act_ref):
    i = pl.program_id(0)
    tm = h_ref.shape[0]
    tc = FFN_TC

    @pl.when(i % tiles_per_seq == 0)
    def _():
        halo_ref[...] = jnp.zeros_like(halo_ref)

    h = h_ref[...]
    hb = h.astype(BF16)
    for c0 in range(0, D_FF, tc):
        cols = slice(c0, c0 + tc)
        a = jnp.dot(hb, wu_ref[:, cols], preferred_element_type=F32)
        gate = jnp.dot(hb, wu_ref[:, D_FF + c0:D_FF + c0 + tc], preferred_element_type=F32)
        conv_ref[0:HALO, cols] = halo_ref[:, cols]
        conv_ref[HALO:HALO + tm, cols] = a
        y = cb_ref[:, cols] + cw_ref[FFN_CONV - 1:FFN_CONV, cols] * a
        for kk in range(FFN_CONV - 1):
            off = HALO - (FFN_CONV - 1) + kk
            y = y + cw_ref[kk:kk + 1, cols] * conv_ref[off:off + tm, cols]
        halo_ref[:, cols] = a[tm - HALO:tm, :]
        act_ref[:, cols] = (y * _sigmoid(y) * gate).astype(BF16)

    ffn = jnp.dot(act_ref[...], wd_ref[...], preferred_element_type=F32)
    out_ref[...] = _layer_norm(DEEPNORM_ALPHA * h + ffn, g_ref[...], b_ref[...])


def _ffn(h1, w_up, conv_w, conv_b, w_down, ln_g, ln_b, seq):
    T = h1.shape[0]
    tm = FFN_TM
    D = D_MODEL
    const = lambda shape: pl.BlockSpec(shape, lambda i: (0, 0), pipeline_mode=pl.Buffered(1))
    return pl.pallas_call(
        functools.partial(_ffn_kernel, seq // tm),
        grid=(T // tm,),
        in_specs=[
            pl.BlockSpec((tm, D), lambda i: (i, 0)),
            const((D, 2 * D_FF)),
            const((FFN_CONV, D_FF)),
            const((1, D_FF)),
            const((D_FF, D)),
            const((1, D)),
            const((1, D)),
        ],
        out_specs=pl.BlockSpec((tm, D), lambda i: (i, 0)),
        out_shape=jax.ShapeDtypeStruct((T, D), F32),
        scratch_shapes=[
            pltpu.VMEM((HALO + tm, D_FF), F32),
            pltpu.VMEM((HALO, D_FF), F32),
            pltpu.VMEM((tm, D_FF), BF16),
        ],
        compiler_params=pltpu.CompilerParams(
            dimension_semantics=("arbitrary",), vmem_limit_bytes=VMEM_LIMIT),
        name="conv_ffn",
    )(h1, w_up, conv_w, conv_b, w_down, ln_g, ln_b)


def _layer(h2, batch, seq, l, w_in, b_in, mconv_w, mconv_b, mnorm_w, lq1, lk1, lq2, lk2, dnorm_w, bias_tiles,
           w_bm, w_bd, w_o, ln1_g, ln1_b, w_up, fconv_w, fconv_b, w_down, ln2_g, ln2_b):
    lam_init = 0.8 - 0.6 * math.exp(-0.3 * l)
    n_m = 4 * MLSTM_WIDTH
    n_gate = 2 * MLSTM_HEADS
    w_main = jnp.concatenate([w_in[:, :n_m], w_in[:, n_m + n_gate:]], axis=1).astype(BF16)
    b_main = jnp.concatenate([b_in[:n_m], b_in[n_m + n_gate:]])[None, :].astype(F32)
    w_gate = w_in[:, n_m:n_m + n_gate]
    b_gate = b_in[n_m:n_m + n_gate].astype(F32)
    w_g = jnp.pad(w_gate, ((0, 0), (0, GATE_PAD - n_gate))).astype(BF16)
    b_g = jnp.pad(b_gate, (0, GATE_PAD - n_gate))[None, :]
    w_gt = jnp.pad(w_gate.T, ((0, GATE_ROWS - n_gate), (0, 0))).astype(BF16)
    b_gt = jnp.pad(b_gate, (0, GATE_ROWS - n_gate))[:, None]

    proj, gcol, grow = _in_proj(h2, w_main, b_main, w_g, b_g, w_gt, b_gt,
                                mconv_w.astype(F32), mconv_b[None, :].astype(F32), seq)
    hm = _mlstm(proj, gcol, grow, mnorm_w[None, :].astype(F32), batch, seq)
    hd = _diff_attn(proj, bias_tiles, lq1[None, :].astype(F32), lk1[None, :].astype(F32),
                    lq2[None, :].astype(F32), lk2[None, :].astype(F32), dnorm_w[:, None].astype(F32),
                    lam_init, batch, seq)
    h1 = _merge(hm, hd, proj, h2, w_bm.astype(BF16), w_bd.astype(BF16), w_o.astype(BF16),
                ln1_g[None, :].astype(F32), ln1_b[None, :].astype(F32))
    return _ffn(h1, w_up.astype(BF16), fconv_w.astype(F32), fconv_b[None, :].astype(F32),
                w_down.astype(BF16), ln2_g[None, :].astype(F32), ln2_b[None, :].astype(F32), seq)


def kernel(x, w_in, b_in, mlstm_conv_w, mlstm_conv_b, mlstm_norm_w, lambda_q1, lambda_k1, lambda_q2, lambda_k2,
           diff_norm_w, rel_bias, w_branch_mlstm, w_branch_diff, w_out, ln1_g, ln1_b, w_ffn_up, ffn_conv_w,
           ffn_conv_b, w_ffn_down, ln2_g, ln2_b):
    batch, seq, d_model = x.shape
    assert d_model == D_MODEL and seq % max(PROJ_TM, FFN_TM, ATT_T, MLSTM_L) == 0
    bias_tiles = _rel_bias_tiles(rel_bias)
    h2 = x.reshape(batch * seq, d_model)
    for l in range(w_in.shape[0]):
        h2 = _layer(h2, batch, seq, l, w_in[l], b_in[l], mlstm_conv_w[l], mlstm_conv_b[l], mlstm_norm_w[l],
                    lambda_q1[l], lambda_k1[l], lambda_q2[l], lambda_k2[l], diff_norm_w[l], bias_tiles,
                    w_branch_mlstm[l], w_branch_diff[l], w_out[l], ln1_g[l], ln1_b[l], w_ffn_up[l],
                    ffn_conv_w[l], ffn_conv_b[l], w_ffn_down[l], ln2_g[l], ln2_b[l])
    return h2.reshape(batch, seq, d_model).astype(x.dtype)
```

```python
import functools
import math

import jax
import jax.numpy as jnp
from jax import lax
from jax.experimental import pallas as pl
from jax.experimental.pallas import tpu as pltpu

F32 = jnp.float32
BF16 = jnp.bfloat16

D_MODEL = 1024
MLSTM_HEADS = 4
MLSTM_HEAD_DIM = 256
MLSTM_WIDTH = MLSTM_HEADS * MLSTM_HEAD_DIM
MLSTM_CONV = 4
DIFF_HEADS = 8
DIFF_HEAD_DIM = 64
DIFF_V_DIM = 2 * DIFF_HEAD_DIM
DIFF_WIDTH = DIFF_HEADS * DIFF_V_DIM
REL_BUCKETS = 32
REL_MAX_DIST = 128
D_FF = 2816
FFN_CONV = 3
DEPTH = 1
DEEPNORM_ALPHA = (2.0 * DEPTH) ** 0.25
LN_EPS = 1e-5
LOG2E = math.log2(math.e)

COL_MQ = 0
COL_MK = 1024
COL_DQ = 2048
COL_DK = 3072
COL_DV = 4096
COL_GM = 5120
COL_GD = 6144
N_MAIN = 7168
ROW_MV = 0
ROW_MO = 1024
N_TRANS = 2048
GATE_PAD = 128
GATE_ROWS = 16
MLSTM_AUG = 16

HALO = 8
PROJ_TM = 512
PROJ_TN = 512
MLSTM_L = 256
ATT_T = 512
ATT_SUM_ROWS = 16
MERGE_TM = 512
FFN_TM = 512
FFN_TC = 256
VMEM_LIMIT = 56 * 1024 * 1024


def _sigmoid(v):
    return 1.0 / (1.0 + jnp.exp(-v))


def _log_sigmoid(v):
    return jnp.minimum(v, 0.0) - jnp.log(1.0 + jnp.exp(-jnp.abs(v)))


def _split3(v):
    hi = v.astype(BF16)
    rest = v - hi.astype(F32)
    mid = rest.astype(BF16)
    lo = (rest - mid.astype(F32)).astype(BF16)
    return hi, mid, lo


def _rel_bias_kernel(table_ref, out_ref):
    h = pl.program_id(0)
    T = out_ref.shape[-1]
    kpos = lax.broadcasted_iota(jnp.int32, (T, T), 0)
    qpos = lax.broadcasted_iota(jnp.int32, (T, T), 1)
    max_exact = REL_BUCKETS // 2
    far = table_ref[REL_BUCKETS - 1, h]
    for t in range(2):
        dist = qpos - kpos + (1 - t) * T
        n = jnp.maximum(dist, 0)
        nf = jnp.maximum(n, 1).astype(F32)
        large = max_exact + (jnp.log(nf / max_exact) / math.log(REL_MAX_DIST / max_exact)
                             * (REL_BUCKETS - max_exact)).astype(jnp.int32)
        large = jnp.minimum(large, REL_BUCKETS - 1)
        bucket = jnp.where(n < max_exact, n, large)
        bias = jnp.zeros((T, T), F32)
        for kk in range(REL_BUCKETS):
            bias = jnp.where(bucket == kk, table_ref[kk, h], bias)
        bias = (bias - far) * LOG2E
        out_ref[0, t] = jnp.where(dist >= 0, bias, -jnp.inf)


def _rel_bias_tiles(rel_bias):
    T = ATT_T
    return pl.pallas_call(
        _rel_bias_kernel,
        grid=(DIFF_HEADS,),
        in_specs=[pl.BlockSpec(memory_space=pltpu.SMEM)],
        out_specs=pl.BlockSpec((1, 2, T, T), lambda h: (h, 0, 0, 0)),
        out_shape=jax.ShapeDtypeStruct((DIFF_HEADS, 2, T, T), F32),
        name="rel_bias_tiles",
    )(rel_bias.astype(F32))


def _in_proj_kernel(tiles_per_seq, x_ref, w_ref, b_ref, wt_ref, bt_ref, wg_ref, bg_ref, wgt_ref, bgt_ref,
                    cw_ref, cb_ref, o_ref, ot_ref, gcol_ref, grow_ref, conv_ref, halo_ref):
    i = pl.program_id(0)
    tm = x_ref.shape[0]
    tn = PROJ_TN
    n_conv_cols = 2 * MLSTM_WIDTH
    nt_dims = (((1,), (1,)), ((), ()))

    @pl.when(i % tiles_per_seq == 0)
    def _():
        halo_ref[...] = jnp.zeros_like(halo_ref)

    xb = x_ref[...].astype(BF16)
    gcol_ref[...] = jnp.dot(xb, wg_ref[...], preferred_element_type=F32) + bg_ref[...]
    grow_ref[...] = lax.dot_general(wgt_ref[...], xb, nt_dims, preferred_element_type=F32) + bgt_ref[...]

    for r0 in range(0, N_TRANS, tn):
        rows = slice(r0, r0 + tn)
        acc_t = lax.dot_general(wt_ref[rows, :], xb, nt_dims, preferred_element_type=F32) + bt_ref[rows, :]
        if r0 >= ROW_MO:
            acc_t = _sigmoid(acc_t)
        ot_ref[rows, :] = acc_t.astype(BF16)

    for c0 in range(0, N_MAIN, tn):
        cols = slice(c0, c0 + tn)
        acc = jnp.dot(xb, w_ref[:, cols], preferred_element_type=F32) + b_ref[:, cols]
        if c0 < n_conv_cols:
            conv_ref[0:HALO, cols] = halo_ref[:, cols]
            conv_ref[HALO:HALO + tm, cols] = acc
            y = cb_ref[:, cols] + cw_ref[MLSTM_CONV - 1:MLSTM_CONV, cols] * acc
            for kk in range(MLSTM_CONV - 1):
                off = HALO - (MLSTM_CONV - 1) + kk
                y = y + cw_ref[kk:kk + 1, cols] * conv_ref[off:off + tm, cols]
            halo_ref[:, cols] = acc[tm - HALO:tm, :]
            y = y * _sigmoid(y)
            if c0 >= COL_MK:
                y = y * (MLSTM_HEAD_DIM ** -0.5)
            o_ref[:, cols] = y.astype(BF16)
        elif c0 >= COL_GM:
            o_ref[:, cols] = _sigmoid(acc).astype(BF16)
        elif COL_DQ <= c0 < COL_DK:
            o_ref[:, cols] = (acc * (DIFF_HEAD_DIM ** -0.5 * LOG2E)).astype(BF16)
        else:
            o_ref[:, cols] = acc.astype(BF16)


def _in_proj(x2, w_main, b_main, w_t, b_t, w_g, b_g, w_gt, b_gt, conv_w, conv_b, seq):
    T = x2.shape[0]
    tm = PROJ_TM
    n_conv_cols = 2 * MLSTM_WIDTH
    const = lambda shape: pl.BlockSpec(shape, lambda i: (0, 0), pipeline_mode=pl.Buffered(1))
    return pl.pallas_call(
        functools.partial(_in_proj_kernel, seq // tm),
        grid=(T // tm,),
        in_specs=[
            pl.BlockSpec((tm, D_MODEL), lambda i: (i, 0)),
            const((D_MODEL, N_MAIN)),
            const((1, N_MAIN)),
            const((N_TRANS, D_MODEL)),
            const((N_TRANS, 1)),
            const((D_MODEL, GATE_PAD)),
            const((1, GATE_PAD)),
            const((GATE_ROWS, D_MODEL)),
            const((GATE_ROWS, 1)),
            const((MLSTM_CONV, n_conv_cols)),
            const((1, n_conv_cols)),
        ],
        out_specs=[
            pl.BlockSpec((tm, N_MAIN), lambda i: (i, 0)),
            pl.BlockSpec((N_TRANS, tm), lambda i: (0, i)),
            pl.BlockSpec((tm, GATE_PAD), lambda i: (i, 0)),
            pl.BlockSpec((GATE_ROWS, tm), lambda i: (0, i)),
        ],
        out_shape=[
            jax.ShapeDtypeStruct((T, N_MAIN), BF16),
            jax.ShapeDtypeStruct((N_TRANS, T), BF16),
            jax.ShapeDtypeStruct((T, GATE_PAD), F32),
            jax.ShapeDtypeStruct((GATE_ROWS, T), F32),
        ],
        scratch_shapes=[
            pltpu.VMEM((HALO + tm, n_conv_cols), F32),
            pltpu.VMEM((HALO, n_conv_cols), F32),
        ],
        compiler_params=pltpu.CompilerParams(
            dimension_semantics=("arbitrary",), vmem_limit_bytes=VMEM_LIMIT),
        name="in_proj",
    )(x2, w_main, b_main, w_t, b_t, w_g, b_g, w_gt, b_gt, conv_w, conv_b)


def _mlstm_kernel(q_ref, k_ref, vt_ref, ot_ref, gcol_ref, grow_ref, nw_ref, out_ref, ct_ref, m_ref, nwb_ref):
    c = pl.program_id(1)
    L = q_ref.shape[0]
    d = MLSTM_HEAD_DIM
    H = MLSTM_HEADS
    nt_dims = (((1,), (1,)), ((), ()))

    @pl.when(c == 0)
    def _():
        ct_ref[...] = jnp.zeros_like(ct_ref)
        m_ref[...] = jnp.zeros_like(m_ref)
        nwb_ref[...] = jnp.broadcast_to(nw_ref[...], nwb_ref.shape)

    key = lax.broadcasted_iota(jnp.int32, (L, L), 0)
    qry = lax.broadcasted_iota(jnp.int32, (L, L), 1)
    causal_t = key <= qry
    tri = jnp.where(key >= qry, 1.0, 0.0).astype(BF16)

    gcol = gcol_ref[...]
    grow = grow_ref[...]
    bc3 = jnp.dot(tri, jnp.concatenate(_split3(_log_sigmoid(gcol)), axis=1), preferred_element_type=F32)
    bcol = bc3[:, 0:GATE_PAD] + bc3[:, GATE_PAD:2 * GATE_PAD] + bc3[:, 2 * GATE_PAD:3 * GATE_PAD]
    br3 = lax.dot_general(jnp.concatenate(_split3(_log_sigmoid(grow)), axis=0), tri, nt_dims,
                          preferred_element_type=F32)
    brow = br3[0:GATE_ROWS] + br3[GATE_ROWS:2 * GATE_ROWS] + br3[2 * GATE_ROWS:3 * GATE_ROWS]
    gdiff = gcol - pltpu.roll(bcol, shift=GATE_PAD - H, axis=1)
    ones = jnp.ones((MLSTM_AUG, L), BF16)

    for h in range(H):
        cols = slice(h * d, (h + 1) * d)
        q = q_ref[:, cols]
        k = k_ref[:, cols]
        v_aug = jnp.concatenate([vt_ref[cols, :], ones], axis=0)
        b_r = brow[H + h:H + h + 1, :]
        i_r = grow[h:h + 1, :]
        g_c = gdiff[:, h:h + 1]
        m_prev = m_ref[h]
        ct = ct_ref[h]

        dmat = jnp.where(causal_t, b_r + g_c, -jnp.inf)
        a = b_r + m_prev
        m_row = jnp.maximum(a, jnp.max(dmat, axis=0, keepdims=True))
        w = jnp.exp(dmat - m_row)
        inter = jnp.exp(a - m_row)
        kq = lax.dot_general(k, q, nt_dims, preferred_element_type=F32)
        sqk = (kq * w).astype(BF16)
        nd = inter * lax.dot_general(ct.astype(BF16), q, nt_dims, preferred_element_type=F32) \
            + jnp.dot(v_aug, sqk, preferred_element_type=F32)
        rinv = 1.0 / jnp.maximum(jnp.abs(nd[d:d + 1, :]), jnp.exp(-m_row))
        hid = nd[0:d, :] * rinv * ot_ref[cols, :].astype(F32)
        mu = jnp.mean(hid, axis=0, keepdims=True)
        cen = hid - mu
        var = jnp.mean(cen * cen, axis=0, keepdims=True)
        out_ref[:, cols] = (cen * lax.rsqrt(var + LN_EPS) * nwb_ref[cols, :]).T.astype(BF16)

        b_last = b_r[:, L - 1:L]
        g_r = b_last - b_r + i_r
        m_new = jnp.maximum(b_last + m_prev, jnp.max(g_r, axis=-1, keepdims=True))
        decay = jnp.exp(b_last + m_prev - m_new)
        ws = jnp.exp(g_r - m_new)
        vw = (v_aug.astype(F32) * ws).astype(BF16)
        ct_ref[h] = decay * ct + jnp.dot(vw, k, preferred_element_type=F32)
        m_ref[h] = m_new


def _mlstm(proj, proj_t, gcol, grow, norm_w, batch, seq):
    L = MLSTM_L
    nc = seq // L
    W = MLSTM_WIDTH
    row = lambda b, c: b * nc + c
    return pl.pallas_call(
        _mlstm_kernel,
        grid=(batch, nc),
        in_specs=[
            pl.BlockSpec((L, W), lambda b, c: (row(b, c), COL_MQ // W)),
            pl.BlockSpec((L, W), lambda b, c: (row(b, c), COL_MK // W)),
            pl.BlockSpec((W, L), lambda b, c: (ROW_MV // W, row(b, c))),
            pl.BlockSpec((W, L), lambda b, c: (ROW_MO // W, row(b, c))),
            pl.BlockSpec((L, GATE_PAD), lambda b, c: (row(b, c), 0)),
            pl.BlockSpec((GATE_ROWS, L), lambda b, c: (0, row(b, c))),
            pl.BlockSpec((W, 1), lambda b, c: (0, 0)),
        ],
        out_specs=pl.BlockSpec((L, W), lambda b, c: (row(b, c), 0)),
        out_shape=jax.ShapeDtypeStruct((batch * seq, W), BF16),
        scratch_shapes=[
            pltpu.VMEM((MLSTM_HEADS, MLSTM_HEAD_DIM + MLSTM_AUG, MLSTM_HEAD_DIM), F32),
            pltpu.VMEM((MLSTM_HEADS, 1, 1), F32),
            pltpu.VMEM((W, L), F32),
        ],
        compiler_params=pltpu.CompilerParams(
            dimension_semantics=("arbitrary", "arbitrary"), vmem_limit_bytes=VMEM_LIMIT),
        name="mlstm",
    )(proj, proj, proj_t, proj_t, gcol, grow, norm_w)


def _diff_attn_kernel(lam_init, q_ref, k_ref, v_ref, bias_ref, lq1_ref, lk1_ref, lq2_ref, lk2_ref, nw_ref,
                      out_ref, qs_ref, vt_ref, s_ref, p_ref, acc_ref):
    T = ATT_T
    dh = DIFF_HEAD_DIM
    dv = DIFF_V_DIM
    nq = q_ref.shape[0] // T
    R = 2 * T

    ones = jnp.ones((ATT_SUM_ROWS, T), BF16)
    for jj in range(nq):
        vt_ref[jj, 0:dv, :] = v_ref[jj * T:(jj + 1) * T, :].astype(F32).T.astype(BF16)
        vt_ref[jj, dv:dv + ATT_SUM_ROWS, :] = ones

    lane = lax.broadcasted_iota(jnp.int32, (T, dv), 1)
    for ii in range(nq):
        qt = q_ref[ii * T:(ii + 1) * T, :]
        zero = jnp.zeros_like(qt)
        qs_ref[ii, 0:T, :] = jnp.where(lane < dh, qt, zero)
        qs_ref[ii, T:R, :] = jnp.where(lane >= dh, qt, zero)

    lam = (jnp.exp(jnp.sum(lq1_ref[...] * lk1_ref[...], axis=-1, keepdims=True))
           - jnp.exp(jnp.sum(lq2_ref[...] * lk2_ref[...], axis=-1, keepdims=True)) + lam_init)

    pairs = [(i, j) for i in range(nq) for j in range(i + 1)]

    def scores(n):
        i, j = pairs[n]
        s_ref[n % 2] = lax.dot_general(k_ref[j * T:(j + 1) * T, :], qs_ref[i], (((1,), (1,)), ((), ())),
                                       preferred_element_type=F32)

    def softmax(n, m_prev):
        i, j = pairs[n]
        s = s_ref[n % 2]
        if j >= i - 1:
            bias = bias_ref[0, j - i + 1]
            s = s + jnp.concatenate([bias, bias], axis=1)
        m_tile = jnp.max(s, axis=0, keepdims=True)
        m_new = m_tile if j == 0 else jnp.maximum(m_prev, m_tile)
        p_ref[n % 2] = jnp.exp2(s - m_new).astype(BF16)
        alpha = None if j == 0 else jnp.exp2(m_prev - m_new)
        return m_new, alpha

    def values(n, alpha):
        i, j = pairs[n]
        pv = jnp.dot(vt_ref[j], p_ref[n % 2], preferred_element_type=F32)
        if j == 0:
            acc_ref[...] = pv
        else:
            acc_ref[...] = alpha * acc_ref[...] + pv
        if j == i:
            acc = acc_ref[...]
            out = acc[0:dv, :] / acc[dv:dv + 1, :]
            hd = out[:, 0:T] - lam * out[:, T:R]
            hd = hd * lax.rsqrt(jnp.mean(hd * hd, axis=0, keepdims=True) + LN_EPS) * nw_ref[...]
            out_ref[i * T:(i + 1) * T, :] = (hd * (1.0 - lam_init)).T.astype(BF16)

    scores(0)
    m_run = None
    alphas = {}
    for n in range(len(pairs)):
        if n + 1 < len(pairs):
            scores(n + 1)
        m_run, alphas[n] = softmax(n, m_run)
        if n >= 1:
            values(n - 1, alphas.pop(n - 1))
    values(len(pairs) - 1, alphas.pop(len(pairs) - 1))


def _diff_attn(proj, bias_tiles, lq1, lk1, lq2, lk2, norm_w, lam_init, batch, seq):
    T = ATT_T
    nq = seq // T
    dv = DIFF_V_DIM
    small = lambda shape: pl.BlockSpec(shape, lambda b, h: (0, 0))
    return pl.pallas_call(
        functools.partial(_diff_attn_kernel, lam_init),
        grid=(batch, DIFF_HEADS),
        in_specs=[
            pl.BlockSpec((seq, dv), lambda b, h: (b, COL_DQ // dv + h)),
            pl.BlockSpec((seq, dv), lambda b, h: (b, COL_DK // dv + h)),
            pl.BlockSpec((seq, dv), lambda b, h: (b, COL_DV // dv + h)),
            pl.BlockSpec((1, 2, T, T), lambda b, h: (h, 0, 0, 0)),
            small((1, DIFF_HEAD_DIM)), small((1, DIFF_HEAD_DIM)),
            small((1, DIFF_HEAD_DIM)), small((1, DIFF_HEAD_DIM)),
            small((dv, 1)),
        ],
        out_specs=pl.BlockSpec((seq, dv), lambda b, h: (b, h)),
        out_shape=jax.ShapeDtypeStruct((batch * seq, DIFF_WIDTH), BF16),
        scratch_shapes=[
            pltpu.VMEM((nq, 2 * T, dv), BF16),
            pltpu.VMEM((nq, dv + ATT_SUM_ROWS, T), BF16),
            pltpu.VMEM((2, T, 2 * T), F32),
            pltpu.VMEM((2, T, 2 * T), BF16),
            pltpu.VMEM((dv + ATT_SUM_ROWS, 2 * T), F32),
        ],
        compiler_params=pltpu.CompilerParams(
            dimension_semantics=("arbitrary", "arbitrary"), vmem_limit_bytes=VMEM_LIMIT),
        name="diff_attn",
    )(proj, proj, proj, bias_tiles, lq1, lk1, lq2, lk2, norm_w)


def _layer_norm(y, g, b):
    mu = jnp.mean(y, axis=-1, keepdims=True)
    cen = y - mu
    var = jnp.mean(cen * cen, axis=-1, keepdims=True)
    return cen * lax.rsqrt(var + LN_EPS) * g + b


def _merge_kernel(hm_ref, hd_ref, gm_ref, gd_ref, x_ref, wbm_ref, wbd_ref, wo_ref, g_ref, b_ref, out_ref):
    pm = jnp.dot(hm_ref[...], wbm_ref[...], preferred_element_type=F32)
    pd = jnp.dot(hd_ref[...], wbd_ref[...], preferred_element_type=F32)
    merged = gm_ref[...].astype(F32) * pm + gd_ref[...].astype(F32) * pd
    mix = jnp.dot(merged.astype(BF16), wo_ref[...], preferred_element_type=F32)
    out_ref[...] = _layer_norm(DEEPNORM_ALPHA * x_ref[...] + mix, g_ref[...], b_ref[...])


def _merge(hm, hd, proj, x2, w_bm, w_bd, w_o, ln_g, ln_b):
    T = x2.shape[0]
    tm = MERGE_TM
    D = D_MODEL
    rows = lambda col: pl.BlockSpec((tm, D), lambda i: (i, col))
    full = lambda shape: pl.BlockSpec(shape, lambda i: (0, 0))
    return pl.pallas_call(
        _merge_kernel,
        grid=(T // tm,),
        in_specs=[rows(0), rows(0), rows(COL_GM // D), rows(COL_GD // D), rows(0),
                  full((MLSTM_WIDTH, D)), full((DIFF_WIDTH, D)), full((D, D)), full((1, D)), full((1, D))],
        out_specs=rows(0),
        out_shape=jax.ShapeDtypeStruct((T, D), F32),
        compiler_params=pltpu.CompilerParams(
            dimension_semantics=("arbitrary",), vmem_limit_bytes=VMEM_LIMIT),
        name="merge_ln",
    )(hm, hd, proj, proj, x2, w_bm, w_bd, w_o, ln_g, ln_b)


def _ffn_kernel(tiles_per_seq, h_ref, wu_ref, cw_ref, cb_ref, wd_ref, g_ref, b_ref, out_ref,
                conv_ref, halo_ref, act_ref):
    i = pl.program_id(0)
    tm = h_ref.shape[0]
    tc = FFN_TC

    @pl.when(i % tiles_per_seq == 0)
    def _():
        halo_ref[...] = jnp.zeros_like(halo_ref)

    h = h_ref[...]
    hb = h.astype(BF16)
    for c0 in range(0, D_FF, tc):
        cols = slice(c0, c0 + tc)
        a = jnp.dot(hb, wu_ref[:, cols], preferred_element_type=F32)
        gate = jnp.dot(hb, wu_ref[:, D_FF + c0:D_FF + c0 + tc], preferred_element_type=F32)
        conv_ref[0:HALO, cols] = halo_ref[:, cols]
        conv_ref[HALO:HALO + tm, cols] = a
        y = cb_ref[:, cols] + cw_ref[FFN_CONV - 1:FFN_CONV, cols] * a
        for kk in range(FFN_CONV - 1):
            off = HALO - (FFN_CONV - 1) + kk
            y = y + cw_ref[kk:kk + 1, cols] * conv_ref[off:off + tm, cols]
        halo_ref[:, cols] = a[tm - HALO:tm, :]
        act_ref[:, cols] = (y * _sigmoid(y) * gate).astype(BF16)

    ffn = jnp.dot(act_ref[...], wd_ref[...], preferred_element_type=F32)
    out_ref[...] = _layer_norm(DEEPNORM_ALPHA * h + ffn, g_ref[...], b_ref[...])


def _ffn(h1, w_up, conv_w, conv_b, w_down, ln_g, ln_b, seq):
    T = h1.shape[0]
    tm = FFN_TM
    D = D_MODEL
    const = lambda shape: pl.BlockSpec(shape, lambda i: (0, 0), pipeline_mode=pl.Buffered(1))
    return pl.pallas_call(
        functools.partial(_ffn_kernel, seq // tm),
        grid=(T // tm,),
        in_specs=[
            pl.BlockSpec((tm, D), lambda i: (i, 0)),
            const((D, 2 * D_FF)),
            const((FFN_CONV, D_FF)),
            const((1, D_FF)),
            const((D_FF, D)),
            const((1, D)),
            const((1, D)),
        ],
        out_specs=pl.BlockSpec((tm, D), lambda i: (i, 0)),
        out_shape=jax.ShapeDtypeStruct((T, D), F32),
        scratch_shapes=[
            pltpu.VMEM((HALO + tm, D_FF), F32),
            pltpu.VMEM((HALO, D_FF), F32),
            pltpu.VMEM((tm, D_FF), BF16),
        ],
        compiler_params=pltpu.CompilerParams(
            dimension_semantics=("arbitrary",), vmem_limit_bytes=VMEM_LIMIT),
        name="conv_ffn",
    )(h1, w_up, conv_w, conv_b, w_down, ln_g, ln_b)


def _layer(h2, batch, seq, l, w_in, b_in, mconv_w, mconv_b, mnorm_w, lq1, lk1, lq2, lk2, dnorm_w, bias_tiles,
           w_bm, w_bd, w_o, ln1_g, ln1_b, w_up, fconv_w, fconv_b, w_down, ln2_g, ln2_b):
    lam_init = 0.8 - 0.6 * math.exp(-0.3 * l)
    n_m = 4 * MLSTM_WIDTH
    n_gate = 2 * MLSTM_HEADS
    n_qk = 2 * MLSTM_WIDTH
    w_main = jnp.concatenate([w_in[:, :n_qk], w_in[:, n_m + n_gate:]], axis=1).astype(BF16)
    b_main = jnp.concatenate([b_in[:n_qk], b_in[n_m + n_gate:]])[None, :].astype(F32)
    w_t = w_in[:, n_qk:n_m].T.astype(BF16)
    b_t = b_in[n_qk:n_m][:, None].astype(F32)
    w_gate = w_in[:, n_m:n_m + n_gate]
    b_gate = b_in[n_m:n_m + n_gate].astype(F32)
    w_g = jnp.pad(w_gate, ((0, 0), (0, GATE_PAD - n_gate))).astype(BF16)
    b_g = jnp.pad(b_gate, (0, GATE_PAD - n_gate))[None, :]
    w_gt = jnp.pad(w_gate.T, ((0, GATE_ROWS - n_gate), (0, 0))).astype(BF16)
    b_gt = jnp.pad(b_gate, (0, GATE_ROWS - n_gate))[:, None]

    proj, proj_t, gcol, grow = _in_proj(h2, w_main, b_main, w_t, b_t, w_g, b_g, w_gt, b_gt,
                                        mconv_w.astype(F32), mconv_b[None, :].astype(F32), seq)
    hm = _mlstm(proj, proj_t, gcol, grow, mnorm_w[:, None].astype(F32), batch, seq)
    hd = _diff_attn(proj, bias_tiles, lq1[None, :].astype(F32), lk1[None, :].astype(F32),
                    lq2[None, :].astype(F32), lk2[None, :].astype(F32), dnorm_w[:, None].astype(F32),
                    lam_init, batch, seq)
    h1 = _merge(hm, hd, proj, h2, w_bm.astype(BF16), w_bd.astype(BF16), w_o.astype(BF16),
                ln1_g[None, :].astype(F32), ln1_b[None, :].astype(F32))
    return _ffn(h1, w_up.astype(BF16), fconv_w.astype(F32), fconv_b[None, :].astype(F32),
                w_down.astype(BF16), ln2_g[None, :].astype(F32), ln2_b[None, :].astype(F32), seq)


def kernel(x, w_in, b_in, mlstm_conv_w, mlstm_conv_b, mlstm_norm_w, lambda_q1, lambda_k1, lambda_q2, lambda_k2,
           diff_norm_w, rel_bias, w_branch_mlstm, w_branch_diff, w_out, ln1_g, ln1_b, w_ffn_up, ffn_conv_w,
           ffn_conv_b, w_ffn_down, ln2_g, ln2_b):
    batch, seq, d_model = x.shape
    assert d_model == D_MODEL and seq % max(PROJ_TM, FFN_TM, ATT_T, MLSTM_L) == 0
    bias_tiles = _rel_bias_tiles(rel_bias)
    h2 = x.reshape(batch * seq, d_model)
    for l in range(w_in.shape[0]):
        h2 = _layer(h2, batch, seq, l, w_in[l], b_in[l], mlstm_conv_w[l], mlstm_conv_b[l], mlstm_norm_w[l],
                    lambda_q1[l], lambda_k1[l], lambda_q2[l], lambda_k2[l], diff_norm_w[l], bias_tiles,
                    w_branch_mlstm[l], w_branch_diff[l], w_out[l], ln1_g[l], ln1_b[l], w_ffn_up[l],
                    ffn_conv_w[l], ffn_conv_b[l], w_ffn_down[l], ln2_g[l], ln2_b[l])
    return h2.reshape(batch, seq, d_model).astype(x.dtype)
```

```python
import functools
import math

import jax
import jax.numpy as jnp
from jax import lax
from jax.experimental import pallas as pl
from jax.experimental.pallas import tpu as pltpu

F32 = jnp.float32
BF16 = jnp.bfloat16

D_MODEL = 1024
MLSTM_HEADS = 4
MLSTM_HEAD_DIM = 256
MLSTM_WIDTH = MLSTM_HEADS * MLSTM_HEAD_DIM
MLSTM_CONV = 4
DIFF_HEADS = 8
DIFF_HEAD_DIM = 64
DIFF_V_DIM = 2 * DIFF_HEAD_DIM
DIFF_WIDTH = DIFF_HEADS * DIFF_V_DIM
REL_BUCKETS = 32
REL_MAX_DIST = 128
D_FF = 2816
FFN_CONV = 3
DEPTH = 1
DEEPNORM_ALPHA = (2.0 * DEPTH) ** 0.25
LN_EPS = 1e-5
LOG2E = math.log2(math.e)

COL_MQ = 0
COL_MK = 1024
COL_DQ = 2048
COL_DK = 3072
COL_DV = 4096
COL_GM = 5120
COL_GD = 6144
N_MAIN = 7168
ROW_MV = 0
ROW_MO = 1024
N_TRANS = 2048
GATE_PAD = 128
GATE_ROWS = 16
MLSTM_AUG = 16

HALO = 8
PROJ_TM = 512
PROJ_TN = 512
MLSTM_L = 256
ATT_T = 512
ATT_SUM_ROWS = 16
MERGE_TM = 512
FFN_TM = 512
FFN_TC = 256
VMEM_LIMIT = 56 * 1024 * 1024


def _sigmoid(v):
    return 1.0 / (1.0 + jnp.exp(-v))


def _log_sigmoid(v):
    return jnp.minimum(v, 0.0) - jnp.log(1.0 + jnp.exp(-jnp.abs(v)))


def _split3(v):
    hi = v.astype(BF16)
    rest = v - hi.astype(F32)
    mid = rest.astype(BF16)
    lo = (rest - mid.astype(F32)).astype(BF16)
    return hi, mid, lo


def _rel_bias_kernel(table_ref, out_ref):
    h = pl.program_id(0)
    T = out_ref.shape[-1]
    max_exact = REL_BUCKETS // 2
    far = table_ref[REL_BUCKETS - 1, h]
    n = lax.broadcasted_iota(jnp.int32, (8, T), 1)
    nf = jnp.maximum(n, 1).astype(F32)
    large = max_exact + (jnp.log(nf / max_exact) / math.log(REL_MAX_DIST / max_exact)
                         * (REL_BUCKETS - max_exact)).astype(jnp.int32)
    large = jnp.minimum(large, REL_BUCKETS - 1)
    bucket = jnp.where(n < max_exact, n, large)
    by_dist = jnp.zeros((8, T), F32)
    for kk in range(REL_BUCKETS):
        by_dist = jnp.where(bucket == kk, table_ref[kk, h], by_dist)
    by_dist = (by_dist - far) * LOG2E
    rolled = pltpu.roll(jnp.broadcast_to(by_dist[0:1, :], (T, T)), 0, 1, stride=1, stride_axis=0)
    kpos = lax.broadcasted_iota(jnp.int32, (T, T), 0)
    qpos = lax.broadcasted_iota(jnp.int32, (T, T), 1)
    out_ref[0, 0] = jnp.where(qpos < kpos, rolled, 0.0)
    out_ref[0, 1] = jnp.where(qpos >= kpos, rolled, -jnp.inf)


def _rel_bias_tiles(rel_bias):
    T = ATT_T
    return pl.pallas_call(
        _rel_bias_kernel,
        grid=(DIFF_HEADS,),
        in_specs=[pl.BlockSpec(memory_space=pltpu.SMEM)],
        out_specs=pl.BlockSpec((1, 2, T, T), lambda h: (h, 0, 0, 0)),
        out_shape=jax.ShapeDtypeStruct((DIFF_HEADS, 2, T, T), F32),
        name="rel_bias_tiles",
    )(rel_bias.astype(F32))


def _in_proj_kernel(tiles_per_seq, x_ref, wa_ref, ba_ref, wb_ref, bb_ref, wt_ref, bt_ref, wg_ref, bg_ref,
                    wgt_ref, bgt_ref, cw_ref, cb_ref, o_ref, ot_ref, gcol_ref, grow_ref, conv_ref, halo_ref):
    i = pl.program_id(0)
    tm = x_ref.shape[0]
    tn = PROJ_TN
    n_conv_cols = 2 * MLSTM_WIDTH
    nt_dims = (((1,), (1,)), ((), ()))

    @pl.when(i % tiles_per_seq == 0)
    def _():
        halo_ref[...] = jnp.zeros_like(halo_ref)

    xb = x_ref[...].astype(BF16)
    gcol_ref[...] = jnp.dot(xb, wg_ref[...], preferred_element_type=F32) + bg_ref[...]
    grow_ref[...] = lax.dot_general(wgt_ref[...], xb, nt_dims, preferred_element_type=F32) + bgt_ref[...]

    for r0 in range(0, N_TRANS, tn):
        rows = slice(r0, r0 + tn)
        acc_t = lax.dot_general(wt_ref[rows, :], xb, nt_dims, preferred_element_type=F32) + bt_ref[rows, :]
        if r0 >= ROW_MO:
            acc_t = _sigmoid(acc_t)
        ot_ref[rows, :] = acc_t.astype(BF16)

    for c0 in range(0, N_MAIN, tn):
        cols = slice(c0, c0 + tn)
        if c0 < n_conv_cols:
            acc = jnp.dot(xb, wa_ref[:, cols], preferred_element_type=F32) + ba_ref[:, cols]
        else:
            wcols = slice(c0 - n_conv_cols, c0 - n_conv_cols + tn)
            acc = jnp.dot(xb, wb_ref[:, wcols], preferred_element_type=F32) + bb_ref[:, wcols]
        if c0 < n_conv_cols:
            conv_ref[0:HALO, cols] = halo_ref[:, cols]
            conv_ref[HALO:HALO + tm, cols] = acc
            y = cb_ref[:, cols] + cw_ref[MLSTM_CONV - 1:MLSTM_CONV, cols] * acc
            for kk in range(MLSTM_CONV - 1):
                off = HALO - (MLSTM_CONV - 1) + kk
                y = y + cw_ref[kk:kk + 1, cols] * conv_ref[off:off + tm, cols]
            halo_ref[:, cols] = acc[tm - HALO:tm, :]
            y = y * _sigmoid(y)
            if c0 >= COL_MK:
                y = y * (MLSTM_HEAD_DIM ** -0.5)
            o_ref[:, cols] = y.astype(BF16)
        elif c0 >= COL_GM:
            o_ref[:, cols] = _sigmoid(acc).astype(BF16)
        elif COL_DQ <= c0 < COL_DK:
            o_ref[:, cols] = (acc * (DIFF_HEAD_DIM ** -0.5 * LOG2E)).astype(BF16)
        else:
            o_ref[:, cols] = acc.astype(BF16)


def _in_proj(x2, w_a, b_a, w_b, b_b, w_t, b_t, w_g, b_g, w_gt, b_gt, conv_w, conv_b, seq):
    T = x2.shape[0]
    tm = PROJ_TM
    n_conv_cols = 2 * MLSTM_WIDTH
    const = lambda shape: pl.BlockSpec(shape, lambda i: (0, 0), pipeline_mode=pl.Buffered(1))
    return pl.pallas_call(
        functools.partial(_in_proj_kernel, seq // tm),
        grid=(T // tm,),
        in_specs=[
            pl.BlockSpec((tm, D_MODEL), lambda i: (i, 0)),
            const((D_MODEL, n_conv_cols)),
            const((1, n_conv_cols)),
            const((D_MODEL, N_MAIN - n_conv_cols)),
            const((1, N_MAIN - n_conv_cols)),
            const((N_TRANS, D_MODEL)),
            const((N_TRANS, 1)),
            const((D_MODEL, GATE_PAD)),
            const((1, GATE_PAD)),
            const((GATE_ROWS, D_MODEL)),
            const((GATE_ROWS, 1)),
            const((MLSTM_CONV, n_conv_cols)),
            const((1, n_conv_cols)),
        ],
        out_specs=[
            pl.BlockSpec((tm, N_MAIN), lambda i: (i, 0)),
            pl.BlockSpec((N_TRANS, tm), lambda i: (0, i)),
            pl.BlockSpec((tm, GATE_PAD), lambda i: (i, 0)),
            pl.BlockSpec((GATE_ROWS, tm), lambda i: (0, i)),
        ],
        out_shape=[
            jax.ShapeDtypeStruct((T, N_MAIN), BF16),
            jax.ShapeDtypeStruct((N_TRANS, T), BF16),
            jax.ShapeDtypeStruct((T, GATE_PAD), F32),
            jax.ShapeDtypeStruct((GATE_ROWS, T), F32),
        ],
        scratch_shapes=[
            pltpu.VMEM((HALO + tm, n_conv_cols), F32),
            pltpu.VMEM((HALO, n_conv_cols), F32),
        ],
        compiler_params=pltpu.CompilerParams(
            dimension_semantics=("arbitrary",), vmem_limit_bytes=VMEM_LIMIT),
        name="in_proj",
    )(x2, w_a, b_a, w_b, b_b, w_t, b_t, w_g, b_g, w_gt, b_gt, conv_w, conv_b)


def _mlstm_kernel(q_ref, k_ref, vt_ref, ot_ref, gcol_ref, grow_ref, nw_ref, out_ref, ct_ref, m_ref, nwb_ref):
    c = pl.program_id(1)
    L = q_ref.shape[0]
    d = MLSTM_HEAD_DIM
    H = MLSTM_HEADS
    nt_dims = (((1,), (1,)), ((), ()))

    @pl.when(c == 0)
    def _():
        ct_ref[...] = jnp.zeros_like(ct_ref)
        m_ref[...] = jnp.zeros_like(m_ref)
        nwb_ref[...] = jnp.broadcast_to(nw_ref[...], nwb_ref.shape)

    key = lax.broadcasted_iota(jnp.int32, (L, L), 0)
    qry = lax.broadcasted_iota(jnp.int32, (L, L), 1)
    causal_t = key <= qry
    tri = jnp.where(key >= qry, 1.0, 0.0).astype(BF16)

    gcol = gcol_ref[...]
    grow = grow_ref[...]
    bc3 = jnp.dot(tri, jnp.concatenate(_split3(_log_sigmoid(gcol)), axis=1), preferred_element_type=F32)
    bcol = bc3[:, 0:GATE_PAD] + bc3[:, GATE_PAD:2 * GATE_PAD] + bc3[:, 2 * GATE_PAD:3 * GATE_PAD]
    br3 = lax.dot_general(jnp.concatenate(_split3(_log_sigmoid(grow)), axis=0), tri, nt_dims,
                          preferred_element_type=F32)
    brow = br3[0:GATE_ROWS] + br3[GATE_ROWS:2 * GATE_ROWS] + br3[2 * GATE_ROWS:3 * GATE_ROWS]
    gdiff = gcol - pltpu.roll(bcol, shift=GATE_PAD - H, axis=1)
    ones = jnp.ones((MLSTM_AUG, L), BF16)

    for h in range(H):
        cols = slice(h * d, (h + 1) * d)
        q = q_ref[:, cols]
        k = k_ref[:, cols]
        v_aug = jnp.concatenate([vt_ref[cols, :], ones], axis=0)
        b_r = brow[H + h:H + h + 1, :]
        i_r = grow[h:h + 1, :]
        g_c = gdiff[:, h:h + 1]
        m_prev = m_ref[h]
        ct = ct_ref[h]

        dmat = jnp.where(causal_t, b_r + g_c, -jnp.inf)
        a = b_r + m_prev
        m_row = jnp.maximum(a, jnp.max(dmat, axis=0, keepdims=True))
        w = jnp.exp(dmat - m_row)
        inter = jnp.exp(a - m_row)
        kq = lax.dot_general(k, q, nt_dims, preferred_element_type=F32)
        sqk = (kq * w).astype(BF16)
        nd = inter * lax.dot_general(ct.astype(BF16), q, nt_dims, preferred_element_type=F32) \
            + jnp.dot(v_aug, sqk, preferred_element_type=F32)
        rinv = 1.0 / jnp.maximum(jnp.abs(nd[d:d + 1, :]), jnp.exp(-m_row))
        hid = nd[0:d, :] * rinv * ot_ref[cols, :].astype(F32)
        mu = jnp.mean(hid, axis=0, keepdims=True)
        cen = hid - mu
        var = jnp.mean(cen * cen, axis=0, keepdims=True)
        out_ref[:, cols] = (cen * lax.rsqrt(var + LN_EPS) * nwb_ref[cols, :]).T.astype(BF16)

        b_last = b_r[:, L - 1:L]
        g_r = b_last - b_r + i_r
        m_new = jnp.maximum(b_last + m_prev, jnp.max(g_r, axis=-1, keepdims=True))
        decay = jnp.exp(b_last + m_prev - m_new)
        ws = jnp.exp(g_r - m_new)
        vw = (v_aug.astype(F32) * ws).astype(BF16)
        ct_ref[h] = decay * ct + jnp.dot(vw, k, preferred_element_type=F32)
        m_ref[h] = m_new


def _mlstm(proj, proj_t, gcol, grow, norm_w, batch, seq):
    L = MLSTM_L
    nc = seq // L
    W = MLSTM_WIDTH
    row = lambda b, c: b * nc + c
    return pl.pallas_call(
        _mlstm_kernel,
        grid=(batch, nc),
        in_specs=[
            pl.BlockSpec((L, W), lambda b, c: (row(b, c), COL_MQ // W)),
            pl.BlockSpec((L, W), lambda b, c: (row(b, c), COL_MK // W)),
            pl.BlockSpec((W, L), lambda b, c: (ROW_MV // W, row(b, c))),
            pl.BlockSpec((W, L), lambda b, c: (ROW_MO // W, row(b, c))),
            pl.BlockSpec((L, GATE_PAD), lambda b, c: (row(b, c), 0)),
            pl.BlockSpec((GATE_ROWS, L), lambda b, c: (0, row(b, c))),
            pl.BlockSpec((W, 1), lambda b, c: (0, 0)),
        ],
        out_specs=pl.BlockSpec((L, W), lambda b, c: (row(b, c), 0)),
        out_shape=jax.ShapeDtypeStruct((batch * seq, W), BF16),
        scratch_shapes=[
            pltpu.VMEM((MLSTM_HEADS, MLSTM_HEAD_DIM + MLSTM_AUG, MLSTM_HEAD_DIM), F32),
            pltpu.VMEM((MLSTM_HEADS, 1, 1), F32),
            pltpu.VMEM((W, L), F32),
        ],
        compiler_params=pltpu.CompilerParams(
            dimension_semantics=("arbitrary", "arbitrary"), vmem_limit_bytes=VMEM_LIMIT),
        name="mlstm",
    )(proj, proj, proj_t, proj_t, gcol, grow, norm_w)


def _diff_attn_kernel(lam_init, q_ref, k_ref, v_ref, bias_ref, lq1_ref, lk1_ref, lq2_ref, lk2_ref, nw_ref,
                      out_ref, qs_ref, vt_ref, s_ref, p_ref, acc_ref):
    T = ATT_T
    dh = DIFF_HEAD_DIM
    dv = DIFF_V_DIM
    nq = q_ref.shape[0] // T
    R = 2 * T

    ones = jnp.ones((ATT_SUM_ROWS, T), BF16)
    for jj in range(nq):
        vt_ref[jj, 0:dv, :] = v_ref[jj * T:(jj + 1) * T, :].astype(F32).T.astype(BF16)
        vt_ref[jj, dv:dv + ATT_SUM_ROWS, :] = ones

    lane = lax.broadcasted_iota(jnp.int32, (T, dv), 1)
    for ii in range(nq):
        qt = q_ref[ii * T:(ii + 1) * T, :]
        zero = jnp.zeros_like(qt)
        qs_ref[ii, 0:T, :] = jnp.where(lane < dh, qt, zero)
        qs_ref[ii, T:R, :] = jnp.where(lane >= dh, qt, zero)

    lam = (jnp.exp(jnp.sum(lq1_ref[...] * lk1_ref[...], axis=-1, keepdims=True))
           - jnp.exp(jnp.sum(lq2_ref[...] * lk2_ref[...], axis=-1, keepdims=True)) + lam_init)

    pairs = [(i, j) for i in range(nq) for j in range(i + 1)]

    def scores(n):
        i, j = pairs[n]
        s = lax.dot_general(k_ref[j * T:(j + 1) * T, :], qs_ref[i], (((1,), (1,)), ((), ())),
                            preferred_element_type=F32)
        if j >= i - 1:
            bias = bias_ref[0, j - i + 1]
            s = s + jnp.concatenate([bias, bias], axis=1)
        s_ref[n % 2] = s
        return jnp.max(s, axis=0, keepdims=True)

    def softmax(n, m_tile, m_prev):
        i, j = pairs[n]
        m_new = m_tile if j == 0 else jnp.maximum(m_prev, m_tile)
        p_ref[n % 2] = jnp.exp2(s_ref[n % 2] - m_new).astype(BF16)
        alpha = None if j == 0 else jnp.exp2(m_prev - m_new)
        return m_new, alpha

    def values(n, alpha):
        i, j = pairs[n]
        pv = jnp.dot(vt_ref[j], p_ref[n % 2], preferred_element_type=F32)
        if j == 0:
            acc_ref[...] = pv
        else:
            acc_ref[...] = alpha * acc_ref[...] + pv
        if j == i:
            acc = acc_ref[...]
            out = acc[0:dv, :] / acc[dv:dv + 1, :]
            hd = out[:, 0:T] - lam * out[:, T:R]
            hd = hd * lax.rsqrt(jnp.mean(hd * hd, axis=0, keepdims=True) + LN_EPS) * nw_ref[...]
            out_ref[i * T:(i + 1) * T, :] = (hd * (1.0 - lam_init)).T.astype(BF16)

    m_tiles = {0: scores(0)}
    m_run = None
    alphas = {}
    for n in range(len(pairs)):
        if n + 1 < len(pairs):
            m_tiles[n + 1] = scores(n + 1)
        m_run, alphas[n] = softmax(n, m_tiles.pop(n), m_run)
        if n >= 1:
            values(n - 1, alphas.pop(n - 1))
    values(len(pairs) - 1, alphas.pop(len(pairs) - 1))


def _diff_attn(proj, bias_tiles, lq1, lk1, lq2, lk2, norm_w, lam_init, batch, seq):
    T = ATT_T
    nq = seq // T
    dv = DIFF_V_DIM
    small = lambda shape: pl.BlockSpec(shape, lambda h, b: (0, 0))
    return pl.pallas_call(
        functools.partial(_diff_attn_kernel, lam_init),
        grid=(DIFF_HEADS, batch),
        in_specs=[
            pl.BlockSpec((seq, dv), lambda h, b: (b, COL_DQ // dv + h)),
            pl.BlockSpec((seq, dv), lambda h, b: (b, COL_DK // dv + h)),
            pl.BlockSpec((seq, dv), lambda h, b: (b, COL_DV // dv + h)),
            pl.BlockSpec((1, 2, T, T), lambda h, b: (h, 0, 0, 0)),
            small((1, DIFF_HEAD_DIM)), small((1, DIFF_HEAD_DIM)),
            small((1, DIFF_HEAD_DIM)), small((1, DIFF_HEAD_DIM)),
            small((dv, 1)),
        ],
        out_specs=pl.BlockSpec((seq, dv), lambda h, b: (b, h)),
        out_shape=jax.ShapeDtypeStruct((batch * seq, DIFF_WIDTH), BF16),
        scratch_shapes=[
            pltpu.VMEM((nq, 2 * T, dv), BF16),
            pltpu.VMEM((nq, dv + ATT_SUM_ROWS, T), BF16),
            pltpu.VMEM((2, T, 2 * T), F32),
            pltpu.VMEM((2, T, 2 * T), BF16),
            pltpu.VMEM((dv + ATT_SUM_ROWS, 2 * T), F32),
        ],
        compiler_params=pltpu.CompilerParams(
            dimension_semantics=("arbitrary", "arbitrary"), vmem_limit_bytes=VMEM_LIMIT),
        name="diff_attn",
    )(proj, proj, proj, bias_tiles, lq1, lk1, lq2, lk2, norm_w)


def _layer_norm(y, g, b):
    mu = jnp.mean(y, axis=-1, keepdims=True)
    cen = y - mu
    var = jnp.mean(cen * cen, axis=-1, keepdims=True)
    return cen * lax.rsqrt(var + LN_EPS) * g + b


def _merge_kernel(hm_ref, hd_ref, gm_ref, gd_ref, x_ref, wbm_ref, wbd_ref, wo_ref, g_ref, b_ref, out_ref):
    pm = jnp.dot(hm_ref[...], wbm_ref[...], preferred_element_type=F32)
    pd = jnp.dot(hd_ref[...], wbd_ref[...], preferred_element_type=F32)
    merged = gm_ref[...].astype(F32) * pm + gd_ref[...].astype(F32) * pd
    mix = jnp.dot(merged.astype(BF16), wo_ref[...], preferred_element_type=F32)
    out_ref[...] = _layer_norm(DEEPNORM_ALPHA * x_ref[...] + mix, g_ref[...], b_ref[...])


def _merge(hm, hd, proj, x2, w_bm, w_bd, w_o, ln_g, ln_b):
    T = x2.shape[0]
    tm = MERGE_TM
    D = D_MODEL
    rows = lambda col: pl.BlockSpec((tm, D), lambda i: (i, col))
    full = lambda shape: pl.BlockSpec(shape, lambda i: (0, 0))
    return pl.pallas_call(
        _merge_kernel,
        grid=(T // tm,),
        in_specs=[rows(0), rows(0), rows(COL_GM // D), rows(COL_GD // D), rows(0),
                  full((MLSTM_WIDTH, D)), full((DIFF_WIDTH, D)), full((D, D)), full((1, D)), full((1, D))],
        out_specs=rows(0),
        out_shape=jax.ShapeDtypeStruct((T, D), F32),
        compiler_params=pltpu.CompilerParams(
            dimension_semantics=("arbitrary",), vmem_limit_bytes=VMEM_LIMIT),
        name="merge_ln",
    )(hm, hd, proj, proj, x2, w_bm, w_bd, w_o, ln_g, ln_b)


def _ffn_kernel(tiles_per_seq, h_ref, wu_ref, cw_ref, cb_ref, wd_ref, g_ref, b_ref, out_ref,
                conv_ref, halo_ref, act_ref):
    i = pl.program_id(0)
    tm = h_ref.shape[0]
    tc = FFN_TC

    @pl.when(i % tiles_per_seq == 0)
    def _():
        halo_ref[...] = jnp.zeros_like(halo_ref)

    h = h_ref[...]
    hb = h.astype(BF16)
    for c0 in range(0, D_FF, tc):
        cols = slice(c0, c0 + tc)
        a = jnp.dot(hb, wu_ref[:, cols], preferred_element_type=F32)
        gate = jnp.dot(hb, wu_ref[:, D_FF + c0:D_FF + c0 + tc], preferred_element_type=F32)
        conv_ref[0:HALO, cols] = halo_ref[:, cols]
        conv_ref[HALO:HALO + tm, cols] = a
        y = cb_ref[:, cols] + cw_ref[FFN_CONV - 1:FFN_CONV, cols] * a
        for kk in range(FFN_CONV - 1):
            off = HALO - (FFN_CONV - 1) + kk
            y = y + cw_ref[kk:kk + 1, cols] * conv_ref[off:off + tm, cols]
        halo_ref[:, cols] = a[tm - HALO:tm, :]
        act_ref[:, cols] = (y * _sigmoid(y) * gate).astype(BF16)

    ffn = jnp.dot(act_ref[...], wd_ref[...], preferred_element_type=F32)
    out_ref[...] = _layer_norm(DEEPNORM_ALPHA * h + ffn, g_ref[...], b_ref[...])


def _ffn(h1, w_up, conv_w, conv_b, w_down, ln_g, ln_b, seq):
    T = h1.shape[0]
    tm = FFN_TM
    D = D_MODEL
    const = lambda shape: pl.BlockSpec(shape, lambda i: (0, 0), pipeline_mode=pl.Buffered(1))
    return pl.pallas_call(
        functools.partial(_ffn_kernel, seq // tm),
        grid=(T // tm,),
        in_specs=[
            pl.BlockSpec((tm, D), lambda i: (i, 0)),
            const((D, 2 * D_FF)),
            const((FFN_CONV, D_FF)),
            const((1, D_FF)),
            const((D_FF, D)),
            const((1, D)),
            const((1, D)),
        ],
        out_specs=pl.BlockSpec((tm, D), lambda i: (i, 0)),
        out_shape=jax.ShapeDtypeStruct((T, D), F32),
        scratch_shapes=[
            pltpu.VMEM((HALO + tm, D_FF), F32),
            pltpu.VMEM((HALO, D_FF), F32),
            pltpu.VMEM((tm, D_FF), BF16),
        ],
        compiler_params=pltpu.CompilerParams(
            dimension_semantics=("arbitrary",), vmem_limit_bytes=VMEM_LIMIT),
        name="conv_ffn",
    )(h1, w_up, conv_w, conv_b, w_down, ln_g, ln_b)


def _layer(h2, batch, seq, l, w_in, b_in, mconv_w, mconv_b, mnorm_w, lq1, lk1, lq2, lk2, dnorm_w, bias_tiles,
           w_bm, w_bd, w_o, ln1_g, ln1_b, w_up, fconv_w, fconv_b, w_down, ln2_g, ln2_b):
    lam_init = 0.8 - 0.6 * math.exp(-0.3 * l)
    n_m = 4 * MLSTM_WIDTH
    n_gate = 2 * MLSTM_HEADS
    n_qk = 2 * MLSTM_WIDTH
    w_a = w_in[:, :n_qk].astype(BF16)
    b_a = b_in[:n_qk][None, :].astype(F32)
    w_b = w_in[:, n_m + n_gate:].astype(BF16)
    b_b = b_in[n_m + n_gate:][None, :].astype(F32)
    w_t = w_in[:, n_qk:n_m].T.astype(BF16)
    b_t = b_in[n_qk:n_m][:, None].astype(F32)
    w_gate = w_in[:, n_m:n_m + n_gate]
    b_gate = b_in[n_m:n_m + n_gate].astype(F32)
    w_g = jnp.pad(w_gate, ((0, 0), (0, GATE_PAD - n_gate))).astype(BF16)
    b_g = jnp.pad(b_gate, (0, GATE_PAD - n_gate))[None, :]
    w_gt = jnp.pad(w_gate.T, ((0, GATE_ROWS - n_gate), (0, 0))).astype(BF16)
    b_gt = jnp.pad(b_gate, (0, GATE_ROWS - n_gate))[:, None]

    proj, proj_t, gcol, grow = _in_proj(h2, w_a, b_a, w_b, b_b, w_t, b_t, w_g, b_g, w_gt, b_gt,
                                        mconv_w.astype(F32), mconv_b[None, :].astype(F32), seq)
    hm = _mlstm(proj, proj_t, gcol, grow, mnorm_w[:, None].astype(F32), batch, seq)
    hd = _diff_attn(proj, bias_tiles, lq1[None, :].astype(F32), lk1[None, :].astype(F32),
                    lq2[None, :].astype(F32), lk2[None, :].astype(F32), dnorm_w[:, None].astype(F32),
                    lam_init, batch, seq)
    h1 = _merge(hm, hd, proj, h2, w_bm.astype(BF16), w_bd.astype(BF16), w_o.astype(BF16),
                ln1_g[None, :].astype(F32), ln1_b[None, :].astype(F32))
    return _ffn(h1, w_up.astype(BF16), fconv_w.astype(F32), fconv_b[None, :].astype(F32),
                w_down.astype(BF16), ln2_g[None, :].astype(F32), ln2_b[None, :].astype(F32), seq)


def kernel(x, w_in, b_in, mlstm_conv_w, mlstm_conv_b, mlstm_norm_w, lambda_q1, lambda_k1, lambda_q2, lambda_k2,
           diff_norm_w, rel_bias, w_branch_mlstm, w_branch_diff, w_out, ln1_g, ln1_b, w_ffn_up, ffn_conv_w,
           ffn_conv_b, w_ffn_down, ln2_g, ln2_b):
    batch, seq, d_model = x.shape
    assert d_model == D_MODEL and seq % max(PROJ_TM, FFN_TM, ATT_T, MLSTM_L) == 0
    bias_tiles = _rel_bias_tiles(rel_bias)
    h2 = x.reshape(batch * seq, d_model)
    for l in range(w_in.shape[0]):
        h2 = _layer(h2, batch, seq, l, w_in[l], b_in[l], mlstm_conv_w[l], mlstm_conv_b[l], mlstm_norm_w[l],
                    lambda_q1[l], lambda_k1[l], lambda_q2[l], lambda_k2[l], diff_norm_w[l], bias_tiles,
                    w_branch_mlstm[l], w_branch_diff[l], w_out[l], ln1_g[l], ln1_b[l], w_ffn_up[l],
                    ffn_conv_w[l], ffn_conv_b[l], w_ffn_down[l], ln2_g[l], ln2_b[l])
    return h2.reshape(batch, seq, d_model).astype(x.dtype)
```

```python
import functools
import math

import jax
import jax.numpy as jnp
from jax import lax
from jax.experimental import pallas as pl
from jax.experimental.pallas import tpu as pltpu

F32 = jnp.float32
BF16 = jnp.bfloat16

D_MODEL = 1024
MLSTM_HEADS = 4
MLSTM_HEAD_DIM = 256
MLSTM_WIDTH = MLSTM_HEADS * MLSTM_HEAD_DIM
MLSTM_CONV = 4
DIFF_HEADS = 8
DIFF_HEAD_DIM = 64
DIFF_V_DIM = 2 * DIFF_HEAD_DIM
DIFF_WIDTH = DIFF_HEADS * DIFF_V_DIM
REL_BUCKETS = 32
REL_MAX_DIST = 128
D_FF = 2816
FFN_CONV = 3
DEPTH = 1
DEEPNORM_ALPHA = (2.0 * DEPTH) ** 0.25
LN_EPS = 1e-5
LOG2E = math.log2(math.e)

COL_MQ = 0
COL_MK = 1024
COL_DQ = 2048
COL_DK = 3072
COL_GM = 4096
COL_GD = 5120
N_MAIN = 6144
ROW_MV = 0
ROW_MO = 1024
ROW_DV = 2048
N_TRANS = 3072
TAIL_DQ = 0
TAIL_DK = 1024
TAIL_DV = 2048
TAIL_GM = 3072
N_TAIL = 5120
GATE_PAD = 128
GATE_ROWS = 16
MLSTM_AUG = 16

HALO = 8
PROJ_TM = 512
PROJ_TN = 512
MLSTM_L = 256
ATT_T = 256
ATT_SUM_ROWS = 16
MERGE_ROWS = 256
FFN_TM = 512
FFN_TC = 256
VMEM_LIMIT = 56 * 1024 * 1024


def _sigmoid(v):
    return 1.0 / (1.0 + jnp.exp(-v))


def _log_sigmoid(v):
    return jnp.minimum(v, 0.0) - jnp.log(1.0 + jnp.exp(-jnp.abs(v)))


def _split3(v):
    hi = v.astype(BF16)
    rest = v - hi.astype(F32)
    mid = rest.astype(BF16)
    lo = (rest - mid.astype(F32)).astype(BF16)
    return hi, mid, lo


def _rel_bias_kernel(table_ref, out_ref):
    h = pl.program_id(0)
    T = out_ref.shape[-1]
    max_exact = REL_BUCKETS // 2
    far = table_ref[REL_BUCKETS - 1, h]
    n = lax.broadcasted_iota(jnp.int32, (8, T), 1)
    nf = jnp.maximum(n, 1).astype(F32)
    large = max_exact + (jnp.log(nf / max_exact) / math.log(REL_MAX_DIST / max_exact)
                         * (REL_BUCKETS - max_exact)).astype(jnp.int32)
    large = jnp.minimum(large, REL_BUCKETS - 1)
    bucket = jnp.where(n < max_exact, n, large)
    by_dist = jnp.zeros((8, T), F32)
    for kk in range(REL_BUCKETS):
        by_dist = jnp.where(bucket == kk, table_ref[kk, h], by_dist)
    by_dist = (by_dist - far) * LOG2E
    rolled = pltpu.roll(jnp.broadcast_to(by_dist[0:1, :], (T, T)), 0, 1, stride=1, stride_axis=0)
    kpos = lax.broadcasted_iota(jnp.int32, (T, T), 0)
    qpos = lax.broadcasted_iota(jnp.int32, (T, T), 1)
    out_ref[0, 0] = jnp.where(qpos < kpos, rolled, 0.0)
    out_ref[0, 1] = jnp.where(qpos >= kpos, rolled, -jnp.inf)


def _rel_bias_tiles(rel_bias):
    T = ATT_T
    return pl.pallas_call(
        _rel_bias_kernel,
        grid=(DIFF_HEADS,),
        in_specs=[pl.BlockSpec(memory_space=pltpu.SMEM)],
        out_specs=pl.BlockSpec((1, 2, T, T), lambda h: (h, 0, 0, 0)),
        out_shape=jax.ShapeDtypeStruct((DIFF_HEADS, 2, T, T), F32),
        name="rel_bias_tiles",
    )(rel_bias.astype(F32))


def _in_proj_kernel(tiles_per_seq, x_ref, wa_ref, ba_ref, wb_ref, bb_ref, wt_ref, bt_ref, wg_ref, bg_ref,
                    bgt_ref, cw_ref, cb_ref, o_ref, ot_ref, gcol_ref, grow_ref, halo_ref):
    i = pl.program_id(0)
    tm = x_ref.shape[0]
    tn = PROJ_TN
    n_conv_cols = 2 * MLSTM_WIDTH
    nt_dims = (((1,), (1,)), ((), ()))

    @pl.when(i % tiles_per_seq == 0)
    def _():
        halo_ref[...] = jnp.zeros_like(halo_ref)

    xb = x_ref[...].astype(BF16)
    gcol = lax.dot_general(xb, wg_ref[...], nt_dims, preferred_element_type=F32) + bg_ref[...]
    grow = lax.dot_general(wg_ref[0:GATE_ROWS, :], xb, nt_dims,
                           preferred_element_type=F32) + bgt_ref[...]
    L = MLSTM_L
    H = MLSTM_HEADS
    tri = jnp.where(lax.broadcasted_iota(jnp.int32, (L, L), 0) >= lax.broadcasted_iota(jnp.int32, (L, L), 1),
                    1.0, 0.0).astype(BF16)
    gate_row = lax.broadcasted_iota(jnp.int32, (GATE_ROWS, L), 0)
    for r0 in range(0, tm, L):
        g_c = gcol[r0:r0 + L, :]
        g_r = grow[:, r0:r0 + L]
        bc3 = jnp.dot(tri, jnp.concatenate(_split3(_log_sigmoid(g_c)), axis=1), preferred_element_type=F32)
        bcol = bc3[:, 0:GATE_PAD] + bc3[:, GATE_PAD:2 * GATE_PAD] + bc3[:, 2 * GATE_PAD:3 * GATE_PAD]
        br3 = lax.dot_general(jnp.concatenate(_split3(_log_sigmoid(g_r)), axis=0), tri, nt_dims,
                              preferred_element_type=F32)
        brow = br3[0:GATE_ROWS] + br3[GATE_ROWS:2 * GATE_ROWS] + br3[2 * GATE_ROWS:3 * GATE_ROWS]
        gcol_ref[r0:r0 + L, :] = g_c - pltpu.roll(bcol, GATE_PAD - H, 1)
        grow_ref[:, r0:r0 + L] = jnp.where(gate_row < H, g_r, brow)

    for r0 in range(0, N_TRANS, tn):
        rows = slice(r0, r0 + tn)
        if r0 < ROW_DV:
            w_rows = wt_ref[rows, :]
        else:
            w_rows = wb_ref[TAIL_DV + r0 - ROW_DV:TAIL_DV + r0 - ROW_DV + tn, :]
        acc_t = lax.dot_general(w_rows, xb, nt_dims, preferred_element_type=F32) + bt_ref[rows, :]
        if ROW_MO <= r0 < ROW_DV:
            acc_t = _sigmoid(acc_t)
        ot_ref[rows, :] = acc_t.astype(BF16)

    sublane3 = lax.broadcasted_iota(jnp.int32, (1, HALO, tn), 1)
    for c0 in range(0, N_MAIN, tn):
        cols = slice(c0, c0 + tn)
        if c0 < n_conv_cols:
            acc = lax.dot_general(xb, wa_ref[cols, :], nt_dims, preferred_element_type=F32) + ba_ref[:, cols]
        else:
            t0 = c0 - COL_DQ + TAIL_DQ if c0 < COL_GM else c0 - COL_GM + TAIL_GM
            wcols = slice(t0, t0 + tn)
            acc = lax.dot_general(xb, wb_ref[wcols, :], nt_dims, preferred_element_type=F32) + bb_ref[:, wcols]
        if c0 < n_conv_cols:
            groups = tm // HALO
            acc3 = acc.reshape(groups, HALO, tn)
            prev = halo_ref[:, cols].reshape(1, HALO, tn)
            y = cb_ref[:, cols] + cw_ref[MLSTM_CONV - 1:MLSTM_CONV, cols] * acc
            for kk in range(MLSTM_CONV - 1):
                shift = MLSTM_CONV - 1 - kk
                rot = pltpu.roll(jnp.concatenate([prev, acc3], axis=0), shift, 1)
                moved = jnp.where(sublane3 < shift, rot[0:groups], rot[1:groups + 1])
                y = y + cw_ref[kk:kk + 1, cols] * moved.reshape(tm, tn)
            halo_ref[:, cols] = acc[tm - HALO:tm, :]
            y = y * _sigmoid(y)
            if c0 >= COL_MK:
                y = y * (MLSTM_HEAD_DIM ** -0.5)
            o_ref[:, cols] = y.astype(BF16)
        elif c0 >= COL_GM:
            o_ref[:, cols] = _sigmoid(acc).astype(BF16)
        elif COL_DQ <= c0 < COL_DK:
            o_ref[:, cols] = (acc * (DIFF_HEAD_DIM ** -0.5 * LOG2E)).astype(BF16)
        else:
            o_ref[:, cols] = acc.astype(BF16)


def _const_spec(shape):
    return pl.BlockSpec(shape, lambda i: (0, 0), pipeline_mode=pl.Buffered(1))


def _in_proj(x2, w_all, b_a, w_b, b_b, b_t, b_g, b_gt, conv_w, conv_b, seq):
    T = x2.shape[0]
    tm = PROJ_TM
    n_conv_cols = 2 * MLSTM_WIDTH
    n_vo = ROW_DV
    assert n_vo == n_conv_cols and (n_conv_cols + n_vo) % GATE_PAD == 0
    const = _const_spec
    row_block = lambda shape, idx: pl.BlockSpec(shape, lambda i: (idx, 0), pipeline_mode=pl.Buffered(1))
    return pl.pallas_call(
        functools.partial(_in_proj_kernel, seq // tm),
        grid=(T // tm,),
        in_specs=[
            pl.BlockSpec((tm, D_MODEL), lambda i: (i, 0)),
            row_block((n_conv_cols, D_MODEL), 0),
            const((1, n_conv_cols)),
            const((N_TAIL, D_MODEL)),
            const((1, N_TAIL)),
            row_block((n_vo, D_MODEL), 1),
            const((N_TRANS, 1)),
            row_block((GATE_PAD, D_MODEL), (n_conv_cols + n_vo) // GATE_PAD),
            const((1, GATE_PAD)),
            const((GATE_ROWS, 1)),
            const((MLSTM_CONV, n_conv_cols)),
            const((1, n_conv_cols)),
        ],
        out_specs=[
            pl.BlockSpec((tm, N_MAIN), lambda i: (i, 0)),
            pl.BlockSpec((N_TRANS, tm), lambda i: (0, i)),
            pl.BlockSpec((tm, GATE_PAD), lambda i: (i, 0)),
            pl.BlockSpec((GATE_ROWS, tm), lambda i: (0, i)),
        ],
        out_shape=[
            jax.ShapeDtypeStruct((T, N_MAIN), BF16),
            jax.ShapeDtypeStruct((N_TRANS, T), BF16),
            jax.ShapeDtypeStruct((T, GATE_PAD), F32),
            jax.ShapeDtypeStruct((GATE_ROWS, T), F32),
        ],
        scratch_shapes=[
            pltpu.VMEM((HALO, n_conv_cols), F32),
        ],
        compiler_params=pltpu.CompilerParams(
            dimension_semantics=("arbitrary",), vmem_limit_bytes=VMEM_LIMIT),
        name="in_proj",
    )(x2, w_all, b_a, w_b, b_b, w_all, b_t, w_all, b_g, b_gt, conv_w, conv_b)


def _mlstm_chunk(c, q_ref, k_ref, vt_ref, ot_ref, gcol_ref, grow_ref, nw_ref, out_ref, ct_ref, m_ref, nwb_ref):
    L = q_ref.shape[0]
    d = MLSTM_HEAD_DIM
    H = MLSTM_HEADS
    nt_dims = (((1,), (1,)), ((), ()))

    @pl.when(c == 0)
    def _():
        ct_ref[...] = jnp.zeros_like(ct_ref)
        m_ref[...] = jnp.zeros_like(m_ref)
        nwb_ref[...] = jnp.broadcast_to(nw_ref[...], nwb_ref.shape)

    key = lax.broadcasted_iota(jnp.int32, (L, L), 0)
    qry = lax.broadcasted_iota(jnp.int32, (L, L), 1)
    causal_t = key <= qry

    gdiff = gcol_ref[...]
    grow = grow_ref[...]
    ones = jnp.ones((MLSTM_AUG, L), BF16)

    def head(h):
        cols = slice(h * d, (h + 1) * d)
        q = q_ref[:, cols]
        k = k_ref[:, cols]
        v_aug = jnp.concatenate([vt_ref[cols, :], ones], axis=0)
        b_r = grow[H + h:H + h + 1, :]
        i_r = grow[h:h + 1, :]
        g_c = gdiff[:, h:h + 1]
        m_prev = m_ref[h]
        ct = ct_ref[h]

        dmat = jnp.where(causal_t, b_r + g_c, -jnp.inf)
        a = b_r + m_prev
        m_row = jnp.maximum(a, jnp.max(dmat, axis=0, keepdims=True))
        w = jnp.exp(dmat - m_row)
        inter = jnp.exp(a - m_row)
        kq = lax.dot_general(k, q, nt_dims, preferred_element_type=F32)
        sqk = (kq * w).astype(BF16)
        nd = inter * lax.dot_general(ct.astype(BF16), q, nt_dims, preferred_element_type=F32) \
            + jnp.dot(v_aug, sqk, preferred_element_type=F32)
        rinv = 1.0 / jnp.maximum(jnp.abs(nd[d:d + 1, :]), jnp.exp(-m_row))
        hid = nd[0:d, :] * rinv * ot_ref[cols, :].astype(F32)
        mu = jnp.mean(hid, axis=0, keepdims=True)
        cen = hid - mu
        var = jnp.mean(cen * cen, axis=0, keepdims=True)
        out_ref[:, cols] = (cen * lax.rsqrt(var + LN_EPS) * nwb_ref[cols, :]).T.astype(BF16)

        b_last = b_r[:, L - 1:L]
        g_r = b_last - b_r + i_r
        m_new = jnp.maximum(b_last + m_prev, jnp.max(g_r, axis=-1, keepdims=True))
        decay = jnp.exp(b_last + m_prev - m_new)
        ws = jnp.exp(g_r - m_new)
        vw = (v_aug.astype(F32) * ws).astype(BF16)
        ct_ref[h] = decay * ct + jnp.dot(vw, k, preferred_element_type=F32)
        m_ref[h] = m_new

    return [functools.partial(head, h) for h in range(H)]


def _diff_attn_head(lam_init, interleaved, q_ref, k_ref, v_ref, bias_ref, lq1_ref, lk1_ref, lq2_ref, lk2_ref,
                    nw_ref, out_ref, qs_ref, vt_ref, s_ref, p_ref, acc_ref):
    T = ATT_T
    dh = DIFF_HEAD_DIM
    dv = DIFF_V_DIM
    nq = q_ref.shape[0] // T
    R = 2 * T

    ones = jnp.ones((ATT_SUM_ROWS, T), BF16)
    for jj in range(nq):
        vt_ref[jj, 0:dv, :] = v_ref[:, jj * T:(jj + 1) * T]
        vt_ref[jj, dv:dv + ATT_SUM_ROWS, :] = ones

    lane = lax.broadcasted_iota(jnp.int32, (T, dv), 1)
    for ii in range(nq):
        qt = q_ref[ii * T:(ii + 1) * T, :]
        zero = jnp.zeros_like(qt)
        qs_ref[ii, 0:T, :] = jnp.where(lane < dh, qt, zero)
        qs_ref[ii, T:R, :] = jnp.where(lane >= dh, qt, zero)

    lam = (jnp.exp(jnp.sum(lq1_ref[...] * lk1_ref[...], axis=-1, keepdims=True))
           - jnp.exp(jnp.sum(lq2_ref[...] * lk2_ref[...], axis=-1, keepdims=True)) + lam_init)

    pairs = [(i, j) for i in range(nq) for j in range(i + 1)]

    def scores(n):
        i, j = pairs[n]
        s = lax.dot_general(k_ref[j * T:(j + 1) * T, :], qs_ref[i], (((1,), (1,)), ((), ())),
                            preferred_element_type=F32)
        if j >= i - 1:
            bias = bias_ref[0, j - i + 1]
            s = s + jnp.concatenate([bias, bias], axis=1)
        s_ref[n % 2] = s
        return jnp.max(s, axis=0, keepdims=True)

    def softmax(n, m_tile, m_prev):
        i, j = pairs[n]
        m_new = m_tile if j == 0 else jnp.maximum(m_prev, m_tile)
        p_ref[n % 2] = jnp.exp2(s_ref[n % 2] - m_new).astype(BF16)
        alpha = None if j == 0 else jnp.exp2(m_prev - m_new)
        return m_new, alpha

    def values(n, alpha):
        i, j = pairs[n]
        pv = jnp.dot(vt_ref[j], p_ref[n % 2], preferred_element_type=F32)
        if j == 0:
            acc_ref[...] = pv
        else:
            acc_ref[...] = alpha * acc_ref[...] + pv
        if j == i:
            acc = acc_ref[...]
            out = acc[0:dv, :] / acc[dv:dv + 1, :]
            hd = out[:, 0:T] - lam * out[:, T:R]
            hd = hd * lax.rsqrt(jnp.mean(hd * hd, axis=0, keepdims=True) + LN_EPS) * nw_ref[...]
            out_ref[i * T:(i + 1) * T, :] = (hd * (1.0 - lam_init)).T.astype(BF16)

    spacing = len(pairs) // (len(interleaved) + 1) if interleaved else 0
    m_tiles = {0: scores(0)}
    m_run = None
    alphas = {}
    for n in range(len(pairs)):
        if n + 1 < len(pairs):
            m_tiles[n + 1] = scores(n + 1)
        m_run, alphas[n] = softmax(n, m_tiles.pop(n), m_run)
        if n >= 1:
            values(n - 1, alphas.pop(n - 1))
        if spacing and (n + 1) % spacing == 0 and (n + 1) // spacing <= len(interleaved):
            interleaved[(n + 1) // spacing - 1]()
    values(len(pairs) - 1, alphas.pop(len(pairs) - 1))


def _mixers_kernel(lam_init, chunks_per_seq, n_attn_refs, *refs):
    (q_ref, k_ref, v_ref, bias_ref, lq1_ref, lk1_ref, lq2_ref, lk2_ref, anw_ref) = refs[:n_attn_refs]
    (mq_ref, mk_ref, mvt_ref, mot_ref, gcol_ref, grow_ref, mnw_ref) = refs[n_attn_refs:n_attn_refs + 7]
    hd_ref, hm_ref = refs[n_attn_refs + 7:n_attn_refs + 9]
    (qs_ref, vt_ref, s_ref, p_ref, acc_ref, ct_ref, m_ref, nwb_ref) = refs[n_attn_refs + 9:]
    c = pl.program_id(0) % chunks_per_seq
    heads = _mlstm_chunk(c, mq_ref, mk_ref, mvt_ref, mot_ref, gcol_ref, grow_ref, mnw_ref, hm_ref,
                         ct_ref, m_ref, nwb_ref)
    _diff_attn_head(lam_init, heads, q_ref, k_ref, v_ref, bias_ref, lq1_ref, lk1_ref, lq2_ref, lk2_ref,
                    anw_ref, hd_ref, qs_ref, vt_ref, s_ref, p_ref, acc_ref)


def _mixers(proj, proj_t, gcol, grow, bias_tiles, lq1, lk1, lq2, lk2, dnorm_w, mnorm_w, lam_init, batch, seq):
    T = ATT_T
    nq = seq // T
    dv = DIFF_V_DIM
    L = MLSTM_L
    nc = seq // L
    W = MLSTM_WIDTH
    assert DIFF_HEADS * batch == batch * nc, "attention (head, batch) steps must pair 1:1 with mLSTM chunks"
    small = lambda shape: pl.BlockSpec(shape, lambda n: (0, 0))
    attn_specs = [
        pl.BlockSpec((seq, dv), lambda n: (n % batch, COL_DQ // dv + n // batch)),
        pl.BlockSpec((seq, dv), lambda n: (n % batch, COL_DK // dv + n // batch)),
        pl.BlockSpec((dv, seq), lambda n: (ROW_DV // dv + n // batch, n % batch)),
        pl.BlockSpec((1, 2, T, T), lambda n: (n // batch, 0, 0, 0)),
        small((1, DIFF_HEAD_DIM)), small((1, DIFF_HEAD_DIM)),
        small((1, DIFF_HEAD_DIM)), small((1, DIFF_HEAD_DIM)),
        small((dv, 1)),
    ]
    mlstm_specs = [
        pl.BlockSpec((L, W), lambda n: (n, COL_MQ // W)),
        pl.BlockSpec((L, W), lambda n: (n, COL_MK // W)),
        pl.BlockSpec((W, L), lambda n: (ROW_MV // W, n)),
        pl.BlockSpec((W, L), lambda n: (ROW_MO // W, n)),
        pl.BlockSpec((L, GATE_PAD), lambda n: (n, 0)),
        pl.BlockSpec((GATE_ROWS, L), lambda n: (0, n)),
        small((W, 1)),
    ]
    return pl.pallas_call(
        functools.partial(_mixers_kernel, lam_init, nc, len(attn_specs)),
        grid=(batch * nc,),
        in_specs=attn_specs + mlstm_specs,
        out_specs=[
            pl.BlockSpec((seq, dv), lambda n: (n % batch, n // batch)),
            pl.BlockSpec((L, W), lambda n: (n, 0)),
        ],
        out_shape=[
            jax.ShapeDtypeStruct((batch * seq, DIFF_WIDTH), BF16),
            jax.ShapeDtypeStruct((batch * seq, W), BF16),
        ],
        scratch_shapes=[
            pltpu.VMEM((nq, 2 * T, dv), BF16),
            pltpu.VMEM((nq, dv + ATT_SUM_ROWS, T), BF16),
            pltpu.VMEM((2, T, 2 * T), F32),
            pltpu.VMEM((2, T, 2 * T), BF16),
            pltpu.VMEM((dv + ATT_SUM_ROWS, 2 * T), F32),
            pltpu.VMEM((MLSTM_HEADS, MLSTM_HEAD_DIM + MLSTM_AUG, MLSTM_HEAD_DIM), F32),
            pltpu.VMEM((MLSTM_HEADS, 1, 1), F32),
            pltpu.VMEM((W, L), F32),
        ],
        compiler_params=pltpu.CompilerParams(
            dimension_semantics=("arbitrary",), vmem_limit_bytes=VMEM_LIMIT),
        name="token_mixers",
    )(proj, proj, proj_t, bias_tiles, lq1, lk1, lq2, lk2, dnorm_w, proj, proj, proj_t, proj_t, gcol, grow, mnorm_w)


def _layer_norm(y, g, b):
    mu = jnp.mean(y, axis=-1, keepdims=True)
    cen = y - mu
    var = jnp.mean(cen * cen, axis=-1, keepdims=True)
    return cen * lax.rsqrt(var + LN_EPS) * g + b


def _merge_kernel(hm_ref, hd_ref, gm_ref, gd_ref, x_ref, wbm_ref, wbd_ref, wo_ref, g_ref, b_ref, out_ref):
    tm = hm_ref.shape[0]
    nb = tm // MERGE_ROWS
    halves = [slice(r * MERGE_ROWS, (r + 1) * MERGE_ROWS) for r in range(nb)]
    merged = []
    for rows in halves:
        pm = jnp.dot(hm_ref[rows, :], wbm_ref[...], preferred_element_type=F32)
        pd = jnp.dot(hd_ref[rows, :], wbd_ref[...], preferred_element_type=F32)
        merged.append((gm_ref[rows, :].astype(F32) * pm + gd_ref[rows, :].astype(F32) * pd).astype(BF16))
    for rows, m in zip(halves, merged):
        mix = jnp.dot(m, wo_ref[...], preferred_element_type=F32)
        out_ref[rows, :] = _layer_norm(DEEPNORM_ALPHA * x_ref[rows, :] + mix, g_ref[...], b_ref[...])


def _ffn_kernel(tiles_per_seq, h_ref, wu_ref, cw_ref, cb_ref, wd_ref, g_ref, b_ref, out_ref,
                conv_ref, halo_ref, act_ref):
    i = pl.program_id(0)
    tm = h_ref.shape[0]
    tc = FFN_TC

    @pl.when(i % tiles_per_seq == 0)
    def _():
        halo_ref[...] = jnp.zeros_like(halo_ref)

    h = h_ref[...]
    hb = h.astype(BF16)
    for c0 in range(0, D_FF, tc):
        cols = slice(c0, c0 + tc)
        a = jnp.dot(hb, wu_ref[:, cols], preferred_element_type=F32)
        gate = jnp.dot(hb, wu_ref[:, D_FF + c0:D_FF + c0 + tc], preferred_element_type=F32)
        conv_ref[0:HALO, cols] = halo_ref[:, cols]
        conv_ref[HALO:HALO + tm, cols] = a
        y = cb_ref[:, cols] + cw_ref[FFN_CONV - 1:FFN_CONV, cols] * a
        for kk in range(FFN_CONV - 1):
            off = HALO - (FFN_CONV - 1) + kk
            y = y + cw_ref[kk:kk + 1, cols] * conv_ref[off:off + tm, cols]
        halo_ref[:, cols] = a[tm - HALO:tm, :]
        act_ref[:, cols] = (y * _sigmoid(y) * gate).astype(BF16)

    ffn = jnp.dot(act_ref[...], wd_ref[...], preferred_element_type=F32)
    out_ref[...] = _layer_norm(DEEPNORM_ALPHA * h + ffn, g_ref[...], b_ref[...])


def _merge_ffn_kernel(tiles_per_seq, hm_ref, hd_ref, gm_ref, gd_ref, x_ref, wbm_ref, wbd_ref, wo_ref, g1_ref,
                      b1_ref, wu_ref, cw_ref, cb_ref, wd_ref, g2_ref, b2_ref, out_ref,
                      h_ref, conv_ref, halo_ref, act_ref):
    _merge_kernel(hm_ref, hd_ref, gm_ref, gd_ref, x_ref, wbm_ref, wbd_ref, wo_ref, g1_ref, b1_ref, h_ref)
    _ffn_kernel(tiles_per_seq, h_ref, wu_ref, cw_ref, cb_ref, wd_ref, g2_ref, b2_ref, out_ref,
                conv_ref, halo_ref, act_ref)


def _merge_ffn(hm, hd, proj, x2, w_bm, w_bd, w_o, ln1_g, ln1_b, w_up, conv_w, conv_b, w_down, ln2_g, ln2_b, seq):
    T = x2.shape[0]
    tm = FFN_TM
    D = D_MODEL
    rows = lambda col: pl.BlockSpec((tm, D), lambda i: (i, col))
    const = _const_spec
    return pl.pallas_call(
        functools.partial(_merge_ffn_kernel, seq // tm),
        grid=(T // tm,),
        in_specs=[
            rows(0), rows(0), rows(COL_GM // D), rows(COL_GD // D), rows(0),
            const((MLSTM_WIDTH, D)), const((DIFF_WIDTH, D)), const((D, D)), const((1, D)), const((1, D)),
            const((D, 2 * D_FF)), const((FFN_CONV, D_FF)), const((1, D_FF)), const((D_FF, D)),
            const((1, D)), const((1, D)),
        ],
        out_specs=rows(0),
        out_shape=jax.ShapeDtypeStruct((T, D), F32),
        scratch_shapes=[
            pltpu.VMEM((tm, D), F32),
            pltpu.VMEM((HALO + tm, D_FF), F32),
            pltpu.VMEM((HALO, D_FF), F32),
            pltpu.VMEM((tm, D_FF), BF16),
        ],
        compiler_params=pltpu.CompilerParams(
            dimension_semantics=("arbitrary",), vmem_limit_bytes=VMEM_LIMIT),
        name="merge_ffn",
    )(hm, hd, proj, proj, x2, w_bm, w_bd, w_o, ln1_g, ln1_b, w_up, conv_w, conv_b, w_down, ln2_g, ln2_b)


def _layer(h2, batch, seq, l, w_in, b_in, mconv_w, mconv_b, mnorm_w, lq1, lk1, lq2, lk2, dnorm_w, bias_tiles,
           w_bm, w_bd, w_o, ln1_g, ln1_b, w_up, fconv_w, fconv_b, w_down, ln2_g, ln2_b):
    lam_init = 0.8 - 0.6 * math.exp(-0.3 * l)
    n_m = 4 * MLSTM_WIDTH
    n_gate = 2 * MLSTM_HEADS
    n_qk = 2 * MLSTM_WIDTH
    w_all = w_in.T.astype(BF16)
    b_a = b_in[:n_qk][None, :].astype(F32)
    w_b = w_all[n_m + n_gate:]
    b_b = b_in[n_m + n_gate:][None, :].astype(F32)
    dv0 = n_m + n_gate + TAIL_DV
    b_t = jnp.concatenate([b_in[n_qk:n_m], b_in[dv0:dv0 + DIFF_WIDTH]])[:, None].astype(F32)
    b_g = b_in[n_m:n_m + GATE_PAD][None, :].astype(F32)
    b_gt = b_in[n_m:n_m + GATE_ROWS][:, None].astype(F32)

    proj, proj_t, gcol, grow = _in_proj(h2, w_all, b_a, w_b, b_b, b_t, b_g, b_gt,
                                        mconv_w.astype(F32), mconv_b[None, :].astype(F32), seq)
    hd, hm = _mixers(proj, proj_t, gcol, grow, bias_tiles, lq1[None, :].astype(F32), lk1[None, :].astype(F32),
                     lq2[None, :].astype(F32), lk2[None, :].astype(F32), dnorm_w[:, None].astype(F32),
                     mnorm_w[:, None].astype(F32), lam_init, batch, seq)
    return _merge_ffn(hm, hd, proj, h2, w_bm.astype(BF16), w_bd.astype(BF16), w_o.astype(BF16),
                      ln1_g[None, :].astype(F32), ln1_b[None, :].astype(F32),
                      w_up.astype(BF16), fconv_w.astype(F32), fconv_b[None, :].astype(F32),
                      w_down.astype(BF16), ln2_g[None, :].astype(F32), ln2_b[None, :].astype(F32), seq)


def kernel(x, w_in, b_in, mlstm_conv_w, mlstm_conv_b, mlstm_norm_w, lambda_q1, lambda_k1, lambda_q2, lambda_k2,
           diff_norm_w, rel_bias, w_branch_mlstm, w_branch_diff, w_out, ln1_g, ln1_b, w_ffn_up, ffn_conv_w,
           ffn_conv_b, w_ffn_down, ln2_g, ln2_b):
    batch, seq, d_model = x.shape
    assert d_model == D_MODEL and seq % max(PROJ_TM, FFN_TM, ATT_T, MLSTM_L) == 0
    bias_tiles = _rel_bias_tiles(rel_bias)
    h2 = x.reshape(batch * seq, d_model)
    for l in range(w_in.shape[0]):
        h2 = _layer(h2, batch, seq, l, w_in[l], b_in[l], mlstm_conv_w[l], mlstm_conv_b[l], mlstm_norm_w[l],
                    lambda_q1[l], lambda_k1[l], lambda_q2[l], lambda_k2[l], diff_norm_w[l], bias_tiles,
                    w_branch_mlstm[l], w_branch_diff[l], w_out[l], ln1_g[l], ln1_b[l], w_ffn_up[l],
                    ffn_conv_w[l], ffn_conv_b[l], w_ffn_down[l], ln2_g[l], ln2_b[l])
    return h2.reshape(batch, seq, d_model).astype(x.dtype)
```

```python
import functools
import math

import jax
import jax.numpy as jnp
from jax import lax
from jax.experimental import pallas as pl
from jax.experimental.pallas import tpu as pltpu

F32 = jnp.float32
BF16 = jnp.bfloat16

D_MODEL = 1024
MLSTM_HEADS = 4
MLSTM_HEAD_DIM = 256
MLSTM_WIDTH = MLSTM_HEADS * MLSTM_HEAD_DIM
MLSTM_CONV = 4
DIFF_HEADS = 8
DIFF_HEAD_DIM = 64
DIFF_V_DIM = 2 * DIFF_HEAD_DIM
DIFF_WIDTH = DIFF_HEADS * DIFF_V_DIM
REL_BUCKETS = 32
REL_MAX_DIST = 128
D_FF = 2816
FFN_CONV = 3
DEPTH = 1
DEEPNORM_ALPHA = (2.0 * DEPTH) ** 0.25
LN_EPS = 1e-5
LOG2E = math.log2(math.e)

COL_MQ = 0
COL_MK = 1024
COL_DQ = 2048
COL_DK = 3072
COL_GM = 4096
COL_GD = 5120
N_MAIN = 6144
ROW_MV = 0
ROW_MO = 1024
ROW_DV = 2048
N_TRANS = 3072
TAIL_DQ = 0
TAIL_DK = 1024
TAIL_DV = 2048
TAIL_GM = 3072
N_TAIL = 5120
GATE_PAD = 128
GATE_ROWS = 16
MLSTM_AUG = 16

HALO = 8
LANES = 128
CONV_PARAMS = 8
PROJ_TM = 512
PROJ_TN = 512
MLSTM_L = 256
ATT_T = 256
ATT_SUM_ROWS = 16
MERGE_ROWS = 256
FFN_TM = 512
FFN_TC = 256
VMEM_LIMIT = 58 * 1024 * 1024


def _sigmoid(v):
    return 1.0 / (1.0 + jnp.exp(-v))


def _log_sigmoid(v):
    return jnp.minimum(v, 0.0) - jnp.log(1.0 + jnp.exp(-jnp.abs(v)))


def _split3(v):
    hi = v.astype(BF16)
    rest = v - hi.astype(F32)
    mid = rest.astype(BF16)
    lo = (rest - mid.astype(F32)).astype(BF16)
    return hi, mid, lo


def _rel_bias_kernel(table_ref, out_ref):
    h = pl.program_id(0)
    T = out_ref.shape[-1]
    max_exact = REL_BUCKETS // 2
    far = table_ref[REL_BUCKETS - 1, h]
    n = lax.broadcasted_iota(jnp.int32, (8, T), 1)
    nf = jnp.maximum(n, 1).astype(F32)
    large = max_exact + (jnp.log(nf / max_exact) / math.log(REL_MAX_DIST / max_exact)
                         * (REL_BUCKETS - max_exact)).astype(jnp.int32)
    large = jnp.minimum(large, REL_BUCKETS - 1)
    bucket = jnp.where(n < max_exact, n, large)
    by_dist = jnp.zeros((8, T), F32)
    for kk in range(REL_BUCKETS):
        by_dist = jnp.where(bucket == kk, table_ref[kk, h], by_dist)
    by_dist = (by_dist - far) * LOG2E
    rolled = pltpu.roll(jnp.broadcast_to(by_dist[0:1, :], (T, T)), 0, 1, stride=1, stride_axis=0)
    kpos = lax.broadcasted_iota(jnp.int32, (T, T), 0)
    qpos = lax.broadcasted_iota(jnp.int32, (T, T), 1)
    out_ref[0, 0] = jnp.where(qpos < kpos, rolled, 0.0)
    out_ref[0, 1] = jnp.where(qpos >= kpos, rolled, -jnp.inf)


def _rel_bias_tiles(rel_bias):
    T = ATT_T
    return pl.pallas_call(
        _rel_bias_kernel,
        grid=(DIFF_HEADS,),
        in_specs=[pl.BlockSpec(memory_space=pltpu.SMEM)],
        out_specs=pl.BlockSpec((1, 2, T, T), lambda h: (h, 0, 0, 0)),
        out_shape=jax.ShapeDtypeStruct((DIFF_HEADS, 2, T, T), F32),
        name="rel_bias_tiles",
    )(rel_bias.astype(F32))


def _in_proj_kernel(tiles_per_seq, x_ref, wa_ref, cp_ref, wb_ref, bb_ref, wt_ref, bt_ref, wg_ref, bg_ref,
                    bgt_ref, o_ref, ot_ref, gcol_ref, grow_ref, halo_ref):
    i = pl.program_id(0)
    tm = x_ref.shape[0]
    tn = PROJ_TN
    n_conv_cols = 2 * MLSTM_WIDTH
    nt_dims = (((1,), (1,)), ((), ()))

    @pl.when(i % tiles_per_seq == 0)
    def _():
        halo_ref[...] = jnp.zeros_like(halo_ref)

    xb = x_ref[...].astype(BF16)
    gcol = lax.dot_general(xb, wg_ref[...], nt_dims, preferred_element_type=F32) + bg_ref[...]
    grow = lax.dot_general(wg_ref[0:GATE_ROWS, :], xb, nt_dims,
                           preferred_element_type=F32) + bgt_ref[...]
    L = MLSTM_L
    H = MLSTM_HEADS
    tri = jnp.where(lax.broadcasted_iota(jnp.int32, (L, L), 0) >= lax.broadcasted_iota(jnp.int32, (L, L), 1),
                    1.0, 0.0).astype(BF16)
    gate_row = lax.broadcasted_iota(jnp.int32, (GATE_ROWS, L), 0)
    for r0 in range(0, tm, L):
        g_c = gcol[r0:r0 + L, :]
        g_r = grow[:, r0:r0 + L]
        bc3 = jnp.dot(tri, jnp.concatenate(_split3(_log_sigmoid(g_c)), axis=1), preferred_element_type=F32)
        bcol = bc3[:, 0:GATE_PAD] + bc3[:, GATE_PAD:2 * GATE_PAD] + bc3[:, 2 * GATE_PAD:3 * GATE_PAD]
        br3 = lax.dot_general(jnp.concatenate(_split3(_log_sigmoid(g_r)), axis=0), tri, nt_dims,
                              preferred_element_type=F32)
        brow = br3[0:GATE_ROWS] + br3[GATE_ROWS:2 * GATE_ROWS] + br3[2 * GATE_ROWS:3 * GATE_ROWS]
        gcol_ref[r0:r0 + L, :] = g_c - pltpu.roll(bcol, GATE_PAD - H, 1)
        grow_ref[:, r0:r0 + L] = jnp.where(gate_row < H, g_r, brow)

    for r0 in range(0, N_TRANS, tn):
        rows = slice(r0, r0 + tn)
        if r0 < ROW_DV:
            w_rows = wt_ref[rows, :]
        else:
            w_rows = wb_ref[TAIL_DV + r0 - ROW_DV:TAIL_DV + r0 - ROW_DV + tn, :]
        acc_t = lax.dot_general(w_rows, xb, nt_dims, preferred_element_type=F32) + bt_ref[rows, :]
        if ROW_MO <= r0 < ROW_DV:
            acc_t = _sigmoid(acc_t)
        ot_ref[rows, :] = acc_t.astype(BF16)

    lane = lax.broadcasted_iota(jnp.int32, (tn, LANES), 1)
    for c0 in range(0, n_conv_cols, tn):
        cols = slice(c0, c0 + tn)
        par = cp_ref[cols, :]
        acc_t = lax.dot_general(wa_ref[cols, :], xb, nt_dims, preferred_element_type=F32) \
            + par[:, MLSTM_CONV + 1:MLSTM_CONV + 2]
        prev = halo_ref[cols, :]
        y = par[:, MLSTM_CONV:MLSTM_CONV + 1] + par[:, MLSTM_CONV - 1:MLSTM_CONV] * acc_t
        for kk in range(MLSTM_CONV - 1):
            shift = MLSTM_CONV - 1 - kk
            rolled = pltpu.roll(acc_t, shift, 1)
            head = jnp.where(lane < shift, pltpu.roll(prev, shift, 1), rolled[:, 0:LANES])
            y = y + par[:, kk:kk + 1] * jnp.concatenate([head, rolled[:, LANES:]], axis=1)
        halo_ref[cols, :] = acc_t[:, tm - LANES:tm]
        y = y * _sigmoid(y)
        if c0 >= COL_MK:
            y = y * (MLSTM_HEAD_DIM ** -0.5)
        o_ref[:, cols] = y.T.astype(BF16)

    for c0 in range(n_conv_cols, N_MAIN, tn):
        cols = slice(c0, c0 + tn)
        t0 = c0 - COL_DQ + TAIL_DQ if c0 < COL_GM else c0 - COL_GM + TAIL_GM
        wcols = slice(t0, t0 + tn)
        acc = lax.dot_general(xb, wb_ref[wcols, :], nt_dims, preferred_element_type=F32) + bb_ref[:, wcols]
        if c0 >= COL_GM:
            o_ref[:, cols] = _sigmoid(acc).astype(BF16)
        elif COL_DQ <= c0 < COL_DK:
            o_ref[:, cols] = (acc * (DIFF_HEAD_DIM ** -0.5 * LOG2E)).astype(BF16)
        else:
            o_ref[:, cols] = acc.astype(BF16)


def _const_spec(shape):
    return pl.BlockSpec(shape, lambda i: (0, 0), pipeline_mode=pl.Buffered(1))


def _in_proj(x2, w_all, conv_params, w_b, b_b, b_t, b_g, b_gt, seq):
    T = x2.shape[0]
    tm = PROJ_TM
    n_conv_cols = 2 * MLSTM_WIDTH
    n_vo = ROW_DV
    assert n_vo == n_conv_cols and (n_conv_cols + n_vo) % GATE_PAD == 0
    const = _const_spec
    row_block = lambda shape, idx: pl.BlockSpec(shape, lambda i: (idx, 0), pipeline_mode=pl.Buffered(1))
    return pl.pallas_call(
        functools.partial(_in_proj_kernel, seq // tm),
        grid=(T // tm,),
        in_specs=[
            pl.BlockSpec((tm, D_MODEL), lambda i: (i, 0)),
            row_block((n_conv_cols, D_MODEL), 0),
            const((n_conv_cols, CONV_PARAMS)),
            const((N_TAIL, D_MODEL)),
            const((1, N_TAIL)),
            row_block((n_vo, D_MODEL), 1),
            const((N_TRANS, 1)),
            row_block((GATE_PAD, D_MODEL), (n_conv_cols + n_vo) // GATE_PAD),
            const((1, GATE_PAD)),
            const((GATE_ROWS, 1)),
        ],
        out_specs=[
            pl.BlockSpec((tm, N_MAIN), lambda i: (i, 0)),
            pl.BlockSpec((N_TRANS, tm), lambda i: (0, i)),
            pl.BlockSpec((tm, GATE_PAD), lambda i: (i, 0)),
            pl.BlockSpec((GATE_ROWS, tm), lambda i: (0, i)),
        ],
        out_shape=[
            jax.ShapeDtypeStruct((T, N_MAIN), BF16),
            jax.ShapeDtypeStruct((N_TRANS, T), BF16),
            jax.ShapeDtypeStruct((T, GATE_PAD), F32),
            jax.ShapeDtypeStruct((GATE_ROWS, T), F32),
        ],
        scratch_shapes=[
            pltpu.VMEM((n_conv_cols, LANES), F32),
        ],
        compiler_params=pltpu.CompilerParams(
            dimension_semantics=("arbitrary",), vmem_limit_bytes=VMEM_LIMIT),
        name="in_proj",
    )(x2, w_all, conv_params, w_b, b_b, w_all, b_t, w_all, b_g, b_gt)


def _mlstm_chunk(c, q_ref, k_ref, vt_ref, ot_ref, gcol_ref, grow_ref, nw_ref, out_ref, ct_ref, m_ref, nwb_ref):
    L = q_ref.shape[0]
    d = MLSTM_HEAD_DIM
    H = MLSTM_HEADS
    nt_dims = (((1,), (1,)), ((), ()))

    @pl.when(c == 0)
    def _():
        ct_ref[...] = jnp.zeros_like(ct_ref)
        m_ref[...] = jnp.zeros_like(m_ref)
        nwb_ref[...] = jnp.broadcast_to(nw_ref[...], nwb_ref.shape)

    key = lax.broadcasted_iota(jnp.int32, (L, L), 0)
    qry = lax.broadcasted_iota(jnp.int32, (L, L), 1)
    causal_t = key <= qry

    gdiff = gcol_ref[...]
    grow = grow_ref[...]
    ones = jnp.ones((MLSTM_AUG, L), BF16)

    def head(h):
        cols = slice(h * d, (h + 1) * d)
        q = q_ref[:, cols]
        k = k_ref[:, cols]
        v_aug = jnp.concatenate([vt_ref[cols, :], ones], axis=0)
        b_r = grow[H + h:H + h + 1, :]
        i_r = grow[h:h + 1, :]
        g_c = gdiff[:, h:h + 1]
        m_prev = m_ref[h]
        ct = ct_ref[h]

        dmat = jnp.where(causal_t, b_r + g_c, -jnp.inf)
        a = b_r + m_prev
        m_row = jnp.maximum(a, jnp.max(dmat, axis=0, keepdims=True))
        w = jnp.exp(dmat - m_row)
        inter = jnp.exp(a - m_row)
        kq = lax.dot_general(k, q, nt_dims, preferred_element_type=F32)
        sqk = (kq * w).astype(BF16)
        nd = inter * lax.dot_general(ct.astype(BF16), q, nt_dims, preferred_element_type=F32) \
            + jnp.dot(v_aug, sqk, preferred_element_type=F32)
        rinv = 1.0 / jnp.maximum(jnp.abs(nd[d:d + 1, :]), jnp.exp(-m_row))
        hid = nd[0:d, :] * rinv * ot_ref[cols, :].astype(F32)
        mu = jnp.mean(hid, axis=0, keepdims=True)
        cen = hid - mu
        var = jnp.mean(cen * cen, axis=0, keepdims=True)
        out_ref[:, cols] = (cen * lax.rsqrt(var + LN_EPS) * nwb_ref[cols, :]).T.astype(BF16)

        b_last = b_r[:, L - 1:L]
        g_r = b_last - b_r + i_r
        m_new = jnp.maximum(b_last + m_prev, jnp.max(g_r, axis=-1, keepdims=True))
        decay = jnp.exp(b_last + m_prev - m_new)
        ws = jnp.exp(g_r - m_new)
        vw = (v_aug.astype(F32) * ws).astype(BF16)
        ct_ref[h] = decay * ct + jnp.dot(vw, k, preferred_element_type=F32)
        m_ref[h] = m_new

    return [functools.partial(head, h) for h in range(H)]


def _diff_attn_head(lam_init, interleaved, q_ref, k_ref, v_ref, bias_ref, lq1_ref, lk1_ref, lq2_ref, lk2_ref,
                    nw_ref, out_ref, qs_ref, vt_ref, s_ref, p_ref, acc_ref):
    T = ATT_T
    dh = DIFF_HEAD_DIM
    dv = DIFF_V_DIM
    nq = q_ref.shape[0] // T
    R = 2 * T

    ones = jnp.ones((ATT_SUM_ROWS, T), BF16)
    for jj in range(nq):
        vt_ref[jj, 0:dv, :] = v_ref[:, jj * T:(jj + 1) * T]
        vt_ref[jj, dv:dv + ATT_SUM_ROWS, :] = ones

    lane = lax.broadcasted_iota(jnp.int32, (T, dv), 1)
    for ii in range(nq):
        qt = q_ref[ii * T:(ii + 1) * T, :]
        zero = jnp.zeros_like(qt)
        qs_ref[ii, 0:T, :] = jnp.where(lane < dh, qt, zero)
        qs_ref[ii, T:R, :] = jnp.where(lane >= dh, qt, zero)

    lam = (jnp.exp(jnp.sum(lq1_ref[...] * lk1_ref[...], axis=-1, keepdims=True))
           - jnp.exp(jnp.sum(lq2_ref[...] * lk2_ref[...], axis=-1, keepdims=True)) + lam_init)

    pairs = [(i, j) for i in range(nq) for j in range(i + 1)]

    def scores(n):
        i, j = pairs[n]
        s = lax.dot_general(k_ref[j * T:(j + 1) * T, :], qs_ref[i], (((1,), (1,)), ((), ())),
                            preferred_element_type=F32)
        if j >= i - 1:
            bias = bias_ref[0, j - i + 1]
            s = s + jnp.concatenate([bias, bias], axis=1)
        s_ref[n % 2] = s
        return jnp.max(s, axis=0, keepdims=True)

    def softmax(n, m_tile, m_prev):
        i, j = pairs[n]
        m_new = m_tile if j == 0 else jnp.maximum(m_prev, m_tile)
        p_ref[n % 2] = jnp.exp2(s_ref[n % 2] - m_new).astype(BF16)
        alpha = None if j == 0 else jnp.exp2(m_prev - m_new)
        return m_new, alpha

    def values(n, alpha):
        i, j = pairs[n]
        pv = jnp.dot(vt_ref[j], p_ref[n % 2], preferred_element_type=F32)
        if j == 0:
            acc_ref[...] = pv
        else:
            acc_ref[...] = alpha * acc_ref[...] + pv
        if j == i:
            acc = acc_ref[...]
            out = acc[0:dv, :] / acc[dv:dv + 1, :]
            hd = out[:, 0:T] - lam * out[:, T:R]
            hd = hd * lax.rsqrt(jnp.mean(hd * hd, axis=0, keepdims=True) + LN_EPS) * nw_ref[...]
            out_ref[i * T:(i + 1) * T, :] = (hd * (1.0 - lam_init)).T.astype(BF16)

    spacing = len(pairs) // (len(interleaved) + 1) if interleaved else 0
    m_tiles = {0: scores(0)}
    m_run = None
    alphas = {}
    for n in range(len(pairs)):
        if n + 1 < len(pairs):
            m_tiles[n + 1] = scores(n + 1)
        m_run, alphas[n] = softmax(n, m_tiles.pop(n), m_run)
        if n >= 1:
            values(n - 1, alphas.pop(n - 1))
        if spacing and (n + 1) % spacing == 0 and (n + 1) // spacing <= len(interleaved):
            interleaved[(n + 1) // spacing - 1]()
    values(len(pairs) - 1, alphas.pop(len(pairs) - 1))


def _mixers_kernel(lam_init, chunks_per_seq, n_attn_refs, *refs):
    (q_ref, k_ref, v_ref, bias_ref, lq1_ref, lk1_ref, lq2_ref, lk2_ref, anw_ref) = refs[:n_attn_refs]
    (mq_ref, mk_ref, mvt_ref, mot_ref, gcol_ref, grow_ref, mnw_ref) = refs[n_attn_refs:n_attn_refs + 7]
    hd_ref, hm_ref = refs[n_attn_refs + 7:n_attn_refs + 9]
    (qs_ref, vt_ref, s_ref, p_ref, acc_ref, ct_ref, m_ref, nwb_ref) = refs[n_attn_refs + 9:]
    c = pl.program_id(0) % chunks_per_seq
    heads = _mlstm_chunk(c, mq_ref, mk_ref, mvt_ref, mot_ref, gcol_ref, grow_ref, mnw_ref, hm_ref,
                         ct_ref, m_ref, nwb_ref)
    _diff_attn_head(lam_init, heads, q_ref, k_ref, v_ref, bias_ref, lq1_ref, lk1_ref, lq2_ref, lk2_ref,
                    anw_ref, hd_ref, qs_ref, vt_ref, s_ref, p_ref, acc_ref)


def _mixers(proj, proj_t, gcol, grow, bias_tiles, lq1, lk1, lq2, lk2, dnorm_w, mnorm_w, lam_init, batch, seq):
    T = ATT_T
    nq = seq // T
    dv = DIFF_V_DIM
    L = MLSTM_L
    nc = seq // L
    W = MLSTM_WIDTH
    assert DIFF_HEADS * batch == batch * nc, "attention (head, batch) steps must pair 1:1 with mLSTM chunks"
    small = lambda shape: pl.BlockSpec(shape, lambda n: (0, 0))
    attn_specs = [
        pl.BlockSpec((seq, dv), lambda n: (n % batch, COL_DQ // dv + n // batch)),
        pl.BlockSpec((seq, dv), lambda n: (n % batch, COL_DK // dv + n // batch)),
        pl.BlockSpec((dv, seq), lambda n: (ROW_DV // dv + n // batch, n % batch)),
        pl.BlockSpec((1, 2, T, T), lambda n: (n // batch, 0, 0, 0)),
        small((1, DIFF_HEAD_DIM)), small((1, DIFF_HEAD_DIM)),
        small((1, DIFF_HEAD_DIM)), small((1, DIFF_HEAD_DIM)),
        small((dv, 1)),
    ]
    mlstm_specs = [
        pl.BlockSpec((L, W), lambda n: (n, COL_MQ // W)),
        pl.BlockSpec((L, W), lambda n: (n, COL_MK // W)),
        pl.BlockSpec((W, L), lambda n: (ROW_MV // W, n)),
        pl.BlockSpec((W, L), lambda n: (ROW_MO // W, n)),
        pl.BlockSpec((L, GATE_PAD), lambda n: (n, 0)),
        pl.BlockSpec((GATE_ROWS, L), lambda n: (0, n)),
        small((W, 1)),
    ]
    return pl.pallas_call(
        functools.partial(_mixers_kernel, lam_init, nc, len(attn_specs)),
        grid=(batch * nc,),
        in_specs=attn_specs + mlstm_specs,
        out_specs=[
            pl.BlockSpec((seq, dv), lambda n: (n % batch, n // batch)),
            pl.BlockSpec((L, W), lambda n: (n, 0)),
        ],
        out_shape=[
            jax.ShapeDtypeStruct((batch * seq, DIFF_WIDTH), BF16),
            jax.ShapeDtypeStruct((batch * seq, W), BF16),
        ],
        scratch_shapes=[
            pltpu.VMEM((nq, 2 * T, dv), BF16),
            pltpu.VMEM((nq, dv + ATT_SUM_ROWS, T), BF16),
            pltpu.VMEM((2, T, 2 * T), F32),
            pltpu.VMEM((2, T, 2 * T), BF16),
            pltpu.VMEM((dv + ATT_SUM_ROWS, 2 * T), F32),
            pltpu.VMEM((MLSTM_HEADS, MLSTM_HEAD_DIM + MLSTM_AUG, MLSTM_HEAD_DIM), F32),
            pltpu.VMEM((MLSTM_HEADS, 1, 1), F32),
            pltpu.VMEM((W, L), F32),
        ],
        compiler_params=pltpu.CompilerParams(
            dimension_semantics=("arbitrary",), vmem_limit_bytes=VMEM_LIMIT),
        name="token_mixers",
    )(proj, proj, proj_t, bias_tiles, lq1, lk1, lq2, lk2, dnorm_w, proj, proj, proj_t, proj_t, gcol, grow, mnorm_w)


def _layer_norm(y, g, b):
    mu = jnp.mean(y, axis=-1, keepdims=True)
    cen = y - mu
    var = jnp.mean(cen * cen, axis=-1, keepdims=True)
    return cen * lax.rsqrt(var + LN_EPS) * g + b


def _merge_kernel(hm_ref, hd_ref, gm_ref, gd_ref, x_ref, wbm_ref, wbd_ref, wo_ref, g_ref, b_ref, out_ref):
    tm = hm_ref.shape[0]
    nb = tm // MERGE_ROWS
    halves = [slice(r * MERGE_ROWS, (r + 1) * MERGE_ROWS) for r in range(nb)]
    merged = []
    for rows in halves:
        pm = jnp.dot(hm_ref[rows, :], wbm_ref[...], preferred_element_type=F32)
        pd = jnp.dot(hd_ref[rows, :], wbd_ref[...], preferred_element_type=F32)
        merged.append((gm_ref[rows, :].astype(F32) * pm + gd_ref[rows, :].astype(F32) * pd).astype(BF16))
    for rows, m in zip(halves, merged):
        mix = jnp.dot(m, wo_ref[...], preferred_element_type=F32)
        out_ref[rows, :] = _layer_norm(DEEPNORM_ALPHA * x_ref[rows, :] + mix, g_ref[...], b_ref[...])


def _ffn_kernel(tiles_per_seq, h_ref, wu_ref, cw_ref, cb_ref, wd_ref, g_ref, b_ref, out_ref,
                conv_ref, halo_ref, act_ref):
    i = pl.program_id(0)
    tm = h_ref.shape[0]
    tc = FFN_TC

    @pl.when(i % tiles_per_seq == 0)
    def _():
        halo_ref[...] = jnp.zeros_like(halo_ref)

    h = h_ref[...]
    hb = h.astype(BF16)
    for c0 in range(0, D_FF, tc):
        cols = slice(c0, c0 + tc)
        a = jnp.dot(hb, wu_ref[:, cols], preferred_element_type=F32)
        gate = jnp.dot(hb, wu_ref[:, D_FF + c0:D_FF + c0 + tc], preferred_element_type=F32)
        conv_ref[0:HALO, cols] = halo_ref[:, cols]
        conv_ref[HALO:HALO + tm, cols] = a
        y = cb_ref[:, cols] + cw_ref[FFN_CONV - 1:FFN_CONV, cols] * a
        for kk in range(FFN_CONV - 1):
            off = HALO - (FFN_CONV - 1) + kk
            y = y + cw_ref[kk:kk + 1, cols] * conv_ref[off:off + tm, cols]
        halo_ref[:, cols] = a[tm - HALO:tm, :]
        act_ref[:, cols] = (y * _sigmoid(y) * gate).astype(BF16)

    ffn = jnp.dot(act_ref[...], wd_ref[...], preferred_element_type=F32)
    out_ref[...] = _layer_norm(DEEPNORM_ALPHA * h + ffn, g_ref[...], b_ref[...])


def _merge_ffn_kernel(tiles_per_seq, hm_ref, hd_ref, gm_ref, gd_ref, x_ref, wbm_ref, wbd_ref, wo_ref, g1_ref,
                      b1_ref, wu_ref, cw_ref, cb_ref, wd_ref, g2_ref, b2_ref, out_ref,
                      h_ref, conv_ref, halo_ref, act_ref):
    _merge_kernel(hm_ref, hd_ref, gm_ref, gd_ref, x_ref, wbm_ref, wbd_ref, wo_ref, g1_ref, b1_ref, h_ref)
    _ffn_kernel(tiles_per_seq, h_ref, wu_ref, cw_ref, cb_ref, wd_ref, g2_ref, b2_ref, out_ref,
                conv_ref, halo_ref, act_ref)


def _merge_ffn(hm, hd, proj, x2, w_bm, w_bd, w_o, ln1_g, ln1_b, w_up, conv_w, conv_b, w_down, ln2_g, ln2_b, seq):
    T = x2.shape[0]
    tm = FFN_TM
    D = D_MODEL
    rows = lambda col: pl.BlockSpec((tm, D), lambda i: (i, col))
    const = _const_spec
    return pl.pallas_call(
        functools.partial(_merge_ffn_kernel, seq // tm),
        grid=(T // tm,),
        in_specs=[
            rows(0), rows(0), rows(COL_GM // D), rows(COL_GD // D), rows(0),
            const((MLSTM_WIDTH, D)), const((DIFF_WIDTH, D)), const((D, D)), const((1, D)), const((1, D)),
            const((D, 2 * D_FF)), const((FFN_CONV, D_FF)), const((1, D_FF)), const((D_FF, D)),
            const((1, D)), const((1, D)),
        ],
        out_specs=rows(0),
        out_shape=jax.ShapeDtypeStruct((T, D), F32),
        scratch_shapes=[
            pltpu.VMEM((tm, D), F32),
            pltpu.VMEM((HALO + tm, D_FF), F32),
            pltpu.VMEM((HALO, D_FF), F32),
            pltpu.VMEM((tm, D_FF), BF16),
        ],
        compiler_params=pltpu.CompilerParams(
            dimension_semantics=("arbitrary",), vmem_limit_bytes=VMEM_LIMIT),
        name="merge_ffn",
    )(hm, hd, proj, proj, x2, w_bm, w_bd, w_o, ln1_g, ln1_b, w_up, conv_w, conv_b, w_down, ln2_g, ln2_b)


def _layer(h2, batch, seq, l, w_in, b_in, mconv_w, mconv_b, mnorm_w, lq1, lk1, lq2, lk2, dnorm_w, bias_tiles,
           w_bm, w_bd, w_o, ln1_g, ln1_b, w_up, fconv_w, fconv_b, w_down, ln2_g, ln2_b):
    lam_init = 0.8 - 0.6 * math.exp(-0.3 * l)
    n_m = 4 * MLSTM_WIDTH
    n_gate = 2 * MLSTM_HEADS
    n_qk = 2 * MLSTM_WIDTH
    w_all = w_in.T.astype(BF16)
    conv_params = jnp.concatenate(
        [mconv_w.astype(F32).T, mconv_b.astype(F32)[:, None], b_in[:n_qk].astype(F32)[:, None],
         jnp.zeros((n_qk, CONV_PARAMS - MLSTM_CONV - 2), F32)], axis=1)
    w_b = w_all[n_m + n_gate:]
    b_b = b_in[n_m + n_gate:][None, :].astype(F32)
    dv0 = n_m + n_gate + TAIL_DV
    b_t = jnp.concatenate([b_in[n_qk:n_m], b_in[dv0:dv0 + DIFF_WIDTH]])[:, None].astype(F32)
    b_g = b_in[n_m:n_m + GATE_PAD][None, :].astype(F32)
    b_gt = b_in[n_m:n_m + GATE_ROWS][:, None].astype(F32)

    proj, proj_t, gcol, grow = _in_proj(h2, w_all, conv_params, w_b, b_b, b_t, b_g, b_gt, seq)
    hd, hm = _mixers(proj, proj_t, gcol, grow, bias_tiles, lq1[None, :].astype(F32), lk1[None, :].astype(F32),
                     lq2[None, :].astype(F32), lk2[None, :].astype(F32), dnorm_w[:, None].astype(F32),
                     mnorm_w[:, None].astype(F32), lam_init, batch, seq)
    return _merge_ffn(hm, hd, proj, h2, w_bm.astype(BF16), w_bd.astype(BF16), w_o.astype(BF16),
                      ln1_g[None, :].astype(F32), ln1_b[None, :].astype(F32),
                      w_up.astype(BF16), fconv_w.astype(F32), fconv_b[None, :].astype(F32),
                      w_down.astype(BF16), ln2_g[None, :].astype(F32), ln2_b[None, :].astype(F32), seq)


def kernel(x, w_in, b_in, mlstm_conv_w, mlstm_conv_b, mlstm_norm_w, lambda_q1, lambda_k1, lambda_q2, lambda_k2,
           diff_norm_w, rel_bias, w_branch_mlstm, w_branch_diff, w_out, ln1_g, ln1_b, w_ffn_up, ffn_conv_w,
           ffn_conv_b, w_ffn_down, ln2_g, ln2_b):
    batch, seq, d_model = x.shape
    assert d_model == D_MODEL and seq % max(PROJ_TM, FFN_TM, ATT_T, MLSTM_L) == 0
    bias_tiles = _rel_bias_tiles(rel_bias)
    h2 = x.reshape(batch * seq, d_model)
    for l in range(w_in.shape[0]):
        h2 = _layer(h2, batch, seq, l, w_in[l], b_in[l], mlstm_conv_w[l], mlstm_conv_b[l], mlstm_norm_w[l],
                    lambda_q1[l], lambda_k1[l], lambda_q2[l], lambda_k2[l], diff_norm_w[l], bias_tiles,
                    w_branch_mlstm[l], w_branch_diff[l], w_out[l], ln1_g[l], ln1_b[l], w_ffn_up[l],
                    ffn_conv_w[l], ffn_conv_b[l], w_ffn_down[l], ln2_g[l], ln2_b[l])
    return h2.reshape(batch, seq, d_model).astype(x.dtype)
```

```python
import functools
import math

import jax
import jax.numpy as jnp
from jax import lax
from jax.experimental import pallas as pl
from jax.experimental.pallas import tpu as pltpu

F32 = jnp.float32
BF16 = jnp.bfloat16

D_MODEL = 1024
MLSTM_HEADS = 4
MLSTM_HEAD_DIM = 256
MLSTM_WIDTH = MLSTM_HEADS * MLSTM_HEAD_DIM
MLSTM_CONV = 4
DIFF_HEADS = 8
DIFF_HEAD_DIM = 64
DIFF_V_DIM = 2 * DIFF_HEAD_DIM
DIFF_WIDTH = DIFF_HEADS * DIFF_V_DIM
REL_BUCKETS = 32
REL_MAX_DIST = 128
D_FF = 2816
FFN_CONV = 3
DEPTH = 1
DEEPNORM_ALPHA = (2.0 * DEPTH) ** 0.25
LN_EPS = 1e-5
LOG2E = math.log2(math.e)

COL_MK = 0
COL_DQ = 1024
COL_DK = 2048
COL_GM = 3072
COL_GD = 4096
N_MAIN = 5120
ROW_MV = 0
ROW_MO = 1024
ROW_DV = 2048
ROW_MQ = 3072
N_TRANS = 4096
TAIL_DQ = 0
TAIL_DK = 1024
TAIL_DV = 2048
TAIL_GM = 3072
N_TAIL = 5120
GATE_PAD = 128
GATE_ROWS = 16
MLSTM_AUG = 16

HALO = 8
LANES = 128
CONV_PARAMS = 8
PROJ_TM = 512
PROJ_TN = 512
MLSTM_L = 256
ATT_T = 256
ATT_SUM_ROWS = 16
INTERLEAVE_EVERY = 4
MERGE_ROWS = 256
FFN_TM = 512
FFN_TC = 256
VMEM_LIMIT = 58 * 1024 * 1024


def _sigmoid(v):
    return 1.0 / (1.0 + jnp.exp(-v))


def _log_sigmoid(v):
    return jnp.minimum(v, 0.0) - jnp.log(1.0 + jnp.exp(-jnp.abs(v)))


def _split3(v):
    hi = v.astype(BF16)
    rest = v - hi.astype(F32)
    mid = rest.astype(BF16)
    lo = (rest - mid.astype(F32)).astype(BF16)
    return hi, mid, lo


def _rel_bias_kernel(table_ref, out_ref):
    h = pl.program_id(0)
    T = out_ref.shape[-1]
    max_exact = REL_BUCKETS // 2
    far = table_ref[REL_BUCKETS - 1, h]
    n = lax.broadcasted_iota(jnp.int32, (8, T), 1)
    nf = jnp.maximum(n, 1).astype(F32)
    large = max_exact + (jnp.log(nf / max_exact) / math.log(REL_MAX_DIST / max_exact)
                         * (REL_BUCKETS - max_exact)).astype(jnp.int32)
    large = jnp.minimum(large, REL_BUCKETS - 1)
    bucket = jnp.where(n < max_exact, n, large)
    by_dist = jnp.zeros((8, T), F32)
    for kk in range(REL_BUCKETS):
        by_dist = jnp.where(bucket == kk, table_ref[kk, h], by_dist)
    by_dist = (by_dist - far) * LOG2E
    rolled = pltpu.roll(jnp.broadcast_to(by_dist[0:1, :], (T, T)), 0, 1, stride=1, stride_axis=0)
    kpos = lax.broadcasted_iota(jnp.int32, (T, T), 0)
    qpos = lax.broadcasted_iota(jnp.int32, (T, T), 1)
    out_ref[0, 0] = jnp.where(qpos < kpos, rolled, 0.0)
    out_ref[0, 1] = jnp.where(qpos >= kpos, rolled, -jnp.inf)


def _rel_bias_tiles(rel_bias):
    T = ATT_T
    return pl.pallas_call(
        _rel_bias_kernel,
        grid=(DIFF_HEADS,),
        in_specs=[pl.BlockSpec(memory_space=pltpu.SMEM)],
        out_specs=pl.BlockSpec((1, 2, T, T), lambda h: (h, 0, 0, 0)),
        out_shape=jax.ShapeDtypeStruct((DIFF_HEADS, 2, T, T), F32),
        name="rel_bias_tiles",
    )(rel_bias.astype(F32))


def _in_proj_kernel(tiles_per_seq, x_ref, wa_ref, cp_ref, wb_ref, bb_ref, wt_ref, bt_ref, wg_ref, bg_ref,
                    bgt_ref, o_ref, ot_ref, gcol_ref, grow_ref, halo_ref):
    i = pl.program_id(0)
    tm = x_ref.shape[0]
    tn = PROJ_TN
    n_conv_cols = 2 * MLSTM_WIDTH
    nt_dims = (((1,), (1,)), ((), ()))

    @pl.when(i % tiles_per_seq == 0)
    def _():
        halo_ref[...] = jnp.zeros_like(halo_ref)

    xb = x_ref[...].astype(BF16)
    gcol = lax.dot_general(xb, wg_ref[...], nt_dims, preferred_element_type=F32) + bg_ref[...]
    grow = lax.dot_general(wg_ref[0:GATE_ROWS, :], xb, nt_dims,
                           preferred_element_type=F32) + bgt_ref[...]
    L = MLSTM_L
    H = MLSTM_HEADS
    tri = jnp.where(lax.broadcasted_iota(jnp.int32, (L, L), 0) >= lax.broadcasted_iota(jnp.int32, (L, L), 1),
                    1.0, 0.0).astype(BF16)
    gate_row = lax.broadcasted_iota(jnp.int32, (GATE_ROWS, L), 0)
    for r0 in range(0, tm, L):
        g_c = gcol[r0:r0 + L, :]
        g_r = grow[:, r0:r0 + L]
        bc3 = jnp.dot(tri, jnp.concatenate(_split3(_log_sigmoid(g_c)), axis=1), preferred_element_type=F32)
        bcol = bc3[:, 0:GATE_PAD] + bc3[:, GATE_PAD:2 * GATE_PAD] + bc3[:, 2 * GATE_PAD:3 * GATE_PAD]
        br3 = lax.dot_general(jnp.concatenate(_split3(_log_sigmoid(g_r)), axis=0), tri, nt_dims,
                              preferred_element_type=F32)
        brow = br3[0:GATE_ROWS] + br3[GATE_ROWS:2 * GATE_ROWS] + br3[2 * GATE_ROWS:3 * GATE_ROWS]
        gcol_ref[r0:r0 + L, :] = g_c - pltpu.roll(bcol, GATE_PAD - H, 1)
        grow_ref[:, r0:r0 + L] = jnp.where(gate_row < H, g_r, brow)

    for r0 in range(0, ROW_MQ, tn):
        rows = slice(r0, r0 + tn)
        if r0 < ROW_DV:
            w_rows = wt_ref[rows, :]
        else:
            w_rows = wb_ref[TAIL_DV + r0 - ROW_DV:TAIL_DV + r0 - ROW_DV + tn, :]
        acc_t = lax.dot_general(w_rows, xb, nt_dims, preferred_element_type=F32) + bt_ref[rows, :]
        if ROW_MO <= r0 < ROW_DV:
            acc_t = _sigmoid(acc_t)
        ot_ref[rows, :] = acc_t.astype(BF16)

    lane = lax.broadcasted_iota(jnp.int32, (tn, LANES), 1)
    for c0 in range(0, n_conv_cols, tn):
        cols = slice(c0, c0 + tn)
        par = cp_ref[cols, :]
        acc_t = lax.dot_general(wa_ref[cols, :], xb, nt_dims, preferred_element_type=F32) \
            + par[:, MLSTM_CONV + 1:MLSTM_CONV + 2]
        prev = halo_ref[cols, :]
        y = par[:, MLSTM_CONV:MLSTM_CONV + 1] + par[:, MLSTM_CONV - 1:MLSTM_CONV] * acc_t
        for kk in range(MLSTM_CONV - 1):
            shift = MLSTM_CONV - 1 - kk
            rolled = pltpu.roll(acc_t, shift, 1)
            head = jnp.where(lane < shift, pltpu.roll(prev, shift, 1), rolled[:, 0:LANES])
            y = y + par[:, kk:kk + 1] * jnp.concatenate([head, rolled[:, LANES:]], axis=1)
        halo_ref[cols, :] = acc_t[:, tm - LANES:tm]
        y = y * _sigmoid(y)
        if c0 < MLSTM_WIDTH:
            ot_ref[ROW_MQ + c0:ROW_MQ + c0 + tn, :] = y.astype(BF16)
        else:
            k0 = COL_MK + c0 - MLSTM_WIDTH
            o_ref[:, k0:k0 + tn] = (y * (MLSTM_HEAD_DIM ** -0.5)).T.astype(BF16)

    for c0 in range(COL_DQ, N_MAIN, tn):
        cols = slice(c0, c0 + tn)
        t0 = c0 - COL_DQ + TAIL_DQ if c0 < COL_GM else c0 - COL_GM + TAIL_GM
        wcols = slice(t0, t0 + tn)
        acc = lax.dot_general(xb, wb_ref[wcols, :], nt_dims, preferred_element_type=F32) + bb_ref[:, wcols]
        if c0 >= COL_GM:
            o_ref[:, cols] = _sigmoid(acc).astype(BF16)
        elif COL_DQ <= c0 < COL_DK:
            o_ref[:, cols] = (acc * (DIFF_HEAD_DIM ** -0.5 * LOG2E)).astype(BF16)
        else:
            o_ref[:, cols] = acc.astype(BF16)


def _const_spec(shape):
    return pl.BlockSpec(shape, lambda i: (0, 0), pipeline_mode=pl.Buffered(1))


def _in_proj(x2, w_all, conv_params, w_b, b_b, b_t, b_g, b_gt, seq):
    T = x2.shape[0]
    tm = PROJ_TM
    n_conv_cols = 2 * MLSTM_WIDTH
    n_vo = ROW_DV
    assert n_vo == n_conv_cols and (n_conv_cols + n_vo) % GATE_PAD == 0
    const = _const_spec
    row_block = lambda shape, idx: pl.BlockSpec(shape, lambda i: (idx, 0), pipeline_mode=pl.Buffered(1))
    return pl.pallas_call(
        functools.partial(_in_proj_kernel, seq // tm),
        grid=(T // tm,),
        in_specs=[
            pl.BlockSpec((tm, D_MODEL), lambda i: (i, 0)),
            row_block((n_conv_cols, D_MODEL), 0),
            const((n_conv_cols, CONV_PARAMS)),
            const((N_TAIL, D_MODEL)),
            const((1, N_TAIL)),
            row_block((n_vo, D_MODEL), 1),
            const((ROW_MQ, 1)),
            row_block((GATE_PAD, D_MODEL), (n_conv_cols + n_vo) // GATE_PAD),
            const((1, GATE_PAD)),
            const((GATE_ROWS, 1)),
        ],
        out_specs=[
            pl.BlockSpec((tm, N_MAIN), lambda i: (i, 0)),
            pl.BlockSpec((N_TRANS, tm), lambda i: (0, i)),
            pl.BlockSpec((tm, GATE_PAD), lambda i: (i, 0)),
            pl.BlockSpec((GATE_ROWS, tm), lambda i: (0, i)),
        ],
        out_shape=[
            jax.ShapeDtypeStruct((T, N_MAIN), BF16),
            jax.ShapeDtypeStruct((N_TRANS, T), BF16),
            jax.ShapeDtypeStruct((T, GATE_PAD), F32),
            jax.ShapeDtypeStruct((GATE_ROWS, T), F32),
        ],
        scratch_shapes=[
            pltpu.VMEM((n_conv_cols, LANES), F32),
        ],
        compiler_params=pltpu.CompilerParams(
            dimension_semantics=("arbitrary",), vmem_limit_bytes=VMEM_LIMIT),
        name="in_proj",
    )(x2, w_all, conv_params, w_b, b_b, w_all, b_t, w_all, b_g, b_gt)


def _mlstm_chunk(c, qt_ref, k_ref, vt_ref, ot_ref, gcol_ref, grow_ref, nw_ref, out_ref, ct_ref, m_ref, nwb_ref):
    L = k_ref.shape[0]
    d = MLSTM_HEAD_DIM
    H = MLSTM_HEADS
    nt_dims = (((1,), (1,)), ((), ()))

    @pl.when(c == 0)
    def _():
        ct_ref[...] = jnp.zeros_like(ct_ref)
        m_ref[...] = jnp.zeros_like(m_ref)
        nwb_ref[...] = jnp.broadcast_to(nw_ref[...], nwb_ref.shape)

    key = lax.broadcasted_iota(jnp.int32, (L, L), 0)
    qry = lax.broadcasted_iota(jnp.int32, (L, L), 1)
    causal_t = key <= qry

    gdiff = gcol_ref[...]
    grow = grow_ref[...]
    ones = jnp.ones((MLSTM_AUG, L), BF16)

    def head(h):
        cols = slice(h * d, (h + 1) * d)
        q_t = qt_ref[cols, :]
        k = k_ref[:, cols]
        v_aug = jnp.concatenate([vt_ref[cols, :], ones], axis=0)
        b_r = grow[H + h:H + h + 1, :]
        i_r = grow[h:h + 1, :]
        g_c = gdiff[:, h:h + 1]
        m_prev = m_ref[h]
        ct = ct_ref[h]

        dmat = jnp.where(causal_t, b_r + g_c, -jnp.inf)
        a = b_r + m_prev
        m_row = jnp.maximum(a, jnp.max(dmat, axis=0, keepdims=True))
        w = jnp.exp(dmat - m_row)
        inter = jnp.exp(a - m_row)
        kq = jnp.dot(k, q_t, preferred_element_type=F32)
        sqk = (kq * w).astype(BF16)
        nd = inter * jnp.dot(ct.astype(BF16), q_t, preferred_element_type=F32) \
            + jnp.dot(v_aug, sqk, preferred_element_type=F32)
        rinv = 1.0 / jnp.maximum(jnp.abs(nd[d:d + 1, :]), jnp.exp(-m_row))
        hid = nd[0:d, :] * rinv * ot_ref[cols, :].astype(F32)
        mu = jnp.mean(hid, axis=0, keepdims=True)
        cen = hid - mu
        var = jnp.mean(cen * cen, axis=0, keepdims=True)
        out_ref[:, cols] = (cen * lax.rsqrt(var + LN_EPS) * nwb_ref[cols, :]).T.astype(BF16)

        b_last = b_r[:, L - 1:L]
        g_r = b_last - b_r + i_r
        m_new = jnp.maximum(b_last + m_prev, jnp.max(g_r, axis=-1, keepdims=True))
        decay = jnp.exp(b_last + m_prev - m_new)
        ws = jnp.exp(g_r - m_new)
        vw = (v_aug.astype(F32) * ws).astype(BF16)
        ct_ref[h] = decay * ct + jnp.dot(vw, k, preferred_element_type=F32)
        m_ref[h] = m_new

    return [functools.partial(head, h) for h in range(H)]


def _diff_attn_head(lam_init, interleaved, q_ref, k_ref, v_ref, bias_ref, lq1_ref, lk1_ref, lq2_ref, lk2_ref,
                    nw_ref, out_ref, qs_ref, vt_ref, s_ref, p_ref, acc_ref):
    T = ATT_T
    dh = DIFF_HEAD_DIM
    dv = DIFF_V_DIM
    nq = q_ref.shape[0] // T
    R = 2 * T

    ones = jnp.ones((ATT_SUM_ROWS, T), BF16)
    for jj in range(nq):
        vt_ref[jj, 0:dv, :] = v_ref[:, jj * T:(jj + 1) * T]
        vt_ref[jj, dv:dv + ATT_SUM_ROWS, :] = ones

    lane = lax.broadcasted_iota(jnp.int32, (T, dv), 1)
    for ii in range(nq):
        qt = q_ref[ii * T:(ii + 1) * T, :]
        zero = jnp.zeros_like(qt)
        qs_ref[ii, 0:T, :] = jnp.where(lane < dh, qt, zero)
        qs_ref[ii, T:R, :] = jnp.where(lane >= dh, qt, zero)

    lam = (jnp.exp(jnp.sum(lq1_ref[...] * lk1_ref[...], axis=-1, keepdims=True))
           - jnp.exp(jnp.sum(lq2_ref[...] * lk2_ref[...], axis=-1, keepdims=True)) + lam_init)

    pairs = [(i, j) for i in range(nq) for j in range(i + 1)]

    def scores(n):
        i, j = pairs[n]
        s = lax.dot_general(k_ref[j * T:(j + 1) * T, :], qs_ref[i], (((1,), (1,)), ((), ())),
                            preferred_element_type=F32)
        if j >= i - 1:
            bias = bias_ref[0, j - i + 1]
            s = s + jnp.concatenate([bias, bias], axis=1)
        s_ref[n % 2] = s
        return jnp.max(s, axis=0, keepdims=True)

    def softmax(n, m_tile, m_prev):
        i, j = pairs[n]
        m_new = m_tile if j == 0 else jnp.maximum(m_prev, m_tile)
        p_ref[n % 2] = jnp.exp2(s_ref[n % 2] - m_new).astype(BF16)
        alpha = None if j == 0 else jnp.exp2(m_prev - m_new)
        return m_new, alpha

    def values(n, alpha):
        i, j = pairs[n]
        pv = jnp.dot(vt_ref[j], p_ref[n % 2], preferred_element_type=F32)
        if j == 0:
            acc_ref[...] = pv
        else:
            acc_ref[...] = alpha * acc_ref[...] + pv
        if j == i:
            acc = acc_ref[...]
            out = acc[0:dv, :] / acc[dv:dv + 1, :]
            hd = out[:, 0:T] - lam * out[:, T:R]
            hd = hd * lax.rsqrt(jnp.mean(hd * hd, axis=0, keepdims=True) + LN_EPS) * nw_ref[...]
            out_ref[i * T:(i + 1) * T, :] = (hd * (1.0 - lam_init)).T.astype(BF16)

    spacing = INTERLEAVE_EVERY if interleaved else 0
    m_tiles = {0: scores(0)}
    m_run = None
    alphas = {}
    for n in range(len(pairs)):
        if n + 1 < len(pairs):
            m_tiles[n + 1] = scores(n + 1)
        m_run, alphas[n] = softmax(n, m_tiles.pop(n), m_run)
        if n >= 1:
            values(n - 1, alphas.pop(n - 1))
        if spacing and (n + 1) % spacing == 0 and (n + 1) // spacing <= len(interleaved):
            interleaved[(n + 1) // spacing - 1]()
    values(len(pairs) - 1, alphas.pop(len(pairs) - 1))


def _mixers_kernel(lam_init, chunks_per_seq, n_attn_refs, *refs):
    (q_ref, k_ref, v_ref, bias_ref, lq1_ref, lk1_ref, lq2_ref, lk2_ref, anw_ref) = refs[:n_attn_refs]
    (mq_ref, mk_ref, mvt_ref, mot_ref, gcol_ref, grow_ref, mnw_ref) = refs[n_attn_refs:n_attn_refs + 7]
    hd_ref, hm_ref = refs[n_attn_refs + 7:n_attn_refs + 9]
    (qs_ref, vt_ref, s_ref, p_ref, acc_ref, ct_ref, m_ref, nwb_ref) = refs[n_attn_refs + 9:]
    c = pl.program_id(0) % chunks_per_seq
    heads = _mlstm_chunk(c, mq_ref, mk_ref, mvt_ref, mot_ref, gcol_ref, grow_ref, mnw_ref, hm_ref,
                         ct_ref, m_ref, nwb_ref)
    _diff_attn_head(lam_init, heads, q_ref, k_ref, v_ref, bias_ref, lq1_ref, lk1_ref, lq2_ref, lk2_ref,
                    anw_ref, hd_ref, qs_ref, vt_ref, s_ref, p_ref, acc_ref)


def _mixers(proj, proj_t, gcol, grow, bias_tiles, lq1, lk1, lq2, lk2, dnorm_w, mnorm_w, lam_init, batch, seq):
    T = ATT_T
    nq = seq // T
    dv = DIFF_V_DIM
    L = MLSTM_L
    nc = seq // L
    W = MLSTM_WIDTH
    assert DIFF_HEADS * batch == batch * nc, "attention (head, batch) steps must pair 1:1 with mLSTM chunks"
    small = lambda shape: pl.BlockSpec(shape, lambda n: (0, 0))
    attn_specs = [
        pl.BlockSpec((seq, dv), lambda n: (n % batch, COL_DQ // dv + n // batch)),
        pl.BlockSpec((seq, dv), lambda n: (n % batch, COL_DK // dv + n // batch)),
        pl.BlockSpec((dv, seq), lambda n: (ROW_DV // dv + n // batch, n % batch)),
        pl.BlockSpec((1, 2, T, T), lambda n: (n // batch, 0, 0, 0)),
        small((1, DIFF_HEAD_DIM)), small((1, DIFF_HEAD_DIM)),
        small((1, DIFF_HEAD_DIM)), small((1, DIFF_HEAD_DIM)),
        small((dv, 1)),
    ]
    mlstm_specs = [
        pl.BlockSpec((W, L), lambda n: (ROW_MQ // W, n)),
        pl.BlockSpec((L, W), lambda n: (n, COL_MK // W)),
        pl.BlockSpec((W, L), lambda n: (ROW_MV // W, n)),
        pl.BlockSpec((W, L), lambda n: (ROW_MO // W, n)),
        pl.BlockSpec((L, GATE_PAD), lambda n: (n, 0)),
        pl.BlockSpec((GATE_ROWS, L), lambda n: (0, n)),
        small((W, 1)),
    ]
    return pl.pallas_call(
        functools.partial(_mixers_kernel, lam_init, nc, len(attn_specs)),
        grid=(batch * nc,),
        in_specs=attn_specs + mlstm_specs,
        out_specs=[
            pl.BlockSpec((seq, dv), lambda n: (n % batch, n // batch)),
            pl.BlockSpec((L, W), lambda n: (n, 0)),
        ],
        out_shape=[
            jax.ShapeDtypeStruct((batch * seq, DIFF_WIDTH), BF16),
            jax.ShapeDtypeStruct((batch * seq, W), BF16),
        ],
        scratch_shapes=[
            pltpu.VMEM((nq, 2 * T, dv), BF16),
            pltpu.VMEM((nq, dv + ATT_SUM_ROWS, T), BF16),
            pltpu.VMEM((2, T, 2 * T), F32),
            pltpu.VMEM((2, T, 2 * T), BF16),
            pltpu.VMEM((dv + ATT_SUM_ROWS, 2 * T), F32),
            pltpu.VMEM((MLSTM_HEADS, MLSTM_HEAD_DIM + MLSTM_AUG, MLSTM_HEAD_DIM), F32),
            pltpu.VMEM((MLSTM_HEADS, 1, 1), F32),
            pltpu.VMEM((W, L), F32),
        ],
        compiler_params=pltpu.CompilerParams(
            dimension_semantics=("arbitrary",), vmem_limit_bytes=VMEM_LIMIT),
        name="token_mixers",
    )(proj, proj, proj_t, bias_tiles, lq1, lk1, lq2, lk2, dnorm_w, proj_t, proj, proj_t, proj_t, gcol, grow, mnorm_w)


def _layer_norm(y, g, b):
    mu = jnp.mean(y, axis=-1, keepdims=True)
    cen = y - mu
    var = jnp.mean(cen * cen, axis=-1, keepdims=True)
    return cen * lax.rsqrt(var + LN_EPS) * g + b


def _merge_kernel(hm_ref, hd_ref, gm_ref, gd_ref, x_ref, wbm_ref, wbd_ref, wo_ref, g_ref, b_ref, out_ref):
    tm = hm_ref.shape[0]
    nb = tm // MERGE_ROWS
    halves = [slice(r * MERGE_ROWS, (r + 1) * MERGE_ROWS) for r in range(nb)]
    merged = []
    for rows in halves:
        pm = jnp.dot(hm_ref[rows, :], wbm_ref[...], preferred_element_type=F32)
        pd = jnp.dot(hd_ref[rows, :], wbd_ref[...], preferred_element_type=F32)
        merged.append((gm_ref[rows, :].astype(F32) * pm + gd_ref[rows, :].astype(F32) * pd).astype(BF16))
    for rows, m in zip(halves, merged):
        mix = jnp.dot(m, wo_ref[...], preferred_element_type=F32)
        out_ref[rows, :] = _layer_norm(DEEPNORM_ALPHA * x_ref[rows, :] + mix, g_ref[...], b_ref[...])


def _ffn_kernel(tiles_per_seq, h_ref, wu_ref, cw_ref, cb_ref, wd_ref, g_ref, b_ref, out_ref,
                conv_ref, halo_ref, act_ref):
    i = pl.program_id(0)
    tm = h_ref.shape[0]
    tc = FFN_TC

    @pl.when(i % tiles_per_seq == 0)
    def _():
        halo_ref[...] = jnp.zeros_like(halo_ref)

    h = h_ref[...]
    hb = h.astype(BF16)
    for c0 in range(0, D_FF, tc):
        cols = slice(c0, c0 + tc)
        a = jnp.dot(hb, wu_ref[:, cols], preferred_element_type=F32)
        gate = jnp.dot(hb, wu_ref[:, D_FF + c0:D_FF + c0 + tc], preferred_element_type=F32)
        conv_ref[0:HALO, cols] = halo_ref[:, cols]
        conv_ref[HALO:HALO + tm, cols] = a
        y = cb_ref[:, cols] + cw_ref[FFN_CONV - 1:FFN_CONV, cols] * a
        for kk in range(FFN_CONV - 1):
            off = HALO - (FFN_CONV - 1) + kk
            y = y + cw_ref[kk:kk + 1, cols] * conv_ref[off:off + tm, cols]
        halo_ref[:, cols] = a[tm - HALO:tm, :]
        act_ref[:, cols] = (y * _sigmoid(y) * gate).astype(BF16)

    ffn = jnp.dot(act_ref[...], wd_ref[...], preferred_element_type=F32)
    out_ref[...] = _layer_norm(DEEPNORM_ALPHA * h + ffn, g_ref[...], b_ref[...])


def _merge_ffn_kernel(tiles_per_seq, hm_ref, hd_ref, gm_ref, gd_ref, x_ref, wbm_ref, wbd_ref, wo_ref, g1_ref,
                      b1_ref, wu_ref, cw_ref, cb_ref, wd_ref, g2_ref, b2_ref, out_ref,
                      h_ref, conv_ref, halo_ref, act_ref):
    _merge_kernel(hm_ref, hd_ref, gm_ref, gd_ref, x_ref, wbm_ref, wbd_ref, wo_ref, g1_ref, b1_ref, h_ref)
    _ffn_kernel(tiles_per_seq, h_ref, wu_ref, cw_ref, cb_ref, wd_ref, g2_ref, b2_ref, out_ref,
                conv_ref, halo_ref, act_ref)


def _merge_ffn(hm, hd, proj, x2, w_bm, w_bd, w_o, ln1_g, ln1_b, w_up, conv_w, conv_b, w_down, ln2_g, ln2_b, seq):
    T = x2.shape[0]
    tm = FFN_TM
    D = D_MODEL
    rows = lambda col: pl.BlockSpec((tm, D), lambda i: (i, col))
    const = _const_spec
    return pl.pallas_call(
        functools.partial(_merge_ffn_kernel, seq // tm),
        grid=(T // tm,),
        in_specs=[
            rows(0), rows(0), rows(COL_GM // D), rows(COL_GD // D), rows(0),
            const((MLSTM_WIDTH, D)), const((DIFF_WIDTH, D)), const((D, D)), const((1, D)), const((1, D)),
            const((D, 2 * D_FF)), const((FFN_CONV, D_FF)), const((1, D_FF)), const((D_FF, D)),
            const((1, D)), const((1, D)),
        ],
        out_specs=rows(0),
        out_shape=jax.ShapeDtypeStruct((T, D), F32),
        scratch_shapes=[
            pltpu.VMEM((tm, D), F32),
            pltpu.VMEM((HALO + tm, D_FF), F32),
            pltpu.VMEM((HALO, D_FF), F32),
            pltpu.VMEM((tm, D_FF), BF16),
        ],
        compiler_params=pltpu.CompilerParams(
            dimension_semantics=("arbitrary",), vmem_limit_bytes=VMEM_LIMIT),
        name="merge_ffn",
    )(hm, hd, proj, proj, x2, w_bm, w_bd, w_o, ln1_g, ln1_b, w_up, conv_w, conv_b, w_down, ln2_g, ln2_b)


def _layer(h2, batch, seq, l, w_in, b_in, mconv_w, mconv_b, mnorm_w, lq1, lk1, lq2, lk2, dnorm_w, bias_tiles,
           w_bm, w_bd, w_o, ln1_g, ln1_b, w_up, fconv_w, fconv_b, w_down, ln2_g, ln2_b):
    lam_init = 0.8 - 0.6 * math.exp(-0.3 * l)
    n_m = 4 * MLSTM_WIDTH
    n_gate = 2 * MLSTM_HEADS
    n_qk = 2 * MLSTM_WIDTH
    w_all = w_in.T.astype(BF16)
    conv_params = jnp.concatenate(
        [mconv_w.astype(F32).T, mconv_b.astype(F32)[:, None], b_in[:n_qk].astype(F32)[:, None],
         jnp.zeros((n_qk, CONV_PARAMS - MLSTM_CONV - 2), F32)], axis=1)
    w_b = w_all[n_m + n_gate:]
    b_b = b_in[n_m + n_gate:][None, :].astype(F32)
    dv0 = n_m + n_gate + TAIL_DV
    b_t = jnp.concatenate([b_in[n_qk:n_m], b_in[dv0:dv0 + DIFF_WIDTH]])[:, None].astype(F32)
    b_g = b_in[n_m:n_m + GATE_PAD][None, :].astype(F32)
    b_gt = b_in[n_m:n_m + GATE_ROWS][:, None].astype(F32)

    proj, proj_t, gcol, grow = _in_proj(h2, w_all, conv_params, w_b, b_b, b_t, b_g, b_gt, seq)
    hd, hm = _mixers(proj, proj_t, gcol, grow, bias_tiles, lq1[None, :].astype(F32), lk1[None, :].astype(F32),
                     lq2[None, :].astype(F32), lk2[None, :].astype(F32), dnorm_w[:, None].astype(F32),
                     mnorm_w[:, None].astype(F32), lam_init, batch, seq)
    return _merge_ffn(hm, hd, proj, h2, w_bm.astype(BF16), w_bd.astype(BF16), w_o.astype(BF16),
                      ln1_g[None, :].astype(F32), ln1_b[None, :].astype(F32),
                      w_up.astype(BF16), fconv_w.astype(F32), fconv_b[None, :].astype(F32),
                      w_down.astype(BF16), ln2_g[None, :].astype(F32), ln2_b[None, :].astype(F32), seq)


def kernel(x, w_in, b_in, mlstm_conv_w, mlstm_conv_b, mlstm_norm_w, lambda_q1, lambda_k1, lambda_q2, lambda_k2,
           diff_norm_w, rel_bias, w_branch_mlstm, w_branch_diff, w_out, ln1_g, ln1_b, w_ffn_up, ffn_conv_w,
           ffn_conv_b, w_ffn_down, ln2_g, ln2_b):
    batch, seq, d_model = x.shape
    assert d_model == D_MODEL and seq % max(PROJ_TM, FFN_TM, ATT_T, MLSTM_L) == 0
    bias_tiles = _rel_bias_tiles(rel_bias)
    h2 = x.reshape(batch * seq, d_model)
    for l in range(w_in.shape[0]):
        h2 = _layer(h2, batch, seq, l, w_in[l], b_in[l], mlstm_conv_w[l], mlstm_conv_b[l], mlstm_norm_w[l],
                    lambda_q1[l], lambda_k1[l], lambda_q2[l], lambda_k2[l], diff_norm_w[l], bias_tiles,
                    w_branch_mlstm[l], w_branch_diff[l], w_out[l], ln1_g[l], ln1_b[l], w_ffn_up[l],
                    ffn_conv_w[l], ffn_conv_b[l], w_ffn_down[l], ln2_g[l], ln2_b[l])
    return h2.reshape(batch, seq, d_model).astype(x.dtype)
```

```python
import functools
import math

import jax
import jax.numpy as jnp
from jax import lax
from jax.experimental import pallas as pl
from jax.experimental.pallas import tpu as pltpu

F32 = jnp.float32
BF16 = jnp.bfloat16

D_MODEL = 1024
MLSTM_HEADS = 4
MLSTM_HEAD_DIM = 256
MLSTM_WIDTH = MLSTM_HEADS * MLSTM_HEAD_DIM
MLSTM_CONV = 4
DIFF_HEADS = 8
DIFF_HEAD_DIM = 64
DIFF_V_DIM = 2 * DIFF_HEAD_DIM
DIFF_WIDTH = DIFF_HEADS * DIFF_V_DIM
REL_BUCKETS = 32
REL_MAX_DIST = 128
D_FF = 2816
FFN_CONV = 3
DEPTH = 1
DEEPNORM_ALPHA = (2.0 * DEPTH) ** 0.25
LN_EPS = 1e-5
LOG2E = math.log2(math.e)

COL_MK = 0
COL_DQ = 1024
COL_DK = 2048
COL_GM = 3072
COL_GD = 4096
N_MAIN = 5120
ROW_MV = 0
ROW_MO = 1024
ROW_DV = 2048
ROW_MQ = 3072
N_TRANS = 4096
TAIL_DQ = 0
TAIL_DK = 1024
TAIL_DV = 2048
TAIL_GM = 3072
N_TAIL = 5120
GATE_PAD = 128
GATE_ROWS = 16
MLSTM_AUG = 16

HALO = 8
LANES = 128
CONV_PARAMS = 8
PROJ_TM = 512
PROJ_TN = 512
MLSTM_L = 256
ATT_T = 256
ATT_FAR_TILES = 1
ATT_SUM_ROWS = 16
INTERLEAVE_EVERY = 4
MERGE_ROWS = 256
FFN_TM = 512
FFN_TC = 256
VMEM_LIMIT = 58 * 1024 * 1024


def _sigmoid(v):
    return 1.0 / (1.0 + jnp.exp(-v))


def _log_sigmoid(v):
    return jnp.minimum(v, 0.0) - jnp.log(1.0 + jnp.exp(-jnp.abs(v)))


def _split3(v):
    hi = v.astype(BF16)
    rest = v - hi.astype(F32)
    mid = rest.astype(BF16)
    lo = (rest - mid.astype(F32)).astype(BF16)
    return hi, mid, lo


def _write_rel_rel_table(table_ref, h, out_ref):
    T = out_ref.shape[-1]
    max_exact = REL_BUCKETS // 2
    far = table_ref[REL_BUCKETS - 1, h]
    n = lax.broadcasted_iota(jnp.int32, (8, T), 1)
    nf = jnp.maximum(n, 1).astype(F32)
    large = max_exact + (jnp.log(nf / max_exact) / math.log(REL_MAX_DIST / max_exact)
                         * (REL_BUCKETS - max_exact)).astype(jnp.int32)
    large = jnp.minimum(large, REL_BUCKETS - 1)
    bucket = jnp.where(n < max_exact, n, large)
    by_dist = jnp.zeros((8, T), F32)
    for kk in range(REL_BUCKETS):
        by_dist = jnp.where(bucket == kk, table_ref[kk, h], by_dist)
    by_dist = (by_dist - far) * LOG2E
    rolled = pltpu.roll(jnp.broadcast_to(by_dist[0:1, :], (T, T)), 0, 1, stride=1, stride_axis=0)
    kpos = lax.broadcasted_iota(jnp.int32, (T, T), 0)
    qpos = lax.broadcasted_iota(jnp.int32, (T, T), 1)
    out_ref[0, 0] = jnp.where(qpos < kpos, rolled, 0.0)
    out_ref[0, 1] = jnp.where(qpos >= kpos, rolled, -jnp.inf)


def _in_proj_kernel(tiles_per_seq, x_ref, wa_ref, cp_ref, wb_ref, bb_ref, wt_ref, bt_ref, wg_ref, bg_ref,
                    bgt_ref, o_ref, ot_ref, gcol_ref, grow_ref, halo_ref):
    i = pl.program_id(0)
    tm = x_ref.shape[0]
    tn = PROJ_TN
    n_conv_cols = 2 * MLSTM_WIDTH
    nt_dims = (((1,), (1,)), ((), ()))

    @pl.when(i % tiles_per_seq == 0)
    def _():
        halo_ref[...] = jnp.zeros_like(halo_ref)

    xb = x_ref[...].astype(BF16)
    gcol = lax.dot_general(xb, wg_ref[...], nt_dims, preferred_element_type=F32) + bg_ref[...]
    grow = lax.dot_general(wg_ref[0:GATE_ROWS, :], xb, nt_dims,
                           preferred_element_type=F32) + bgt_ref[...]
    L = MLSTM_L
    H = MLSTM_HEADS
    tri = jnp.where(lax.broadcasted_iota(jnp.int32, (L, L), 0) >= lax.broadcasted_iota(jnp.int32, (L, L), 1),
                    1.0, 0.0).astype(BF16)
    gate_row = lax.broadcasted_iota(jnp.int32, (GATE_ROWS, L), 0)
    for r0 in range(0, tm, L):
        g_c = gcol[r0:r0 + L, :]
        g_r = grow[:, r0:r0 + L]
        bc3 = jnp.dot(tri, jnp.concatenate(_split3(_log_sigmoid(g_c)), axis=1), preferred_element_type=F32)
        bcol = bc3[:, 0:GATE_PAD] + bc3[:, GATE_PAD:2 * GATE_PAD] + bc3[:, 2 * GATE_PAD:3 * GATE_PAD]
        br3 = lax.dot_general(jnp.concatenate(_split3(_log_sigmoid(g_r)), axis=0), tri, nt_dims,
                              preferred_element_type=F32)
        brow = br3[0:GATE_ROWS] + br3[GATE_ROWS:2 * GATE_ROWS] + br3[2 * GATE_ROWS:3 * GATE_ROWS]
        gcol_ref[r0:r0 + L, :] = g_c - pltpu.roll(bcol, GATE_PAD - H, 1)
        grow_ref[:, r0:r0 + L] = jnp.where(gate_row < H, g_r, brow)

    for r0 in range(0, ROW_MQ, tn):
        rows = slice(r0, r0 + tn)
        if r0 < ROW_DV:
            w_rows = wt_ref[rows, :]
        else:
            w_rows = wb_ref[TAIL_DV + r0 - ROW_DV:TAIL_DV + r0 - ROW_DV + tn, :]
        acc_t = lax.dot_general(w_rows, xb, nt_dims, preferred_element_type=F32) + bt_ref[rows, :]
        if ROW_MO <= r0 < ROW_DV:
            acc_t = _sigmoid(acc_t)
        ot_ref[rows, :] = acc_t.astype(BF16)

    lane = lax.broadcasted_iota(jnp.int32, (tn, LANES), 1)
    for c0 in range(0, n_conv_cols, tn):
        cols = slice(c0, c0 + tn)
        par = cp_ref[cols, :]
        acc_t = lax.dot_general(wa_ref[cols, :], xb, nt_dims, preferred_element_type=F32) \
            + par[:, MLSTM_CONV + 1:MLSTM_CONV + 2]
        prev = halo_ref[cols, :]
        y = par[:, MLSTM_CONV:MLSTM_CONV + 1] + par[:, MLSTM_CONV - 1:MLSTM_CONV] * acc_t
        for kk in range(MLSTM_CONV - 1):
            shift = MLSTM_CONV - 1 - kk
            rolled = pltpu.roll(acc_t, shift, 1)
            head = jnp.where(lane < shift, pltpu.roll(prev, shift, 1), rolled[:, 0:LANES])
            y = y + par[:, kk:kk + 1] * jnp.concatenate([head, rolled[:, LANES:]], axis=1)
        halo_ref[cols, :] = acc_t[:, tm - LANES:tm]
        y = y * _sigmoid(y)
        if c0 < MLSTM_WIDTH:
            ot_ref[ROW_MQ + c0:ROW_MQ + c0 + tn, :] = y.astype(BF16)
        else:
            k0 = COL_MK + c0 - MLSTM_WIDTH
            o_ref[:, k0:k0 + tn] = (y * (MLSTM_HEAD_DIM ** -0.5)).T.astype(BF16)

    for c0 in range(COL_DQ, N_MAIN, tn):
        cols = slice(c0, c0 + tn)
        t0 = c0 - COL_DQ + TAIL_DQ if c0 < COL_GM else c0 - COL_GM + TAIL_GM
        wcols = slice(t0, t0 + tn)
        acc = lax.dot_general(xb, wb_ref[wcols, :], nt_dims, preferred_element_type=F32) + bb_ref[:, wcols]
        if c0 >= COL_GM:
            o_ref[:, cols] = _sigmoid(acc).astype(BF16)
        elif COL_DQ <= c0 < COL_DK:
            o_ref[:, cols] = (acc * (DIFF_HEAD_DIM ** -0.5 * LOG2E)).astype(BF16)
        else:
            o_ref[:, cols] = acc.astype(BF16)


def _const_spec(shape):
    return pl.BlockSpec(shape, lambda i: (0, 0), pipeline_mode=pl.Buffered(1))


def _in_proj(x2, w_all, conv_params, w_b, b_b, b_t, b_g, b_gt, seq):
    T = x2.shape[0]
    tm = PROJ_TM
    n_conv_cols = 2 * MLSTM_WIDTH
    n_vo = ROW_DV
    assert n_vo == n_conv_cols and (n_conv_cols + n_vo) % GATE_PAD == 0
    const = _const_spec
    row_block = lambda shape, idx: pl.BlockSpec(shape, lambda i: (idx, 0), pipeline_mode=pl.Buffered(1))
    return pl.pallas_call(
        functools.partial(_in_proj_kernel, seq // tm),
        grid=(T // tm,),
        in_specs=[
            pl.BlockSpec((tm, D_MODEL), lambda i: (i, 0)),
            row_block((n_conv_cols, D_MODEL), 0),
            const((n_conv_cols, CONV_PARAMS)),
            const((N_TAIL, D_MODEL)),
            const((1, N_TAIL)),
            row_block((n_vo, D_MODEL), 1),
            const((ROW_MQ, 1)),
            row_block((GATE_PAD, D_MODEL), (n_conv_cols + n_vo) // GATE_PAD),
            const((1, GATE_PAD)),
            const((GATE_ROWS, 1)),
        ],
        out_specs=[
            pl.BlockSpec((tm, N_MAIN), lambda i: (i, 0)),
            pl.BlockSpec((N_TRANS, tm), lambda i: (0, i)),
            pl.BlockSpec((tm, GATE_PAD), lambda i: (i, 0)),
            pl.BlockSpec((GATE_ROWS, tm), lambda i: (0, i)),
        ],
        out_shape=[
            jax.ShapeDtypeStruct((T, N_MAIN), BF16),
            jax.ShapeDtypeStruct((N_TRANS, T), BF16),
            jax.ShapeDtypeStruct((T, GATE_PAD), F32),
            jax.ShapeDtypeStruct((GATE_ROWS, T), F32),
        ],
        scratch_shapes=[
            pltpu.VMEM((n_conv_cols, LANES), F32),
        ],
        compiler_params=pltpu.CompilerParams(
            dimension_semantics=("arbitrary",), vmem_limit_bytes=VMEM_LIMIT),
        name="in_proj",
    )(x2, w_all, conv_params, w_b, b_b, w_all, b_t, w_all, b_g, b_gt)


def _mlstm_chunk(c, qt_ref, k_ref, vt_ref, ot_ref, gcol_ref, grow_ref, nw_ref, out_ref, ct_ref, m_ref, nwb_ref):
    L = k_ref.shape[0]
    d = MLSTM_HEAD_DIM
    H = MLSTM_HEADS
    nt_dims = (((1,), (1,)), ((), ()))

    @pl.when(c == 0)
    def _():
        ct_ref[...] = jnp.zeros_like(ct_ref)
        m_ref[...] = jnp.zeros_like(m_ref)
        nwb_ref[...] = jnp.broadcast_to(nw_ref[...], nwb_ref.shape)

    key = lax.broadcasted_iota(jnp.int32, (L, L), 0)
    qry = lax.broadcasted_iota(jnp.int32, (L, L), 1)
    causal_t = key <= qry

    gdiff = gcol_ref[...]
    grow = grow_ref[...]
    ones = jnp.ones((MLSTM_AUG, L), BF16)

    def head(h):
        cols = slice(h * d, (h + 1) * d)
        q_t = qt_ref[cols, :]
        k = k_ref[:, cols]
        v_aug = jnp.concatenate([vt_ref[cols, :], ones], axis=0)
        b_r = grow[H + h:H + h + 1, :]
        i_r = grow[h:h + 1, :]
        g_c = gdiff[:, h:h + 1]
        m_prev = m_ref[h]
        ct = ct_ref[h]

        dmat = jnp.where(causal_t, b_r + g_c, -jnp.inf)
        a = b_r + m_prev
        m_row = jnp.maximum(a, jnp.max(dmat, axis=0, keepdims=True))
        w = jnp.exp(dmat - m_row)
        inter = jnp.exp(a - m_row)
        kq = jnp.dot(k, q_t, preferred_element_type=F32)
        sqk = (kq * w).astype(BF16)
        nd = inter * jnp.dot(ct.astype(BF16), q_t, preferred_element_type=F32) \
            + jnp.dot(v_aug, sqk, preferred_element_type=F32)
        rinv = 1.0 / jnp.maximum(jnp.abs(nd[d:d + 1, :]), jnp.exp(-m_row))
        hid = nd[0:d, :] * rinv * ot_ref[cols, :].astype(F32)
        mu = jnp.mean(hid, axis=0, keepdims=True)
        cen = hid - mu
        var = jnp.mean(cen * cen, axis=0, keepdims=True)
        out_ref[:, cols] = (cen * lax.rsqrt(var + LN_EPS) * nwb_ref[cols, :]).T.astype(BF16)

        b_last = b_r[:, L - 1:L]
        g_r = b_last - b_r + i_r
        m_new = jnp.maximum(b_last + m_prev, jnp.max(g_r, axis=-1, keepdims=True))
        decay = jnp.exp(b_last + m_prev - m_new)
        ws = jnp.exp(g_r - m_new)
        vw = (v_aug.astype(F32) * ws).astype(BF16)
        ct_ref[h] = decay * ct + jnp.dot(vw, k, preferred_element_type=F32)
        m_ref[h] = m_new

    return [functools.partial(head, h) for h in range(H)]


def _diff_attn_head(lam_init, interleaved, q_ref, k_ref, v_ref, bias_ref, lq1_ref, lk1_ref, lq2_ref, lk2_ref,
                    nw_ref, out_ref, qs_ref, vt_ref, s_ref, p_ref, acc_ref):
    T = ATT_T
    dh = DIFF_HEAD_DIM
    dv = DIFF_V_DIM
    nq = q_ref.shape[0] // T
    R = 2 * T

    vt_ref[0:dv, :] = v_ref[...]
    vt_ref[dv:dv + ATT_SUM_ROWS, :] = jnp.ones((ATT_SUM_ROWS, v_ref.shape[1]), BF16)

    lane = lax.broadcasted_iota(jnp.int32, (T, dv), 1)
    for ii in range(nq):
        qt = q_ref[ii * T:(ii + 1) * T, :]
        zero = jnp.zeros_like(qt)
        qs_ref[ii, 0:T, :] = jnp.where(lane < dh, qt, zero)
        qs_ref[ii, T:R, :] = jnp.where(lane >= dh, qt, zero)

    lam = (jnp.exp(jnp.sum(lq1_ref[...] * lk1_ref[...], axis=-1, keepdims=True))
           - jnp.exp(jnp.sum(lq2_ref[...] * lk2_ref[...], axis=-1, keepdims=True)) + lam_init)

    pairs = []
    for i in range(nq):
        k0 = 0
        while k0 < (i - 1) * T:
            klen = min(ATT_FAR_TILES * T, (i - 1) * T - k0)
            pairs.append((i, k0, klen, None))
            k0 += klen
        if i >= 1:
            pairs.append((i, (i - 1) * T, T, 0))
        pairs.append((i, i * T, T, 1))

    def scores(n):
        i, k0, klen, bias_idx = pairs[n]
        s = lax.dot_general(k_ref[k0:k0 + klen, :], qs_ref[i], (((1,), (1,)), ((), ())),
                            preferred_element_type=F32)
        if bias_idx is not None:
            bias = bias_ref[0, bias_idx]
            s = s + jnp.concatenate([bias, bias], axis=1)
        s_ref[n % 2, 0:klen, :] = s
        return jnp.max(s, axis=0, keepdims=True)

    def softmax(n, m_tile, m_prev):
        i, k0, klen, _ = pairs[n]
        m_new = m_tile if k0 == 0 else jnp.maximum(m_prev, m_tile)
        p_ref[n % 2, 0:klen, :] = jnp.exp2(s_ref[n % 2, 0:klen, :] - m_new).astype(BF16)
        alpha = None if k0 == 0 else jnp.exp2(m_prev - m_new)
        return m_new, alpha

    def values(n, alpha):
        i, k0, klen, _ = pairs[n]
        pv = jnp.dot(vt_ref[:, k0:k0 + klen], p_ref[n % 2, 0:klen, :],
                     preferred_element_type=F32)
        if k0 == 0:
            acc_ref[...] = pv
        else:
            acc_ref[...] = alpha * acc_ref[...] + pv
        if k0 == i * T:
            acc = acc_ref[...]
            out = acc[0:dv, :] / acc[dv:dv + 1, :]
            hd = out[:, 0:T] - lam * out[:, T:R]
            hd = hd * lax.rsqrt(jnp.mean(hd * hd, axis=0, keepdims=True) + LN_EPS) * nw_ref[...]
            out_ref[i * T:(i + 1) * T, :] = (hd * (1.0 - lam_init)).T.astype(BF16)

    spacing = INTERLEAVE_EVERY if interleaved else 0
    m_tiles = {0: scores(0)}
    m_run = None
    alphas = {}
    for n in range(len(pairs)):
        if n + 1 < len(pairs):
            m_tiles[n + 1] = scores(n + 1)
        m_run, alphas[n] = softmax(n, m_tiles.pop(n), m_run)
        if n >= 1:
            values(n - 1, alphas.pop(n - 1))
        if spacing and (n + 1) % spacing == 0 and (n + 1) // spacing <= len(interleaved):
            interleaved[(n + 1) // spacing - 1]()
    values(len(pairs) - 1, alphas.pop(len(pairs) - 1))


def _mixers_kernel(lam_init, chunks_per_seq, batch, n_attn_refs, *refs):
    (q_ref, k_ref, v_ref, table_ref, lq1_ref, lk1_ref, lq2_ref, lk2_ref, anw_ref) = refs[:n_attn_refs]
    (mq_ref, mk_ref, mvt_ref, mot_ref, gcol_ref, grow_ref, mnw_ref) = refs[n_attn_refs:n_attn_refs + 7]
    hd_ref, hm_ref = refs[n_attn_refs + 7:n_attn_refs + 9]
    (qs_ref, vt_ref, s_ref, p_ref, acc_ref, bias_ref, ct_ref, m_ref, nwb_ref) = refs[n_attn_refs + 9:]

    @pl.when(pl.program_id(0) % batch == 0)
    def _():
        _write_rel_rel_table(table_ref, pl.program_id(0) // batch, bias_ref)

    c = pl.program_id(0) % chunks_per_seq
    heads = _mlstm_chunk(c, mq_ref, mk_ref, mvt_ref, mot_ref, gcol_ref, grow_ref, mnw_ref, hm_ref,
                         ct_ref, m_ref, nwb_ref)
    _diff_attn_head(lam_init, heads, q_ref, k_ref, v_ref, bias_ref, lq1_ref, lk1_ref, lq2_ref, lk2_ref,
                    anw_ref, hd_ref, qs_ref, vt_ref, s_ref, p_ref, acc_ref)


def _mixers(proj, proj_t, gcol, grow, rel_bias, lq1, lk1, lq2, lk2, dnorm_w, mnorm_w, lam_init, batch, seq):
    T = ATT_T
    nq = seq // T
    dv = DIFF_V_DIM
    L = MLSTM_L
    nc = seq // L
    W = MLSTM_WIDTH
    assert DIFF_HEADS * batch == batch * nc, "attention (head, batch) steps must pair 1:1 with mLSTM chunks"
    small = lambda shape: pl.BlockSpec(shape, lambda n: (0, 0))
    attn_specs = [
        pl.BlockSpec((seq, dv), lambda n: (n % batch, COL_DQ // dv + n // batch)),
        pl.BlockSpec((seq, dv), lambda n: (n % batch, COL_DK // dv + n // batch)),
        pl.BlockSpec((dv, seq), lambda n: (ROW_DV // dv + n // batch, n % batch)),
        pl.BlockSpec(memory_space=pltpu.SMEM),
        small((1, DIFF_HEAD_DIM)), small((1, DIFF_HEAD_DIM)),
        small((1, DIFF_HEAD_DIM)), small((1, DIFF_HEAD_DIM)),
        small((dv, 1)),
    ]
    mlstm_specs = [
        pl.BlockSpec((W, L), lambda n: (ROW_MQ // W, n)),
        pl.BlockSpec((L, W), lambda n: (n, COL_MK // W)),
        pl.BlockSpec((W, L), lambda n: (ROW_MV // W, n)),
        pl.BlockSpec((W, L), lambda n: (ROW_MO // W, n)),
        pl.BlockSpec((L, GATE_PAD), lambda n: (n, 0)),
        pl.BlockSpec((GATE_ROWS, L), lambda n: (0, n)),
        small((W, 1)),
    ]
    return pl.pallas_call(
        functools.partial(_mixers_kernel, lam_init, nc, batch, len(attn_specs)),
        grid=(batch * nc,),
        in_specs=attn_specs + mlstm_specs,
        out_specs=[
            pl.BlockSpec((seq, dv), lambda n: (n % batch, n // batch)),
            pl.BlockSpec((L, W), lambda n: (n, 0)),
        ],
        out_shape=[
            jax.ShapeDtypeStruct((batch * seq, DIFF_WIDTH), BF16),
            jax.ShapeDtypeStruct((batch * seq, W), BF16),
        ],
        scratch_shapes=[
            pltpu.VMEM((nq, 2 * T, dv), BF16),
            pltpu.VMEM((dv + ATT_SUM_ROWS, seq), BF16),
            pltpu.VMEM((2, ATT_FAR_TILES * T, 2 * T), F32),
            pltpu.VMEM((2, ATT_FAR_TILES * T, 2 * T), BF16),
            pltpu.VMEM((dv + ATT_SUM_ROWS, 2 * T), F32),
            pltpu.VMEM((1, 2, T, T), F32),
            pltpu.VMEM((MLSTM_HEADS, MLSTM_HEAD_DIM + MLSTM_AUG, MLSTM_HEAD_DIM), F32),
            pltpu.VMEM((MLSTM_HEADS, 1, 1), F32),
            pltpu.VMEM((W, L), F32),
        ],
        compiler_params=pltpu.CompilerParams(
            dimension_semantics=("arbitrary",), vmem_limit_bytes=VMEM_LIMIT),
        name="token_mixers",
    )(proj, proj, proj_t, rel_bias, lq1, lk1, lq2, lk2, dnorm_w, proj_t, proj, proj_t, proj_t, gcol, grow, mnorm_w)


def _layer_norm(y, g, b):
    mu = jnp.mean(y, axis=-1, keepdims=True)
    cen = y - mu
    var = jnp.mean(cen * cen, axis=-1, keepdims=True)
    return cen * lax.rsqrt(var + LN_EPS) * g + b


def _merge_kernel(hm_ref, hd_ref, gm_ref, gd_ref, x_ref, wbm_ref, wbd_ref, wo_ref, g_ref, b_ref, out_ref):
    tm = hm_ref.shape[0]
    nb = tm // MERGE_ROWS
    halves = [slice(r * MERGE_ROWS, (r + 1) * MERGE_ROWS) for r in range(nb)]
    merged = []
    for rows in halves:
        pm = jnp.dot(hm_ref[rows, :], wbm_ref[...], preferred_element_type=F32)
        pd = jnp.dot(hd_ref[rows, :], wbd_ref[...], preferred_element_type=F32)
        merged.append((gm_ref[rows, :].astype(F32) * pm + gd_ref[rows, :].astype(F32) * pd).astype(BF16))
    for rows, m in zip(halves, merged):
        mix = jnp.dot(m, wo_ref[...], preferred_element_type=F32)
        out_ref[rows, :] = _layer_norm(DEEPNORM_ALPHA * x_ref[rows, :] + mix, g_ref[...], b_ref[...])


def _ffn_kernel(tiles_per_seq, h_ref, wu_ref, cw_ref, cb_ref, wd_ref, g_ref, b_ref, out_ref,
                conv_ref, halo_ref, act_ref):
    i = pl.program_id(0)
    tm = h_ref.shape[0]
    tc = FFN_TC

    @pl.when(i % tiles_per_seq == 0)
    def _():
        halo_ref[...] = jnp.zeros_like(halo_ref)

    h = h_ref[...]
    hb = h.astype(BF16)
    for c0 in range(0, D_FF, tc):
        cols = slice(c0, c0 + tc)
        a = jnp.dot(hb, wu_ref[:, cols], preferred_element_type=F32)
        gate = jnp.dot(hb, wu_ref[:, D_FF + c0:D_FF + c0 + tc], preferred_element_type=F32)
        conv_ref[0:HALO, cols] = halo_ref[:, cols]
        conv_ref[HALO:HALO + tm, cols] = a
        y = cb_ref[:, cols] + cw_ref[FFN_CONV - 1:FFN_CONV, cols] * a
        for kk in range(FFN_CONV - 1):
            off = HALO - (FFN_CONV - 1) + kk
            y = y + cw_ref[kk:kk + 1, cols] * conv_ref[off:off + tm, cols]
        halo_ref[:, cols] = a[tm - HALO:tm, :]
        act_ref[:, cols] = (y * _sigmoid(y) * gate).astype(BF16)

    for r0 in range(0, tm, tm // 2):
        rows = slice(r0, r0 + tm // 2)
        ffn = jnp.dot(act_ref[rows, :], wd_ref[...], preferred_element_type=F32)
        out_ref[rows, :] = _layer_norm(DEEPNORM_ALPHA * h[rows, :] + ffn, g_ref[...], b_ref[...])


def _merge_ffn_kernel(tiles_per_seq, hm_ref, hd_ref, gm_ref, gd_ref, x_ref, wbm_ref, wbd_ref, wo_ref, g1_ref,
                      b1_ref, wu_ref, cw_ref, cb_ref, wd_ref, g2_ref, b2_ref, out_ref,
                      h_ref, conv_ref, halo_ref, act_ref):
    _merge_kernel(hm_ref, hd_ref, gm_ref, gd_ref, x_ref, wbm_ref, wbd_ref, wo_ref, g1_ref, b1_ref, h_ref)
    _ffn_kernel(tiles_per_seq, h_ref, wu_ref, cw_ref, cb_ref, wd_ref, g2_ref, b2_ref, out_ref,
                conv_ref, halo_ref, act_ref)


def _merge_ffn(hm, hd, proj, x2, w_bm, w_bd, w_o, ln1_g, ln1_b, w_up, conv_w, conv_b, w_down, ln2_g, ln2_b, seq):
    T = x2.shape[0]
    tm = FFN_TM
    D = D_MODEL
    rows = lambda col: pl.BlockSpec((tm, D), lambda i: (i, col))
    const = _const_spec
    return pl.pallas_call(
        functools.partial(_merge_ffn_kernel, seq // tm),
        grid=(T // tm,),
        in_specs=[
            rows(0), rows(0), rows(COL_GM // D), rows(COL_GD // D), rows(0),
            const((MLSTM_WIDTH, D)), const((DIFF_WIDTH, D)), const((D, D)), const((1, D)), const((1, D)),
            const((D, 2 * D_FF)), const((FFN_CONV, D_FF)), const((1, D_FF)), const((D_FF, D)),
            const((1, D)), const((1, D)),
        ],
        out_specs=rows(0),
        out_shape=jax.ShapeDtypeStruct((T, D), F32),
        scratch_shapes=[
            pltpu.VMEM((tm, D), F32),
            pltpu.VMEM((HALO + tm, D_FF), F32),
            pltpu.VMEM((HALO, D_FF), F32),
            pltpu.VMEM((tm, D_FF), BF16),
        ],
        compiler_params=pltpu.CompilerParams(
            dimension_semantics=("arbitrary",), vmem_limit_bytes=VMEM_LIMIT),
        name="merge_ffn",
    )(hm, hd, proj, proj, x2, w_bm, w_bd, w_o, ln1_g, ln1_b, w_up, conv_w, conv_b, w_down, ln2_g, ln2_b)


def _layer(h2, batch, seq, l, w_in, b_in, mconv_w, mconv_b, mnorm_w, lq1, lk1, lq2, lk2, dnorm_w, rel_table,
           w_bm, w_bd, w_o, ln1_g, ln1_b, w_up, fconv_w, fconv_b, w_down, ln2_g, ln2_b):
    lam_init = 0.8 - 0.6 * math.exp(-0.3 * l)
    n_m = 4 * MLSTM_WIDTH
    n_gate = 2 * MLSTM_HEADS
    n_qk = 2 * MLSTM_WIDTH
    w_all = w_in.T.astype(BF16)
    conv_params = jnp.concatenate(
        [mconv_w.astype(F32).T, mconv_b.astype(F32)[:, None], b_in[:n_qk].astype(F32)[:, None],
         jnp.zeros((n_qk, CONV_PARAMS - MLSTM_CONV - 2), F32)], axis=1)
    w_b = w_all[n_m + n_gate:]
    b_b = b_in[n_m + n_gate:][None, :].astype(F32)
    dv0 = n_m + n_gate + TAIL_DV
    b_t = jnp.concatenate([b_in[n_qk:n_m], b_in[dv0:dv0 + DIFF_WIDTH]])[:, None].astype(F32)
    b_g = b_in[n_m:n_m + GATE_PAD][None, :].astype(F32)
    b_gt = b_in[n_m:n_m + GATE_ROWS][:, None].astype(F32)

    proj, proj_t, gcol, grow = _in_proj(h2, w_all, conv_params, w_b, b_b, b_t, b_g, b_gt, seq)
    hd, hm = _mixers(proj, proj_t, gcol, grow, rel_table, lq1[None, :].astype(F32), lk1[None, :].astype(F32),
                     lq2[None, :].astype(F32), lk2[None, :].astype(F32), dnorm_w[:, None].astype(F32),
                     mnorm_w[:, None].astype(F32), lam_init, batch, seq)
    return _merge_ffn(hm, hd, proj, h2, w_bm.astype(BF16), w_bd.astype(BF16), w_o.astype(BF16),
                      ln1_g[None, :].astype(F32), ln1_b[None, :].astype(F32),
                      w_up.astype(BF16), fconv_w.astype(F32), fconv_b[None, :].astype(F32),
                      w_down.astype(BF16), ln2_g[None, :].astype(F32), ln2_b[None, :].astype(F32), seq)


def kernel(x, w_in, b_in, mlstm_conv_w, mlstm_conv_b, mlstm_norm_w, lambda_q1, lambda_k1, lambda_q2, lambda_k2,
           diff_norm_w, rel_bias, w_branch_mlstm, w_branch_diff, w_out, ln1_g, ln1_b, w_ffn_up, ffn_conv_w,
           ffn_conv_b, w_ffn_down, ln2_g, ln2_b):
    batch, seq, d_model = x.shape
    assert d_model == D_MODEL and seq % max(PROJ_TM, FFN_TM, ATT_T, MLSTM_L) == 0
    rel_table = rel_bias.astype(F32)
    h2 = x.reshape(batch * seq, d_model)
    for l in range(w_in.shape[0]):
        h2 = _layer(h2, batch, seq, l, w_in[l], b_in[l], mlstm_conv_w[l], mlstm_conv_b[l], mlstm_norm_w[l],
                    lambda_q1[l], lambda_k1[l], lambda_q2[l], lambda_k2[l], diff_norm_w[l], rel_table,
                    w_branch_mlstm[l], w_branch_diff[l], w_out[l], ln1_g[l], ln1_b[l], w_ffn_up[l],
                    ffn_conv_w[l], ffn_conv_b[l], w_ffn_down[l], ln2_g[l], ln2_b[l])
    return h2.reshape(batch, seq, d_model).astype(x.dtype)
```

```python
import functools
import math

import jax
import jax.numpy as jnp
from jax import lax
from jax.experimental import pallas as pl
from jax.experimental.pallas import tpu as pltpu

F32 = jnp.float32
BF16 = jnp.bfloat16

D_MODEL = 1024
MLSTM_HEADS = 4
MLSTM_HEAD_DIM = 256
MLSTM_WIDTH = MLSTM_HEADS * MLSTM_HEAD_DIM
MLSTM_CONV = 4
DIFF_HEADS = 8
DIFF_HEAD_DIM = 64
DIFF_V_DIM = 2 * DIFF_HEAD_DIM
DIFF_WIDTH = DIFF_HEADS * DIFF_V_DIM
REL_BUCKETS = 32
REL_MAX_DIST = 128
D_FF = 2816
FFN_CONV = 3
DEPTH = 1
DEEPNORM_ALPHA = (2.0 * DEPTH) ** 0.25
LN_EPS = 1e-5
LOG2E = math.log2(math.e)

COL_MK = 0
COL_DQ = 1024
COL_DK = 2048
COL_GM = 3072
COL_GD = 4096
N_MAIN = 5120
ROW_MV = 0
ROW_MO = 1024
ROW_MQ = 2048
ROW_DV = 3072
N_TRANS = 4096
TAIL_DQ = 0
TAIL_DK = 1024
TAIL_DV = 2048
TAIL_GM = 3072
N_TAIL = 5120
GATE_PAD = 128
GATE_ROWS = 16
MLSTM_AUG = 16

HALO = 8
LANES = 128
CONV_PARAMS = 8
PROJ_TM = 512
PROJ_TN = 512
MLSTM_L = 256
ATT_T = 256
ATT_FAR_TILES = 1
ATT_SUM_ROWS = 16
INTERLEAVE_EVERY = 4
MERGE_ROWS = 256
FFN_TM = 512
FFN_TC = 256
VMEM_LIMIT = 58 * 1024 * 1024


def _sigmoid(v):
    return 1.0 / (1.0 + jnp.exp(-v))


def _log_sigmoid(v):
    return jnp.minimum(v, 0.0) - jnp.log(1.0 + jnp.exp(-jnp.abs(v)))


def _split3(v):
    hi = v.astype(BF16)
    rest = v - hi.astype(F32)
    mid = rest.astype(BF16)
    lo = (rest - mid.astype(F32)).astype(BF16)
    return hi, mid, lo


def _rel_bias_kernel(table_ref, out_ref):
    h = pl.program_id(0)
    T = out_ref.shape[-1]
    max_exact = REL_BUCKETS // 2
    far = table_ref[REL_BUCKETS - 1, h]
    n = lax.broadcasted_iota(jnp.int32, (8, T), 1)
    nf = jnp.maximum(n, 1).astype(F32)
    large = max_exact + (jnp.log(nf / max_exact) / math.log(REL_MAX_DIST / max_exact)
                         * (REL_BUCKETS - max_exact)).astype(jnp.int32)
    large = jnp.minimum(large, REL_BUCKETS - 1)
    bucket = jnp.where(n < max_exact, n, large)
    by_dist = jnp.zeros((8, T), F32)
    for kk in range(REL_BUCKETS):
        by_dist = jnp.where(bucket == kk, table_ref[kk, h], by_dist)
    by_dist = (by_dist - far) * LOG2E
    rolled = pltpu.roll(jnp.broadcast_to(by_dist[0:1, :], (T, T)), 0, 1, stride=1, stride_axis=0)
    kpos = lax.broadcasted_iota(jnp.int32, (T, T), 0)
    qpos = lax.broadcasted_iota(jnp.int32, (T, T), 1)
    out_ref[0, 0] = jnp.where(qpos < kpos, rolled, 0.0)
    out_ref[0, 1] = jnp.where(qpos >= kpos, rolled, -jnp.inf)


def _rel_bias_tiles(rel_bias):
    T = ATT_T
    return pl.pallas_call(
        _rel_bias_kernel,
        grid=(DIFF_HEADS,),
        in_specs=[pl.BlockSpec(memory_space=pltpu.SMEM)],
        out_specs=pl.BlockSpec((1, 2, T, T), lambda h: (h, 0, 0, 0)),
        out_shape=jax.ShapeDtypeStruct((DIFF_HEADS, 2, T, T), F32),
        name="rel_bias_tiles",
    )(rel_bias.astype(F32))


def _in_proj_kernel(tiles_per_seq, x_ref, wa_ref, cp_ref, wb_ref, bb_ref, wt_ref, bt_ref, wg_ref, bg_ref,
                    bgt_ref, o_ref, ot_ref, gcol_ref, grow_ref, halo_ref):
    i = pl.program_id(0)
    tm = x_ref.shape[0]
    tn = PROJ_TN
    n_conv_cols = 2 * MLSTM_WIDTH
    nt_dims = (((1,), (1,)), ((), ()))

    @pl.when(i % tiles_per_seq == 0)
    def _():
        halo_ref[...] = jnp.zeros_like(halo_ref)

    xb = x_ref[...].astype(BF16)
    gcol = lax.dot_general(xb, wg_ref[...], nt_dims, preferred_element_type=F32) + bg_ref[...]
    grow = lax.dot_general(wg_ref[0:GATE_ROWS, :], xb, nt_dims,
                           preferred_element_type=F32) + bgt_ref[...]
    L = MLSTM_L
    H = MLSTM_HEADS
    tri = jnp.where(lax.broadcasted_iota(jnp.int32, (L, L), 0) >= lax.broadcasted_iota(jnp.int32, (L, L), 1),
                    1.0, 0.0).astype(BF16)
    gate_row = lax.broadcasted_iota(jnp.int32, (GATE_ROWS, L), 0)
    for r0 in range(0, tm, L):
        g_c = gcol[r0:r0 + L, :]
        g_r = grow[:, r0:r0 + L]
        bc3 = jnp.dot(tri, jnp.concatenate(_split3(_log_sigmoid(g_c)), axis=1), preferred_element_type=F32)
        bcol = bc3[:, 0:GATE_PAD] + bc3[:, GATE_PAD:2 * GATE_PAD] + bc3[:, 2 * GATE_PAD:3 * GATE_PAD]
        br3 = lax.dot_general(jnp.concatenate(_split3(_log_sigmoid(g_r)), axis=0), tri, nt_dims,
                              preferred_element_type=F32)
        brow = br3[0:GATE_ROWS] + br3[GATE_ROWS:2 * GATE_ROWS] + br3[2 * GATE_ROWS:3 * GATE_ROWS]
        gcol_ref[r0:r0 + L, :] = g_c - pltpu.roll(bcol, GATE_PAD - H, 1)
        grow_ref[:, r0:r0 + L] = jnp.where(gate_row < H, g_r, brow)

    n_vo = 2 * MLSTM_WIDTH
    for r0 in list(range(ROW_MV, ROW_MV + n_vo, tn)) + list(range(ROW_DV, ROW_DV + DIFF_WIDTH, tn)):
        if r0 < ROW_DV:
            src = slice(r0 - ROW_MV, r0 - ROW_MV + tn)
            w_rows, bias_col = wt_ref[src, :], bt_ref[src, :]
        else:
            w_rows = wb_ref[TAIL_DV + r0 - ROW_DV:TAIL_DV + r0 - ROW_DV + tn, :]
            bias_col = bt_ref[n_vo + r0 - ROW_DV:n_vo + r0 - ROW_DV + tn, :]
        acc_t = lax.dot_general(w_rows, xb, nt_dims, preferred_element_type=F32) + bias_col
        if ROW_MO <= r0 < ROW_MO + MLSTM_WIDTH:
            acc_t = _sigmoid(acc_t)
        ot_ref[r0:r0 + tn, :] = acc_t.astype(BF16)

    lane = lax.broadcasted_iota(jnp.int32, (tn, LANES), 1)
    for c0 in range(0, n_conv_cols, tn):
        cols = slice(c0, c0 + tn)
        par = cp_ref[cols, :]
        acc_t = lax.dot_general(wa_ref[cols, :], xb, nt_dims, preferred_element_type=F32) \
            + par[:, MLSTM_CONV + 1:MLSTM_CONV + 2]
        prev = halo_ref[cols, :]
        y = par[:, MLSTM_CONV:MLSTM_CONV + 1] + par[:, MLSTM_CONV - 1:MLSTM_CONV] * acc_t
        for kk in range(MLSTM_CONV - 1):
            shift = MLSTM_CONV - 1 - kk
            rolled = pltpu.roll(acc_t, shift, 1)
            head = jnp.where(lane < shift, pltpu.roll(prev, shift, 1), rolled[:, 0:LANES])
            y = y + par[:, kk:kk + 1] * jnp.concatenate([head, rolled[:, LANES:]], axis=1)
        halo_ref[cols, :] = acc_t[:, tm - LANES:tm]
        y = y * _sigmoid(y)
        if c0 < MLSTM_WIDTH:
            ot_ref[ROW_MQ + c0:ROW_MQ + c0 + tn, :] = y.astype(BF16)
        else:
            k0 = COL_MK + c0 - MLSTM_WIDTH
            o_ref[:, k0:k0 + tn] = (y * (MLSTM_HEAD_DIM ** -0.5)).T.astype(BF16)

    for c0 in range(COL_DQ, N_MAIN, tn):
        cols = slice(c0, c0 + tn)
        t0 = c0 - COL_DQ + TAIL_DQ if c0 < COL_GM else c0 - COL_GM + TAIL_GM
        wcols = slice(t0, t0 + tn)
        acc = lax.dot_general(xb, wb_ref[wcols, :], nt_dims, preferred_element_type=F32) + bb_ref[:, wcols]
        if c0 >= COL_GM:
            o_ref[:, cols] = _sigmoid(acc).astype(BF16)
        elif COL_DQ <= c0 < COL_DK:
            o_ref[:, cols] = (acc * (DIFF_HEAD_DIM ** -0.5 * LOG2E)).astype(BF16)
        else:
            o_ref[:, cols] = acc.astype(BF16)


def _const_spec(shape):
    return pl.BlockSpec(shape, lambda i: (0, 0), pipeline_mode=pl.Buffered(1))


def _in_proj(x2, w_all, conv_params, w_b, b_b, b_t, b_g, b_gt, seq):
    T = x2.shape[0]
    tm = PROJ_TM
    n_conv_cols = 2 * MLSTM_WIDTH
    n_vo = 2 * MLSTM_WIDTH
    assert n_vo == n_conv_cols and (n_conv_cols + n_vo) % GATE_PAD == 0
    const = _const_spec
    row_block = lambda shape, idx: pl.BlockSpec(shape, lambda i: (idx, 0), pipeline_mode=pl.Buffered(1))
    return pl.pallas_call(
        functools.partial(_in_proj_kernel, seq // tm),
        grid=(T // tm,),
        in_specs=[
            pl.BlockSpec((tm, D_MODEL), lambda i: (i, 0)),
            row_block((n_conv_cols, D_MODEL), 0),
            const((n_conv_cols, CONV_PARAMS)),
            const((N_TAIL, D_MODEL)),
            const((1, N_TAIL)),
            row_block((n_vo, D_MODEL), 1),
            const((n_vo + DIFF_WIDTH, 1)),
            row_block((GATE_PAD, D_MODEL), (n_conv_cols + n_vo) // GATE_PAD),
            const((1, GATE_PAD)),
            const((GATE_ROWS, 1)),
        ],
        out_specs=[
            pl.BlockSpec((tm, N_MAIN), lambda i: (i, 0)),
            pl.BlockSpec((N_TRANS, tm), lambda i: (0, i)),
            pl.BlockSpec((tm, GATE_PAD), lambda i: (i, 0)),
            pl.BlockSpec((GATE_ROWS, tm), lambda i: (0, i)),
        ],
        out_shape=[
            jax.ShapeDtypeStruct((T, N_MAIN), BF16),
            jax.ShapeDtypeStruct((N_TRANS, T), BF16),
            jax.ShapeDtypeStruct((T, GATE_PAD), F32),
            jax.ShapeDtypeStruct((GATE_ROWS, T), F32),
        ],
        scratch_shapes=[
            pltpu.VMEM((n_conv_cols, LANES), F32),
        ],
        compiler_params=pltpu.CompilerParams(
            dimension_semantics=("arbitrary",), vmem_limit_bytes=VMEM_LIMIT),
        name="in_proj",
    )(x2, w_all, conv_params, w_b, b_b, w_all, b_t, w_all, b_g, b_gt)


def _mlstm_chunk(c, qt_ref, k_ref, vt_ref, ot_ref, gcol_ref, grow_ref, nw_ref, out_ref, ct_ref, m_ref, nwb_ref):
    L = k_ref.shape[0]
    d = MLSTM_HEAD_DIM
    H = MLSTM_HEADS
    nt_dims = (((1,), (1,)), ((), ()))

    @pl.when(c == 0)
    def _():
        ct_ref[...] = jnp.zeros_like(ct_ref)
        m_ref[...] = jnp.zeros_like(m_ref)
        nwb_ref[...] = jnp.broadcast_to(nw_ref[...], nwb_ref.shape)

    key = lax.broadcasted_iota(jnp.int32, (L, L), 0)
    qry = lax.broadcasted_iota(jnp.int32, (L, L), 1)
    causal_t = key <= qry

    gdiff = gcol_ref[...]
    grow = grow_ref[...]
    ones = jnp.ones((MLSTM_AUG, L), BF16)

    def head(h):
        cols = slice(h * d, (h + 1) * d)
        q_t = qt_ref[cols, :]
        k = k_ref[:, cols]
        v_aug = jnp.concatenate([vt_ref[cols, :], ones], axis=0)
        b_r = grow[H + h:H + h + 1, :]
        i_r = grow[h:h + 1, :]
        g_c = gdiff[:, h:h + 1]
        m_prev = m_ref[h]
        ct = ct_ref[h]

        dmat = jnp.where(causal_t, b_r + g_c, -jnp.inf)
        a = b_r + m_prev
        m_row = jnp.maximum(a, jnp.max(dmat, axis=0, keepdims=True))
        w = jnp.exp(dmat - m_row)
        inter = jnp.exp(a - m_row)
        kq = jnp.dot(k, q_t, preferred_element_type=F32)
        sqk = (kq * w).astype(BF16)
        nd = inter * jnp.dot(ct.astype(BF16), q_t, preferred_element_type=F32) \
            + jnp.dot(v_aug, sqk, preferred_element_type=F32)
        rinv = 1.0 / jnp.maximum(jnp.abs(nd[d:d + 1, :]), jnp.exp(-m_row))
        hid = nd[0:d, :] * rinv * ot_ref[cols, :].astype(F32)
        mu = jnp.mean(hid, axis=0, keepdims=True)
        cen = hid - mu
        var = jnp.mean(cen * cen, axis=0, keepdims=True)
        out_ref[:, cols] = (cen * lax.rsqrt(var + LN_EPS) * nwb_ref[cols, :]).T.astype(BF16)

        b_last = b_r[:, L - 1:L]
        g_r = b_last - b_r + i_r
        m_new = jnp.maximum(b_last + m_prev, jnp.max(g_r, axis=-1, keepdims=True))
        decay = jnp.exp(b_last + m_prev - m_new)
        ws = jnp.exp(g_r - m_new)
        vw = (v_aug.astype(F32) * ws).astype(BF16)
        ct_ref[h] = decay * ct + jnp.dot(vw, k, preferred_element_type=F32)
        m_ref[h] = m_new

    return [functools.partial(head, h) for h in range(H)]


def _diff_attn_head(lam_init, interleaved, q_ref, k_ref, v_ref, bias_ref, lq1_ref, lk1_ref, lq2_ref, lk2_ref,
                    nw_ref, out_ref, qs_ref, vt_ref, s_ref, p_ref, acc_ref):
    T = ATT_T
    dh = DIFF_HEAD_DIM
    dv = DIFF_V_DIM
    nq = q_ref.shape[0] // T
    R = 2 * T

    vt_ref[0:dv, :] = v_ref[...]
    vt_ref[dv:dv + ATT_SUM_ROWS, :] = jnp.ones((ATT_SUM_ROWS, v_ref.shape[1]), BF16)

    lane = lax.broadcasted_iota(jnp.int32, (T, dv), 1)
    for ii in range(nq):
        qt = q_ref[ii * T:(ii + 1) * T, :]
        zero = jnp.zeros_like(qt)
        qs_ref[ii, 0:T, :] = jnp.where(lane < dh, qt, zero)
        qs_ref[ii, T:R, :] = jnp.where(lane >= dh, qt, zero)

    lam = (jnp.exp(jnp.sum(lq1_ref[...] * lk1_ref[...], axis=-1, keepdims=True))
           - jnp.exp(jnp.sum(lq2_ref[...] * lk2_ref[...], axis=-1, keepdims=True)) + lam_init)

    pairs = []
    for i in range(nq):
        k0 = 0
        while k0 < (i - 1) * T:
            klen = min(ATT_FAR_TILES * T, (i - 1) * T - k0)
            pairs.append((i, k0, klen, None))
            k0 += klen
        if i >= 1:
            pairs.append((i, (i - 1) * T, T, 0))
        pairs.append((i, i * T, T, 1))

    def scores(n):
        i, k0, klen, bias_idx = pairs[n]
        s = lax.dot_general(k_ref[k0:k0 + klen, :], qs_ref[i], (((1,), (1,)), ((), ())),
                            preferred_element_type=F32)
        if bias_idx is not None:
            bias = bias_ref[0, bias_idx]
            s = s + jnp.concatenate([bias, bias], axis=1)
        s_ref[n % 2, 0:klen, :] = s
        return jnp.max(s, axis=0, keepdims=True)

    def softmax(n, m_tile, m_prev):
        i, k0, klen, _ = pairs[n]
        m_new = m_tile if k0 == 0 else jnp.maximum(m_prev, m_tile)
        p_ref[n % 2, 0:klen, :] = jnp.exp2(s_ref[n % 2, 0:klen, :] - m_new).astype(BF16)
        alpha = None if k0 == 0 else jnp.exp2(m_prev - m_new)
        return m_new, alpha

    def values(n, alpha):
        i, k0, klen, _ = pairs[n]
        pv = jnp.dot(vt_ref[:, k0:k0 + klen], p_ref[n % 2, 0:klen, :],
                     preferred_element_type=F32)
        if k0 == 0:
            acc_ref[...] = pv
        else:
            acc_ref[...] = alpha * acc_ref[...] + pv
        if k0 == i * T:
            acc = acc_ref[...]
            out = acc[0:dv, :] / acc[dv:dv + 1, :]
            hd = out[:, 0:T] - lam * out[:, T:R]
            hd = hd * lax.rsqrt(jnp.mean(hd * hd, axis=0, keepdims=True) + LN_EPS) * nw_ref[...]
            out_ref[i * T:(i + 1) * T, :] = (hd * (1.0 - lam_init)).T.astype(BF16)

    spacing = INTERLEAVE_EVERY if interleaved else 0
    m_tiles = {0: scores(0)}
    m_run = None
    alphas = {}
    for n in range(len(pairs)):
        if n + 1 < len(pairs):
            m_tiles[n + 1] = scores(n + 1)
        m_run, alphas[n] = softmax(n, m_tiles.pop(n), m_run)
        if n >= 1:
            values(n - 1, alphas.pop(n - 1))
        if spacing and (n + 1) % spacing == 0 and (n + 1) // spacing <= len(interleaved):
            interleaved[(n + 1) // spacing - 1]()
    values(len(pairs) - 1, alphas.pop(len(pairs) - 1))


def _mixers_kernel(lam_init, chunks_per_seq, n_attn_refs, *refs):
    (q_ref, k_ref, v_ref, bias_ref, lq1_ref, lk1_ref, lq2_ref, lk2_ref, anw_ref) = refs[:n_attn_refs]
    (voq_ref, mk_ref, gcol_ref, grow_ref, mnw_ref) = refs[n_attn_refs:n_attn_refs + 5]
    hd_ref, hm_ref = refs[n_attn_refs + 5:n_attn_refs + 7]
    (qs_ref, vt_ref, s_ref, p_ref, acc_ref, ct_ref, m_ref, nwb_ref) = refs[n_attn_refs + 7:]
    W = MLSTM_WIDTH
    mvt_ref, mot_ref, mq_ref = (voq_ref.at[r:r + W] for r in (ROW_MV, ROW_MO, ROW_MQ))
    c = pl.program_id(0) % chunks_per_seq
    heads = _mlstm_chunk(c, mq_ref, mk_ref, mvt_ref, mot_ref, gcol_ref, grow_ref, mnw_ref, hm_ref,
                         ct_ref, m_ref, nwb_ref)
    _diff_attn_head(lam_init, heads, q_ref, k_ref, v_ref, bias_ref, lq1_ref, lk1_ref, lq2_ref, lk2_ref,
                    anw_ref, hd_ref, qs_ref, vt_ref, s_ref, p_ref, acc_ref)


def _mixers(proj, proj_t, gcol, grow, bias_tiles, lq1, lk1, lq2, lk2, dnorm_w, mnorm_w, lam_init, batch, seq):
    T = ATT_T
    nq = seq // T
    dv = DIFF_V_DIM
    L = MLSTM_L
    nc = seq // L
    W = MLSTM_WIDTH
    assert DIFF_HEADS * batch == batch * nc, "attention (head, batch) steps must pair 1:1 with mLSTM chunks"
    small = lambda shape: pl.BlockSpec(shape, lambda n: (0, 0))
    attn_specs = [
        pl.BlockSpec((seq, dv), lambda n: (n % batch, COL_DQ // dv + n // batch)),
        pl.BlockSpec((seq, dv), lambda n: (n % batch, COL_DK // dv + n // batch)),
        pl.BlockSpec((dv, seq), lambda n: (ROW_DV // dv + n // batch, n % batch)),
        pl.BlockSpec((1, 2, T, T), lambda n: (n // batch, 0, 0, 0)),
        small((1, DIFF_HEAD_DIM)), small((1, DIFF_HEAD_DIM)),
        small((1, DIFF_HEAD_DIM)), small((1, DIFF_HEAD_DIM)),
        small((dv, 1)),
    ]
    mlstm_specs = [
        pl.BlockSpec((ROW_DV, L), lambda n: (0, n)),
        pl.BlockSpec((L, W), lambda n: (n, COL_MK // W)),
        pl.BlockSpec((L, GATE_PAD), lambda n: (n, 0)),
        pl.BlockSpec((GATE_ROWS, L), lambda n: (0, n)),
        small((W, 1)),
    ]
    return pl.pallas_call(
        functools.partial(_mixers_kernel, lam_init, nc, len(attn_specs)),
        grid=(batch * nc,),
        in_specs=attn_specs + mlstm_specs,
        out_specs=[
            pl.BlockSpec((seq, dv), lambda n: (n % batch, n // batch)),
            pl.BlockSpec((L, W), lambda n: (n, 0)),
        ],
        out_shape=[
            jax.ShapeDtypeStruct((batch * seq, DIFF_WIDTH), BF16),
            jax.ShapeDtypeStruct((batch * seq, W), BF16),
        ],
        scratch_shapes=[
            pltpu.VMEM((nq, 2 * T, dv), BF16),
            pltpu.VMEM((dv + ATT_SUM_ROWS, seq), BF16),
            pltpu.VMEM((2, ATT_FAR_TILES * T, 2 * T), F32),
            pltpu.VMEM((2, ATT_FAR_TILES * T, 2 * T), BF16),
            pltpu.VMEM((dv + ATT_SUM_ROWS, 2 * T), F32),
            pltpu.VMEM((MLSTM_HEADS, MLSTM_HEAD_DIM + MLSTM_AUG, MLSTM_HEAD_DIM), F32),
            pltpu.VMEM((MLSTM_HEADS, 1, 1), F32),
            pltpu.VMEM((W, L), F32),
        ],
        compiler_params=pltpu.CompilerParams(
            dimension_semantics=("arbitrary",), vmem_limit_bytes=VMEM_LIMIT),
        name="token_mixers",
    )(proj, proj, proj_t, bias_tiles, lq1, lk1, lq2, lk2, dnorm_w, proj_t, proj, gcol, grow, mnorm_w)


def _layer_norm(y, g, b):
    mu = jnp.mean(y, axis=-1, keepdims=True)
    cen = y - mu
    var = jnp.mean(cen * cen, axis=-1, keepdims=True)
    return cen * lax.rsqrt(var + LN_EPS) * g + b


def _merge_kernel(hm_ref, hd_ref, gm_ref, gd_ref, x_ref, wbm_ref, wbd_ref, wo_ref, g_ref, b_ref, out_ref):
    tm = hm_ref.shape[0]
    nb = tm // MERGE_ROWS
    halves = [slice(r * MERGE_ROWS, (r + 1) * MERGE_ROWS) for r in range(nb)]
    merged = []
    for rows in halves:
        pm = jnp.dot(hm_ref[rows, :], wbm_ref[...], preferred_element_type=F32)
        pd = jnp.dot(hd_ref[rows, :], wbd_ref[...], preferred_element_type=F32)
        merged.append((gm_ref[rows, :].astype(F32) * pm + gd_ref[rows, :].astype(F32) * pd).astype(BF16))
    for rows, m in zip(halves, merged):
        mix = jnp.dot(m, wo_ref[...], preferred_element_type=F32)
        out_ref[rows, :] = _layer_norm(DEEPNORM_ALPHA * x_ref[rows, :] + mix, g_ref[...], b_ref[...])


def _ffn_kernel(tiles_per_seq, h_ref, wu_ref, cw_ref, cb_ref, wd_ref, g_ref, b_ref, out_ref,
                conv_ref, halo_ref, act_ref):
    i = pl.program_id(0)
    tm = h_ref.shape[0]
    tc = FFN_TC

    @pl.when(i % tiles_per_seq == 0)
    def _():
        halo_ref[...] = jnp.zeros_like(halo_ref)

    h = h_ref[...]
    hb = h.astype(BF16)
    for c0 in range(0, D_FF, tc):
        cols = slice(c0, c0 + tc)
        a = jnp.dot(hb, wu_ref[:, cols], preferred_element_type=F32)
        gate = jnp.dot(hb, wu_ref[:, D_FF + c0:D_FF + c0 + tc], preferred_element_type=F32)
        conv_ref[0:HALO, cols] = halo_ref[:, cols]
        conv_ref[HALO:HALO + tm, cols] = a
        y = cb_ref[:, cols] + cw_ref[FFN_CONV - 1:FFN_CONV, cols] * a
        for kk in range(FFN_CONV - 1):
            off = HALO - (FFN_CONV - 1) + kk
            y = y + cw_ref[kk:kk + 1, cols] * conv_ref[off:off + tm, cols]
        halo_ref[:, cols] = a[tm - HALO:tm, :]
        act_ref[:, cols] = (y * _sigmoid(y) * gate).astype(BF16)

    for r0 in range(0, tm, tm // 2):
        rows = slice(r0, r0 + tm // 2)
        ffn = jnp.dot(act_ref[rows, :], wd_ref[...], preferred_element_type=F32)
        out_ref[rows, :] = _layer_norm(DEEPNORM_ALPHA * h[rows, :] + ffn, g_ref[...], b_ref[...])


def _merge_ffn_kernel(tiles_per_seq, hm_ref, hd_ref, gm_ref, gd_ref, x_ref, wbm_ref, wbd_ref, wo_ref, g1_ref,
                      b1_ref, wu_ref, cw_ref, cb_ref, wd_ref, g2_ref, b2_ref, out_ref,
                      h_ref, conv_ref, halo_ref, act_ref):
    _merge_kernel(hm_ref, hd_ref, gm_ref, gd_ref, x_ref, wbm_ref, wbd_ref, wo_ref, g1_ref, b1_ref, h_ref)
    _ffn_kernel(tiles_per_seq, h_ref, wu_ref, cw_ref, cb_ref, wd_ref, g2_ref, b2_ref, out_ref,
                conv_ref, halo_ref, act_ref)


def _merge_ffn(hm, hd, proj, x2, w_bm, w_bd, w_o, ln1_g, ln1_b, w_up, conv_w, conv_b, w_down, ln2_g, ln2_b, seq):
    T = x2.shape[0]
    tm = FFN_TM
    D = D_MODEL
    rows = lambda col: pl.BlockSpec((tm, D), lambda i: (i, col))
    const = _const_spec
    return pl.pallas_call(
        functools.partial(_merge_ffn_kernel, seq // tm),
        grid=(T // tm,),
        in_specs=[
            rows(0), rows(0), rows(COL_GM // D), rows(COL_GD // D), rows(0),
            const((MLSTM_WIDTH, D)), const((DIFF_WIDTH, D)), const((D, D)), const((1, D)), const((1, D)),
            const((D, 2 * D_FF)), const((FFN_CONV, D_FF)), const((1, D_FF)), const((D_FF, D)),
            const((1, D)), const((1, D)),
        ],
        out_specs=rows(0),
        out_shape=jax.ShapeDtypeStruct((T, D), F32),
        scratch_shapes=[
            pltpu.VMEM((tm, D), F32),
            pltpu.VMEM((HALO + tm, D_FF), F32),
            pltpu.VMEM((HALO, D_FF), F32),
            pltpu.VMEM((tm, D_FF), BF16),
        ],
        compiler_params=pltpu.CompilerParams(
            dimension_semantics=("arbitrary",), vmem_limit_bytes=VMEM_LIMIT),
        name="merge_ffn",
    )(hm, hd, proj, proj, x2, w_bm, w_bd, w_o, ln1_g, ln1_b, w_up, conv_w, conv_b, w_down, ln2_g, ln2_b)


def _layer(h2, batch, seq, l, w_in, b_in, mconv_w, mconv_b, mnorm_w, lq1, lk1, lq2, lk2, dnorm_w, bias_tiles,
           w_bm, w_bd, w_o, ln1_g, ln1_b, w_up, fconv_w, fconv_b, w_down, ln2_g, ln2_b):
    lam_init = 0.8 - 0.6 * math.exp(-0.3 * l)
    n_m = 4 * MLSTM_WIDTH
    n_gate = 2 * MLSTM_HEADS
    n_qk = 2 * MLSTM_WIDTH
    w_all = w_in.T.astype(BF16)
    conv_params = jnp.concatenate(
        [mconv_w.astype(F32).T, mconv_b.astype(F32)[:, None], b_in[:n_qk].astype(F32)[:, None],
         jnp.zeros((n_qk, CONV_PARAMS - MLSTM_CONV - 2), F32)], axis=1)
    w_b = w_all[n_m + n_gate:]
    b_b = b_in[n_m + n_gate:][None, :].astype(F32)
    dv0 = n_m + n_gate + TAIL_DV
    b_t = jnp.concatenate([b_in[n_qk:n_m], b_in[dv0:dv0 + DIFF_WIDTH]])[:, None].astype(F32)
    b_g = b_in[n_m:n_m + GATE_PAD][None, :].astype(F32)
    b_gt = b_in[n_m:n_m + GATE_ROWS][:, None].astype(F32)

    proj, proj_t, gcol, grow = _in_proj(h2, w_all, conv_params, w_b, b_b, b_t, b_g, b_gt, seq)
    hd, hm = _mixers(proj, proj_t, gcol, grow, bias_tiles, lq1[None, :].astype(F32), lk1[None, :].astype(F32),
                     lq2[None, :].astype(F32), lk2[None, :].astype(F32), dnorm_w[:, None].astype(F32),
                     mnorm_w[:, None].astype(F32), lam_init, batch, seq)
    return _merge_ffn(hm, hd, proj, h2, w_bm.astype(BF16), w_bd.astype(BF16), w_o.astype(BF16),
                      ln1_g[None, :].astype(F32), ln1_b[None, :].astype(F32),
                      w_up.astype(BF16), fconv_w.astype(F32), fconv_b[None, :].astype(F32),
                      w_down.astype(BF16), ln2_g[None, :].astype(F32), ln2_b[None, :].astype(F32), seq)


def kernel(x, w_in, b_in, mlstm_conv_w, mlstm_conv_b, mlstm_norm_w, lambda_q1, lambda_k1, lambda_q2, lambda_k2,
           diff_norm_w, rel_bias, w_branch_mlstm, w_branch_diff, w_out, ln1_g, ln1_b, w_ffn_up, ffn_conv_w,
           ffn_conv_b, w_ffn_down, ln2_g, ln2_b):
    batch, seq, d_model = x.shape
    assert d_model == D_MODEL and seq % max(PROJ_TM, FFN_TM, ATT_T, MLSTM_L) == 0
    bias_tiles = _rel_bias_tiles(rel_bias)
    h2 = x.reshape(batch * seq, d_model)
    for l in range(w_in.shape[0]):
        h2 = _layer(h2, batch, seq, l, w_in[l], b_in[l], mlstm_conv_w[l], mlstm_conv_b[l], mlstm_norm_w[l],
                    lambda_q1[l], lambda_k1[l], lambda_q2[l], lambda_k2[l], diff_norm_w[l], bias_tiles,
                    w_branch_mlstm[l], w_branch_diff[l], w_out[l], ln1_g[l], ln1_b[l], w_ffn_up[l],
                    ffn_conv_w[l], ffn_conv_b[l], w_ffn_down[l], ln2_g[l], ln2_b[l])
    return h2.reshape(batch, seq, d_model).astype(x.dtype)
```

```python
import functools
import math

import jax
import jax.numpy as jnp
from jax import lax
from jax.experimental import pallas as pl
from jax.experimental.pallas import tpu as pltpu

F32 = jnp.float32
BF16 = jnp.bfloat16

D_MODEL = 1024
MLSTM_HEADS = 4
MLSTM_HEAD_DIM = 256
MLSTM_WIDTH = MLSTM_HEADS * MLSTM_HEAD_DIM
MLSTM_CONV = 4
DIFF_HEADS = 8
DIFF_HEAD_DIM = 64
DIFF_V_DIM = 2 * DIFF_HEAD_DIM
DIFF_WIDTH = DIFF_HEADS * DIFF_V_DIM
REL_BUCKETS = 32
REL_MAX_DIST = 128
D_FF = 2816
FFN_CONV = 3
DEPTH = 1
DEEPNORM_ALPHA = (2.0 * DEPTH) ** 0.25
LN_EPS = 1e-5
LOG2E = math.log2(math.e)

COL_GM = 0
COL_GD = 1024
COL_MK = 2048
COL_QK = 3072
N_MAIN = 5120
ROW_MV = 0
ROW_MO = 1024
ROW_MQ = 2048
ROW_DV = 3072
N_TRANS = 4096
TAIL_DQ = 0
TAIL_DK = 1024
TAIL_DV = 2048
TAIL_GM = 3072
N_TAIL = 5120
GATE_PAD = 128
GATE_ROWS = 16
MLSTM_AUG = 16

HALO = 8
LANES = 128
CONV_PARAMS = 8
PROJ_TM = 512
PROJ_TN = 512
MLSTM_L = 256
ATT_T = 256
ATT_FAR_TILES = 1
ATT_SUM_ROWS = 16
INTERLEAVE_EVERY = 4
MERGE_ROWS = 256
FFN_TM = 512
FFN_TC = 256
VMEM_LIMIT = 58 * 1024 * 1024


def _sigmoid(v):
    return 1.0 / (1.0 + jnp.exp(-v))


def _log_sigmoid(v):
    return jnp.minimum(v, 0.0) - jnp.log(1.0 + jnp.exp(-jnp.abs(v)))


def _split3(v):
    hi = v.astype(BF16)
    rest = v - hi.astype(F32)
    mid = rest.astype(BF16)
    lo = (rest - mid.astype(F32)).astype(BF16)
    return hi, mid, lo


def _rel_bias_kernel(table_ref, out_ref):
    h = pl.program_id(0)
    T = out_ref.shape[-1]
    max_exact = REL_BUCKETS // 2
    far = table_ref[REL_BUCKETS - 1, h]
    n = lax.broadcasted_iota(jnp.int32, (8, T), 1)
    nf = jnp.maximum(n, 1).astype(F32)
    large = max_exact + (jnp.log(nf / max_exact) / math.log(REL_MAX_DIST / max_exact)
                         * (REL_BUCKETS - max_exact)).astype(jnp.int32)
    large = jnp.minimum(large, REL_BUCKETS - 1)
    bucket = jnp.where(n < max_exact, n, large)
    by_dist = jnp.zeros((8, T), F32)
    for kk in range(REL_BUCKETS):
        by_dist = jnp.where(bucket == kk, table_ref[kk, h], by_dist)
    by_dist = (by_dist - far) * LOG2E
    rolled = pltpu.roll(jnp.broadcast_to(by_dist[0:1, :], (T, T)), 0, 1, stride=1, stride_axis=0)
    kpos = lax.broadcasted_iota(jnp.int32, (T, T), 0)
    qpos = lax.broadcasted_iota(jnp.int32, (T, T), 1)
    out_ref[0, 0] = jnp.where(qpos < kpos, rolled, 0.0)
    out_ref[0, 1] = jnp.where(qpos >= kpos, rolled, -jnp.inf)


def _rel_bias_tiles(rel_bias):
    T = ATT_T
    return pl.pallas_call(
        _rel_bias_kernel,
        grid=(DIFF_HEADS,),
        in_specs=[pl.BlockSpec(memory_space=pltpu.SMEM)],
        out_specs=pl.BlockSpec((1, 2, T, T), lambda h: (h, 0, 0, 0)),
        out_shape=jax.ShapeDtypeStruct((DIFF_HEADS, 2, T, T), F32),
        name="rel_bias_tiles",
    )(rel_bias.astype(F32))


def _in_proj_kernel(tiles_per_seq, x_ref, wa_ref, cp_ref, wb_ref, bb_ref, wt_ref, bt_ref, wg_ref, bg_ref,
                    bgt_ref, o_ref, ot_ref, gcol_ref, grow_ref, halo_ref):
    i = pl.program_id(0)
    tm = x_ref.shape[0]
    tn = PROJ_TN
    n_conv_cols = 2 * MLSTM_WIDTH
    nt_dims = (((1,), (1,)), ((), ()))

    @pl.when(i % tiles_per_seq == 0)
    def _():
        halo_ref[...] = jnp.zeros_like(halo_ref)

    xb = x_ref[...].astype(BF16)
    gcol = lax.dot_general(xb, wg_ref[...], nt_dims, preferred_element_type=F32) + bg_ref[...]
    grow = lax.dot_general(wg_ref[0:GATE_ROWS, :], xb, nt_dims,
                           preferred_element_type=F32) + bgt_ref[...]
    L = MLSTM_L
    H = MLSTM_HEADS
    tri = jnp.where(lax.broadcasted_iota(jnp.int32, (L, L), 0) >= lax.broadcasted_iota(jnp.int32, (L, L), 1),
                    1.0, 0.0).astype(BF16)
    gate_row = lax.broadcasted_iota(jnp.int32, (GATE_ROWS, L), 0)
    for r0 in range(0, tm, L):
        g_c = gcol[r0:r0 + L, :]
        g_r = grow[:, r0:r0 + L]
        bc3 = jnp.dot(tri, jnp.concatenate(_split3(_log_sigmoid(g_c)), axis=1), preferred_element_type=F32)
        bcol = bc3[:, 0:GATE_PAD] + bc3[:, GATE_PAD:2 * GATE_PAD] + bc3[:, 2 * GATE_PAD:3 * GATE_PAD]
        br3 = lax.dot_general(jnp.concatenate(_split3(_log_sigmoid(g_r)), axis=0), tri, nt_dims,
                              preferred_element_type=F32)
        brow = br3[0:GATE_ROWS] + br3[GATE_ROWS:2 * GATE_ROWS] + br3[2 * GATE_ROWS:3 * GATE_ROWS]
        gcol_ref[r0:r0 + L, :] = g_c - pltpu.roll(bcol, GATE_PAD - H, 1)
        grow_ref[:, r0:r0 + L] = jnp.where(gate_row < H, g_r, brow)

    n_vo = 2 * MLSTM_WIDTH
    for r0 in list(range(ROW_MV, ROW_MV + n_vo, tn)) + list(range(ROW_DV, ROW_DV + DIFF_WIDTH, tn)):
        if r0 < ROW_DV:
            src = slice(r0 - ROW_MV, r0 - ROW_MV + tn)
            w_rows, bias_col = wt_ref[src, :], bt_ref[src, :]
        else:
            w_rows = wb_ref[TAIL_DV + r0 - ROW_DV:TAIL_DV + r0 - ROW_DV + tn, :]
            bias_col = bt_ref[n_vo + r0 - ROW_DV:n_vo + r0 - ROW_DV + tn, :]
        acc_t = lax.dot_general(w_rows, xb, nt_dims, preferred_element_type=F32) + bias_col
        if ROW_MO <= r0 < ROW_MO + MLSTM_WIDTH:
            acc_t = _sigmoid(acc_t)
        ot_ref[r0:r0 + tn, :] = acc_t.astype(BF16)

    lane = lax.broadcasted_iota(jnp.int32, (tn, LANES), 1)
    for c0 in range(0, n_conv_cols, tn):
        cols = slice(c0, c0 + tn)
        par = cp_ref[cols, :]
        acc_t = lax.dot_general(wa_ref[cols, :], xb, nt_dims, preferred_element_type=F32) \
            + par[:, MLSTM_CONV + 1:MLSTM_CONV + 2]
        prev = halo_ref[cols, :]
        y = par[:, MLSTM_CONV:MLSTM_CONV + 1] + par[:, MLSTM_CONV - 1:MLSTM_CONV] * acc_t
        for kk in range(MLSTM_CONV - 1):
            shift = MLSTM_CONV - 1 - kk
            rolled = pltpu.roll(acc_t, shift, 1)
            head = jnp.where(lane < shift, pltpu.roll(prev, shift, 1), rolled[:, 0:LANES])
            y = y + par[:, kk:kk + 1] * jnp.concatenate([head, rolled[:, LANES:]], axis=1)
        halo_ref[cols, :] = acc_t[:, tm - LANES:tm]
        y = y * _sigmoid(y)
        if c0 < MLSTM_WIDTH:
            ot_ref[ROW_MQ + c0:ROW_MQ + c0 + tn, :] = y.astype(BF16)
        else:
            k0 = COL_MK + c0 - MLSTM_WIDTH
            o_ref[:, k0:k0 + tn] = (y * (MLSTM_HEAD_DIM ** -0.5)).T.astype(BF16)

    for g0 in range(0, 2 * D_MODEL, tn):
        wrows = slice(TAIL_GM + g0, TAIL_GM + g0 + tn)
        acc = lax.dot_general(xb, wb_ref[wrows, :], nt_dims, preferred_element_type=F32) + bb_ref[:, wrows]
        o_ref[:, COL_GM + g0:COL_GM + g0 + tn] = _sigmoid(acc).astype(BF16)

    dv = DIFF_V_DIM
    heads_per_chunk = tn // (2 * dv)
    q_scale = jnp.full((1, dv), DIFF_HEAD_DIM ** -0.5 * LOG2E, F32)
    scale_row = jnp.concatenate([q_scale, jnp.ones((1, dv), F32)] * heads_per_chunk, axis=1)
    for h0 in range(0, DIFF_HEADS, heads_per_chunk):
        starts = []
        for h in range(h0, h0 + heads_per_chunk):
            starts += [TAIL_DQ + h * dv, TAIL_DK + h * dv]
        w_rows = jnp.concatenate([wb_ref[r:r + dv, :] for r in starts], axis=0)
        b_row = jnp.concatenate([bb_ref[:, r:r + dv] for r in starts], axis=1)
        acc = lax.dot_general(xb, w_rows, nt_dims, preferred_element_type=F32) + b_row
        c0 = COL_QK + h0 * 2 * dv
        o_ref[:, c0:c0 + tn] = (acc * scale_row).astype(BF16)


def _const_spec(shape):
    return pl.BlockSpec(shape, lambda i: (0, 0), pipeline_mode=pl.Buffered(1))


def _in_proj(x2, w_all, conv_params, w_b, b_b, b_t, b_g, b_gt, seq):
    T = x2.shape[0]
    tm = PROJ_TM
    n_conv_cols = 2 * MLSTM_WIDTH
    n_vo = 2 * MLSTM_WIDTH
    assert n_vo == n_conv_cols and (n_conv_cols + n_vo) % GATE_PAD == 0
    const = _const_spec
    row_block = lambda shape, idx: pl.BlockSpec(shape, lambda i: (idx, 0), pipeline_mode=pl.Buffered(1))
    return pl.pallas_call(
        functools.partial(_in_proj_kernel, seq // tm),
        grid=(T // tm,),
        in_specs=[
            pl.BlockSpec((tm, D_MODEL), lambda i: (i, 0)),
            row_block((n_conv_cols, D_MODEL), 0),
            const((n_conv_cols, CONV_PARAMS)),
            const((N_TAIL, D_MODEL)),
            const((1, N_TAIL)),
            row_block((n_vo, D_MODEL), 1),
            const((n_vo + DIFF_WIDTH, 1)),
            row_block((GATE_PAD, D_MODEL), (n_conv_cols + n_vo) // GATE_PAD),
            const((1, GATE_PAD)),
            const((GATE_ROWS, 1)),
        ],
        out_specs=[
            pl.BlockSpec((tm, N_MAIN), lambda i: (i, 0)),
            pl.BlockSpec((N_TRANS, tm), lambda i: (0, i)),
            pl.BlockSpec((tm, GATE_PAD), lambda i: (i, 0)),
            pl.BlockSpec((GATE_ROWS, tm), lambda i: (0, i)),
        ],
        out_shape=[
            jax.ShapeDtypeStruct((T, N_MAIN), BF16),
            jax.ShapeDtypeStruct((N_TRANS, T), BF16),
            jax.ShapeDtypeStruct((T, GATE_PAD), F32),
            jax.ShapeDtypeStruct((GATE_ROWS, T), F32),
        ],
        scratch_shapes=[
            pltpu.VMEM((n_conv_cols, LANES), F32),
        ],
        compiler_params=pltpu.CompilerParams(
            dimension_semantics=("arbitrary",), vmem_limit_bytes=VMEM_LIMIT),
        name="in_proj",
    )(x2, w_all, conv_params, w_b, b_b, w_all, b_t, w_all, b_g, b_gt)


def _mlstm_chunk(c, qt_ref, k_ref, vt_ref, ot_ref, gcol_ref, grow_ref, nw_ref, out_ref, ct_ref, m_ref, nwb_ref):
    L = k_ref.shape[0]
    d = MLSTM_HEAD_DIM
    H = MLSTM_HEADS
    nt_dims = (((1,), (1,)), ((), ()))

    @pl.when(c == 0)
    def _():
        ct_ref[...] = jnp.zeros_like(ct_ref)
        m_ref[...] = jnp.zeros_like(m_ref)
        nwb_ref[...] = jnp.broadcast_to(nw_ref[...], nwb_ref.shape)

    key = lax.broadcasted_iota(jnp.int32, (L, L), 0)
    qry = lax.broadcasted_iota(jnp.int32, (L, L), 1)
    causal_t = key <= qry

    gdiff = gcol_ref[...]
    grow = grow_ref[...]
    ones = jnp.ones((MLSTM_AUG, L), BF16)

    def head(h):
        cols = slice(h * d, (h + 1) * d)
        q_t = qt_ref[cols, :]
        k = k_ref[:, cols]
        v_aug = jnp.concatenate([vt_ref[cols, :], ones], axis=0)
        b_r = grow[H + h:H + h + 1, :]
        i_r = grow[h:h + 1, :]
        g_c = gdiff[:, h:h + 1]
        m_prev = m_ref[h]
        ct = ct_ref[h]

        dmat = jnp.where(causal_t, b_r + g_c, -jnp.inf)
        a = b_r + m_prev
        m_row = jnp.maximum(a, jnp.max(dmat, axis=0, keepdims=True))
        w = jnp.exp(dmat - m_row)
        inter = jnp.exp(a - m_row)
        kq = jnp.dot(k, q_t, preferred_element_type=F32)
        sqk = (kq * w).astype(BF16)
        nd = inter * jnp.dot(ct.astype(BF16), q_t, preferred_element_type=F32) \
            + jnp.dot(v_aug, sqk, preferred_element_type=F32)
        rinv = 1.0 / jnp.maximum(jnp.abs(nd[d:d + 1, :]), jnp.exp(-m_row))
        hid = nd[0:d, :] * rinv * ot_ref[cols, :].astype(F32)
        mu = jnp.mean(hid, axis=0, keepdims=True)
        cen = hid - mu
        var = jnp.mean(cen * cen, axis=0, keepdims=True)
        out_ref[:, cols] = (cen * lax.rsqrt(var + LN_EPS) * nwb_ref[cols, :]).T.astype(BF16)

        b_last = b_r[:, L - 1:L]
        g_r = b_last - b_r + i_r
        m_new = jnp.maximum(b_last + m_prev, jnp.max(g_r, axis=-1, keepdims=True))
        decay = jnp.exp(b_last + m_prev - m_new)
        ws = jnp.exp(g_r - m_new)
        vw = (v_aug.astype(F32) * ws).astype(BF16)
        ct_ref[h] = decay * ct + jnp.dot(vw, k, preferred_element_type=F32)
        m_ref[h] = m_new

    return [functools.partial(head, h) for h in range(H)]


def _diff_attn_head(lam_init, interleaved, q_ref, k_ref, v_ref, bias_ref, lq1_ref, lk1_ref, lq2_ref, lk2_ref,
                    nw_ref, out_ref, qs_ref, vt_ref, s_ref, p_ref, acc_ref):
    T = ATT_T
    dh = DIFF_HEAD_DIM
    dv = DIFF_V_DIM
    nq = q_ref.shape[0] // T
    R = 2 * T

    vt_ref[0:dv, :] = v_ref[...]
    vt_ref[dv:dv + ATT_SUM_ROWS, :] = jnp.ones((ATT_SUM_ROWS, v_ref.shape[1]), BF16)

    lane = lax.broadcasted_iota(jnp.int32, (T, dv), 1)
    for ii in range(nq):
        qt = q_ref[ii * T:(ii + 1) * T, :]
        zero = jnp.zeros_like(qt)
        qs_ref[ii, 0:T, :] = jnp.where(lane < dh, qt, zero)
        qs_ref[ii, T:R, :] = jnp.where(lane >= dh, qt, zero)

    lam = (jnp.exp(jnp.sum(lq1_ref[...] * lk1_ref[...], axis=-1, keepdims=True))
           - jnp.exp(jnp.sum(lq2_ref[...] * lk2_ref[...], axis=-1, keepdims=True)) + lam_init)

    pairs = []
    for i in range(nq):
        k0 = 0
        while k0 < (i - 1) * T:
            klen = min(ATT_FAR_TILES * T, (i - 1) * T - k0)
            pairs.append((i, k0, klen, None))
            k0 += klen
        if i >= 1:
            pairs.append((i, (i - 1) * T, T, 0))
        pairs.append((i, i * T, T, 1))

    def scores(n):
        i, k0, klen, bias_idx = pairs[n]
        s = lax.dot_general(k_ref[k0:k0 + klen, :], qs_ref[i], (((1,), (1,)), ((), ())),
                            preferred_element_type=F32)
        if bias_idx is not None:
            bias = bias_ref[0, bias_idx]
            s = s + jnp.concatenate([bias, bias], axis=1)
        s_ref[n % 2, 0:klen, :] = s
        return jnp.max(s, axis=0, keepdims=True)

    def softmax(n, m_tile, m_prev):
        i, k0, klen, _ = pairs[n]
        m_new = m_tile if k0 == 0 else jnp.maximum(m_prev, m_tile)
        p_ref[n % 2, 0:klen, :] = jnp.exp2(s_ref[n % 2, 0:klen, :] - m_new).astype(BF16)
        alpha = None if k0 == 0 else jnp.exp2(m_prev - m_new)
        return m_new, alpha

    def values(n, alpha):
        i, k0, klen, _ = pairs[n]
        pv = jnp.dot(vt_ref[:, k0:k0 + klen], p_ref[n % 2, 0:klen, :],
                     preferred_element_type=F32)
        if k0 == 0:
            acc_ref[...] = pv
        else:
            acc_ref[...] = alpha * acc_ref[...] + pv
        if k0 == i * T:
            acc = acc_ref[...]
            out = acc[0:dv, :] / acc[dv:dv + 1, :]
            hd = out[:, 0:T] - lam * out[:, T:R]
            hd = hd * lax.rsqrt(jnp.mean(hd * hd, axis=0, keepdims=True) + LN_EPS) * nw_ref[...]
            out_ref[i * T:(i + 1) * T, :] = (hd * (1.0 - lam_init)).T.astype(BF16)

    spacing = INTERLEAVE_EVERY if interleaved else 0
    m_tiles = {0: scores(0)}
    m_run = None
    alphas = {}
    for n in range(len(pairs)):
        if n + 1 < len(pairs):
            m_tiles[n + 1] = scores(n + 1)
        m_run, alphas[n] = softmax(n, m_tiles.pop(n), m_run)
        if n >= 1:
            values(n - 1, alphas.pop(n - 1))
        if spacing and (n + 1) % spacing == 0 and (n + 1) // spacing <= len(interleaved):
            interleaved[(n + 1) // spacing - 1]()
    values(len(pairs) - 1, alphas.pop(len(pairs) - 1))


def _mixers_kernel(lam_init, chunks_per_seq, n_attn_refs, *refs):
    (qk_ref, v_ref, bias_ref, lq1_ref, lk1_ref, lq2_ref, lk2_ref, anw_ref) = refs[:n_attn_refs]
    q_ref, k_ref = qk_ref.at[:, 0:DIFF_V_DIM], qk_ref.at[:, DIFF_V_DIM:2 * DIFF_V_DIM]
    (voq_ref, mk_ref, gcol_ref, grow_ref, mnw_ref) = refs[n_attn_refs:n_attn_refs + 5]
    hd_ref, hm_ref = refs[n_attn_refs + 5:n_attn_refs + 7]
    (qs_ref, vt_ref, s_ref, p_ref, acc_ref, ct_ref, m_ref, nwb_ref) = refs[n_attn_refs + 7:]
    W = MLSTM_WIDTH
    mvt_ref, mot_ref, mq_ref = (voq_ref.at[r:r + W] for r in (ROW_MV, ROW_MO, ROW_MQ))
    c = pl.program_id(0) % chunks_per_seq
    heads = _mlstm_chunk(c, mq_ref, mk_ref, mvt_ref, mot_ref, gcol_ref, grow_ref, mnw_ref, hm_ref,
                         ct_ref, m_ref, nwb_ref)
    _diff_attn_head(lam_init, heads, q_ref, k_ref, v_ref, bias_ref, lq1_ref, lk1_ref, lq2_ref, lk2_ref,
                    anw_ref, hd_ref, qs_ref, vt_ref, s_ref, p_ref, acc_ref)


def _mixers(proj, proj_t, gcol, grow, bias_tiles, lq1, lk1, lq2, lk2, dnorm_w, mnorm_w, lam_init, batch, seq):
    T = ATT_T
    nq = seq // T
    dv = DIFF_V_DIM
    L = MLSTM_L
    nc = seq // L
    W = MLSTM_WIDTH
    assert DIFF_HEADS * batch == batch * nc, "attention (head, batch) steps must pair 1:1 with mLSTM chunks"
    small = lambda shape: pl.BlockSpec(shape, lambda n: (0, 0))
    attn_specs = [
        pl.BlockSpec((seq, 2 * dv), lambda n: (n % batch, COL_QK // (2 * dv) + n // batch)),
        pl.BlockSpec((dv, seq), lambda n: (ROW_DV // dv + n // batch, n % batch)),
        pl.BlockSpec((1, 2, T, T), lambda n: (n // batch, 0, 0, 0)),
        small((1, DIFF_HEAD_DIM)), small((1, DIFF_HEAD_DIM)),
        small((1, DIFF_HEAD_DIM)), small((1, DIFF_HEAD_DIM)),
        small((dv, 1)),
    ]
    mlstm_specs = [
        pl.BlockSpec((ROW_DV, L), lambda n: (0, n)),
        pl.BlockSpec((L, W), lambda n: (n, COL_MK // W)),
        pl.BlockSpec((L, GATE_PAD), lambda n: (n, 0)),
        pl.BlockSpec((GATE_ROWS, L), lambda n: (0, n)),
        small((W, 1)),
    ]
    return pl.pallas_call(
        functools.partial(_mixers_kernel, lam_init, nc, len(attn_specs)),
        grid=(batch * nc,),
        in_specs=attn_specs + mlstm_specs,
        out_specs=[
            pl.BlockSpec((seq, dv), lambda n: (n % batch, n // batch)),
            pl.BlockSpec((L, W), lambda n: (n, 0)),
        ],
        out_shape=[
            jax.ShapeDtypeStruct((batch * seq, DIFF_WIDTH), BF16),
            jax.ShapeDtypeStruct((batch * seq, W), BF16),
        ],
        scratch_shapes=[
            pltpu.VMEM((nq, 2 * T, dv), BF16),
            pltpu.VMEM((dv + ATT_SUM_ROWS, seq), BF16),
            pltpu.VMEM((2, ATT_FAR_TILES * T, 2 * T), F32),
            pltpu.VMEM((2, ATT_FAR_TILES * T, 2 * T), BF16),
            pltpu.VMEM((dv + ATT_SUM_ROWS, 2 * T), F32),
            pltpu.VMEM((MLSTM_HEADS, MLSTM_HEAD_DIM + MLSTM_AUG, MLSTM_HEAD_DIM), F32),
            pltpu.VMEM((MLSTM_HEADS, 1, 1), F32),
            pltpu.VMEM((W, L), F32),
        ],
        compiler_params=pltpu.CompilerParams(
            dimension_semantics=("arbitrary",), vmem_limit_bytes=VMEM_LIMIT),
        name="token_mixers",
    )(proj, proj_t, bias_tiles, lq1, lk1, lq2, lk2, dnorm_w, proj_t, proj, gcol, grow, mnorm_w)


def _layer_norm(y, g, b):
    mu = jnp.mean(y, axis=-1, keepdims=True)
    cen = y - mu
    var = jnp.mean(cen * cen, axis=-1, keepdims=True)
    return cen * lax.rsqrt(var + LN_EPS) * g + b


def _merge_kernel(hm_ref, hd_ref, gates_ref, x_ref, wbm_ref, wbd_ref, wo_ref, g_ref, b_ref, out_ref):
    tm = hm_ref.shape[0]
    gm_ref = gates_ref.at[:, 0:D_MODEL]
    gd_ref = gates_ref.at[:, D_MODEL:2 * D_MODEL]
    nb = tm // MERGE_ROWS
    halves = [slice(r * MERGE_ROWS, (r + 1) * MERGE_ROWS) for r in range(nb)]
    merged = []
    for rows in halves:
        pm = jnp.dot(hm_ref[rows, :], wbm_ref[...], preferred_element_type=F32)
        pd = jnp.dot(hd_ref[rows, :], wbd_ref[...], preferred_element_type=F32)
        merged.append((gm_ref[rows, :].astype(F32) * pm + gd_ref[rows, :].astype(F32) * pd).astype(BF16))
    for rows, m in zip(halves, merged):
        mix = jnp.dot(m, wo_ref[...], preferred_element_type=F32)
        out_ref[rows, :] = _layer_norm(DEEPNORM_ALPHA * x_ref[rows, :] + mix, g_ref[...], b_ref[...])


def _ffn_kernel(tiles_per_seq, h_ref, wu_ref, cw_ref, cb_ref, wd_ref, g_ref, b_ref, out_ref,
                conv_ref, halo_ref, act_ref):
    i = pl.program_id(0)
    tm = h_ref.shape[0]
    tc = FFN_TC

    @pl.when(i % tiles_per_seq == 0)
    def _():
        halo_ref[...] = jnp.zeros_like(halo_ref)

    h = h_ref[...]
    hb = h.astype(BF16)
    for c0 in range(0, D_FF, tc):
        cols = slice(c0, c0 + tc)
        a = jnp.dot(hb, wu_ref[:, cols], preferred_element_type=F32)
        gate = jnp.dot(hb, wu_ref[:, D_FF + c0:D_FF + c0 + tc], preferred_element_type=F32)
        conv_ref[0:HALO, cols] = halo_ref[:, cols]
        conv_ref[HALO:HALO + tm, cols] = a
        y = cb_ref[:, cols] + cw_ref[FFN_CONV - 1:FFN_CONV, cols] * a
        for kk in range(FFN_CONV - 1):
            off = HALO - (FFN_CONV - 1) + kk
            y = y + cw_ref[kk:kk + 1, cols] * conv_ref[off:off + tm, cols]
        halo_ref[:, cols] = a[tm - HALO:tm, :]
        act_ref[:, cols] = (y * _sigmoid(y) * gate).astype(BF16)

    for r0 in range(0, tm, tm // 2):
        rows = slice(r0, r0 + tm // 2)
        ffn = jnp.dot(act_ref[rows, :], wd_ref[...], preferred_element_type=F32)
        out_ref[rows, :] = _layer_norm(DEEPNORM_ALPHA * h[rows, :] + ffn, g_ref[...], b_ref[...])


def _merge_ffn_kernel(tiles_per_seq, hm_ref, hd_ref, gates_ref, x_ref, wbm_ref, wbd_ref, wo_ref, g1_ref,
                      b1_ref, wu_ref, cw_ref, cb_ref, wd_ref, g2_ref, b2_ref, out_ref,
                      h_ref, conv_ref, halo_ref, act_ref):
    _merge_kernel(hm_ref, hd_ref, gates_ref, x_ref, wbm_ref, wbd_ref, wo_ref, g1_ref, b1_ref, h_ref)
    _ffn_kernel(tiles_per_seq, h_ref, wu_ref, cw_ref, cb_ref, wd_ref, g2_ref, b2_ref, out_ref,
                conv_ref, halo_ref, act_ref)


def _merge_ffn(hm, hd, proj, x2, w_bm, w_bd, w_o, ln1_g, ln1_b, w_up, conv_w, conv_b, w_down, ln2_g, ln2_b, seq):
    T = x2.shape[0]
    tm = FFN_TM
    D = D_MODEL
    rows = lambda col: pl.BlockSpec((tm, D), lambda i: (i, col))
    const = _const_spec
    return pl.pallas_call(
        functools.partial(_merge_ffn_kernel, seq // tm),
        grid=(T // tm,),
        in_specs=[
            rows(0), rows(0), pl.BlockSpec((tm, 2 * D), lambda i: (i, COL_GM // (2 * D))), rows(0),
            const((MLSTM_WIDTH, D)), const((DIFF_WIDTH, D)), const((D, D)), const((1, D)), const((1, D)),
            const((D, 2 * D_FF)), const((FFN_CONV, D_FF)), const((1, D_FF)), const((D_FF, D)),
            const((1, D)), const((1, D)),
        ],
        out_specs=rows(0),
        out_shape=jax.ShapeDtypeStruct((T, D), F32),
        scratch_shapes=[
            pltpu.VMEM((tm, D), F32),
            pltpu.VMEM((HALO + tm, D_FF), F32),
            pltpu.VMEM((HALO, D_FF), F32),
            pltpu.VMEM((tm, D_FF), BF16),
        ],
        compiler_params=pltpu.CompilerParams(
            dimension_semantics=("arbitrary",), vmem_limit_bytes=VMEM_LIMIT),
        name="merge_ffn",
    )(hm, hd, proj, x2, w_bm, w_bd, w_o, ln1_g, ln1_b, w_up, conv_w, conv_b, w_down, ln2_g, ln2_b)


def _layer(h2, batch, seq, l, w_in, b_in, mconv_w, mconv_b, mnorm_w, lq1, lk1, lq2, lk2, dnorm_w, bias_tiles,
           w_bm, w_bd, w_o, ln1_g, ln1_b, w_up, fconv_w, fconv_b, w_down, ln2_g, ln2_b):
    lam_init = 0.8 - 0.6 * math.exp(-0.3 * l)
    n_m = 4 * MLSTM_WIDTH
    n_gate = 2 * MLSTM_HEADS
    n_qk = 2 * MLSTM_WIDTH
    w_all = w_in.T.astype(BF16)
    conv_params = jnp.concatenate(
        [mconv_w.astype(F32).T, mconv_b.astype(F32)[:, None], b_in[:n_qk].astype(F32)[:, None],
         jnp.zeros((n_qk, CONV_PARAMS - MLSTM_CONV - 2), F32)], axis=1)
    w_b = w_all[n_m + n_gate:]
    b_b = b_in[n_m + n_gate:][None, :].astype(F32)
    dv0 = n_m + n_gate + TAIL_DV
    b_t = jnp.concatenate([b_in[n_qk:n_m], b_in[dv0:dv0 + DIFF_WIDTH]])[:, None].astype(F32)
    b_g = b_in[n_m:n_m + GATE_PAD][None, :].astype(F32)
    b_gt = b_in[n_m:n_m + GATE_ROWS][:, None].astype(F32)

    proj, proj_t, gcol, grow = _in_proj(h2, w_all, conv_params, w_b, b_b, b_t, b_g, b_gt, seq)
    hd, hm = _mixers(proj, proj_t, gcol, grow, bias_tiles, lq1[None, :].astype(F32), lk1[None, :].astype(F32),
                     lq2[None, :].astype(F32), lk2[None, :].astype(F32), dnorm_w[:, None].astype(F32),
                     mnorm_w[:, None].astype(F32), lam_init, batch, seq)
    return _merge_ffn(hm, hd, proj, h2, w_bm.astype(BF16), w_bd.astype(BF16), w_o.astype(BF16),
                      ln1_g[None, :].astype(F32), ln1_b[None, :].astype(F32),
                      w_up.astype(BF16), fconv_w.astype(F32), fconv_b[None, :].astype(F32),
                      w_down.astype(BF16), ln2_g[None, :].astype(F32), ln2_b[None, :].astype(F32), seq)


def kernel(x, w_in, b_in, mlstm_conv_w, mlstm_conv_b, mlstm_norm_w, lambda_q1, lambda_k1, lambda_q2, lambda_k2,
           diff_norm_w, rel_bias, w_branch_mlstm, w_branch_diff, w_out, ln1_g, ln1_b, w_ffn_up, ffn_conv_w,
           ffn_conv_b, w_ffn_down, ln2_g, ln2_b):
    batch, seq, d_model = x.shape
    assert d_model == D_MODEL and seq % max(PROJ_TM, FFN_TM, ATT_T, MLSTM_L) == 0
    bias_tiles = _rel_bias_tiles(rel_bias)
    h2 = x.reshape(batch * seq, d_model)
    for l in range(w_in.shape[0]):
        h2 = _layer(h2, batch, seq, l, w_in[l], b_in[l], mlstm_conv_w[l], mlstm_conv_b[l], mlstm_norm_w[l],
                    lambda_q1[l], lambda_k1[l], lambda_q2[l], lambda_k2[l], diff_norm_w[l], bias_tiles,
                    w_branch_mlstm[l], w_branch_diff[l], w_out[l], ln1_g[l], ln1_b[l], w_ffn_up[l],
                    ffn_conv_w[l], ffn_conv_b[l], w_ffn_down[l], ln2_g[l], ln2_b[l])
    return h2.reshape(batch, seq, d_model).astype(x.dtype)
```

```python
import functools
import math

import jax
import jax.numpy as jnp
from jax import lax
from jax.experimental import pallas as pl
from jax.experimental.pallas import tpu as pltpu

F32 = jnp.float32
BF16 = jnp.bfloat16

D_MODEL = 1024
MLSTM_HEADS = 4
MLSTM_HEAD_DIM = 256
MLSTM_WIDTH = MLSTM_HEADS * MLSTM_HEAD_DIM
MLSTM_CONV = 4
DIFF_HEADS = 8
DIFF_HEAD_DIM = 64
DIFF_V_DIM = 2 * DIFF_HEAD_DIM
DIFF_WIDTH = DIFF_HEADS * DIFF_V_DIM
REL_BUCKETS = 32
REL_MAX_DIST = 128
D_FF = 2816
FFN_CONV = 3
DEPTH = 1
DEEPNORM_ALPHA = (2.0 * DEPTH) ** 0.25
LN_EPS = 1e-5
LOG2E = math.log2(math.e)

COL_GM = 0
COL_GD = 1024
COL_MK = 2048
COL_QK = 3072
N_MAIN = 5120
ROW_MV = 0
ROW_MO = 1024
ROW_MQ = 2048
ROW_DV = 3072
N_TRANS = 4096
TAIL_DQ = 0
TAIL_DK = 1024
TAIL_DV = 2048
TAIL_GM = 3072
N_TAIL = 5120
GATE_PAD = 128
GATE_ROWS = 16
MLSTM_AUG = 16

HALO = 8
LANES = 128
CONV_PARAMS = 8
PROJ_TM = 512
PROJ_TN = 512
MLSTM_L = 256
ATT_T = 256
ATT_FAR_TILES = 1
ATT_SUM_ROWS = 16
INTERLEAVE_EVERY = 4
MIXERS_PER_STEP = 2
MERGE_ROWS = 256
FFN_TM = 512
FFN_TC = 256
VMEM_LIMIT = 58 * 1024 * 1024


def _sigmoid(v):
    return 1.0 / (1.0 + jnp.exp(-v))


def _log_sigmoid(v):
    return jnp.minimum(v, 0.0) - jnp.log(1.0 + jnp.exp(-jnp.abs(v)))


def _split3(v):
    hi = v.astype(BF16)
    rest = v - hi.astype(F32)
    mid = rest.astype(BF16)
    lo = (rest - mid.astype(F32)).astype(BF16)
    return hi, mid, lo


def _rel_bias_kernel(table_ref, out_ref):
    h = pl.program_id(0)
    T = out_ref.shape[-1]
    max_exact = REL_BUCKETS // 2
    far = table_ref[REL_BUCKETS - 1, h]
    n = lax.broadcasted_iota(jnp.int32, (8, T), 1)
    nf = jnp.maximum(n, 1).astype(F32)
    large = max_exact + (jnp.log(nf / max_exact) / math.log(REL_MAX_DIST / max_exact)
                         * (REL_BUCKETS - max_exact)).astype(jnp.int32)
    large = jnp.minimum(large, REL_BUCKETS - 1)
    bucket = jnp.where(n < max_exact, n, large)
    by_dist = jnp.zeros((8, T), F32)
    for kk in range(REL_BUCKETS):
        by_dist = jnp.where(bucket == kk, table_ref[kk, h], by_dist)
    by_dist = (by_dist - far) * LOG2E
    rolled = pltpu.roll(jnp.broadcast_to(by_dist[0:1, :], (T, T)), 0, 1, stride=1, stride_axis=0)
    kpos = lax.broadcasted_iota(jnp.int32, (T, T), 0)
    qpos = lax.broadcasted_iota(jnp.int32, (T, T), 1)
    out_ref[0, 0] = jnp.where(qpos < kpos, rolled, 0.0)
    out_ref[0, 1] = jnp.where(qpos >= kpos, rolled, -jnp.inf)


def _rel_bias_tiles(rel_bias):
    T = ATT_T
    return pl.pallas_call(
        _rel_bias_kernel,
        grid=(DIFF_HEADS,),
        in_specs=[pl.BlockSpec(memory_space=pltpu.SMEM)],
        out_specs=pl.BlockSpec((1, 2, T, T), lambda h: (h, 0, 0, 0)),
        out_shape=jax.ShapeDtypeStruct((DIFF_HEADS, 2, T, T), F32),
        name="rel_bias_tiles",
    )(rel_bias.astype(F32))


def _in_proj_kernel(tiles_per_seq, x_ref, wa_ref, cp_ref, wb_ref, bb_ref, wt_ref, bt_ref, wg_ref, bg_ref,
                    bgt_ref, o_ref, ot_ref, gcol_ref, grow_ref, halo_ref):
    i = pl.program_id(0)
    tm = x_ref.shape[0]
    tn = PROJ_TN
    n_conv_cols = 2 * MLSTM_WIDTH
    nt_dims = (((1,), (1,)), ((), ()))

    @pl.when(i % tiles_per_seq == 0)
    def _():
        halo_ref[...] = jnp.zeros_like(halo_ref)

    xb = x_ref[...].astype(BF16)
    gcol = lax.dot_general(xb, wg_ref[...], nt_dims, preferred_element_type=F32) + bg_ref[...]
    grow = lax.dot_general(wg_ref[0:GATE_ROWS, :], xb, nt_dims,
                           preferred_element_type=F32) + bgt_ref[...]
    L = MLSTM_L
    H = MLSTM_HEADS
    tri = jnp.where(lax.broadcasted_iota(jnp.int32, (L, L), 0) >= lax.broadcasted_iota(jnp.int32, (L, L), 1),
                    1.0, 0.0).astype(BF16)
    gate_row = lax.broadcasted_iota(jnp.int32, (GATE_ROWS, L), 0)
    for r0 in range(0, tm, L):
        g_c = gcol[r0:r0 + L, :]
        g_r = grow[:, r0:r0 + L]
        bc3 = jnp.dot(tri, jnp.concatenate(_split3(_log_sigmoid(g_c)), axis=1), preferred_element_type=F32)
        bcol = bc3[:, 0:GATE_PAD] + bc3[:, GATE_PAD:2 * GATE_PAD] + bc3[:, 2 * GATE_PAD:3 * GATE_PAD]
        br3 = lax.dot_general(jnp.concatenate(_split3(_log_sigmoid(g_r)), axis=0), tri, nt_dims,
                              preferred_element_type=F32)
        brow = br3[0:GATE_ROWS] + br3[GATE_ROWS:2 * GATE_ROWS] + br3[2 * GATE_ROWS:3 * GATE_ROWS]
        gcol_ref[r0:r0 + L, :] = g_c - pltpu.roll(bcol, GATE_PAD - H, 1)
        grow_ref[:, r0:r0 + L] = jnp.where(gate_row < H, g_r, brow)

    n_vo = 2 * MLSTM_WIDTH
    for r0 in list(range(ROW_MV, ROW_MV + n_vo, tn)) + list(range(ROW_DV, ROW_DV + DIFF_WIDTH, tn)):
        if r0 < ROW_DV:
            src = slice(r0 - ROW_MV, r0 - ROW_MV + tn)
            w_rows, bias_col = wt_ref[src, :], bt_ref[src, :]
        else:
            w_rows = wb_ref[TAIL_DV + r0 - ROW_DV:TAIL_DV + r0 - ROW_DV + tn, :]
            bias_col = bt_ref[n_vo + r0 - ROW_DV:n_vo + r0 - ROW_DV + tn, :]
        acc_t = lax.dot_general(w_rows, xb, nt_dims, preferred_element_type=F32) + bias_col
        if ROW_MO <= r0 < ROW_MO + MLSTM_WIDTH:
            acc_t = _sigmoid(acc_t)
        ot_ref[r0:r0 + tn, :] = acc_t.astype(BF16)

    lane = lax.broadcasted_iota(jnp.int32, (tn, LANES), 1)
    for c0 in range(0, n_conv_cols, tn):
        cols = slice(c0, c0 + tn)
        par = cp_ref[cols, :]
        acc_t = lax.dot_general(wa_ref[cols, :], xb, nt_dims, preferred_element_type=F32) \
            + par[:, MLSTM_CONV + 1:MLSTM_CONV + 2]
        prev = halo_ref[cols, :]
        y = par[:, MLSTM_CONV:MLSTM_CONV + 1] + par[:, MLSTM_CONV - 1:MLSTM_CONV] * acc_t
        for kk in range(MLSTM_CONV - 1):
            shift = MLSTM_CONV - 1 - kk
            rolled = pltpu.roll(acc_t, shift, 1)
            head = jnp.where(lane < shift, pltpu.roll(prev, shift, 1), rolled[:, 0:LANES])
            y = y + par[:, kk:kk + 1] * jnp.concatenate([head, rolled[:, LANES:]], axis=1)
        halo_ref[cols, :] = acc_t[:, tm - LANES:tm]
        y = y * _sigmoid(y)
        if c0 < MLSTM_WIDTH:
            ot_ref[ROW_MQ + c0:ROW_MQ + c0 + tn, :] = y.astype(BF16)
        else:
            k0 = COL_MK + c0 - MLSTM_WIDTH
            o_ref[:, k0:k0 + tn] = (y * (MLSTM_HEAD_DIM ** -0.5)).T.astype(BF16)

    for g0 in range(0, 2 * D_MODEL, tn):
        wrows = slice(TAIL_GM + g0, TAIL_GM + g0 + tn)
        acc = lax.dot_general(xb, wb_ref[wrows, :], nt_dims, preferred_element_type=F32) + bb_ref[:, wrows]
        o_ref[:, COL_GM + g0:COL_GM + g0 + tn] = _sigmoid(acc).astype(BF16)

    dv = DIFF_V_DIM
    heads_per_chunk = tn // (2 * dv)
    q_scale = jnp.full((1, dv), DIFF_HEAD_DIM ** -0.5 * LOG2E, F32)
    scale_row = jnp.concatenate([q_scale, jnp.ones((1, dv), F32)] * heads_per_chunk, axis=1)
    for h0 in range(0, DIFF_HEADS, heads_per_chunk):
        starts = []
        for h in range(h0, h0 + heads_per_chunk):
            starts += [TAIL_DQ + h * dv, TAIL_DK + h * dv]
        w_rows = jnp.concatenate([wb_ref[r:r + dv, :] for r in starts], axis=0)
        b_row = jnp.concatenate([bb_ref[:, r:r + dv] for r in starts], axis=1)
        acc = lax.dot_general(xb, w_rows, nt_dims, preferred_element_type=F32) + b_row
        c0 = COL_QK + h0 * 2 * dv
        o_ref[:, c0:c0 + tn] = (acc * scale_row).astype(BF16)


def _const_spec(shape):
    return pl.BlockSpec(shape, lambda i: (0, 0), pipeline_mode=pl.Buffered(1))


def _in_proj(x2, w_all, conv_params, w_b, b_b, b_t, b_g, b_gt, seq):
    T = x2.shape[0]
    tm = PROJ_TM
    n_conv_cols = 2 * MLSTM_WIDTH
    n_vo = 2 * MLSTM_WIDTH
    assert n_vo == n_conv_cols and (n_conv_cols + n_vo) % GATE_PAD == 0
    const = _const_spec
    row_block = lambda shape, idx: pl.BlockSpec(shape, lambda i: (idx, 0), pipeline_mode=pl.Buffered(1))
    return pl.pallas_call(
        functools.partial(_in_proj_kernel, seq // tm),
        grid=(T // tm,),
        in_specs=[
            pl.BlockSpec((tm, D_MODEL), lambda i: (i, 0)),
            row_block((n_conv_cols, D_MODEL), 0),
            const((n_conv_cols, CONV_PARAMS)),
            const((N_TAIL, D_MODEL)),
            const((1, N_TAIL)),
            row_block((n_vo, D_MODEL), 1),
            const((n_vo + DIFF_WIDTH, 1)),
            row_block((GATE_PAD, D_MODEL), (n_conv_cols + n_vo) // GATE_PAD),
            const((1, GATE_PAD)),
            const((GATE_ROWS, 1)),
        ],
        out_specs=[
            pl.BlockSpec((tm, N_MAIN), lambda i: (i, 0)),
            pl.BlockSpec((N_TRANS, tm), lambda i: (0, i)),
            pl.BlockSpec((tm, GATE_PAD), lambda i: (i, 0)),
            pl.BlockSpec((GATE_ROWS, tm), lambda i: (0, i)),
        ],
        out_shape=[
            jax.ShapeDtypeStruct((T, N_MAIN), BF16),
            jax.ShapeDtypeStruct((N_TRANS, T), BF16),
            jax.ShapeDtypeStruct((T, GATE_PAD), F32),
            jax.ShapeDtypeStruct((GATE_ROWS, T), F32),
        ],
        scratch_shapes=[
            pltpu.VMEM((n_conv_cols, LANES), F32),
        ],
        compiler_params=pltpu.CompilerParams(
            dimension_semantics=("arbitrary",), vmem_limit_bytes=VMEM_LIMIT),
        name="in_proj",
    )(x2, w_all, conv_params, w_b, b_b, w_all, b_t, w_all, b_g, b_gt)


def _mlstm_chunk(c, qt_ref, k_ref, vt_ref, ot_ref, gcol_ref, grow_ref, nw_ref, out_ref, ct_ref, m_ref, nwb_ref):
    L = k_ref.shape[0]
    d = MLSTM_HEAD_DIM
    H = MLSTM_HEADS
    nt_dims = (((1,), (1,)), ((), ()))

    @pl.when(c == 0)
    def _():
        ct_ref[...] = jnp.zeros_like(ct_ref)
        m_ref[...] = jnp.zeros_like(m_ref)
        nwb_ref[...] = jnp.broadcast_to(nw_ref[...], nwb_ref.shape)

    key = lax.broadcasted_iota(jnp.int32, (L, L), 0)
    qry = lax.broadcasted_iota(jnp.int32, (L, L), 1)
    causal_t = key <= qry

    gdiff = gcol_ref[...]
    grow = grow_ref[...]
    ones = jnp.ones((MLSTM_AUG, L), BF16)

    def head(h):
        cols = slice(h * d, (h + 1) * d)
        q_t = qt_ref[cols, :]
        k = k_ref[:, cols]
        v_aug = jnp.concatenate([vt_ref[cols, :], ones], axis=0)
        b_r = grow[H + h:H + h + 1, :]
        i_r = grow[h:h + 1, :]
        g_c = gdiff[:, h:h + 1]
        m_prev = m_ref[h]
        ct = ct_ref[h]

        dmat = jnp.where(causal_t, b_r + g_c, -jnp.inf)
        a = b_r + m_prev
        m_row = jnp.maximum(a, jnp.max(dmat, axis=0, keepdims=True))
        w = jnp.exp(dmat - m_row)
        inter = jnp.exp(a - m_row)
        kq = jnp.dot(k, q_t, preferred_element_type=F32)
        sqk = (kq * w).astype(BF16)
        nd = inter * jnp.dot(ct.astype(BF16), q_t, preferred_element_type=F32) \
            + jnp.dot(v_aug, sqk, preferred_element_type=F32)
        rinv = 1.0 / jnp.maximum(jnp.abs(nd[d:d + 1, :]), jnp.exp(-m_row))
        hid = nd[0:d, :] * rinv * ot_ref[cols, :].astype(F32)
        mu = jnp.mean(hid, axis=0, keepdims=True)
        cen = hid - mu
        var = jnp.mean(cen * cen, axis=0, keepdims=True)
        out_ref[:, cols] = (cen * lax.rsqrt(var + LN_EPS) * nwb_ref[cols, :]).T.astype(BF16)

        b_last = b_r[:, L - 1:L]
        g_r = b_last - b_r + i_r
        m_new = jnp.maximum(b_last + m_prev, jnp.max(g_r, axis=-1, keepdims=True))
        decay = jnp.exp(b_last + m_prev - m_new)
        ws = jnp.exp(g_r - m_new)
        vw = (v_aug.astype(F32) * ws).astype(BF16)
        ct_ref[h] = decay * ct + jnp.dot(vw, k, preferred_element_type=F32)
        m_ref[h] = m_new

    return [functools.partial(head, h) for h in range(H)]


def _diff_attn_head(lam_init, interleaved, q_ref, k_ref, v_ref, bias_ref, lq1_ref, lk1_ref, lq2_ref, lk2_ref,
                    nw_ref, out_ref, qs_ref, vt_ref, s_ref, p_ref, acc_ref):
    T = ATT_T
    dh = DIFF_HEAD_DIM
    dv = DIFF_V_DIM
    nq = q_ref.shape[0] // T
    R = 2 * T

    vt_ref[0:dv, :] = v_ref[...]
    vt_ref[dv:dv + ATT_SUM_ROWS, :] = jnp.ones((ATT_SUM_ROWS, v_ref.shape[1]), BF16)

    lane = lax.broadcasted_iota(jnp.int32, (T, dv), 1)
    for ii in range(nq):
        qt = q_ref[ii * T:(ii + 1) * T, :]
        zero = jnp.zeros_like(qt)
        qs_ref[ii, 0:T, :] = jnp.where(lane < dh, qt, zero)
        qs_ref[ii, T:R, :] = jnp.where(lane >= dh, qt, zero)

    lam = (jnp.exp(jnp.sum(lq1_ref[...] * lk1_ref[...], axis=-1, keepdims=True))
           - jnp.exp(jnp.sum(lq2_ref[...] * lk2_ref[...], axis=-1, keepdims=True)) + lam_init)

    pairs = []
    for i in range(nq):
        k0 = 0
        while k0 < (i - 1) * T:
            klen = min(ATT_FAR_TILES * T, (i - 1) * T - k0)
            pairs.append((i, k0, klen, None))
            k0 += klen
        if i >= 1:
            pairs.append((i, (i - 1) * T, T, 0))
        pairs.append((i, i * T, T, 1))

    def scores(n):
        i, k0, klen, bias_idx = pairs[n]
        s = lax.dot_general(k_ref[k0:k0 + klen, :], qs_ref[i], (((1,), (1,)), ((), ())),
                            preferred_element_type=F32)
        if bias_idx is not None:
            bias = bias_ref[0, bias_idx]
            s = s + jnp.concatenate([bias, bias], axis=1)
        s_ref[n % 2, 0:klen, :] = s
        return jnp.max(s, axis=0, keepdims=True)

    def softmax(n, m_tile, m_prev):
        i, k0, klen, _ = pairs[n]
        m_new = m_tile if k0 == 0 else jnp.maximum(m_prev, m_tile)
        p_ref[n % 2, 0:klen, :] = jnp.exp2(s_ref[n % 2, 0:klen, :] - m_new).astype(BF16)
        alpha = None if k0 == 0 else jnp.exp2(m_prev - m_new)
        return m_new, alpha

    def values(n, alpha):
        i, k0, klen, _ = pairs[n]
        pv = jnp.dot(vt_ref[:, k0:k0 + klen], p_ref[n % 2, 0:klen, :],
                     preferred_element_type=F32)
        if k0 == 0:
            acc_ref[...] = pv
        else:
            acc_ref[...] = alpha * acc_ref[...] + pv
        if k0 == i * T:
            acc = acc_ref[...]
            out = acc[0:dv, :] / acc[dv:dv + 1, :]
            hd = out[:, 0:T] - lam * out[:, T:R]
            hd = hd * lax.rsqrt(jnp.mean(hd * hd, axis=0, keepdims=True) + LN_EPS) * nw_ref[...]
            out_ref[i * T:(i + 1) * T, :] = (hd * (1.0 - lam_init)).T.astype(BF16)

    spacing = INTERLEAVE_EVERY if interleaved else 0
    m_tiles = {0: scores(0)}
    m_run = None
    alphas = {}
    for n in range(len(pairs)):
        if n + 1 < len(pairs):
            m_tiles[n + 1] = scores(n + 1)
        m_run, alphas[n] = softmax(n, m_tiles.pop(n), m_run)
        if n >= 1:
            values(n - 1, alphas.pop(n - 1))
        if spacing and (n + 1) % spacing == 0 and (n + 1) // spacing <= len(interleaved):
            interleaved[(n + 1) // spacing - 1]()
    values(len(pairs) - 1, alphas.pop(len(pairs) - 1))


def _mixers_kernel(lam_init, chunks_per_seq, n_attn_refs, *refs):
    (qk_all, v_all, bias_ref, lq1_ref, lk1_ref, lq2_ref, lk2_ref, anw_ref) = refs[:n_attn_refs]
    (voq_all, mk_all, gcol_all, grow_all, mnw_ref) = refs[n_attn_refs:n_attn_refs + 5]
    hd_all, hm_all = refs[n_attn_refs + 5:n_attn_refs + 7]
    (qs_ref, vt_ref, s_ref, p_ref, acc_ref, ct_ref, m_ref, nwb_ref) = refs[n_attn_refs + 7:]
    W = MLSTM_WIDTH
    L = MLSTM_L
    seq = qk_all.shape[0] // MIXERS_PER_STEP
    for sub in range(MIXERS_PER_STEP):
        tok = slice(sub * seq, (sub + 1) * seq)
        chunk = slice(sub * L, (sub + 1) * L)
        q_ref, k_ref = qk_all.at[tok, 0:DIFF_V_DIM], qk_all.at[tok, DIFF_V_DIM:2 * DIFF_V_DIM]
        voq_ref = voq_all.at[:, chunk]
        mvt_ref, mot_ref, mq_ref = (voq_ref.at[r:r + W] for r in (ROW_MV, ROW_MO, ROW_MQ))
        c = (pl.program_id(0) * MIXERS_PER_STEP + sub) % chunks_per_seq
        heads = _mlstm_chunk(c, mq_ref, mk_all.at[chunk], mvt_ref, mot_ref, gcol_all.at[chunk],
                             grow_all.at[:, chunk], mnw_ref, hm_all.at[chunk], ct_ref, m_ref, nwb_ref)
        _diff_attn_head(lam_init, heads, q_ref, k_ref, v_all.at[:, tok], bias_ref, lq1_ref, lk1_ref, lq2_ref,
                        lk2_ref, anw_ref, hd_all.at[tok], qs_ref, vt_ref, s_ref, p_ref, acc_ref)


def _mixers(proj, proj_t, gcol, grow, bias_tiles, lq1, lk1, lq2, lk2, dnorm_w, mnorm_w, lam_init, batch, seq):
    T = ATT_T
    nq = seq // T
    dv = DIFF_V_DIM
    L = MLSTM_L
    nc = seq // L
    W = MLSTM_WIDTH
    P = MIXERS_PER_STEP
    assert DIFF_HEADS * batch == batch * nc, "attention (head, batch) steps must pair 1:1 with mLSTM chunks"
    assert batch % P == 0 and nc % P == 0
    bp = batch // P
    small = lambda shape: pl.BlockSpec(shape, lambda n: (0, 0))
    attn_specs = [
        pl.BlockSpec((P * seq, 2 * dv), lambda n: (n % bp, COL_QK // (2 * dv) + n // bp)),
        pl.BlockSpec((dv, P * seq), lambda n: (ROW_DV // dv + n // bp, n % bp)),
        pl.BlockSpec((1, 2, T, T), lambda n: (n // bp, 0, 0, 0)),
        small((1, DIFF_HEAD_DIM)), small((1, DIFF_HEAD_DIM)),
        small((1, DIFF_HEAD_DIM)), small((1, DIFF_HEAD_DIM)),
        small((dv, 1)),
    ]
    mlstm_specs = [
        pl.BlockSpec((ROW_DV, P * L), lambda n: (0, n)),
        pl.BlockSpec((P * L, W), lambda n: (n, COL_MK // W)),
        pl.BlockSpec((P * L, GATE_PAD), lambda n: (n, 0)),
        pl.BlockSpec((GATE_ROWS, P * L), lambda n: (0, n)),
        small((W, 1)),
    ]
    return pl.pallas_call(
        functools.partial(_mixers_kernel, lam_init, nc, len(attn_specs)),
        grid=(batch * nc // P,),
        in_specs=attn_specs + mlstm_specs,
        out_specs=[
            pl.BlockSpec((P * seq, dv), lambda n: (n % bp, n // bp)),
            pl.BlockSpec((P * L, W), lambda n: (n, 0)),
        ],
        out_shape=[
            jax.ShapeDtypeStruct((batch * seq, DIFF_WIDTH), BF16),
            jax.ShapeDtypeStruct((batch * seq, W), BF16),
        ],
        scratch_shapes=[
            pltpu.VMEM((nq, 2 * T, dv), BF16),
            pltpu.VMEM((dv + ATT_SUM_ROWS, seq), BF16),
            pltpu.VMEM((2, ATT_FAR_TILES * T, 2 * T), F32),
            pltpu.VMEM((2, ATT_FAR_TILES * T, 2 * T), BF16),
            pltpu.VMEM((dv + ATT_SUM_ROWS, 2 * T), F32),
            pltpu.VMEM((MLSTM_HEADS, MLSTM_HEAD_DIM + MLSTM_AUG, MLSTM_HEAD_DIM), F32),
            pltpu.VMEM((MLSTM_HEADS, 1, 1), F32),
            pltpu.VMEM((W, L), F32),
        ],
        compiler_params=pltpu.CompilerParams(
            dimension_semantics=("arbitrary",), vmem_limit_bytes=VMEM_LIMIT),
        name="token_mixers",
    )(proj, proj_t, bias_tiles, lq1, lk1, lq2, lk2, dnorm_w, proj_t, proj, gcol, grow, mnorm_w)


def _layer_norm(y, g, b):
    mu = jnp.mean(y, axis=-1, keepdims=True)
    cen = y - mu
    var = jnp.mean(cen * cen, axis=-1, keepdims=True)
    return cen * lax.rsqrt(var + LN_EPS) * g + b


def _merge_kernel(hm_ref, hd_ref, gates_ref, x_ref, wbm_ref, wbd_ref, wo_ref, g_ref, b_ref, out_ref):
    tm = hm_ref.shape[0]
    gm_ref = gates_ref.at[:, 0:D_MODEL]
    gd_ref = gates_ref.at[:, D_MODEL:2 * D_MODEL]
    nb = tm // MERGE_ROWS
    halves = [slice(r * MERGE_ROWS, (r + 1) * MERGE_ROWS) for r in range(nb)]
    merged = []
    for rows in halves:
        pm = jnp.dot(hm_ref[rows, :], wbm_ref[...], preferred_element_type=F32)
        pd = jnp.dot(hd_ref[rows, :], wbd_ref[...], preferred_element_type=F32)
        merged.append((gm_ref[rows, :].astype(F32) * pm + gd_ref[rows, :].astype(F32) * pd).astype(BF16))
    for rows, m in zip(halves, merged):
        mix = jnp.dot(m, wo_ref[...], preferred_element_type=F32)
        out_ref[rows, :] = _layer_norm(DEEPNORM_ALPHA * x_ref[rows, :] + mix, g_ref[...], b_ref[...])


def _ffn_kernel(tiles_per_seq, h_ref, wu_ref, cw_ref, cb_ref, wd_ref, g_ref, b_ref, out_ref,
                conv_ref, halo_ref, act_ref):
    i = pl.program_id(0)
    tm = h_ref.shape[0]
    tc = FFN_TC

    @pl.when(i % tiles_per_seq == 0)
    def _():
        halo_ref[...] = jnp.zeros_like(halo_ref)

    h = h_ref[...]
    hb = h.astype(BF16)
    for c0 in range(0, D_FF, tc):
        cols = slice(c0, c0 + tc)
        a = jnp.dot(hb, wu_ref[:, cols], preferred_element_type=F32)
        gate = jnp.dot(hb, wu_ref[:, D_FF + c0:D_FF + c0 + tc], preferred_element_type=F32)
        conv_ref[0:HALO, cols] = halo_ref[:, cols]
        conv_ref[HALO:HALO + tm, cols] = a
        y = cb_ref[:, cols] + cw_ref[FFN_CONV - 1:FFN_CONV, cols] * a
        for kk in range(FFN_CONV - 1):
            off = HALO - (FFN_CONV - 1) + kk
            y = y + cw_ref[kk:kk + 1, cols] * conv_ref[off:off + tm, cols]
        halo_ref[:, cols] = a[tm - HALO:tm, :]
        act_ref[:, cols] = (y * _sigmoid(y) * gate).astype(BF16)

    for r0 in range(0, tm, tm // 2):
        rows = slice(r0, r0 + tm // 2)
        ffn = jnp.dot(act_ref[rows, :], wd_ref[...], preferred_element_type=F32)
        out_ref[rows, :] = _layer_norm(DEEPNORM_ALPHA * h[rows, :] + ffn, g_ref[...], b_ref[...])


def _merge_ffn_kernel(tiles_per_seq, hm_ref, hd_ref, gates_ref, x_ref, wbm_ref, wbd_ref, wo_ref, g1_ref,
                      b1_ref, wu_ref, cw_ref, cb_ref, wd_ref, g2_ref, b2_ref, out_ref,
                      h_ref, conv_ref, halo_ref, act_ref):
    _merge_kernel(hm_ref, hd_ref, gates_ref, x_ref, wbm_ref, wbd_ref, wo_ref, g1_ref, b1_ref, h_ref)
    _ffn_kernel(tiles_per_seq, h_ref, wu_ref, cw_ref, cb_ref, wd_ref, g2_ref, b2_ref, out_ref,
                conv_ref, halo_ref, act_ref)


def _merge_ffn(hm, hd, proj, x2, w_bm, w_bd, w_o, ln1_g, ln1_b, w_up, conv_w, conv_b, w_down, ln2_g, ln2_b, seq):
    T = x2.shape[0]
    tm = FFN_TM
    D = D_MODEL
    rows = lambda col: pl.BlockSpec((tm, D), lambda i: (i, col))
    const = _const_spec
    return pl.pallas_call(
        functools.partial(_merge_ffn_kernel, seq // tm),
        grid=(T // tm,),
        in_specs=[
            rows(0), rows(0), pl.BlockSpec((tm, 2 * D), lambda i: (i, COL_GM // (2 * D))), rows(0),
            const((MLSTM_WIDTH, D)), const((DIFF_WIDTH, D)), const((D, D)), const((1, D)), const((1, D)),
            const((D, 2 * D_FF)), const((FFN_CONV, D_FF)), const((1, D_FF)), const((D_FF, D)),
            const((1, D)), const((1, D)),
        ],
        out_specs=rows(0),
        out_shape=jax.ShapeDtypeStruct((T, D), F32),
        scratch_shapes=[
            pltpu.VMEM((tm, D), F32),
            pltpu.VMEM((HALO + tm, D_FF), F32),
            pltpu.VMEM((HALO, D_FF), F32),
            pltpu.VMEM((tm, D_FF), BF16),
        ],
        compiler_params=pltpu.CompilerParams(
            dimension_semantics=("arbitrary",), vmem_limit_bytes=VMEM_LIMIT),
        name="merge_ffn",
    )(hm, hd, proj, x2, w_bm, w_bd, w_o, ln1_g, ln1_b, w_up, conv_w, conv_b, w_down, ln2_g, ln2_b)


def _layer(h2, batch, seq, l, w_in, b_in, mconv_w, mconv_b, mnorm_w, lq1, lk1, lq2, lk2, dnorm_w, bias_tiles,
           w_bm, w_bd, w_o, ln1_g, ln1_b, w_up, fconv_w, fconv_b, w_down, ln2_g, ln2_b):
    lam_init = 0.8 - 0.6 * math.exp(-0.3 * l)
    n_m = 4 * MLSTM_WIDTH
    n_gate = 2 * MLSTM_HEADS
    n_qk = 2 * MLSTM_WIDTH
    w_all = w_in.T.astype(BF16)
    conv_params = jnp.concatenate(
        [mconv_w.astype(F32).T, mconv_b.astype(F32)[:, None], b_in[:n_qk].astype(F32)[:, None],
         jnp.zeros((n_qk, CONV_PARAMS - MLSTM_CONV - 2), F32)], axis=1)
    w_b = w_all[n_m + n_gate:]
    b_b = b_in[n_m + n_gate:][None, :].astype(F32)
    dv0 = n_m + n_gate + TAIL_DV
    b_t = jnp.concatenate([b_in[n_qk:n_m], b_in[dv0:dv0 + DIFF_WIDTH]])[:, None].astype(F32)
    b_g = b_in[n_m:n_m + GATE_PAD][None, :].astype(F32)
    b_gt = b_in[n_m:n_m + GATE_ROWS][:, None].astype(F32)

    proj, proj_t, gcol, grow = _in_proj(h2, w_all, conv_params, w_b, b_b, b_t, b_g, b_gt, seq)
    hd, hm = _mixers(proj, proj_t, gcol, grow, bias_tiles, lq1[None, :].astype(F32), lk1[None, :].astype(F32),
                     lq2[None, :].astype(F32), lk2[None, :].astype(F32), dnorm_w[:, None].astype(F32),
                     mnorm_w[:, None].astype(F32), lam_init, batch, seq)
    return _merge_ffn(hm, hd, proj, h2, w_bm.astype(BF16), w_bd.astype(BF16), w_o.astype(BF16),
                      ln1_g[None, :].astype(F32), ln1_b[None, :].astype(F32),
                      w_up.astype(BF16), fconv_w.astype(F32), fconv_b[None, :].astype(F32),
                      w_down.astype(BF16), ln2_g[None, :].astype(F32), ln2_b[None, :].astype(F32), seq)


def kernel(x, w_in, b_in, mlstm_conv_w, mlstm_conv_b, mlstm_norm_w, lambda_q1, lambda_k1, lambda_q2, lambda_k2,
           diff_norm_w, rel_bias, w_branch_mlstm, w_branch_diff, w_out, ln1_g, ln1_b, w_ffn_up, ffn_conv_w,
           ffn_conv_b, w_ffn_down, ln2_g, ln2_b):
    batch, seq, d_model = x.shape
    assert d_model == D_MODEL and seq % max(PROJ_TM, FFN_TM, ATT_T, MLSTM_L) == 0
    bias_tiles = _rel_bias_tiles(rel_bias)
    h2 = x.reshape(batch * seq, d_model)
    for l in range(w_in.shape[0]):
        h2 = _layer(h2, batch, seq, l, w_in[l], b_in[l], mlstm_conv_w[l], mlstm_conv_b[l], mlstm_norm_w[l],
                    lambda_q1[l], lambda_k1[l], lambda_q2[l], lambda_k2[l], diff_norm_w[l], bias_tiles,
                    w_branch_mlstm[l], w_branch_diff[l], w_out[l], ln1_g[l], ln1_b[l], w_ffn_up[l],
                    ffn_conv_w[l], ffn_conv_b[l], w_ffn_down[l], ln2_g[l], ln2_b[l])
    return h2.reshape(batch, seq, d_model).astype(x.dtype)
```

```python
import functools
import math

import jax
import jax.numpy as jnp
from jax import lax
from jax.experimental import pallas as pl
from jax.experimental.pallas import tpu as pltpu

F32 = jnp.float32
BF16 = jnp.bfloat16

D_MODEL = 1024
MLSTM_HEADS = 4
MLSTM_HEAD_DIM = 256
MLSTM_WIDTH = MLSTM_HEADS * MLSTM_HEAD_DIM
MLSTM_CONV = 4
DIFF_HEADS = 8
DIFF_HEAD_DIM = 64
DIFF_V_DIM = 2 * DIFF_HEAD_DIM
DIFF_WIDTH = DIFF_HEADS * DIFF_V_DIM
REL_BUCKETS = 32
REL_MAX_DIST = 128
D_FF = 2816
FFN_CONV = 3
DEPTH = 1
DEEPNORM_ALPHA = (2.0 * DEPTH) ** 0.25
LN_EPS = 1e-5
LOG2E = math.log2(math.e)

COL_GM = 0
COL_GD = 1024
COL_MK = 2048
COL_QK = 3072
N_MAIN = 5120
ROW_MV = 0
ROW_MO = 1024
ROW_MQ = 2048
ROW_DV = 3072
N_TRANS = 4096
TAIL_DQ = 0
TAIL_DK = 1024
TAIL_DV = 2048
TAIL_GM = 3072
N_TAIL = 5120
GATE_PAD = 128
GATE_ROWS = 16
MLSTM_AUG = 16

HALO = 8
LANES = 128
CONV_PARAMS = 8
PROJ_TM = 512
PROJ_TN = 512
MLSTM_L = 256
ATT_T = 256
ATT_FAR_TILES = 1
ATT_SUM_ROWS = 16
INTERLEAVE_EVERY = 4
CAST_STEPS = 8
MERGE_ROWS = 256
FFN_TM = 512
FFN_TC = 256
VMEM_LIMIT = 58 * 1024 * 1024


def _sigmoid(v):
    return 1.0 / (1.0 + jnp.exp(-v))


def _log_sigmoid(v):
    return jnp.minimum(v, 0.0) - jnp.log(1.0 + jnp.exp(-jnp.abs(v)))


def _split3(v):
    hi = v.astype(BF16)
    rest = v - hi.astype(F32)
    mid = rest.astype(BF16)
    lo = (rest - mid.astype(F32)).astype(BF16)
    return hi, mid, lo


def _rel_bias_kernel(table_ref, out_ref):
    h = pl.program_id(0)
    T = out_ref.shape[-1]
    max_exact = REL_BUCKETS // 2
    far = table_ref[REL_BUCKETS - 1, h]
    n = lax.broadcasted_iota(jnp.int32, (8, T), 1)
    nf = jnp.maximum(n, 1).astype(F32)
    large = max_exact + (jnp.log(nf / max_exact) / math.log(REL_MAX_DIST / max_exact)
                         * (REL_BUCKETS - max_exact)).astype(jnp.int32)
    large = jnp.minimum(large, REL_BUCKETS - 1)
    bucket = jnp.where(n < max_exact, n, large)
    by_dist = jnp.zeros((8, T), F32)
    for kk in range(REL_BUCKETS):
        by_dist = jnp.where(bucket == kk, table_ref[kk, h], by_dist)
    by_dist = (by_dist - far) * LOG2E
    rolled = pltpu.roll(jnp.broadcast_to(by_dist[0:1, :], (T, T)), 0, 1, stride=1, stride_axis=0)
    kpos = lax.broadcasted_iota(jnp.int32, (T, T), 0)
    qpos = lax.broadcasted_iota(jnp.int32, (T, T), 1)
    out_ref[0, 0] = jnp.where(qpos < kpos, rolled, 0.0)
    out_ref[0, 1] = jnp.where(qpos >= kpos, rolled, -jnp.inf)


def _rel_bias_tiles(rel_bias):
    T = ATT_T
    return pl.pallas_call(
        _rel_bias_kernel,
        grid=(DIFF_HEADS,),
        in_specs=[pl.BlockSpec(memory_space=pltpu.SMEM)],
        out_specs=pl.BlockSpec((1, 2, T, T), lambda h: (h, 0, 0, 0)),
        out_shape=jax.ShapeDtypeStruct((DIFF_HEADS, 2, T, T), F32),
        name="rel_bias_tiles",
    )(rel_bias.astype(F32))


def _in_proj_kernel(tiles_per_seq, x_ref, wa_ref, cp_ref, wb_ref, bb_ref, wt_ref, bt_ref, wg_ref, bg_ref,
                    bgt_ref, o_ref, ot_ref, gcol_ref, grow_ref, halo_ref):
    i = pl.program_id(0)
    tm = x_ref.shape[0]
    tn = PROJ_TN
    n_conv_cols = 2 * MLSTM_WIDTH
    nt_dims = (((1,), (1,)), ((), ()))

    @pl.when(i % tiles_per_seq == 0)
    def _():
        halo_ref[...] = jnp.zeros_like(halo_ref)

    xb = x_ref[...].astype(BF16)
    gcol = lax.dot_general(xb, wg_ref[...], nt_dims, preferred_element_type=F32) + bg_ref[...]
    grow = lax.dot_general(wg_ref[0:GATE_ROWS, :], xb, nt_dims,
                           preferred_element_type=F32) + bgt_ref[...]
    L = MLSTM_L
    H = MLSTM_HEADS
    tri = jnp.where(lax.broadcasted_iota(jnp.int32, (L, L), 0) >= lax.broadcasted_iota(jnp.int32, (L, L), 1),
                    1.0, 0.0).astype(BF16)
    gate_row = lax.broadcasted_iota(jnp.int32, (GATE_ROWS, L), 0)
    for r0 in range(0, tm, L):
        g_c = gcol[r0:r0 + L, :]
        g_r = grow[:, r0:r0 + L]
        bc3 = jnp.dot(tri, jnp.concatenate(_split3(_log_sigmoid(g_c)), axis=1), preferred_element_type=F32)
        bcol = bc3[:, 0:GATE_PAD] + bc3[:, GATE_PAD:2 * GATE_PAD] + bc3[:, 2 * GATE_PAD:3 * GATE_PAD]
        br3 = lax.dot_general(jnp.concatenate(_split3(_log_sigmoid(g_r)), axis=0), tri, nt_dims,
                              preferred_element_type=F32)
        brow = br3[0:GATE_ROWS] + br3[GATE_ROWS:2 * GATE_ROWS] + br3[2 * GATE_ROWS:3 * GATE_ROWS]
        gcol_ref[r0:r0 + L, :] = g_c - pltpu.roll(bcol, GATE_PAD - H, 1)
        grow_ref[:, r0:r0 + L] = jnp.where(gate_row < H, g_r, brow)

    n_vo = 2 * MLSTM_WIDTH
    for r0 in list(range(ROW_MV, ROW_MV + n_vo, tn)) + list(range(ROW_DV, ROW_DV + DIFF_WIDTH, tn)):
        if r0 < ROW_DV:
            src = slice(r0 - ROW_MV, r0 - ROW_MV + tn)
            w_rows, bias_col = wt_ref[src, :], bt_ref[src, :]
        else:
            w_rows = wb_ref[TAIL_DV + r0 - ROW_DV:TAIL_DV + r0 - ROW_DV + tn, :]
            bias_col = bt_ref[n_vo + r0 - ROW_DV:n_vo + r0 - ROW_DV + tn, :]
        acc_t = lax.dot_general(w_rows, xb, nt_dims, preferred_element_type=F32) + bias_col
        if ROW_MO <= r0 < ROW_MO + MLSTM_WIDTH:
            acc_t = _sigmoid(acc_t)
        ot_ref[r0:r0 + tn, :] = acc_t.astype(BF16)

    lane = lax.broadcasted_iota(jnp.int32, (tn, LANES), 1)
    for c0 in range(0, n_conv_cols, tn):
        cols = slice(c0, c0 + tn)
        par = cp_ref[cols, :]
        acc_t = lax.dot_general(wa_ref[cols, :], xb, nt_dims, preferred_element_type=F32) \
            + par[:, MLSTM_CONV + 1:MLSTM_CONV + 2]
        prev = halo_ref[cols, :]
        y = par[:, MLSTM_CONV:MLSTM_CONV + 1] + par[:, MLSTM_CONV - 1:MLSTM_CONV] * acc_t
        for kk in range(MLSTM_CONV - 1):
            shift = MLSTM_CONV - 1 - kk
            rolled = pltpu.roll(acc_t, shift, 1)
            head = jnp.where(lane < shift, pltpu.roll(prev, shift, 1), rolled[:, 0:LANES])
            y = y + par[:, kk:kk + 1] * jnp.concatenate([head, rolled[:, LANES:]], axis=1)
        halo_ref[cols, :] = acc_t[:, tm - LANES:tm]
        y = y * _sigmoid(y)
        if c0 < MLSTM_WIDTH:
            ot_ref[ROW_MQ + c0:ROW_MQ + c0 + tn, :] = y.astype(BF16)
        else:
            k0 = COL_MK + c0 - MLSTM_WIDTH
            o_ref[:, k0:k0 + tn] = (y * (MLSTM_HEAD_DIM ** -0.5)).T.astype(BF16)

    for g0 in range(0, 2 * D_MODEL, tn):
        wrows = slice(TAIL_GM + g0, TAIL_GM + g0 + tn)
        acc = lax.dot_general(xb, wb_ref[wrows, :], nt_dims, preferred_element_type=F32) + bb_ref[:, wrows]
        o_ref[:, COL_GM + g0:COL_GM + g0 + tn] = _sigmoid(acc).astype(BF16)

    dv = DIFF_V_DIM
    heads_per_chunk = tn // (2 * dv)
    q_scale = jnp.full((1, dv), DIFF_HEAD_DIM ** -0.5 * LOG2E, F32)
    scale_row = jnp.concatenate([q_scale, jnp.ones((1, dv), F32)] * heads_per_chunk, axis=1)
    for h0 in range(0, DIFF_HEADS, heads_per_chunk):
        starts = []
        for h in range(h0, h0 + heads_per_chunk):
            starts += [TAIL_DQ + h * dv, TAIL_DK + h * dv]
        w_rows = jnp.concatenate([wb_ref[r:r + dv, :] for r in starts], axis=0)
        b_row = jnp.concatenate([bb_ref[:, r:r + dv] for r in starts], axis=1)
        acc = lax.dot_general(xb, w_rows, nt_dims, preferred_element_type=F32) + b_row
        c0 = COL_QK + h0 * 2 * dv
        o_ref[:, c0:c0 + tn] = (acc * scale_row).astype(BF16)


def _const_spec(shape):
    return pl.BlockSpec(shape, lambda i: (0, 0), pipeline_mode=pl.Buffered(1))


def _in_proj(x2, w_all, conv_params, w_b, b_b, b_t, b_g, b_gt, seq):
    T = x2.shape[0]
    tm = PROJ_TM
    n_conv_cols = 2 * MLSTM_WIDTH
    n_vo = 2 * MLSTM_WIDTH
    assert n_vo == n_conv_cols and (n_conv_cols + n_vo) % GATE_PAD == 0
    const = _const_spec
    row_block = lambda shape, idx: pl.BlockSpec(shape, lambda i: (idx, 0), pipeline_mode=pl.Buffered(1))
    return pl.pallas_call(
        functools.partial(_in_proj_kernel, seq // tm),
        grid=(T // tm,),
        in_specs=[
            pl.BlockSpec((tm, D_MODEL), lambda i: (i, 0)),
            row_block((n_conv_cols, D_MODEL), 0),
            const((n_conv_cols, CONV_PARAMS)),
            const((N_TAIL, D_MODEL)),
            const((1, N_TAIL)),
            row_block((n_vo, D_MODEL), 1),
            const((n_vo + DIFF_WIDTH, 1)),
            row_block((GATE_PAD, D_MODEL), (n_conv_cols + n_vo) // GATE_PAD),
            const((1, GATE_PAD)),
            const((GATE_ROWS, 1)),
        ],
        out_specs=[
            pl.BlockSpec((tm, N_MAIN), lambda i: (i, 0)),
            pl.BlockSpec((N_TRANS, tm), lambda i: (0, i)),
            pl.BlockSpec((tm, GATE_PAD), lambda i: (i, 0)),
            pl.BlockSpec((GATE_ROWS, tm), lambda i: (0, i)),
        ],
        out_shape=[
            jax.ShapeDtypeStruct((T, N_MAIN), BF16),
            jax.ShapeDtypeStruct((N_TRANS, T), BF16),
            jax.ShapeDtypeStruct((T, GATE_PAD), F32),
            jax.ShapeDtypeStruct((GATE_ROWS, T), F32),
        ],
        scratch_shapes=[
            pltpu.VMEM((n_conv_cols, LANES), F32),
        ],
        compiler_params=pltpu.CompilerParams(
            dimension_semantics=("arbitrary",), vmem_limit_bytes=VMEM_LIMIT),
        name="in_proj",
    )(x2, w_all, conv_params, w_b, b_b, w_all, b_t, w_all, b_g, b_gt)


def _mlstm_chunk(c, qt_ref, k_ref, vt_ref, ot_ref, gcol_ref, grow_ref, nw_ref, out_ref, ct_ref, m_ref, nwb_ref):
    L = k_ref.shape[0]
    d = MLSTM_HEAD_DIM
    H = MLSTM_HEADS
    nt_dims = (((1,), (1,)), ((), ()))

    @pl.when(c == 0)
    def _():
        ct_ref[...] = jnp.zeros_like(ct_ref)
        m_ref[...] = jnp.zeros_like(m_ref)
        nwb_ref[...] = jnp.broadcast_to(nw_ref[...], nwb_ref.shape)

    key = lax.broadcasted_iota(jnp.int32, (L, L), 0)
    qry = lax.broadcasted_iota(jnp.int32, (L, L), 1)
    causal_t = key <= qry

    gdiff = gcol_ref[...]
    grow = grow_ref[...]
    ones = jnp.ones((MLSTM_AUG, L), BF16)

    def head(h):
        cols = slice(h * d, (h + 1) * d)
        q_t = qt_ref[cols, :]
        k = k_ref[:, cols]
        v_aug = jnp.concatenate([vt_ref[cols, :], ones], axis=0)
        b_r = grow[H + h:H + h + 1, :]
        i_r = grow[h:h + 1, :]
        g_c = gdiff[:, h:h + 1]
        m_prev = m_ref[h]
        ct = ct_ref[h]

        dmat = jnp.where(causal_t, b_r + g_c, -jnp.inf)
        a = b_r + m_prev
        m_row = jnp.maximum(a, jnp.max(dmat, axis=0, keepdims=True))
        w = jnp.exp(dmat - m_row)
        inter = jnp.exp(a - m_row)
        kq = jnp.dot(k, q_t, preferred_element_type=F32)
        sqk = (kq * w).astype(BF16)
        nd = inter * jnp.dot(ct.astype(BF16), q_t, preferred_element_type=F32) \
            + jnp.dot(v_aug, sqk, preferred_element_type=F32)
        rinv = 1.0 / jnp.maximum(jnp.abs(nd[d:d + 1, :]), jnp.exp(-m_row))
        hid = nd[0:d, :] * rinv * ot_ref[cols, :].astype(F32)
        mu = jnp.mean(hid, axis=0, keepdims=True)
        cen = hid - mu
        var = jnp.mean(cen * cen, axis=0, keepdims=True)
        out_ref[:, cols] = (cen * lax.rsqrt(var + LN_EPS) * nwb_ref[cols, :]).T.astype(BF16)

        b_last = b_r[:, L - 1:L]
        g_r = b_last - b_r + i_r
        m_new = jnp.maximum(b_last + m_prev, jnp.max(g_r, axis=-1, keepdims=True))
        decay = jnp.exp(b_last + m_prev - m_new)
        ws = jnp.exp(g_r - m_new)
        vw = (v_aug.astype(F32) * ws).astype(BF16)
        ct_ref[h] = decay * ct + jnp.dot(vw, k, preferred_element_type=F32)
        m_ref[h] = m_new

    return [functools.partial(head, h) for h in range(H)]


def _diff_attn_head(lam_init, interleaved, q_ref, k_ref, v_ref, bias_ref, lq1_ref, lk1_ref, lq2_ref, lk2_ref,
                    nw_ref, out_ref, qs_ref, vt_ref, s_ref, p_ref, acc_ref):
    T = ATT_T
    dh = DIFF_HEAD_DIM
    dv = DIFF_V_DIM
    nq = q_ref.shape[0] // T
    R = 2 * T

    vt_ref[0:dv, :] = v_ref[...]
    vt_ref[dv:dv + ATT_SUM_ROWS, :] = jnp.ones((ATT_SUM_ROWS, v_ref.shape[1]), BF16)

    lane = lax.broadcasted_iota(jnp.int32, (T, dv), 1)
    for ii in range(nq):
        qt = q_ref[ii * T:(ii + 1) * T, :]
        zero = jnp.zeros_like(qt)
        qs_ref[ii, 0:T, :] = jnp.where(lane < dh, qt, zero)
        qs_ref[ii, T:R, :] = jnp.where(lane >= dh, qt, zero)

    lam = (jnp.exp(jnp.sum(lq1_ref[...] * lk1_ref[...], axis=-1, keepdims=True))
           - jnp.exp(jnp.sum(lq2_ref[...] * lk2_ref[...], axis=-1, keepdims=True)) + lam_init)

    pairs = []
    for i in range(nq):
        k0 = 0
        while k0 < (i - 1) * T:
            klen = min(ATT_FAR_TILES * T, (i - 1) * T - k0)
            pairs.append((i, k0, klen, None))
            k0 += klen
        if i >= 1:
            pairs.append((i, (i - 1) * T, T, 0))
        pairs.append((i, i * T, T, 1))

    def scores(n):
        i, k0, klen, bias_idx = pairs[n]
        s = lax.dot_general(k_ref[k0:k0 + klen, :], qs_ref[i], (((1,), (1,)), ((), ())),
                            preferred_element_type=F32)
        if bias_idx is not None:
            bias = bias_ref[0, bias_idx]
            s = s + jnp.concatenate([bias, bias], axis=1)
        s_ref[n % 2, 0:klen, :] = s
        return jnp.max(s, axis=0, keepdims=True)

    def softmax(n, m_tile, m_prev):
        i, k0, klen, _ = pairs[n]
        m_new = m_tile if k0 == 0 else jnp.maximum(m_prev, m_tile)
        p_ref[n % 2, 0:klen, :] = jnp.exp2(s_ref[n % 2, 0:klen, :] - m_new).astype(BF16)
        alpha = None if k0 == 0 else jnp.exp2(m_prev - m_new)
        return m_new, alpha

    def values(n, alpha):
        i, k0, klen, _ = pairs[n]
        pv = jnp.dot(vt_ref[:, k0:k0 + klen], p_ref[n % 2, 0:klen, :],
                     preferred_element_type=F32)
        if k0 == 0:
            acc_ref[...] = pv
        else:
            acc_ref[...] = alpha * acc_ref[...] + pv
        if k0 == i * T:
            acc = acc_ref[...]
            out = acc[0:dv, :] / acc[dv:dv + 1, :]
            hd = out[:, 0:T] - lam * out[:, T:R]
            hd = hd * lax.rsqrt(jnp.mean(hd * hd, axis=0, keepdims=True) + LN_EPS) * nw_ref[...]
            out_ref[i * T:(i + 1) * T, :] = (hd * (1.0 - lam_init)).T.astype(BF16)

    spacing = INTERLEAVE_EVERY if interleaved else 0
    m_tiles = {0: scores(0)}
    m_run = None
    alphas = {}
    for n in range(len(pairs)):
        if n + 1 < len(pairs):
            m_tiles[n + 1] = scores(n + 1)
        m_run, alphas[n] = softmax(n, m_tiles.pop(n), m_run)
        if n >= 1:
            values(n - 1, alphas.pop(n - 1))
        if spacing and (n + 1) % spacing == 0 and (n + 1) // spacing <= len(interleaved):
            interleaved[(n + 1) // spacing - 1]()
    values(len(pairs) - 1, alphas.pop(len(pairs) - 1))


def _mixers_kernel(lam_init, chunks_per_seq, n_attn_refs, n_cast, *refs):
    (qk_ref, v_ref, bias_ref, lq1_ref, lk1_ref, lq2_ref, lk2_ref, anw_ref) = refs[:n_attn_refs]
    q_ref, k_ref = qk_ref.at[:, 0:DIFF_V_DIM], qk_ref.at[:, DIFF_V_DIM:2 * DIFF_V_DIM]
    (voq_ref, mk_ref, gcol_ref, grow_ref, mnw_ref) = refs[n_attn_refs:n_attn_refs + 5]
    n_in = n_attn_refs + 5 + n_cast
    cast_in = refs[n_attn_refs + 5:n_in]
    hd_ref, hm_ref = refs[n_in:n_in + 2]
    cast_out = refs[n_in + 2:n_in + 2 + n_cast]
    (qs_ref, vt_ref, s_ref, p_ref, acc_ref, ct_ref, m_ref, nwb_ref) = refs[n_in + 2 + n_cast:]
    W = MLSTM_WIDTH
    mvt_ref, mot_ref, mq_ref = (voq_ref.at[r:r + W] for r in (ROW_MV, ROW_MO, ROW_MQ))
    c = pl.program_id(0) % chunks_per_seq
    heads = _mlstm_chunk(c, mq_ref, mk_ref, mvt_ref, mot_ref, gcol_ref, grow_ref, mnw_ref, hm_ref,
                         ct_ref, m_ref, nwb_ref)
    _diff_attn_head(lam_init, heads, q_ref, k_ref, v_ref, bias_ref, lq1_ref, lk1_ref, lq2_ref, lk2_ref,
                    anw_ref, hd_ref, qs_ref, vt_ref, s_ref, p_ref, acc_ref)

    @pl.when(pl.program_id(0) < CAST_STEPS)
    def _():
        for w_ref, o_ref in zip(cast_in, cast_out):
            o_ref[...] = w_ref[...].astype(BF16)


def _mixers(proj, proj_t, gcol, grow, bias_tiles, lq1, lk1, lq2, lk2, dnorm_w, mnorm_w, lam_init, batch, seq,
            next_weights):
    T = ATT_T
    nq = seq // T
    dv = DIFF_V_DIM
    L = MLSTM_L
    nc = seq // L
    W = MLSTM_WIDTH
    assert DIFF_HEADS * batch == batch * nc, "attention (head, batch) steps must pair 1:1 with mLSTM chunks"
    small = lambda shape: pl.BlockSpec(shape, lambda n: (0, 0))
    attn_specs = [
        pl.BlockSpec((seq, 2 * dv), lambda n: (n % batch, COL_QK // (2 * dv) + n // batch)),
        pl.BlockSpec((dv, seq), lambda n: (ROW_DV // dv + n // batch, n % batch)),
        pl.BlockSpec((1, 2, T, T), lambda n: (n // batch, 0, 0, 0)),
        small((1, DIFF_HEAD_DIM)), small((1, DIFF_HEAD_DIM)),
        small((1, DIFF_HEAD_DIM)), small((1, DIFF_HEAD_DIM)),
        small((dv, 1)),
    ]
    mlstm_specs = [
        pl.BlockSpec((ROW_DV, L), lambda n: (0, n)),
        pl.BlockSpec((L, W), lambda n: (n, COL_MK // W)),
        pl.BlockSpec((L, GATE_PAD), lambda n: (n, 0)),
        pl.BlockSpec((GATE_ROWS, L), lambda n: (0, n)),
        small((W, 1)),
    ]
    def cast_specs():
        return [pl.BlockSpec((w.shape[0] // CAST_STEPS, w.shape[1]),
                             lambda n: (jnp.minimum(n, CAST_STEPS - 1), 0)) for w in next_weights]

    assert all(w.shape[0] % (16 * CAST_STEPS) == 0 for w in next_weights)
    return pl.pallas_call(
        functools.partial(_mixers_kernel, lam_init, nc, len(attn_specs), len(next_weights)),
        grid=(batch * nc,),
        in_specs=attn_specs + mlstm_specs + cast_specs(),
        out_specs=[
            pl.BlockSpec((seq, dv), lambda n: (n % batch, n // batch)),
            pl.BlockSpec((L, W), lambda n: (n, 0)),
        ] + cast_specs(),
        out_shape=[
            jax.ShapeDtypeStruct((batch * seq, DIFF_WIDTH), BF16),
            jax.ShapeDtypeStruct((batch * seq, W), BF16),
        ] + [jax.ShapeDtypeStruct(w.shape, BF16) for w in next_weights],
        scratch_shapes=[
            pltpu.VMEM((nq, 2 * T, dv), BF16),
            pltpu.VMEM((dv + ATT_SUM_ROWS, seq), BF16),
            pltpu.VMEM((2, ATT_FAR_TILES * T, 2 * T), F32),
            pltpu.VMEM((2, ATT_FAR_TILES * T, 2 * T), BF16),
            pltpu.VMEM((dv + ATT_SUM_ROWS, 2 * T), F32),
            pltpu.VMEM((MLSTM_HEADS, MLSTM_HEAD_DIM + MLSTM_AUG, MLSTM_HEAD_DIM), F32),
            pltpu.VMEM((MLSTM_HEADS, 1, 1), F32),
            pltpu.VMEM((W, L), F32),
        ],
        compiler_params=pltpu.CompilerParams(
            dimension_semantics=("arbitrary",), vmem_limit_bytes=VMEM_LIMIT),
        name="token_mixers",
    )(proj, proj_t, bias_tiles, lq1, lk1, lq2, lk2, dnorm_w, proj_t, proj, gcol, grow, mnorm_w, *next_weights)


def _layer_norm(y, g, b):
    mu = jnp.mean(y, axis=-1, keepdims=True)
    cen = y - mu
    var = jnp.mean(cen * cen, axis=-1, keepdims=True)
    return cen * lax.rsqrt(var + LN_EPS) * g + b


def _merge_kernel(hm_ref, hd_ref, gates_ref, x_ref, wbm_ref, wbd_ref, wo_ref, g_ref, b_ref, out_ref):
    tm = hm_ref.shape[0]
    gm_ref = gates_ref.at[:, 0:D_MODEL]
    gd_ref = gates_ref.at[:, D_MODEL:2 * D_MODEL]
    nb = tm // MERGE_ROWS
    halves = [slice(r * MERGE_ROWS, (r + 1) * MERGE_ROWS) for r in range(nb)]
    merged = []
    for rows in halves:
        pm = jnp.dot(hm_ref[rows, :], wbm_ref[...], preferred_element_type=F32)
        pd = jnp.dot(hd_ref[rows, :], wbd_ref[...], preferred_element_type=F32)
        merged.append((gm_ref[rows, :].astype(F32) * pm + gd_ref[rows, :].astype(F32) * pd).astype(BF16))
    for rows, m in zip(halves, merged):
        mix = jnp.dot(m, wo_ref[...], preferred_element_type=F32)
        out_ref[rows, :] = _layer_norm(DEEPNORM_ALPHA * x_ref[rows, :] + mix, g_ref[...], b_ref[...])


def _ffn_kernel(tiles_per_seq, h_ref, wu_ref, cw_ref, cb_ref, wd_ref, g_ref, b_ref, out_ref,
                conv_ref, halo_ref, act_ref):
    i = pl.program_id(0)
    tm = h_ref.shape[0]
    tc = FFN_TC

    @pl.when(i % tiles_per_seq == 0)
    def _():
        halo_ref[...] = jnp.zeros_like(halo_ref)

    h = h_ref[...]
    hb = h.astype(BF16)
    for c0 in range(0, D_FF, tc):
        cols = slice(c0, c0 + tc)
        a = jnp.dot(hb, wu_ref[:, cols], preferred_element_type=F32)
        gate = jnp.dot(hb, wu_ref[:, D_FF + c0:D_FF + c0 + tc], preferred_element_type=F32)
        conv_ref[0:HALO, cols] = halo_ref[:, cols]
        conv_ref[HALO:HALO + tm, cols] = a
        y = cb_ref[:, cols] + cw_ref[FFN_CONV - 1:FFN_CONV, cols] * a
        for kk in range(FFN_CONV - 1):
            off = HALO - (FFN_CONV - 1) + kk
            y = y + cw_ref[kk:kk + 1, cols] * conv_ref[off:off + tm, cols]
        halo_ref[:, cols] = a[tm - HALO:tm, :]
        act_ref[:, cols] = (y * _sigmoid(y) * gate).astype(BF16)

    for r0 in range(0, tm, tm // 2):
        rows = slice(r0, r0 + tm // 2)
        ffn = jnp.dot(act_ref[rows, :], wd_ref[...], preferred_element_type=F32)
        out_ref[rows, :] = _layer_norm(DEEPNORM_ALPHA * h[rows, :] + ffn, g_ref[...], b_ref[...])


def _merge_ffn_kernel(tiles_per_seq, hm_ref, hd_ref, gates_ref, x_ref, wbm_ref, wbd_ref, wo_ref, g1_ref,
                      b1_ref, wu_ref, cw_ref, cb_ref, wd_ref, g2_ref, b2_ref, out_ref,
                      h_ref, conv_ref, halo_ref, act_ref):
    _merge_kernel(hm_ref, hd_ref, gates_ref, x_ref, wbm_ref, wbd_ref, wo_ref, g1_ref, b1_ref, h_ref)
    _ffn_kernel(tiles_per_seq, h_ref, wu_ref, cw_ref, cb_ref, wd_ref, g2_ref, b2_ref, out_ref,
                conv_ref, halo_ref, act_ref)


def _merge_ffn(hm, hd, proj, x2, w_bm, w_bd, w_o, ln1_g, ln1_b, w_up, conv_w, conv_b, w_down, ln2_g, ln2_b, seq):
    T = x2.shape[0]
    tm = FFN_TM
    D = D_MODEL
    rows = lambda col: pl.BlockSpec((tm, D), lambda i: (i, col))
    const = _const_spec
    return pl.pallas_call(
        functools.partial(_merge_ffn_kernel, seq // tm),
        grid=(T // tm,),
        in_specs=[
            rows(0), rows(0), pl.BlockSpec((tm, 2 * D), lambda i: (i, COL_GM // (2 * D))), rows(0),
            const((MLSTM_WIDTH, D)), const((DIFF_WIDTH, D)), const((D, D)), const((1, D)), const((1, D)),
            const((D, 2 * D_FF)), const((FFN_CONV, D_FF)), const((1, D_FF)), const((D_FF, D)),
            const((1, D)), const((1, D)),
        ],
        out_specs=rows(0),
        out_shape=jax.ShapeDtypeStruct((T, D), F32),
        scratch_shapes=[
            pltpu.VMEM((tm, D), F32),
            pltpu.VMEM((HALO + tm, D_FF), F32),
            pltpu.VMEM((HALO, D_FF), F32),
            pltpu.VMEM((tm, D_FF), BF16),
        ],
        compiler_params=pltpu.CompilerParams(
            dimension_semantics=("arbitrary",), vmem_limit_bytes=VMEM_LIMIT),
        name="merge_ffn",
    )(hm, hd, proj, x2, w_bm, w_bd, w_o, ln1_g, ln1_b, w_up, conv_w, conv_b, w_down, ln2_g, ln2_b)


def _layer(h2, batch, seq, l, w_in, b_in, mconv_w, mconv_b, mnorm_w, lq1, lk1, lq2, lk2, dnorm_w, bias_tiles,
           w_bm, w_bd, w_o, ln1_g, ln1_b, w_up, fconv_w, fconv_b, w_down, ln2_g, ln2_b):
    lam_init = 0.8 - 0.6 * math.exp(-0.3 * l)
    n_m = 4 * MLSTM_WIDTH
    n_gate = 2 * MLSTM_HEADS
    n_qk = 2 * MLSTM_WIDTH
    w_all = w_in.T.astype(BF16)
    conv_params = jnp.concatenate(
        [mconv_w.astype(F32).T, mconv_b.astype(F32)[:, None], b_in[:n_qk].astype(F32)[:, None],
         jnp.zeros((n_qk, CONV_PARAMS - MLSTM_CONV - 2), F32)], axis=1)
    w_b = w_all[n_m + n_gate:]
    b_b = b_in[n_m + n_gate:][None, :].astype(F32)
    dv0 = n_m + n_gate + TAIL_DV
    b_t = jnp.concatenate([b_in[n_qk:n_m], b_in[dv0:dv0 + DIFF_WIDTH]])[:, None].astype(F32)
    b_g = b_in[n_m:n_m + GATE_PAD][None, :].astype(F32)
    b_gt = b_in[n_m:n_m + GATE_ROWS][:, None].astype(F32)

    proj, proj_t, gcol, grow = _in_proj(h2, w_all, conv_params, w_b, b_b, b_t, b_g, b_gt, seq)
    hd, hm, w_bm16, w_bd16, w_o16, w_up16, w_down16 = _mixers(
        proj, proj_t, gcol, grow, bias_tiles, lq1[None, :].astype(F32), lk1[None, :].astype(F32),
        lq2[None, :].astype(F32), lk2[None, :].astype(F32), dnorm_w[:, None].astype(F32),
        mnorm_w[:, None].astype(F32), lam_init, batch, seq,
        [w.astype(F32) for w in (w_bm, w_bd, w_o, w_up, w_down)])
    return _merge_ffn(hm, hd, proj, h2, w_bm16, w_bd16, w_o16,
                      ln1_g[None, :].astype(F32), ln1_b[None, :].astype(F32),
                      w_up16, fconv_w.astype(F32), fconv_b[None, :].astype(F32),
                      w_down16, ln2_g[None, :].astype(F32), ln2_b[None, :].astype(F32), seq)


def kernel(x, w_in, b_in, mlstm_conv_w, mlstm_conv_b, mlstm_norm_w, lambda_q1, lambda_k1, lambda_q2, lambda_k2,
           diff_norm_w, rel_bias, w_branch_mlstm, w_branch_diff, w_out, ln1_g, ln1_b, w_ffn_up, ffn_conv_w,
           ffn_conv_b, w_ffn_down, ln2_g, ln2_b):
    batch, seq, d_model = x.shape
    assert d_model == D_MODEL and seq % max(PROJ_TM, FFN_TM, ATT_T, MLSTM_L) == 0
    bias_tiles = _rel_bias_tiles(rel_bias)
    h2 = x.reshape(batch * seq, d_model)
    for l in range(w_in.shape[0]):
        h2 = _layer(h2, batch, seq, l, w_in[l], b_in[l], mlstm_conv_w[l], mlstm_conv_b[l], mlstm_norm_w[l],
                    lambda_q1[l], lambda_k1[l], lambda_q2[l], lambda_k2[l], diff_norm_w[l], bias_tiles,
                    w_branch_mlstm[l], w_branch_diff[l], w_out[l], ln1_g[l], ln1_b[l], w_ffn_up[l],
                    ffn_conv_w[l], ffn_conv_b[l], w_ffn_down[l], ln2_g[l], ln2_b[l])
    return h2.reshape(batch, seq, d_model).astype(x.dtype)
```

```python
import functools
import math

import jax
import jax.numpy as jnp
from jax import lax
from jax.experimental import pallas as pl
from jax.experimental.pallas import tpu as pltpu

F32 = jnp.float32
BF16 = jnp.bfloat16

D_MODEL = 1024
MLSTM_HEADS = 4
MLSTM_HEAD_DIM = 256
MLSTM_WIDTH = MLSTM_HEADS * MLSTM_HEAD_DIM
MLSTM_CONV = 4
DIFF_HEADS = 8
DIFF_HEAD_DIM = 64
DIFF_V_DIM = 2 * DIFF_HEAD_DIM
DIFF_WIDTH = DIFF_HEADS * DIFF_V_DIM
REL_BUCKETS = 32
REL_MAX_DIST = 128
D_FF = 2816
FFN_CONV = 3
DEPTH = 1
DEEPNORM_ALPHA = (2.0 * DEPTH) ** 0.25
LN_EPS = 1e-5
LOG2E = math.log2(math.e)

COL_GM = 0
COL_GD = 1024
COL_MK = 2048
COL_QK = 3072
N_MAIN = 5120
ROW_MV = 0
ROW_MO = 1024
ROW_MQ = 2048
ROW_DV = 3072
N_TRANS = 4096
TAIL_DQ = 0
TAIL_DK = 1024
TAIL_DV = 2048
TAIL_GM = 3072
N_TAIL = 5120
GATE_PAD = 128
GATE_ROWS = 16
MLSTM_AUG = 16

HALO = 8
LANES = 128
CONV_PARAMS = 8
PROJ_TM = 512
PROJ_TN = 512
MLSTM_L = 256
ATT_T = 256
ATT_FAR_TILES = 1
ATT_SUM_ROWS = 16
INTERLEAVE_EVERY = 4
W_IN_CAST_ROWS = 512
CAST_STEPS = 8
MERGE_ROWS = 256
FFN_TM = 512
FFN_TC = 256
VMEM_LIMIT = 58 * 1024 * 1024


def _sigmoid(v):
    return 1.0 / (1.0 + jnp.exp(-v))


def _log_sigmoid(v):
    return jnp.minimum(v, 0.0) - jnp.log(1.0 + jnp.exp(-jnp.abs(v)))


def _split3(v):
    hi = v.astype(BF16)
    rest = v - hi.astype(F32)
    mid = rest.astype(BF16)
    lo = (rest - mid.astype(F32)).astype(BF16)
    return hi, mid, lo


def _rel_bias_kernel(table_ref, out_ref):
    h = pl.program_id(0)
    T = out_ref.shape[-1]
    max_exact = REL_BUCKETS // 2
    far = table_ref[REL_BUCKETS - 1, h]
    n = lax.broadcasted_iota(jnp.int32, (8, T), 1)
    nf = jnp.maximum(n, 1).astype(F32)
    large = max_exact + (jnp.log(nf / max_exact) / math.log(REL_MAX_DIST / max_exact)
                         * (REL_BUCKETS - max_exact)).astype(jnp.int32)
    large = jnp.minimum(large, REL_BUCKETS - 1)
    bucket = jnp.where(n < max_exact, n, large)
    by_dist = jnp.zeros((8, T), F32)
    for kk in range(REL_BUCKETS):
        by_dist = jnp.where(bucket == kk, table_ref[kk, h], by_dist)
    by_dist = (by_dist - far) * LOG2E
    rolled = pltpu.roll(jnp.broadcast_to(by_dist[0:1, :], (T, T)), 0, 1, stride=1, stride_axis=0)
    kpos = lax.broadcasted_iota(jnp.int32, (T, T), 0)
    qpos = lax.broadcasted_iota(jnp.int32, (T, T), 1)
    out_ref[0, 0] = jnp.where(qpos < kpos, rolled, 0.0)
    out_ref[0, 1] = jnp.where(qpos >= kpos, rolled, -jnp.inf)


def _rel_bias_tiles(rel_bias):
    T = ATT_T
    return pl.pallas_call(
        _rel_bias_kernel,
        grid=(DIFF_HEADS,),
        in_specs=[pl.BlockSpec(memory_space=pltpu.SMEM)],
        out_specs=pl.BlockSpec((1, 2, T, T), lambda h: (h, 0, 0, 0)),
        out_shape=jax.ShapeDtypeStruct((DIFF_HEADS, 2, T, T), F32),
        name="rel_bias_tiles",
    )(rel_bias.astype(F32))


def _in_proj_kernel(tiles_per_seq, x_ref, wa_ref, cp_ref, wb_ref, bb_ref, wt_ref, bt_ref, wg_ref, bg_ref,
                    bgt_ref, o_ref, ot_ref, gcol_ref, grow_ref, halo_ref):
    i = pl.program_id(0)
    tm = x_ref.shape[0]
    tn = PROJ_TN
    n_conv_cols = 2 * MLSTM_WIDTH
    nt_dims = (((1,), (1,)), ((), ()))

    @pl.when(i % tiles_per_seq == 0)
    def _():
        halo_ref[...] = jnp.zeros_like(halo_ref)

    xb = x_ref[...].astype(BF16)
    gcol = lax.dot_general(xb, wg_ref[...], nt_dims, preferred_element_type=F32) + bg_ref[...]
    grow = lax.dot_general(wg_ref[0:GATE_ROWS, :], xb, nt_dims,
                           preferred_element_type=F32) + bgt_ref[...]
    L = MLSTM_L
    H = MLSTM_HEADS
    tri = jnp.where(lax.broadcasted_iota(jnp.int32, (L, L), 0) >= lax.broadcasted_iota(jnp.int32, (L, L), 1),
                    1.0, 0.0).astype(BF16)
    gate_row = lax.broadcasted_iota(jnp.int32, (GATE_ROWS, L), 0)
    for r0 in range(0, tm, L):
        g_c = gcol[r0:r0 + L, :]
        g_r = grow[:, r0:r0 + L]
        bc3 = jnp.dot(tri, jnp.concatenate(_split3(_log_sigmoid(g_c)), axis=1), preferred_element_type=F32)
        bcol = bc3[:, 0:GATE_PAD] + bc3[:, GATE_PAD:2 * GATE_PAD] + bc3[:, 2 * GATE_PAD:3 * GATE_PAD]
        br3 = lax.dot_general(jnp.concatenate(_split3(_log_sigmoid(g_r)), axis=0), tri, nt_dims,
                              preferred_element_type=F32)
        brow = br3[0:GATE_ROWS] + br3[GATE_ROWS:2 * GATE_ROWS] + br3[2 * GATE_ROWS:3 * GATE_ROWS]
        gcol_ref[r0:r0 + L, :] = g_c - pltpu.roll(bcol, GATE_PAD - H, 1)
        grow_ref[:, r0:r0 + L] = jnp.where(gate_row < H, g_r, brow)

    n_vo = 2 * MLSTM_WIDTH
    for r0 in list(range(ROW_MV, ROW_MV + n_vo, tn)) + list(range(ROW_DV, ROW_DV + DIFF_WIDTH, tn)):
        if r0 < ROW_DV:
            src = slice(r0 - ROW_MV, r0 - ROW_MV + tn)
            w_rows, bias_col = wt_ref[src, :], bt_ref[src, :]
        else:
            w_rows = wb_ref[TAIL_DV + r0 - ROW_DV:TAIL_DV + r0 - ROW_DV + tn, :]
            bias_col = bt_ref[n_vo + r0 - ROW_DV:n_vo + r0 - ROW_DV + tn, :]
        acc_t = lax.dot_general(w_rows, xb, nt_dims, preferred_element_type=F32) + bias_col
        if ROW_MO <= r0 < ROW_MO + MLSTM_WIDTH:
            acc_t = _sigmoid(acc_t)
        ot_ref[r0:r0 + tn, :] = acc_t.astype(BF16)

    lane = lax.broadcasted_iota(jnp.int32, (tn, LANES), 1)
    for c0 in range(0, n_conv_cols, tn):
        cols = slice(c0, c0 + tn)
        par = cp_ref[cols, :]
        acc_t = lax.dot_general(wa_ref[cols, :], xb, nt_dims, preferred_element_type=F32) \
            + par[:, MLSTM_CONV + 1:MLSTM_CONV + 2]
        prev = halo_ref[cols, :]
        y = par[:, MLSTM_CONV:MLSTM_CONV + 1] + par[:, MLSTM_CONV - 1:MLSTM_CONV] * acc_t
        for kk in range(MLSTM_CONV - 1):
            shift = MLSTM_CONV - 1 - kk
            rolled = pltpu.roll(acc_t, shift, 1)
            head = jnp.where(lane < shift, pltpu.roll(prev, shift, 1), rolled[:, 0:LANES])
            y = y + par[:, kk:kk + 1] * jnp.concatenate([head, rolled[:, LANES:]], axis=1)
        halo_ref[cols, :] = acc_t[:, tm - LANES:tm]
        y = y * _sigmoid(y)
        if c0 < MLSTM_WIDTH:
            ot_ref[ROW_MQ + c0:ROW_MQ + c0 + tn, :] = y.astype(BF16)
        else:
            k0 = COL_MK + c0 - MLSTM_WIDTH
            o_ref[:, k0:k0 + tn] = (y * (MLSTM_HEAD_DIM ** -0.5)).T.astype(BF16)

    for g0 in range(0, 2 * D_MODEL, tn):
        wrows = slice(TAIL_GM + g0, TAIL_GM + g0 + tn)
        acc = lax.dot_general(xb, wb_ref[wrows, :], nt_dims, preferred_element_type=F32) + bb_ref[:, wrows]
        o_ref[:, COL_GM + g0:COL_GM + g0 + tn] = _sigmoid(acc).astype(BF16)

    dv = DIFF_V_DIM
    heads_per_chunk = tn // (2 * dv)
    q_scale = jnp.full((1, dv), DIFF_HEAD_DIM ** -0.5 * LOG2E, F32)
    scale_row = jnp.concatenate([q_scale, jnp.ones((1, dv), F32)] * heads_per_chunk, axis=1)
    for h0 in range(0, DIFF_HEADS, heads_per_chunk):
        starts = []
        for h in range(h0, h0 + heads_per_chunk):
            starts += [TAIL_DQ + h * dv, TAIL_DK + h * dv]
        w_rows = jnp.concatenate([wb_ref[r:r + dv, :] for r in starts], axis=0)
        b_row = jnp.concatenate([bb_ref[:, r:r + dv] for r in starts], axis=1)
        acc = lax.dot_general(xb, w_rows, nt_dims, preferred_element_type=F32) + b_row
        c0 = COL_QK + h0 * 2 * dv
        o_ref[:, c0:c0 + tn] = (acc * scale_row).astype(BF16)


def _cast_w_in_kernel(head_ref, tail_ref, head_out, tail_out):
    head_out[...] = head_ref[...].astype(BF16)
    tail_out[...] = tail_ref[...].astype(BF16)


def _cast_w_in(w_in_t, tail_start):
    k = w_in_t.shape[1]
    blk = W_IN_CAST_ROWS
    n_steps = N_TAIL // blk
    n_head_blocks = -(-(tail_start - 2 * MLSTM_HEADS + GATE_PAD) // blk)
    assert N_TAIL % blk == 0 and n_head_blocks <= n_steps and tail_start % 8 == 0
    head_idx = lambda j: (jnp.minimum(j, n_head_blocks - 1), 0)
    return pl.pallas_call(
        _cast_w_in_kernel,
        grid=(n_steps,),
        in_specs=[
            pl.BlockSpec((blk, k), head_idx),
            pl.BlockSpec((pl.Element(blk), pl.Element(k)), lambda j: (pl.multiple_of(tail_start + j * blk, 8), 0)),
        ],
        out_specs=[
            pl.BlockSpec((blk, k), head_idx),
            pl.BlockSpec((blk, k), lambda j: (j, 0)),
        ],
        out_shape=[
            jax.ShapeDtypeStruct((n_head_blocks * blk, k), BF16),
            jax.ShapeDtypeStruct((N_TAIL, k), BF16),
        ],
        name="cast_w_in",
    )(w_in_t, w_in_t)


def _const_spec(shape):
    return pl.BlockSpec(shape, lambda i: (0, 0), pipeline_mode=pl.Buffered(1))


def _in_proj(x2, w_all, conv_params, w_b, b_b, b_t, b_g, b_gt, seq):
    T = x2.shape[0]
    tm = PROJ_TM
    n_conv_cols = 2 * MLSTM_WIDTH
    n_vo = 2 * MLSTM_WIDTH
    assert n_vo == n_conv_cols and (n_conv_cols + n_vo) % GATE_PAD == 0
    const = _const_spec
    row_block = lambda shape, idx: pl.BlockSpec(shape, lambda i: (idx, 0), pipeline_mode=pl.Buffered(1))
    return pl.pallas_call(
        functools.partial(_in_proj_kernel, seq // tm),
        grid=(T // tm,),
        in_specs=[
            pl.BlockSpec((tm, D_MODEL), lambda i: (i, 0)),
            row_block((n_conv_cols, D_MODEL), 0),
            const((n_conv_cols, CONV_PARAMS)),
            const((N_TAIL, D_MODEL)),
            const((1, N_TAIL)),
            row_block((n_vo, D_MODEL), 1),
            const((n_vo + DIFF_WIDTH, 1)),
            row_block((GATE_PAD, D_MODEL), (n_conv_cols + n_vo) // GATE_PAD),
            const((1, GATE_PAD)),
            const((GATE_ROWS, 1)),
        ],
        out_specs=[
            pl.BlockSpec((tm, N_MAIN), lambda i: (i, 0)),
            pl.BlockSpec((N_TRANS, tm), lambda i: (0, i)),
            pl.BlockSpec((tm, GATE_PAD), lambda i: (i, 0)),
            pl.BlockSpec((GATE_ROWS, tm), lambda i: (0, i)),
        ],
        out_shape=[
            jax.ShapeDtypeStruct((T, N_MAIN), BF16),
            jax.ShapeDtypeStruct((N_TRANS, T), BF16),
            jax.ShapeDtypeStruct((T, GATE_PAD), F32),
            jax.ShapeDtypeStruct((GATE_ROWS, T), F32),
        ],
        scratch_shapes=[
            pltpu.VMEM((n_conv_cols, LANES), F32),
        ],
        compiler_params=pltpu.CompilerParams(
            dimension_semantics=("arbitrary",), vmem_limit_bytes=VMEM_LIMIT),
        name="in_proj",
    )(x2, w_all, conv_params, w_b, b_b, w_all, b_t, w_all, b_g, b_gt)


def _mlstm_chunk(c, qt_ref, k_ref, vt_ref, ot_ref, gcol_ref, grow_ref, nw_ref, out_ref, ct_ref, m_ref, nwb_ref):
    L = k_ref.shape[0]
    d = MLSTM_HEAD_DIM
    H = MLSTM_HEADS
    nt_dims = (((1,), (1,)), ((), ()))

    @pl.when(c == 0)
    def _():
        ct_ref[...] = jnp.zeros_like(ct_ref)
        m_ref[...] = jnp.zeros_like(m_ref)
        nwb_ref[...] = jnp.broadcast_to(nw_ref[...], nwb_ref.shape)

    key = lax.broadcasted_iota(jnp.int32, (L, L), 0)
    qry = lax.broadcasted_iota(jnp.int32, (L, L), 1)
    causal_t = key <= qry

    gdiff = gcol_ref[...]
    grow = grow_ref[...]
    ones = jnp.ones((MLSTM_AUG, L), BF16)

    def head(h):
        cols = slice(h * d, (h + 1) * d)
        q_t = qt_ref[cols, :]
        k = k_ref[:, cols]
        v_aug = jnp.concatenate([vt_ref[cols, :], ones], axis=0)
        b_r = grow[H + h:H + h + 1, :]
        i_r = grow[h:h + 1, :]
        g_c = gdiff[:, h:h + 1]
        m_prev = m_ref[h]
        ct = ct_ref[h]

        dmat = jnp.where(causal_t, b_r + g_c, -jnp.inf)
        a = b_r + m_prev
        m_row = jnp.maximum(a, jnp.max(dmat, axis=0, keepdims=True))
        w = jnp.exp(dmat - m_row)
        inter = jnp.exp(a - m_row)
        kq = jnp.dot(k, q_t, preferred_element_type=F32)
        sqk = (kq * w).astype(BF16)
        nd = inter * jnp.dot(ct.astype(BF16), q_t, preferred_element_type=F32) \
            + jnp.dot(v_aug, sqk, preferred_element_type=F32)
        rinv = 1.0 / jnp.maximum(jnp.abs(nd[d:d + 1, :]), jnp.exp(-m_row))
        hid = nd[0:d, :] * rinv * ot_ref[cols, :].astype(F32)
        mu = jnp.mean(hid, axis=0, keepdims=True)
        cen = hid - mu
        var = jnp.mean(cen * cen, axis=0, keepdims=True)
        out_ref[:, cols] = (cen * lax.rsqrt(var + LN_EPS) * nwb_ref[cols, :]).T.astype(BF16)

        b_last = b_r[:, L - 1:L]
        g_r = b_last - b_r + i_r
        m_new = jnp.maximum(b_last + m_prev, jnp.max(g_r, axis=-1, keepdims=True))
        decay = jnp.exp(b_last + m_prev - m_new)
        ws = jnp.exp(g_r - m_new)
        vw = (v_aug.astype(F32) * ws).astype(BF16)
        ct_ref[h] = decay * ct + jnp.dot(vw, k, preferred_element_type=F32)
        m_ref[h] = m_new

    return [functools.partial(head, h) for h in range(H)]


def _diff_attn_head(lam_init, interleaved, q_ref, k_ref, v_ref, bias_ref, lq1_ref, lk1_ref, lq2_ref, lk2_ref,
                    nw_ref, out_ref, qs_ref, vt_ref, s_ref, p_ref, acc_ref):
    T = ATT_T
    dh = DIFF_HEAD_DIM
    dv = DIFF_V_DIM
    nq = q_ref.shape[0] // T
    R = 2 * T

    vt_ref[0:dv, :] = v_ref[...]
    vt_ref[dv:dv + ATT_SUM_ROWS, :] = jnp.ones((ATT_SUM_ROWS, v_ref.shape[1]), BF16)

    lane = lax.broadcasted_iota(jnp.int32, (T, dv), 1)
    for ii in range(nq):
        qt = q_ref[ii * T:(ii + 1) * T, :]
        zero = jnp.zeros_like(qt)
        qs_ref[ii, 0:T, :] = jnp.where(lane < dh, qt, zero)
        qs_ref[ii, T:R, :] = jnp.where(lane >= dh, qt, zero)

    lam = (jnp.exp(jnp.sum(lq1_ref[...] * lk1_ref[...], axis=-1, keepdims=True))
           - jnp.exp(jnp.sum(lq2_ref[...] * lk2_ref[...], axis=-1, keepdims=True)) + lam_init)

    pairs = []
    for i in range(nq):
        k0 = 0
        while k0 < (i - 1) * T:
            klen = min(ATT_FAR_TILES * T, (i - 1) * T - k0)
            pairs.append((i, k0, klen, None))
            k0 += klen
        if i >= 1:
            pairs.append((i, (i - 1) * T, T, 0))
        pairs.append((i, i * T, T, 1))

    def scores(n):
        i, k0, klen, bias_idx = pairs[n]
        s = lax.dot_general(k_ref[k0:k0 + klen, :], qs_ref[i], (((1,), (1,)), ((), ())),
                            preferred_element_type=F32)
        if bias_idx is not None:
            bias = bias_ref[0, bias_idx]
            s = s + jnp.concatenate([bias, bias], axis=1)
        s_ref[n % 2, 0:klen, :] = s
        return jnp.max(s, axis=0, keepdims=True)

    def softmax(n, m_tile, m_prev):
        i, k0, klen, _ = pairs[n]
        m_new = m_tile if k0 == 0 else jnp.maximum(m_prev, m_tile)
        p_ref[n % 2, 0:klen, :] = jnp.exp2(s_ref[n % 2, 0:klen, :] - m_new).astype(BF16)
        alpha = None if k0 == 0 else jnp.exp2(m_prev - m_new)
        return m_new, alpha

    def values(n, alpha):
        i, k0, klen, _ = pairs[n]
        pv = jnp.dot(vt_ref[:, k0:k0 + klen], p_ref[n % 2, 0:klen, :],
                     preferred_element_type=F32)
        if k0 == 0:
            acc_ref[...] = pv
        else:
            acc_ref[...] = alpha * acc_ref[...] + pv
        if k0 == i * T:
            acc = acc_ref[...]
            out = acc[0:dv, :] / acc[dv:dv + 1, :]
            hd = out[:, 0:T] - lam * out[:, T:R]
            hd = hd * lax.rsqrt(jnp.mean(hd * hd, axis=0, keepdims=True) + LN_EPS) * nw_ref[...]
            out_ref[i * T:(i + 1) * T, :] = (hd * (1.0 - lam_init)).T.astype(BF16)

    spacing = INTERLEAVE_EVERY if interleaved else 0
    m_tiles = {0: scores(0)}
    m_run = None
    alphas = {}
    for n in range(len(pairs)):
        if n + 1 < len(pairs):
            m_tiles[n + 1] = scores(n + 1)
        m_run, alphas[n] = softmax(n, m_tiles.pop(n), m_run)
        if n >= 1:
            values(n - 1, alphas.pop(n - 1))
        if spacing and (n + 1) % spacing == 0 and (n + 1) // spacing <= len(interleaved):
            interleaved[(n + 1) // spacing - 1]()
    values(len(pairs) - 1, alphas.pop(len(pairs) - 1))


def _mixers_kernel(lam_init, chunks_per_seq, n_attn_refs, n_cast, *refs):
    (qk_ref, v_ref, bias_ref, lq1_ref, lk1_ref, lq2_ref, lk2_ref, anw_ref) = refs[:n_attn_refs]
    q_ref, k_ref = qk_ref.at[:, 0:DIFF_V_DIM], qk_ref.at[:, DIFF_V_DIM:2 * DIFF_V_DIM]
    (voq_ref, mk_ref, gcol_ref, grow_ref, mnw_ref) = refs[n_attn_refs:n_attn_refs + 5]
    n_in = n_attn_refs + 5 + n_cast
    cast_in = refs[n_attn_refs + 5:n_in]
    hd_ref, hm_ref = refs[n_in:n_in + 2]
    cast_out = refs[n_in + 2:n_in + 2 + n_cast]
    (qs_ref, vt_ref, s_ref, p_ref, acc_ref, ct_ref, m_ref, nwb_ref) = refs[n_in + 2 + n_cast:]
    W = MLSTM_WIDTH
    mvt_ref, mot_ref, mq_ref = (voq_ref.at[r:r + W] for r in (ROW_MV, ROW_MO, ROW_MQ))
    c = pl.program_id(0) % chunks_per_seq
    heads = _mlstm_chunk(c, mq_ref, mk_ref, mvt_ref, mot_ref, gcol_ref, grow_ref, mnw_ref, hm_ref,
                         ct_ref, m_ref, nwb_ref)
    _diff_attn_head(lam_init, heads, q_ref, k_ref, v_ref, bias_ref, lq1_ref, lk1_ref, lq2_ref, lk2_ref,
                    anw_ref, hd_ref, qs_ref, vt_ref, s_ref, p_ref, acc_ref)

    @pl.when(pl.program_id(0) < CAST_STEPS)
    def _():
        for w_ref, o_ref in zip(cast_in, cast_out):
            o_ref[...] = w_ref[...].astype(BF16)


def _mixers(proj, proj_t, gcol, grow, bias_tiles, lq1, lk1, lq2, lk2, dnorm_w, mnorm_w, lam_init, batch, seq,
            next_weights):
    T = ATT_T
    nq = seq // T
    dv = DIFF_V_DIM
    L = MLSTM_L
    nc = seq // L
    W = MLSTM_WIDTH
    assert DIFF_HEADS * batch == batch * nc, "attention (head, batch) steps must pair 1:1 with mLSTM chunks"
    small = lambda shape: pl.BlockSpec(shape, lambda n: (0, 0))
    attn_specs = [
        pl.BlockSpec((seq, 2 * dv), lambda n: (n % batch, COL_QK // (2 * dv) + n // batch)),
        pl.BlockSpec((dv, seq), lambda n: (ROW_DV // dv + n // batch, n % batch)),
        pl.BlockSpec((1, 2, T, T), lambda n: (n // batch, 0, 0, 0)),
        small((1, DIFF_HEAD_DIM)), small((1, DIFF_HEAD_DIM)),
        small((1, DIFF_HEAD_DIM)), small((1, DIFF_HEAD_DIM)),
        small((dv, 1)),
    ]
    mlstm_specs = [
        pl.BlockSpec((ROW_DV, L), lambda n: (0, n)),
        pl.BlockSpec((L, W), lambda n: (n, COL_MK // W)),
        pl.BlockSpec((L, GATE_PAD), lambda n: (n, 0)),
        pl.BlockSpec((GATE_ROWS, L), lambda n: (0, n)),
        small((W, 1)),
    ]
    def cast_specs():
        return [pl.BlockSpec((w.shape[0] // CAST_STEPS, w.shape[1]),
                             lambda n: (jnp.minimum(n, CAST_STEPS - 1), 0)) for w in next_weights]

    assert all(w.shape[0] % (16 * CAST_STEPS) == 0 for w in next_weights)
    return pl.pallas_call(
        functools.partial(_mixers_kernel, lam_init, nc, len(attn_specs), len(next_weights)),
        grid=(batch * nc,),
        in_specs=attn_specs + mlstm_specs + cast_specs(),
        out_specs=[
            pl.BlockSpec((seq, dv), lambda n: (n % batch, n // batch)),
            pl.BlockSpec((L, W), lambda n: (n, 0)),
        ] + cast_specs(),
        out_shape=[
            jax.ShapeDtypeStruct((batch * seq, DIFF_WIDTH), BF16),
            jax.ShapeDtypeStruct((batch * seq, W), BF16),
        ] + [jax.ShapeDtypeStruct(w.shape, BF16) for w in next_weights],
        scratch_shapes=[
            pltpu.VMEM((nq, 2 * T, dv), BF16),
            pltpu.VMEM((dv + ATT_SUM_ROWS, seq), BF16),
            pltpu.VMEM((2, ATT_FAR_TILES * T, 2 * T), F32),
            pltpu.VMEM((2, ATT_FAR_TILES * T, 2 * T), BF16),
            pltpu.VMEM((dv + ATT_SUM_ROWS, 2 * T), F32),
            pltpu.VMEM((MLSTM_HEADS, MLSTM_HEAD_DIM + MLSTM_AUG, MLSTM_HEAD_DIM), F32),
            pltpu.VMEM((MLSTM_HEADS, 1, 1), F32),
            pltpu.VMEM((W, L), F32),
        ],
        compiler_params=pltpu.CompilerParams(
            dimension_semantics=("arbitrary",), vmem_limit_bytes=VMEM_LIMIT),
        name="token_mixers",
    )(proj, proj_t, bias_tiles, lq1, lk1, lq2, lk2, dnorm_w, proj_t, proj, gcol, grow, mnorm_w, *next_weights)


def _layer_norm(y, g, b):
    mu = jnp.mean(y, axis=-1, keepdims=True)
    cen = y - mu
    var = jnp.mean(cen * cen, axis=-1, keepdims=True)
    return cen * lax.rsqrt(var + LN_EPS) * g + b


def _merge_kernel(hm_ref, hd_ref, gates_ref, x_ref, wbm_ref, wbd_ref, wo_ref, g_ref, b_ref, out_ref):
    tm = hm_ref.shape[0]
    gm_ref = gates_ref.at[:, 0:D_MODEL]
    gd_ref = gates_ref.at[:, D_MODEL:2 * D_MODEL]
    nb = tm // MERGE_ROWS
    halves = [slice(r * MERGE_ROWS, (r + 1) * MERGE_ROWS) for r in range(nb)]
    merged = []
    for rows in halves:
        pm = jnp.dot(hm_ref[rows, :], wbm_ref[...], preferred_element_type=F32)
        pd = jnp.dot(hd_ref[rows, :], wbd_ref[...], preferred_element_type=F32)
        merged.append((gm_ref[rows, :].astype(F32) * pm + gd_ref[rows, :].astype(F32) * pd).astype(BF16))
    for rows, m in zip(halves, merged):
        mix = jnp.dot(m, wo_ref[...], preferred_element_type=F32)
        out_ref[rows, :] = _layer_norm(DEEPNORM_ALPHA * x_ref[rows, :] + mix, g_ref[...], b_ref[...])


def _ffn_kernel(tiles_per_seq, h_ref, wu_ref, cw_ref, cb_ref, wd_ref, g_ref, b_ref, out_ref,
                conv_ref, halo_ref, act_ref):
    i = pl.program_id(0)
    tm = h_ref.shape[0]
    tc = FFN_TC

    @pl.when(i % tiles_per_seq == 0)
    def _():
        halo_ref[...] = jnp.zeros_like(halo_ref)

    h = h_ref[...]
    hb = h.astype(BF16)
    for c0 in range(0, D_FF, tc):
        cols = slice(c0, c0 + tc)
        a = jnp.dot(hb, wu_ref[:, cols], preferred_element_type=F32)
        gate = jnp.dot(hb, wu_ref[:, D_FF + c0:D_FF + c0 + tc], preferred_element_type=F32)
        conv_ref[0:HALO, cols] = halo_ref[:, cols]
        conv_ref[HALO:HALO + tm, cols] = a
        y = cb_ref[:, cols] + cw_ref[FFN_CONV - 1:FFN_CONV, cols] * a
        for kk in range(FFN_CONV - 1):
            off = HALO - (FFN_CONV - 1) + kk
            y = y + cw_ref[kk:kk + 1, cols] * conv_ref[off:off + tm, cols]
        halo_ref[:, cols] = a[tm - HALO:tm, :]
        act_ref[:, cols] = (y * _sigmoid(y) * gate).astype(BF16)

    for r0 in range(0, tm, tm // 2):
        rows = slice(r0, r0 + tm // 2)
        ffn = jnp.dot(act_ref[rows, :], wd_ref[...], preferred_element_type=F32)
        out_ref[rows, :] = _layer_norm(DEEPNORM_ALPHA * h[rows, :] + ffn, g_ref[...], b_ref[...])


def _merge_ffn_kernel(tiles_per_seq, hm_ref, hd_ref, gates_ref, x_ref, wbm_ref, wbd_ref, wo_ref, g1_ref,
                      b1_ref, wu_ref, cw_ref, cb_ref, wd_ref, g2_ref, b2_ref, out_ref,
                      h_ref, conv_ref, halo_ref, act_ref):
    _merge_kernel(hm_ref, hd_ref, gates_ref, x_ref, wbm_ref, wbd_ref, wo_ref, g1_ref, b1_ref, h_ref)
    _ffn_kernel(tiles_per_seq, h_ref, wu_ref, cw_ref, cb_ref, wd_ref, g2_ref, b2_ref, out_ref,
                conv_ref, halo_ref, act_ref)


def _merge_ffn(hm, hd, proj, x2, w_bm, w_bd, w_o, ln1_g, ln1_b, w_up, conv_w, conv_b, w_down, ln2_g, ln2_b, seq):
    T = x2.shape[0]
    tm = FFN_TM
    D = D_MODEL
    rows = lambda col: pl.BlockSpec((tm, D), lambda i: (i, col))
    const = _const_spec
    return pl.pallas_call(
        functools.partial(_merge_ffn_kernel, seq // tm),
        grid=(T // tm,),
        in_specs=[
            rows(0), rows(0), pl.BlockSpec((tm, 2 * D), lambda i: (i, COL_GM // (2 * D))), rows(0),
            const((MLSTM_WIDTH, D)), const((DIFF_WIDTH, D)), const((D, D)), const((1, D)), const((1, D)),
            const((D, 2 * D_FF)), const((FFN_CONV, D_FF)), const((1, D_FF)), const((D_FF, D)),
            const((1, D)), const((1, D)),
        ],
        out_specs=rows(0),
        out_shape=jax.ShapeDtypeStruct((T, D), F32),
        scratch_shapes=[
            pltpu.VMEM((tm, D), F32),
            pltpu.VMEM((HALO + tm, D_FF), F32),
            pltpu.VMEM((HALO, D_FF), F32),
            pltpu.VMEM((tm, D_FF), BF16),
        ],
        compiler_params=pltpu.CompilerParams(
            dimension_semantics=("arbitrary",), vmem_limit_bytes=VMEM_LIMIT),
        name="merge_ffn",
    )(hm, hd, proj, x2, w_bm, w_bd, w_o, ln1_g, ln1_b, w_up, conv_w, conv_b, w_down, ln2_g, ln2_b)


def _layer(h2, batch, seq, l, w_in, b_in, mconv_w, mconv_b, mnorm_w, lq1, lk1, lq2, lk2, dnorm_w, bias_tiles,
           w_bm, w_bd, w_o, ln1_g, ln1_b, w_up, fconv_w, fconv_b, w_down, ln2_g, ln2_b):
    lam_init = 0.8 - 0.6 * math.exp(-0.3 * l)
    n_m = 4 * MLSTM_WIDTH
    n_gate = 2 * MLSTM_HEADS
    n_qk = 2 * MLSTM_WIDTH
    w_all, w_b = _cast_w_in(w_in.T.astype(F32), n_m + n_gate)
    conv_params = jnp.concatenate(
        [mconv_w.astype(F32).T, mconv_b.astype(F32)[:, None], b_in[:n_qk].astype(F32)[:, None],
         jnp.zeros((n_qk, CONV_PARAMS - MLSTM_CONV - 2), F32)], axis=1)
    b_b = b_in[n_m + n_gate:][None, :].astype(F32)
    dv0 = n_m + n_gate + TAIL_DV
    b_t = jnp.concatenate([b_in[n_qk:n_m], b_in[dv0:dv0 + DIFF_WIDTH]])[:, None].astype(F32)
    b_g = b_in[n_m:n_m + GATE_PAD][None, :].astype(F32)
    b_gt = b_in[n_m:n_m + GATE_ROWS][:, None].astype(F32)

    proj, proj_t, gcol, grow = _in_proj(h2, w_all, conv_params, w_b, b_b, b_t, b_g, b_gt, seq)
    hd, hm, w_bm16, w_bd16, w_o16, w_up16, w_down16 = _mixers(
        proj, proj_t, gcol, grow, bias_tiles, lq1[None, :].astype(F32), lk1[None, :].astype(F32),
        lq2[None, :].astype(F32), lk2[None, :].astype(F32), dnorm_w[:, None].astype(F32),
        mnorm_w[:, None].astype(F32), lam_init, batch, seq,
        [w.astype(F32) for w in (w_bm, w_bd, w_o, w_up, w_down)])
    return _merge_ffn(hm, hd, proj, h2, w_bm16, w_bd16, w_o16,
                      ln1_g[None, :].astype(F32), ln1_b[None, :].astype(F32),
                      w_up16, fconv_w.astype(F32), fconv_b[None, :].astype(F32),
                      w_down16, ln2_g[None, :].astype(F32), ln2_b[None, :].astype(F32), seq)


def kernel(x, w_in, b_in, mlstm_conv_w, mlstm_conv_b, mlstm_norm_w, lambda_q1, lambda_k1, lambda_q2, lambda_k2,
           diff_norm_w, rel_bias, w_branch_mlstm, w_branch_diff, w_out, ln1_g, ln1_b, w_ffn_up, ffn_conv_w,
           ffn_conv_b, w_ffn_down, ln2_g, ln2_b):
    batch, seq, d_model = x.shape
    assert d_model == D_MODEL and seq % max(PROJ_TM, FFN_TM, ATT_T, MLSTM_L) == 0
    bias_tiles = _rel_bias_tiles(rel_bias)
    h2 = x.reshape(batch * seq, d_model)
    for l in range(w_in.shape[0]):
        h2 = _layer(h2, batch, seq, l, w_in[l], b_in[l], mlstm_conv_w[l], mlstm_conv_b[l], mlstm_norm_w[l],
                    lambda_q1[l], lambda_k1[l], lambda_q2[l], lambda_k2[l], diff_norm_w[l], bias_tiles,
                    w_branch_mlstm[l], w_branch_diff[l], w_out[l], ln1_g[l], ln1_b[l], w_ffn_up[l],
                    ffn_conv_w[l], ffn_conv_b[l], w_ffn_down[l], ln2_g[l], ln2_b[l])
    return h2.reshape(batch, seq, d_model).astype(x.dtype)
```

```python
import functools
import math

import jax
import jax.numpy as jnp
from jax import lax
from jax.experimental import pallas as pl
from jax.experimental.pallas import tpu as pltpu

F32 = jnp.float32
BF16 = jnp.bfloat16

D_MODEL = 1024
MLSTM_HEADS = 4
MLSTM_HEAD_DIM = 256
MLSTM_WIDTH = MLSTM_HEADS * MLSTM_HEAD_DIM
MLSTM_CONV = 4
DIFF_HEADS = 8
DIFF_HEAD_DIM = 64
DIFF_V_DIM = 2 * DIFF_HEAD_DIM
DIFF_WIDTH = DIFF_HEADS * DIFF_V_DIM
REL_BUCKETS = 32
REL_MAX_DIST = 128
D_FF = 2816
FFN_CONV = 3
DEPTH = 1
DEEPNORM_ALPHA = (2.0 * DEPTH) ** 0.25
LN_EPS = 1e-5
LOG2E = math.log2(math.e)

COL_GM = 0
COL_GD = 1024
COL_QK = 2048
N_MAIN = 4096
ROW_MV = 0
ROW_MO = 1024
ROW_MQ = 2048
ROW_MK = 3072
ROW_DV = 4096
N_TRANS = 5120
TAIL_DQ = 0
TAIL_DK = 1024
TAIL_DV = 2048
TAIL_GM = 3072
N_TAIL = 5120
GATE_PAD = 128
GATE_ROWS = 16
MLSTM_AUG = 16

HALO = 8
LANES = 128
CONV_PARAMS = 8
PROJ_TM = 512
PROJ_TN = 512
MLSTM_L = 256
ATT_T = 256
ATT_FAR_TILES = 1
ATT_SUM_ROWS = 16
INTERLEAVE_EVERY = 4
W_IN_CAST_ROWS = 512
CAST_STEPS = 8
MERGE_ROWS = 256
FFN_TM = 512
FFN_TC = 256
VMEM_LIMIT = 58 * 1024 * 1024


def _sigmoid(v):
    return 1.0 / (1.0 + jnp.exp(-v))


def _log_sigmoid(v):
    return jnp.minimum(v, 0.0) - jnp.log(1.0 + jnp.exp(-jnp.abs(v)))


def _split3(v):
    hi = v.astype(BF16)
    rest = v - hi.astype(F32)
    mid = rest.astype(BF16)
    lo = (rest - mid.astype(F32)).astype(BF16)
    return hi, mid, lo


def _rel_bias_kernel(table_ref, out_ref):
    h = pl.program_id(0)
    T = out_ref.shape[-1]
    max_exact = REL_BUCKETS // 2
    far = table_ref[REL_BUCKETS - 1, h]
    n = lax.broadcasted_iota(jnp.int32, (8, T), 1)
    nf = jnp.maximum(n, 1).astype(F32)
    large = max_exact + (jnp.log(nf / max_exact) / math.log(REL_MAX_DIST / max_exact)
                         * (REL_BUCKETS - max_exact)).astype(jnp.int32)
    large = jnp.minimum(large, REL_BUCKETS - 1)
    bucket = jnp.where(n < max_exact, n, large)
    by_dist = jnp.zeros((8, T), F32)
    for kk in range(REL_BUCKETS):
        by_dist = jnp.where(bucket == kk, table_ref[kk, h], by_dist)
    by_dist = (by_dist - far) * LOG2E
    rolled = pltpu.roll(jnp.broadcast_to(by_dist[0:1, :], (T, T)), 0, 1, stride=1, stride_axis=0)
    kpos = lax.broadcasted_iota(jnp.int32, (T, T), 0)
    qpos = lax.broadcasted_iota(jnp.int32, (T, T), 1)
    out_ref[0, 0] = jnp.where(qpos < kpos, rolled, 0.0)
    out_ref[0, 1] = jnp.where(qpos >= kpos, rolled, -jnp.inf)


def _rel_bias_tiles(rel_bias):
    T = ATT_T
    return pl.pallas_call(
        _rel_bias_kernel,
        grid=(DIFF_HEADS,),
        in_specs=[pl.BlockSpec(memory_space=pltpu.SMEM)],
        out_specs=pl.BlockSpec((1, 2, T, T), lambda h: (h, 0, 0, 0)),
        out_shape=jax.ShapeDtypeStruct((DIFF_HEADS, 2, T, T), F32),
        name="rel_bias_tiles",
    )(rel_bias.astype(F32))


def _in_proj_kernel(tiles_per_seq, x_ref, wa_ref, cp_ref, wb_ref, bb_ref, wt_ref, bt_ref, wg_ref, bg_ref,
                    bgt_ref, o_ref, ot_ref, gcol_ref, grow_ref, halo_ref):
    i = pl.program_id(0)
    tm = x_ref.shape[0]
    tn = PROJ_TN
    n_conv_cols = 2 * MLSTM_WIDTH
    nt_dims = (((1,), (1,)), ((), ()))

    @pl.when(i % tiles_per_seq == 0)
    def _():
        halo_ref[...] = jnp.zeros_like(halo_ref)

    xb = x_ref[...].astype(BF16)
    gcol = lax.dot_general(xb, wg_ref[...], nt_dims, preferred_element_type=F32) + bg_ref[...]
    grow = lax.dot_general(wg_ref[0:GATE_ROWS, :], xb, nt_dims,
                           preferred_element_type=F32) + bgt_ref[...]
    L = MLSTM_L
    H = MLSTM_HEADS
    tri = jnp.where(lax.broadcasted_iota(jnp.int32, (L, L), 0) >= lax.broadcasted_iota(jnp.int32, (L, L), 1),
                    1.0, 0.0).astype(BF16)
    gate_row = lax.broadcasted_iota(jnp.int32, (GATE_ROWS, L), 0)
    for r0 in range(0, tm, L):
        g_c = gcol[r0:r0 + L, :]
        g_r = grow[:, r0:r0 + L]
        bc3 = jnp.dot(tri, jnp.concatenate(_split3(_log_sigmoid(g_c)), axis=1), preferred_element_type=F32)
        bcol = bc3[:, 0:GATE_PAD] + bc3[:, GATE_PAD:2 * GATE_PAD] + bc3[:, 2 * GATE_PAD:3 * GATE_PAD]
        br3 = lax.dot_general(jnp.concatenate(_split3(_log_sigmoid(g_r)), axis=0), tri, nt_dims,
                              preferred_element_type=F32)
        brow = br3[0:GATE_ROWS] + br3[GATE_ROWS:2 * GATE_ROWS] + br3[2 * GATE_ROWS:3 * GATE_ROWS]
        gcol_ref[r0:r0 + L, :] = g_c - pltpu.roll(bcol, GATE_PAD - H, 1)
        grow_ref[:, r0:r0 + L] = jnp.where(gate_row < H, g_r, brow)

    n_vo = 2 * MLSTM_WIDTH
    for r0 in list(range(ROW_MV, ROW_MV + n_vo, tn)) + list(range(ROW_DV, ROW_DV + DIFF_WIDTH, tn)):
        if r0 < ROW_DV:
            src = slice(r0 - ROW_MV, r0 - ROW_MV + tn)
            w_rows, bias_col = wt_ref[src, :], bt_ref[src, :]
        else:
            w_rows = wb_ref[TAIL_DV + r0 - ROW_DV:TAIL_DV + r0 - ROW_DV + tn, :]
            bias_col = bt_ref[n_vo + r0 - ROW_DV:n_vo + r0 - ROW_DV + tn, :]
        acc_t = lax.dot_general(w_rows, xb, nt_dims, preferred_element_type=F32) + bias_col
        if ROW_MO <= r0 < ROW_MO + MLSTM_WIDTH:
            acc_t = _sigmoid(acc_t)
        ot_ref[r0:r0 + tn, :] = acc_t.astype(BF16)

    lane = lax.broadcasted_iota(jnp.int32, (tn, LANES), 1)
    for c0 in range(0, n_conv_cols, tn):
        cols = slice(c0, c0 + tn)
        par = cp_ref[cols, :]
        acc_t = lax.dot_general(wa_ref[cols, :], xb, nt_dims, preferred_element_type=F32) \
            + par[:, MLSTM_CONV + 1:MLSTM_CONV + 2]
        prev = halo_ref[cols, :]
        y = par[:, MLSTM_CONV:MLSTM_CONV + 1] + par[:, MLSTM_CONV - 1:MLSTM_CONV] * acc_t
        for kk in range(MLSTM_CONV - 1):
            shift = MLSTM_CONV - 1 - kk
            rolled = pltpu.roll(acc_t, shift, 1)
            head = jnp.where(lane < shift, pltpu.roll(prev, shift, 1), rolled[:, 0:LANES])
            y = y + par[:, kk:kk + 1] * jnp.concatenate([head, rolled[:, LANES:]], axis=1)
        halo_ref[cols, :] = acc_t[:, tm - LANES:tm]
        y = y * _sigmoid(y)
        if c0 >= MLSTM_WIDTH:
            y = y * (MLSTM_HEAD_DIM ** -0.5)
        ot_ref[ROW_MQ + c0:ROW_MQ + c0 + tn, :] = y.astype(BF16)

    for g0 in range(0, 2 * D_MODEL, tn):
        wrows = slice(TAIL_GM + g0, TAIL_GM + g0 + tn)
        acc = lax.dot_general(xb, wb_ref[wrows, :], nt_dims, preferred_element_type=F32) + bb_ref[:, wrows]
        o_ref[:, COL_GM + g0:COL_GM + g0 + tn] = _sigmoid(acc).astype(BF16)

    dv = DIFF_V_DIM
    heads_per_chunk = tn // (2 * dv)
    q_scale = jnp.full((1, dv), DIFF_HEAD_DIM ** -0.5 * LOG2E, F32)
    scale_row = jnp.concatenate([q_scale, jnp.ones((1, dv), F32)] * heads_per_chunk, axis=1)
    for h0 in range(0, DIFF_HEADS, heads_per_chunk):
        starts = []
        for h in range(h0, h0 + heads_per_chunk):
            starts += [TAIL_DQ + h * dv, TAIL_DK + h * dv]
        w_rows = jnp.concatenate([wb_ref[r:r + dv, :] for r in starts], axis=0)
        b_row = jnp.concatenate([bb_ref[:, r:r + dv] for r in starts], axis=1)
        acc = lax.dot_general(xb, w_rows, nt_dims, preferred_element_type=F32) + b_row
        c0 = COL_QK + h0 * 2 * dv
        o_ref[:, c0:c0 + tn] = (acc * scale_row).astype(BF16)


def _cast_w_in_kernel(head_ref, tail_ref, head_out, tail_out):
    head_out[...] = head_ref[...].astype(BF16)
    tail_out[...] = tail_ref[...].astype(BF16)


def _cast_w_in(w_in_t, tail_start):
    k = w_in_t.shape[1]
    blk = W_IN_CAST_ROWS
    n_steps = N_TAIL // blk
    n_head_blocks = -(-(tail_start - 2 * MLSTM_HEADS + GATE_PAD) // blk)
    assert N_TAIL % blk == 0 and n_head_blocks <= n_steps and tail_start % 8 == 0
    head_idx = lambda j: (jnp.minimum(j, n_head_blocks - 1), 0)
    return pl.pallas_call(
        _cast_w_in_kernel,
        grid=(n_steps,),
        in_specs=[
            pl.BlockSpec((blk, k), head_idx),
            pl.BlockSpec((pl.Element(blk), pl.Element(k)), lambda j: (pl.multiple_of(tail_start + j * blk, 8), 0)),
        ],
        out_specs=[
            pl.BlockSpec((blk, k), head_idx),
            pl.BlockSpec((blk, k), lambda j: (j, 0)),
        ],
        out_shape=[
            jax.ShapeDtypeStruct((n_head_blocks * blk, k), BF16),
            jax.ShapeDtypeStruct((N_TAIL, k), BF16),
        ],
        name="cast_w_in",
    )(w_in_t, w_in_t)


def _const_spec(shape):
    return pl.BlockSpec(shape, lambda i: (0, 0), pipeline_mode=pl.Buffered(1))


def _in_proj(x2, w_all, conv_params, w_b, b_b, b_t, b_g, b_gt, seq):
    T = x2.shape[0]
    tm = PROJ_TM
    n_conv_cols = 2 * MLSTM_WIDTH
    n_vo = 2 * MLSTM_WIDTH
    assert n_vo == n_conv_cols and (n_conv_cols + n_vo) % GATE_PAD == 0
    const = _const_spec
    row_block = lambda shape, idx: pl.BlockSpec(shape, lambda i: (idx, 0), pipeline_mode=pl.Buffered(1))
    return pl.pallas_call(
        functools.partial(_in_proj_kernel, seq // tm),
        grid=(T // tm,),
        in_specs=[
            pl.BlockSpec((tm, D_MODEL), lambda i: (i, 0)),
            row_block((n_conv_cols, D_MODEL), 0),
            const((n_conv_cols, CONV_PARAMS)),
            const((N_TAIL, D_MODEL)),
            const((1, N_TAIL)),
            row_block((n_vo, D_MODEL), 1),
            const((n_vo + DIFF_WIDTH, 1)),
            row_block((GATE_PAD, D_MODEL), (n_conv_cols + n_vo) // GATE_PAD),
            const((1, GATE_PAD)),
            const((GATE_ROWS, 1)),
        ],
        out_specs=[
            pl.BlockSpec((tm, N_MAIN), lambda i: (i, 0)),
            pl.BlockSpec((N_TRANS, tm), lambda i: (0, i)),
            pl.BlockSpec((tm, GATE_PAD), lambda i: (i, 0)),
            pl.BlockSpec((GATE_ROWS, tm), lambda i: (0, i)),
        ],
        out_shape=[
            jax.ShapeDtypeStruct((T, N_MAIN), BF16),
            jax.ShapeDtypeStruct((N_TRANS, T), BF16),
            jax.ShapeDtypeStruct((T, GATE_PAD), F32),
            jax.ShapeDtypeStruct((GATE_ROWS, T), F32),
        ],
        scratch_shapes=[
            pltpu.VMEM((n_conv_cols, LANES), F32),
        ],
        compiler_params=pltpu.CompilerParams(
            dimension_semantics=("arbitrary",), vmem_limit_bytes=VMEM_LIMIT),
        name="in_proj",
    )(x2, w_all, conv_params, w_b, b_b, w_all, b_t, w_all, b_g, b_gt)


def _mlstm_chunk(c, qt_ref, kt_ref, vt_ref, ot_ref, gcol_ref, grow_ref, nw_ref, out_ref, ct_ref, m_ref, nwb_ref):
    L = kt_ref.shape[1]
    d = MLSTM_HEAD_DIM
    H = MLSTM_HEADS
    nt_dims = (((1,), (1,)), ((), ()))

    @pl.when(c == 0)
    def _():
        ct_ref[...] = jnp.zeros_like(ct_ref)
        m_ref[...] = jnp.zeros_like(m_ref)
        nwb_ref[...] = jnp.broadcast_to(nw_ref[...], nwb_ref.shape)

    key = lax.broadcasted_iota(jnp.int32, (L, L), 0)
    qry = lax.broadcasted_iota(jnp.int32, (L, L), 1)
    causal_t = key <= qry

    gdiff = gcol_ref[...]
    grow = grow_ref[...]
    ones = jnp.ones((MLSTM_AUG, L), BF16)

    def head(h):
        cols = slice(h * d, (h + 1) * d)
        q_t = qt_ref[cols, :]
        k_t = kt_ref[cols, :]
        v_aug = jnp.concatenate([vt_ref[cols, :], ones], axis=0)
        b_r = grow[H + h:H + h + 1, :]
        i_r = grow[h:h + 1, :]
        g_c = gdiff[:, h:h + 1]
        m_prev = m_ref[h]
        ct = ct_ref[h]

        dmat = jnp.where(causal_t, b_r + g_c, -jnp.inf)
        a = b_r + m_prev
        m_row = jnp.maximum(a, jnp.max(dmat, axis=0, keepdims=True))
        w = jnp.exp(dmat - m_row)
        inter = jnp.exp(a - m_row)
        kq = lax.dot_general(k_t, q_t, (((0,), (0,)), ((), ())), preferred_element_type=F32)
        sqk = (kq * w).astype(BF16)
        nd = inter * jnp.dot(ct.astype(BF16), q_t, preferred_element_type=F32) \
            + jnp.dot(v_aug, sqk, preferred_element_type=F32)
        rinv = 1.0 / jnp.maximum(jnp.abs(nd[d:d + 1, :]), jnp.exp(-m_row))
        hid = nd[0:d, :] * rinv * ot_ref[cols, :].astype(F32)
        mu = jnp.mean(hid, axis=0, keepdims=True)
        cen = hid - mu
        var = jnp.mean(cen * cen, axis=0, keepdims=True)
        out_ref[:, cols] = (cen * lax.rsqrt(var + LN_EPS) * nwb_ref[cols, :]).T.astype(BF16)

        b_last = b_r[:, L - 1:L]
        g_r = b_last - b_r + i_r
        m_new = jnp.maximum(b_last + m_prev, jnp.max(g_r, axis=-1, keepdims=True))
        decay = jnp.exp(b_last + m_prev - m_new)
        ws = jnp.exp(g_r - m_new)
        vw = (v_aug.astype(F32) * ws).astype(BF16)
        ct_ref[h] = decay * ct + lax.dot_general(vw, k_t, nt_dims, preferred_element_type=F32)
        m_ref[h] = m_new

    return [functools.partial(head, h) for h in range(H)]


def _diff_attn_head(lam_init, interleaved, q_ref, k_ref, v_ref, bias_ref, lq1_ref, lk1_ref, lq2_ref, lk2_ref,
                    nw_ref, out_ref, qs_ref, vt_ref, s_ref, p_ref, acc_ref):
    T = ATT_T
    dh = DIFF_HEAD_DIM
    dv = DIFF_V_DIM
    nq = q_ref.shape[0] // T
    R = 2 * T

    vt_ref[0:dv, :] = v_ref[...]
    vt_ref[dv:dv + ATT_SUM_ROWS, :] = jnp.ones((ATT_SUM_ROWS, v_ref.shape[1]), BF16)

    lane = lax.broadcasted_iota(jnp.int32, (T, dv), 1)
    for ii in range(nq):
        qt = q_ref[ii * T:(ii + 1) * T, :]
        zero = jnp.zeros_like(qt)
        qs_ref[ii, 0:T, :] = jnp.where(lane < dh, qt, zero)
        qs_ref[ii, T:R, :] = jnp.where(lane >= dh, qt, zero)

    lam = (jnp.exp(jnp.sum(lq1_ref[...] * lk1_ref[...], axis=-1, keepdims=True))
           - jnp.exp(jnp.sum(lq2_ref[...] * lk2_ref[...], axis=-1, keepdims=True)) + lam_init)

    pairs = []
    for i in range(nq):
        k0 = 0
        while k0 < (i - 1) * T:
            klen = min(ATT_FAR_TILES * T, (i - 1) * T - k0)
            pairs.append((i, k0, klen, None))
            k0 += klen
        if i >= 1:
            pairs.append((i, (i - 1) * T, T, 0))
        pairs.append((i, i * T, T, 1))

    def scores(n):
        i, k0, klen, bias_idx = pairs[n]
        s = lax.dot_general(k_ref[k0:k0 + klen, :], qs_ref[i], (((1,), (1,)), ((), ())),
                            preferred_element_type=F32)
        if bias_idx is not None:
            bias = bias_ref[0, bias_idx]
            s = s + jnp.concatenate([bias, bias], axis=1)
        s_ref[n % 2, 0:klen, :] = s
        return jnp.max(s, axis=0, keepdims=True)

    def softmax(n, m_tile, m_prev):
        i, k0, klen, _ = pairs[n]
        m_new = m_tile if k0 == 0 else jnp.maximum(m_prev, m_tile)
        p_ref[n % 2, 0:klen, :] = jnp.exp2(s_ref[n % 2, 0:klen, :] - m_new).astype(BF16)
        alpha = None if k0 == 0 else jnp.exp2(m_prev - m_new)
        return m_new, alpha

    def values(n, alpha):
        i, k0, klen, _ = pairs[n]
        pv = jnp.dot(vt_ref[:, k0:k0 + klen], p_ref[n % 2, 0:klen, :],
                     preferred_element_type=F32)
        if k0 == 0:
            acc_ref[...] = pv
        else:
            acc_ref[...] = alpha * acc_ref[...] + pv
        if k0 == i * T:
            acc = acc_ref[...]
            out = acc[0:dv, :] / acc[dv:dv + 1, :]
            hd = out[:, 0:T] - lam * out[:, T:R]
            hd = hd * lax.rsqrt(jnp.mean(hd * hd, axis=0, keepdims=True) + LN_EPS) * nw_ref[...]
            out_ref[i * T:(i + 1) * T, :] = (hd * (1.0 - lam_init)).T.astype(BF16)

    spacing = INTERLEAVE_EVERY if interleaved else 0
    m_tiles = {0: scores(0)}
    m_run = None
    alphas = {}
    for n in range(len(pairs)):
        if n + 1 < len(pairs):
            m_tiles[n + 1] = scores(n + 1)
        m_run, alphas[n] = softmax(n, m_tiles.pop(n), m_run)
        if n >= 1:
            values(n - 1, alphas.pop(n - 1))
        if spacing and (n + 1) % spacing == 0 and (n + 1) // spacing <= len(interleaved):
            interleaved[(n + 1) // spacing - 1]()
    values(len(pairs) - 1, alphas.pop(len(pairs) - 1))


def _mixers_kernel(lam_init, chunks_per_seq, n_attn_refs, n_cast, *refs):
    (qk_ref, v_ref, bias_ref, lq1_ref, lk1_ref, lq2_ref, lk2_ref, anw_ref) = refs[:n_attn_refs]
    q_ref, k_ref = qk_ref.at[:, 0:DIFF_V_DIM], qk_ref.at[:, DIFF_V_DIM:2 * DIFF_V_DIM]
    (voqk_ref, gcol_ref, grow_ref, mnw_ref) = refs[n_attn_refs:n_attn_refs + 4]
    n_in = n_attn_refs + 4 + n_cast
    cast_in = refs[n_attn_refs + 4:n_in]
    hd_ref, hm_ref = refs[n_in:n_in + 2]
    cast_out = refs[n_in + 2:n_in + 2 + n_cast]
    (qs_ref, vt_ref, s_ref, p_ref, acc_ref, ct_ref, m_ref, nwb_ref) = refs[n_in + 2 + n_cast:]
    W = MLSTM_WIDTH
    mvt_ref, mot_ref, mq_ref, mk_ref = (voqk_ref.at[r:r + W] for r in (ROW_MV, ROW_MO, ROW_MQ, ROW_MK))
    c = pl.program_id(0) % chunks_per_seq
    heads = _mlstm_chunk(c, mq_ref, mk_ref, mvt_ref, mot_ref, gcol_ref, grow_ref, mnw_ref, hm_ref,
                         ct_ref, m_ref, nwb_ref)
    _diff_attn_head(lam_init, heads, q_ref, k_ref, v_ref, bias_ref, lq1_ref, lk1_ref, lq2_ref, lk2_ref,
                    anw_ref, hd_ref, qs_ref, vt_ref, s_ref, p_ref, acc_ref)

    @pl.when(pl.program_id(0) < CAST_STEPS)
    def _():
        for w_ref, o_ref in zip(cast_in, cast_out):
            o_ref[...] = w_ref[...].astype(BF16)


def _mixers(proj, proj_t, gcol, grow, bias_tiles, lq1, lk1, lq2, lk2, dnorm_w, mnorm_w, lam_init, batch, seq,
            next_weights):
    T = ATT_T
    nq = seq // T
    dv = DIFF_V_DIM
    L = MLSTM_L
    nc = seq // L
    W = MLSTM_WIDTH
    assert DIFF_HEADS * batch == batch * nc, "attention (head, batch) steps must pair 1:1 with mLSTM chunks"
    small = lambda shape: pl.BlockSpec(shape, lambda n: (0, 0))
    attn_specs = [
        pl.BlockSpec((seq, 2 * dv), lambda n: (n % batch, COL_QK // (2 * dv) + n // batch)),
        pl.BlockSpec((dv, seq), lambda n: (ROW_DV // dv + n // batch, n % batch)),
        pl.BlockSpec((1, 2, T, T), lambda n: (n // batch, 0, 0, 0)),
        small((1, DIFF_HEAD_DIM)), small((1, DIFF_HEAD_DIM)),
        small((1, DIFF_HEAD_DIM)), small((1, DIFF_HEAD_DIM)),
        small((dv, 1)),
    ]
    mlstm_specs = [
        pl.BlockSpec((ROW_DV, L), lambda n: (0, n)),
        pl.BlockSpec((L, GATE_PAD), lambda n: (n, 0)),
        pl.BlockSpec((GATE_ROWS, L), lambda n: (0, n)),
        small((W, 1)),
    ]
    def cast_specs():
        return [pl.BlockSpec((w.shape[0] // CAST_STEPS, w.shape[1]),
                             lambda n: (jnp.minimum(n, CAST_STEPS - 1), 0)) for w in next_weights]

    assert all(w.shape[0] % (16 * CAST_STEPS) == 0 for w in next_weights)
    return pl.pallas_call(
        functools.partial(_mixers_kernel, lam_init, nc, len(attn_specs), len(next_weights)),
        grid=(batch * nc,),
        in_specs=attn_specs + mlstm_specs + cast_specs(),
        out_specs=[
            pl.BlockSpec((seq, dv), lambda n: (n % batch, n // batch)),
            pl.BlockSpec((L, W), lambda n: (n, 0)),
        ] + cast_specs(),
        out_shape=[
            jax.ShapeDtypeStruct((batch * seq, DIFF_WIDTH), BF16),
            jax.ShapeDtypeStruct((batch * seq, W), BF16),
        ] + [jax.ShapeDtypeStruct(w.shape, BF16) for w in next_weights],
        scratch_shapes=[
            pltpu.VMEM((nq, 2 * T, dv), BF16),
            pltpu.VMEM((dv + ATT_SUM_ROWS, seq), BF16),
            pltpu.VMEM((2, ATT_FAR_TILES * T, 2 * T), F32),
            pltpu.VMEM((2, ATT_FAR_TILES * T, 2 * T), BF16),
            pltpu.VMEM((dv + ATT_SUM_ROWS, 2 * T), F32),
            pltpu.VMEM((MLSTM_HEADS, MLSTM_HEAD_DIM + MLSTM_AUG, MLSTM_HEAD_DIM), F32),
            pltpu.VMEM((MLSTM_HEADS, 1, 1), F32),
            pltpu.VMEM((W, L), F32),
        ],
        compiler_params=pltpu.CompilerParams(
            dimension_semantics=("arbitrary",), vmem_limit_bytes=VMEM_LIMIT),
        name="token_mixers",
    )(proj, proj_t, bias_tiles, lq1, lk1, lq2, lk2, dnorm_w, proj_t, gcol, grow, mnorm_w, *next_weights)


def _layer_norm(y, g, b):
    mu = jnp.mean(y, axis=-1, keepdims=True)
    cen = y - mu
    var = jnp.mean(cen * cen, axis=-1, keepdims=True)
    return cen * lax.rsqrt(var + LN_EPS) * g + b


def _merge_kernel(hm_ref, hd_ref, gates_ref, x_ref, wbm_ref, wbd_ref, wo_ref, g_ref, b_ref, out_ref):
    tm = hm_ref.shape[0]
    gm_ref = gates_ref.at[:, 0:D_MODEL]
    gd_ref = gates_ref.at[:, D_MODEL:2 * D_MODEL]
    nb = tm // MERGE_ROWS
    halves = [slice(r * MERGE_ROWS, (r + 1) * MERGE_ROWS) for r in range(nb)]
    merged = []
    for rows in halves:
        pm = jnp.dot(hm_ref[rows, :], wbm_ref[...], preferred_element_type=F32)
        pd = jnp.dot(hd_ref[rows, :], wbd_ref[...], preferred_element_type=F32)
        merged.append((gm_ref[rows, :].astype(F32) * pm + gd_ref[rows, :].astype(F32) * pd).astype(BF16))
    for rows, m in zip(halves, merged):
        mix = jnp.dot(m, wo_ref[...], preferred_element_type=F32)
        out_ref[rows, :] = _layer_norm(DEEPNORM_ALPHA * x_ref[rows, :] + mix, g_ref[...], b_ref[...])


def _ffn_kernel(tiles_per_seq, h_ref, wu_ref, cw_ref, cb_ref, wd_ref, g_ref, b_ref, out_ref,
                conv_ref, halo_ref, act_ref):
    i = pl.program_id(0)
    tm = h_ref.shape[0]
    tc = FFN_TC

    @pl.when(i % tiles_per_seq == 0)
    def _():
        halo_ref[...] = jnp.zeros_like(halo_ref)

    h = h_ref[...]
    hb = h.astype(BF16)
    for c0 in range(0, D_FF, tc):
        cols = slice(c0, c0 + tc)
        a = jnp.dot(hb, wu_ref[:, cols], preferred_element_type=F32)
        gate = jnp.dot(hb, wu_ref[:, D_FF + c0:D_FF + c0 + tc], preferred_element_type=F32)
        conv_ref[0:HALO, cols] = halo_ref[:, cols]
        conv_ref[HALO:HALO + tm, cols] = a
        y = cb_ref[:, cols] + cw_ref[FFN_CONV - 1:FFN_CONV, cols] * a
        for kk in range(FFN_CONV - 1):
            off = HALO - (FFN_CONV - 1) + kk
            y = y + cw_ref[kk:kk + 1, cols] * conv_ref[off:off + tm, cols]
        halo_ref[:, cols] = a[tm - HALO:tm, :]
        act_ref[:, cols] = (y * _sigmoid(y) * gate).astype(BF16)

    for r0 in range(0, tm, tm // 2):
        rows = slice(r0, r0 + tm // 2)
        ffn = jnp.dot(act_ref[rows, :], wd_ref[...], preferred_element_type=F32)
        out_ref[rows, :] = _layer_norm(DEEPNORM_ALPHA * h[rows, :] + ffn, g_ref[...], b_ref[...])


def _merge_ffn_kernel(tiles_per_seq, hm_ref, hd_ref, gates_ref, x_ref, wbm_ref, wbd_ref, wo_ref, g1_ref,
                      b1_ref, wu_ref, cw_ref, cb_ref, wd_ref, g2_ref, b2_ref, out_ref,
                      h_ref, conv_ref, halo_ref, act_ref):
    _merge_kernel(hm_ref, hd_ref, gates_ref, x_ref, wbm_ref, wbd_ref, wo_ref, g1_ref, b1_ref, h_ref)
    _ffn_kernel(tiles_per_seq, h_ref, wu_ref, cw_ref, cb_ref, wd_ref, g2_ref, b2_ref, out_ref,
                conv_ref, halo_ref, act_ref)


def _merge_ffn(hm, hd, proj, x2, w_bm, w_bd, w_o, ln1_g, ln1_b, w_up, conv_w, conv_b, w_down, ln2_g, ln2_b, seq):
    T = x2.shape[0]
    tm = FFN_TM
    D = D_MODEL
    rows = lambda col: pl.BlockSpec((tm, D), lambda i: (i, col))
    const = _const_spec
    return pl.pallas_call(
        functools.partial(_merge_ffn_kernel, seq // tm),
        grid=(T // tm,),
        in_specs=[
            rows(0), rows(0), pl.BlockSpec((tm, 2 * D), lambda i: (i, COL_GM // (2 * D))), rows(0),
            const((MLSTM_WIDTH, D)), const((DIFF_WIDTH, D)), const((D, D)), const((1, D)), const((1, D)),
            const((D, 2 * D_FF)), const((FFN_CONV, D_FF)), const((1, D_FF)), const((D_FF, D)),
            const((1, D)), const((1, D)),
        ],
        out_specs=rows(0),
        out_shape=jax.ShapeDtypeStruct((T, D), F32),
        scratch_shapes=[
            pltpu.VMEM((tm, D), F32),
            pltpu.VMEM((HALO + tm, D_FF), F32),
            pltpu.VMEM((HALO, D_FF), F32),
            pltpu.VMEM((tm, D_FF), BF16),
        ],
        compiler_params=pltpu.CompilerParams(
            dimension_semantics=("arbitrary",), vmem_limit_bytes=VMEM_LIMIT),
        name="merge_ffn",
    )(hm, hd, proj, x2, w_bm, w_bd, w_o, ln1_g, ln1_b, w_up, conv_w, conv_b, w_down, ln2_g, ln2_b)


def _layer(h2, batch, seq, l, w_in, b_in, mconv_w, mconv_b, mnorm_w, lq1, lk1, lq2, lk2, dnorm_w, bias_tiles,
           w_bm, w_bd, w_o, ln1_g, ln1_b, w_up, fconv_w, fconv_b, w_down, ln2_g, ln2_b):
    lam_init = 0.8 - 0.6 * math.exp(-0.3 * l)
    n_m = 4 * MLSTM_WIDTH
    n_gate = 2 * MLSTM_HEADS
    n_qk = 2 * MLSTM_WIDTH
    w_all, w_b = _cast_w_in(w_in.T.astype(F32), n_m + n_gate)
    conv_params = jnp.concatenate(
        [mconv_w.astype(F32).T, mconv_b.astype(F32)[:, None], b_in[:n_qk].astype(F32)[:, None],
         jnp.zeros((n_qk, CONV_PARAMS - MLSTM_CONV - 2), F32)], axis=1)
    b_b = b_in[n_m + n_gate:][None, :].astype(F32)
    dv0 = n_m + n_gate + TAIL_DV
    b_t = jnp.concatenate([b_in[n_qk:n_m], b_in[dv0:dv0 + DIFF_WIDTH]])[:, None].astype(F32)
    b_g = b_in[n_m:n_m + GATE_PAD][None, :].astype(F32)
    b_gt = b_in[n_m:n_m + GATE_ROWS][:, None].astype(F32)

    proj, proj_t, gcol, grow = _in_proj(h2, w_all, conv_params, w_b, b_b, b_t, b_g, b_gt, seq)
    hd, hm, w_bm16, w_bd16, w_o16, w_up16, w_down16 = _mixers(
        proj, proj_t, gcol, grow, bias_tiles, lq1[None, :].astype(F32), lk1[None, :].astype(F32),
        lq2[None, :].astype(F32), lk2[None, :].astype(F32), dnorm_w[:, None].astype(F32),
        mnorm_w[:, None].astype(F32), lam_init, batch, seq,
        [w.astype(F32) for w in (w_bm, w_bd, w_o, w_up, w_down)])
    return _merge_ffn(hm, hd, proj, h2, w_bm16, w_bd16, w_o16,
                      ln1_g[None, :].astype(F32), ln1_b[None, :].astype(F32),
                      w_up16, fconv_w.astype(F32), fconv_b[None, :].astype(F32),
                      w_down16, ln2_g[None, :].astype(F32), ln2_b[None, :].astype(F32), seq)


def kernel(x, w_in, b_in, mlstm_conv_w, mlstm_conv_b, mlstm_norm_w, lambda_q1, lambda_k1, lambda_q2, lambda_k2,
           diff_norm_w, rel_bias, w_branch_mlstm, w_branch_diff, w_out, ln1_g, ln1_b, w_ffn_up, ffn_conv_w,
           ffn_conv_b, w_ffn_down, ln2_g, ln2_b):
    batch, seq, d_model = x.shape
    assert d_model == D_MODEL and seq % max(PROJ_TM, FFN_TM, ATT_T, MLSTM_L) == 0
    bias_tiles = _rel_bias_tiles(rel_bias)
    h2 = x.reshape(batch * seq, d_model)
    for l in range(w_in.shape[0]):
        h2 = _layer(h2, batch, seq, l, w_in[l], b_in[l], mlstm_conv_w[l], mlstm_conv_b[l], mlstm_norm_w[l],
                    lambda_q1[l], lambda_k1[l], lambda_q2[l], lambda_k2[l], diff_norm_w[l], bias_tiles,
                    w_branch_mlstm[l], w_branch_diff[l], w_out[l], ln1_g[l], ln1_b[l], w_ffn_up[l],
                    ffn_conv_w[l], ffn_conv_b[l], w_ffn_down[l], ln2_g[l], ln2_b[l])
    return h2.reshape(batch, seq, d_model).astype(x.dtype)
```

```python
import functools
import math

import jax
import jax.numpy as jnp
from jax import lax
from jax.experimental import pallas as pl
from jax.experimental.pallas import tpu as pltpu

F32 = jnp.float32
BF16 = jnp.bfloat16

D_MODEL = 1024
MLSTM_HEADS = 4
MLSTM_HEAD_DIM = 256
MLSTM_WIDTH = MLSTM_HEADS * MLSTM_HEAD_DIM
MLSTM_CONV = 4
DIFF_HEADS = 8
DIFF_HEAD_DIM = 64
DIFF_V_DIM = 2 * DIFF_HEAD_DIM
DIFF_WIDTH = DIFF_HEADS * DIFF_V_DIM
REL_BUCKETS = 32
REL_MAX_DIST = 128
D_FF = 2816
FFN_CONV = 3
DEPTH = 1
DEEPNORM_ALPHA = (2.0 * DEPTH) ** 0.25
LN_EPS = 1e-5
LOG2E = math.log2(math.e)

COL_GM = 0
COL_GD = 1024
COL_MK = 2048
COL_QK = 3072
N_MAIN = 5120
ROW_MV = 0
ROW_MO = 1024
ROW_MQ = 2048
ROW_DV = 3072
N_TRANS = 4096
TAIL_DQ = 0
TAIL_DK = 1024
TAIL_DV = 2048
TAIL_GM = 3072
N_TAIL = 5120
GATE_PAD = 128
GATE_ROWS = 16
MLSTM_AUG = 16

HALO = 8
LANES = 128
CONV_PARAMS = 8
PROJ_TM = 512
PROJ_TN = 512
MLSTM_L = 256
ATT_T = 256
ATT_FAR_TILES = 1
ATT_SUM_ROWS = 16
INTERLEAVE_EVERY = 4
W_IN_CAST_ROWS = 512
CAST_STEPS = 4
MERGE_ROWS = 256
FFN_TM = 512
FFN_TC = 256
VMEM_LIMIT = 58 * 1024 * 1024


def _sigmoid(v):
    return 1.0 / (1.0 + jnp.exp(-v))


def _log_sigmoid(v):
    return jnp.minimum(v, 0.0) - jnp.log(1.0 + jnp.exp(-jnp.abs(v)))


def _split3(v):
    hi = v.astype(BF16)
    rest = v - hi.astype(F32)
    mid = rest.astype(BF16)
    lo = (rest - mid.astype(F32)).astype(BF16)
    return hi, mid, lo


def _rel_bias_kernel(table_ref, out_ref):
    h = pl.program_id(0)
    T = out_ref.shape[-1]
    max_exact = REL_BUCKETS // 2
    far = table_ref[REL_BUCKETS - 1, h]
    n = lax.broadcasted_iota(jnp.int32, (8, T), 1)
    nf = jnp.maximum(n, 1).astype(F32)
    large = max_exact + (jnp.log(nf / max_exact) / math.log(REL_MAX_DIST / max_exact)
                         * (REL_BUCKETS - max_exact)).astype(jnp.int32)
    large = jnp.minimum(large, REL_BUCKETS - 1)
    bucket = jnp.where(n < max_exact, n, large)
    by_dist = jnp.zeros((8, T), F32)
    for kk in range(REL_BUCKETS):
        by_dist = jnp.where(bucket == kk, table_ref[kk, h], by_dist)
    by_dist = (by_dist - far) * LOG2E
    rolled = pltpu.roll(jnp.broadcast_to(by_dist[0:1, :], (T, T)), 0, 1, stride=1, stride_axis=0)
    kpos = lax.broadcasted_iota(jnp.int32, (T, T), 0)
    qpos = lax.broadcasted_iota(jnp.int32, (T, T), 1)
    out_ref[0, 0] = jnp.where(qpos < kpos, rolled, 0.0)
    out_ref[0, 1] = jnp.where(qpos >= kpos, rolled, -jnp.inf)


def _rel_bias_tiles(rel_bias):
    T = ATT_T
    return pl.pallas_call(
        _rel_bias_kernel,
        grid=(DIFF_HEADS,),
        in_specs=[pl.BlockSpec(memory_space=pltpu.SMEM)],
        out_specs=pl.BlockSpec((1, 2, T, T), lambda h: (h, 0, 0, 0)),
        out_shape=jax.ShapeDtypeStruct((DIFF_HEADS, 2, T, T), F32),
        name="rel_bias_tiles",
    )(rel_bias.astype(F32))


def _in_proj_kernel(tiles_per_seq, x_ref, wa_ref, cp_ref, wb_ref, bb_ref, wt_ref, bt_ref, wg_ref, bg_ref,
                    bgt_ref, o_ref, ot_ref, gcol_ref, grow_ref, halo_ref):
    i = pl.program_id(0)
    tm = x_ref.shape[0]
    tn = PROJ_TN
    n_conv_cols = 2 * MLSTM_WIDTH
    nt_dims = (((1,), (1,)), ((), ()))

    @pl.when(i % tiles_per_seq == 0)
    def _():
        halo_ref[...] = jnp.zeros_like(halo_ref)

    xb = x_ref[...].astype(BF16)
    gcol = lax.dot_general(xb, wg_ref[...], nt_dims, preferred_element_type=F32) + bg_ref[...]
    grow = lax.dot_general(wg_ref[0:GATE_ROWS, :], xb, nt_dims,
                           preferred_element_type=F32) + bgt_ref[...]
    L = MLSTM_L
    H = MLSTM_HEADS
    tri = jnp.where(lax.broadcasted_iota(jnp.int32, (L, L), 0) >= lax.broadcasted_iota(jnp.int32, (L, L), 1),
                    1.0, 0.0).astype(BF16)
    gate_row = lax.broadcasted_iota(jnp.int32, (GATE_ROWS, L), 0)
    for r0 in range(0, tm, L):
        g_c = gcol[r0:r0 + L, :]
        g_r = grow[:, r0:r0 + L]
        bc3 = jnp.dot(tri, jnp.concatenate(_split3(_log_sigmoid(g_c)), axis=1), preferred_element_type=F32)
        bcol = bc3[:, 0:GATE_PAD] + bc3[:, GATE_PAD:2 * GATE_PAD] + bc3[:, 2 * GATE_PAD:3 * GATE_PAD]
        br3 = lax.dot_general(jnp.concatenate(_split3(_log_sigmoid(g_r)), axis=0), tri, nt_dims,
                              preferred_element_type=F32)
        brow = br3[0:GATE_ROWS] + br3[GATE_ROWS:2 * GATE_ROWS] + br3[2 * GATE_ROWS:3 * GATE_ROWS]
        gcol_ref[r0:r0 + L, :] = g_c - pltpu.roll(bcol, GATE_PAD - H, 1)
        grow_ref[:, r0:r0 + L] = jnp.where(gate_row < H, g_r, brow)

    n_vo = 2 * MLSTM_WIDTH
    for r0 in list(range(ROW_MV, ROW_MV + n_vo, tn)) + list(range(ROW_DV, ROW_DV + DIFF_WIDTH, tn)):
        if r0 < ROW_DV:
            src = slice(r0 - ROW_MV, r0 - ROW_MV + tn)
            w_rows, bias_col = wt_ref[src, :], bt_ref[src, :]
        else:
            w_rows = wb_ref[TAIL_DV + r0 - ROW_DV:TAIL_DV + r0 - ROW_DV + tn, :]
            bias_col = bt_ref[n_vo + r0 - ROW_DV:n_vo + r0 - ROW_DV + tn, :]
        acc_t = lax.dot_general(w_rows, xb, nt_dims, preferred_element_type=F32) + bias_col
        if ROW_MO <= r0 < ROW_MO + MLSTM_WIDTH:
            acc_t = _sigmoid(acc_t)
        ot_ref[r0:r0 + tn, :] = acc_t.astype(BF16)

    lane = lax.broadcasted_iota(jnp.int32, (tn, LANES), 1)
    for c0 in range(0, n_conv_cols, tn):
        cols = slice(c0, c0 + tn)
        par = cp_ref[cols, :]
        acc_t = lax.dot_general(wa_ref[cols, :], xb, nt_dims, preferred_element_type=F32) \
            + par[:, MLSTM_CONV + 1:MLSTM_CONV + 2]
        prev = halo_ref[cols, :]
        y = par[:, MLSTM_CONV:MLSTM_CONV + 1] + par[:, MLSTM_CONV - 1:MLSTM_CONV] * acc_t
        for kk in range(MLSTM_CONV - 1):
            shift = MLSTM_CONV - 1 - kk
            rolled = pltpu.roll(acc_t, shift, 1)
            head = jnp.where(lane < shift, pltpu.roll(prev, shift, 1), rolled[:, 0:LANES])
            y = y + par[:, kk:kk + 1] * jnp.concatenate([head, rolled[:, LANES:]], axis=1)
        halo_ref[cols, :] = acc_t[:, tm - LANES:tm]
        y = y * _sigmoid(y)
        if c0 < MLSTM_WIDTH:
            ot_ref[ROW_MQ + c0:ROW_MQ + c0 + tn, :] = y.astype(BF16)
        else:
            k0 = COL_MK + c0 - MLSTM_WIDTH
            o_ref[:, k0:k0 + tn] = (y * (MLSTM_HEAD_DIM ** -0.5)).T.astype(BF16)

    for g0 in range(0, 2 * D_MODEL, tn):
        wrows = slice(TAIL_GM + g0, TAIL_GM + g0 + tn)
        acc = lax.dot_general(xb, wb_ref[wrows, :], nt_dims, preferred_element_type=F32) + bb_ref[:, wrows]
        o_ref[:, COL_GM + g0:COL_GM + g0 + tn] = _sigmoid(acc).astype(BF16)

    dv = DIFF_V_DIM
    heads_per_chunk = tn // (2 * dv)
    q_scale = jnp.full((1, dv), DIFF_HEAD_DIM ** -0.5 * LOG2E, F32)
    scale_row = jnp.concatenate([q_scale, jnp.ones((1, dv), F32)] * heads_per_chunk, axis=1)
    for h0 in range(0, DIFF_HEADS, heads_per_chunk):
        starts = []
        for h in range(h0, h0 + heads_per_chunk):
            starts += [TAIL_DQ + h * dv, TAIL_DK + h * dv]
        w_rows = jnp.concatenate([wb_ref[r:r + dv, :] for r in starts], axis=0)
        b_row = jnp.concatenate([bb_ref[:, r:r + dv] for r in starts], axis=1)
        acc = lax.dot_general(xb, w_rows, nt_dims, preferred_element_type=F32) + b_row
        c0 = COL_QK + h0 * 2 * dv
        o_ref[:, c0:c0 + tn] = (acc * scale_row).astype(BF16)


def _cast_w_in_kernel(head_ref, tail_ref, head_out, tail_out):
    head_out[...] = head_ref[...].astype(BF16)
    tail_out[...] = tail_ref[...].astype(BF16)


def _cast_w_in(w_in_t, tail_start):
    k = w_in_t.shape[1]
    blk = W_IN_CAST_ROWS
    n_steps = N_TAIL // blk
    n_head_blocks = -(-(tail_start - 2 * MLSTM_HEADS + GATE_PAD) // blk)
    assert N_TAIL % blk == 0 and n_head_blocks <= n_steps and tail_start % 8 == 0
    head_idx = lambda j: (jnp.minimum(j, n_head_blocks - 1), 0)
    return pl.pallas_call(
        _cast_w_in_kernel,
        grid=(n_steps,),
        in_specs=[
            pl.BlockSpec((blk, k), head_idx),
            pl.BlockSpec((pl.Element(blk), pl.Element(k)), lambda j: (pl.multiple_of(tail_start + j * blk, 8), 0)),
        ],
        out_specs=[
            pl.BlockSpec((blk, k), head_idx),
            pl.BlockSpec((blk, k), lambda j: (j, 0)),
        ],
        out_shape=[
            jax.ShapeDtypeStruct((n_head_blocks * blk, k), BF16),
            jax.ShapeDtypeStruct((N_TAIL, k), BF16),
        ],
        name="cast_w_in",
    )(w_in_t, w_in_t)


def _const_spec(shape):
    return pl.BlockSpec(shape, lambda i: (0, 0), pipeline_mode=pl.Buffered(1))


def _in_proj(x2, w_all, conv_params, w_b, b_b, b_t, b_g, b_gt, seq):
    T = x2.shape[0]
    tm = PROJ_TM
    n_conv_cols = 2 * MLSTM_WIDTH
    n_vo = 2 * MLSTM_WIDTH
    assert n_vo == n_conv_cols and (n_conv_cols + n_vo) % GATE_PAD == 0
    const = _const_spec
    row_block = lambda shape, idx: pl.BlockSpec(shape, lambda i: (idx, 0), pipeline_mode=pl.Buffered(1))
    return pl.pallas_call(
        functools.partial(_in_proj_kernel, seq // tm),
        grid=(T // tm,),
        in_specs=[
            pl.BlockSpec((tm, D_MODEL), lambda i: (i, 0)),
            row_block((n_conv_cols, D_MODEL), 0),
            const((n_conv_cols, CONV_PARAMS)),
            const((N_TAIL, D_MODEL)),
            const((1, N_TAIL)),
            row_block((n_vo, D_MODEL), 1),
            const((n_vo + DIFF_WIDTH, 1)),
            row_block((GATE_PAD, D_MODEL), (n_conv_cols + n_vo) // GATE_PAD),
            const((1, GATE_PAD)),
            const((GATE_ROWS, 1)),
        ],
        out_specs=[
            pl.BlockSpec((tm, N_MAIN), lambda i: (i, 0)),
            pl.BlockSpec((N_TRANS, tm), lambda i: (0, i)),
            pl.BlockSpec((tm, GATE_PAD), lambda i: (i, 0)),
            pl.BlockSpec((GATE_ROWS, tm), lambda i: (0, i)),
        ],
        out_shape=[
            jax.ShapeDtypeStruct((T, N_MAIN), BF16),
            jax.ShapeDtypeStruct((N_TRANS, T), BF16),
            jax.ShapeDtypeStruct((T, GATE_PAD), F32),
            jax.ShapeDtypeStruct((GATE_ROWS, T), F32),
        ],
        scratch_shapes=[
            pltpu.VMEM((n_conv_cols, LANES), F32),
        ],
        compiler_params=pltpu.CompilerParams(
            dimension_semantics=("arbitrary",), vmem_limit_bytes=VMEM_LIMIT),
        name="in_proj",
    )(x2, w_all, conv_params, w_b, b_b, w_all, b_t, w_all, b_g, b_gt)


def _mlstm_chunk(c, qt_ref, k_ref, vt_ref, ot_ref, gcol_ref, grow_ref, nw_ref, out_ref, ct_ref, m_ref, nwb_ref):
    L = k_ref.shape[0]
    d = MLSTM_HEAD_DIM
    H = MLSTM_HEADS
    nt_dims = (((1,), (1,)), ((), ()))

    @pl.when(c == 0)
    def _():
        ct_ref[...] = jnp.zeros_like(ct_ref)
        m_ref[...] = jnp.zeros_like(m_ref)
        nwb_ref[...] = jnp.broadcast_to(nw_ref[...], nwb_ref.shape)

    key = lax.broadcasted_iota(jnp.int32, (L, L), 0)
    qry = lax.broadcasted_iota(jnp.int32, (L, L), 1)
    causal_t = key <= qry

    gdiff = gcol_ref[...]
    grow = grow_ref[...]
    ones = jnp.ones((MLSTM_AUG, L), BF16)

    def head(h):
        cols = slice(h * d, (h + 1) * d)
        q_t = qt_ref[cols, :]
        k = k_ref[:, cols]
        v_aug = jnp.concatenate([vt_ref[cols, :], ones], axis=0)
        b_r = grow[H + h:H + h + 1, :]
        i_r = grow[h:h + 1, :]
        g_c = gdiff[:, h:h + 1]
        m_prev = m_ref[h]
        ct = ct_ref[h]

        dmat = jnp.where(causal_t, b_r + g_c, -jnp.inf)
        a = b_r + m_prev
        m_row = jnp.maximum(a, jnp.max(dmat, axis=0, keepdims=True))
        w = jnp.exp(dmat - m_row)
        inter = jnp.exp(a - m_row)
        kq = jnp.dot(k, q_t, preferred_element_type=F32)
        sqk = (kq * w).astype(BF16)
        nd = inter * jnp.dot(ct.astype(BF16), q_t, preferred_element_type=F32) \
            + jnp.dot(v_aug, sqk, preferred_element_type=F32)
        rinv = 1.0 / jnp.maximum(jnp.abs(nd[d:d + 1, :]), jnp.exp(-m_row))
        hid = nd[0:d, :] * rinv * ot_ref[cols, :].astype(F32)
        mu = jnp.mean(hid, axis=0, keepdims=True)
        cen = hid - mu
        var = jnp.mean(cen * cen, axis=0, keepdims=True)
        out_ref[:, cols] = (cen * lax.rsqrt(var + LN_EPS) * nwb_ref[cols, :]).T.astype(BF16)

        b_last = b_r[:, L - 1:L]
        g_r = b_last - b_r + i_r
        m_new = jnp.maximum(b_last + m_prev, jnp.max(g_r, axis=-1, keepdims=True))
        decay = jnp.exp(b_last + m_prev - m_new)
        ws = jnp.exp(g_r - m_new)
        vw = (v_aug.astype(F32) * ws).astype(BF16)
        ct_ref[h] = decay * ct + jnp.dot(vw, k, preferred_element_type=F32)
        m_ref[h] = m_new

    return [functools.partial(head, h) for h in range(H)]


def _diff_attn_head(lam_init, interleaved, q_ref, k_ref, v_ref, bias_ref, lq1_ref, lk1_ref, lq2_ref, lk2_ref,
                    nw_ref, out_ref, qs_ref, vt_ref, s_ref, p_ref, acc_ref):
    T = ATT_T
    dh = DIFF_HEAD_DIM
    dv = DIFF_V_DIM
    nq = q_ref.shape[0] // T
    R = 2 * T

    vt_ref[0:dv, :] = v_ref[...]
    vt_ref[dv:dv + ATT_SUM_ROWS, :] = jnp.ones((ATT_SUM_ROWS, v_ref.shape[1]), BF16)

    lane = lax.broadcasted_iota(jnp.int32, (T, dv), 1)
    for ii in range(nq):
        qt = q_ref[ii * T:(ii + 1) * T, :]
        zero = jnp.zeros_like(qt)
        qs_ref[ii, 0:T, :] = jnp.where(lane < dh, qt, zero)
        qs_ref[ii, T:R, :] = jnp.where(lane >= dh, qt, zero)

    lam = (jnp.exp(jnp.sum(lq1_ref[...] * lk1_ref[...], axis=-1, keepdims=True))
           - jnp.exp(jnp.sum(lq2_ref[...] * lk2_ref[...], axis=-1, keepdims=True)) + lam_init)

    pairs = []
    for i in range(nq):
        k0 = 0
        while k0 < (i - 1) * T:
            klen = min(ATT_FAR_TILES * T, (i - 1) * T - k0)
            pairs.append((i, k0, klen, None))
            k0 += klen
        if i >= 1:
            pairs.append((i, (i - 1) * T, T, 0))
        pairs.append((i, i * T, T, 1))

    def scores(n):
        i, k0, klen, bias_idx = pairs[n]
        s = lax.dot_general(k_ref[k0:k0 + klen, :], qs_ref[i], (((1,), (1,)), ((), ())),
                            preferred_element_type=F32)
        if bias_idx is not None:
            bias = bias_ref[0, bias_idx]
            s = s + jnp.concatenate([bias, bias], axis=1)
        s_ref[n % 2, 0:klen, :] = s
        return jnp.max(s, axis=0, keepdims=True)

    def softmax(n, m_tile, m_prev):
        i, k0, klen, _ = pairs[n]
        m_new = m_tile if k0 == 0 else jnp.maximum(m_prev, m_tile)
        p_ref[n % 2, 0:klen, :] = jnp.exp2(s_ref[n % 2, 0:klen, :] - m_new).astype(BF16)
        alpha = None if k0 == 0 else jnp.exp2(m_prev - m_new)
        return m_new, alpha

    def values(n, alpha):
        i, k0, klen, _ = pairs[n]
        pv = jnp.dot(vt_ref[:, k0:k0 + klen], p_ref[n % 2, 0:klen, :],
                     preferred_element_type=F32)
        if k0 == 0:
            acc_ref[...] = pv
        else:
            acc_ref[...] = alpha * acc_ref[...] + pv
        if k0 == i * T:
            acc = acc_ref[...]
            out = acc[0:dv, :] / acc[dv:dv + 1, :]
            hd = out[:, 0:T] - lam * out[:, T:R]
            hd = hd * lax.rsqrt(jnp.mean(hd * hd, axis=0, keepdims=True) + LN_EPS) * nw_ref[...]
            out_ref[i * T:(i + 1) * T, :] = (hd * (1.0 - lam_init)).T.astype(BF16)

    spacing = INTERLEAVE_EVERY if interleaved else 0
    m_tiles = {0: scores(0)}
    m_run = None
    alphas = {}
    for n in range(len(pairs)):
        if n + 1 < len(pairs):
            m_tiles[n + 1] = scores(n + 1)
        m_run, alphas[n] = softmax(n, m_tiles.pop(n), m_run)
        if n >= 1:
            values(n - 1, alphas.pop(n - 1))
        if spacing and (n + 1) % spacing == 0 and (n + 1) // spacing <= len(interleaved):
            interleaved[(n + 1) // spacing - 1]()
    values(len(pairs) - 1, alphas.pop(len(pairs) - 1))


def _mixers_kernel(lam_init, chunks_per_seq, n_attn_refs, n_cast, *refs):
    (qk_ref, v_ref, bias_ref, lq1_ref, lk1_ref, lq2_ref, lk2_ref, anw_ref) = refs[:n_attn_refs]
    q_ref, k_ref = qk_ref.at[:, 0:DIFF_V_DIM], qk_ref.at[:, DIFF_V_DIM:2 * DIFF_V_DIM]
    (voq_ref, mk_ref, gcol_ref, grow_ref, mnw_ref) = refs[n_attn_refs:n_attn_refs + 5]
    n_in = n_attn_refs + 5 + n_cast
    cast_in = refs[n_attn_refs + 5:n_in]
    hd_ref, hm_ref = refs[n_in:n_in + 2]
    cast_out = refs[n_in + 2:n_in + 2 + n_cast]
    (qs_ref, vt_ref, s_ref, p_ref, acc_ref, ct_ref, m_ref, nwb_ref) = refs[n_in + 2 + n_cast:]
    W = MLSTM_WIDTH
    mvt_ref, mot_ref, mq_ref = (voq_ref.at[r:r + W] for r in (ROW_MV, ROW_MO, ROW_MQ))
    c = pl.program_id(0) % chunks_per_seq
    heads = _mlstm_chunk(c, mq_ref, mk_ref, mvt_ref, mot_ref, gcol_ref, grow_ref, mnw_ref, hm_ref,
                         ct_ref, m_ref, nwb_ref)
    _diff_attn_head(lam_init, heads, q_ref, k_ref, v_ref, bias_ref, lq1_ref, lk1_ref, lq2_ref, lk2_ref,
                    anw_ref, hd_ref, qs_ref, vt_ref, s_ref, p_ref, acc_ref)

    @pl.when(pl.program_id(0) < CAST_STEPS)
    def _():
        for w_ref, o_ref in zip(cast_in, cast_out):
            o_ref[...] = w_ref[...].astype(BF16)


def _mixers(proj, proj_t, gcol, grow, bias_tiles, lq1, lk1, lq2, lk2, dnorm_w, mnorm_w, lam_init, batch, seq,
            next_weights):
    T = ATT_T
    nq = seq // T
    dv = DIFF_V_DIM
    L = MLSTM_L
    nc = seq // L
    W = MLSTM_WIDTH
    assert DIFF_HEADS * batch == batch * nc, "attention (head, batch) steps must pair 1:1 with mLSTM chunks"
    small = lambda shape: pl.BlockSpec(shape, lambda n: (0, 0))
    attn_specs = [
        pl.BlockSpec((seq, 2 * dv), lambda n: (n % batch, COL_QK // (2 * dv) + n // batch)),
        pl.BlockSpec((dv, seq), lambda n: (ROW_DV // dv + n // batch, n % batch)),
        pl.BlockSpec((1, 2, T, T), lambda n: (n // batch, 0, 0, 0)),
        small((1, DIFF_HEAD_DIM)), small((1, DIFF_HEAD_DIM)),
        small((1, DIFF_HEAD_DIM)), small((1, DIFF_HEAD_DIM)),
        small((dv, 1)),
    ]
    mlstm_specs = [
        pl.BlockSpec((ROW_DV, L), lambda n: (0, n)),
        pl.BlockSpec((L, W), lambda n: (n, COL_MK // W)),
        pl.BlockSpec((L, GATE_PAD), lambda n: (n, 0)),
        pl.BlockSpec((GATE_ROWS, L), lambda n: (0, n)),
        small((W, 1)),
    ]
    def cast_specs():
        return [pl.BlockSpec((w.shape[0] // CAST_STEPS, w.shape[1]),
                             lambda n: (jnp.minimum(n, CAST_STEPS - 1), 0)) for w in next_weights]

    assert all(w.shape[0] % (16 * CAST_STEPS) == 0 for w in next_weights)
    return pl.pallas_call(
        functools.partial(_mixers_kernel, lam_init, nc, len(attn_specs), len(next_weights)),
        grid=(batch * nc,),
        in_specs=attn_specs + mlstm_specs + cast_specs(),
        out_specs=[
            pl.BlockSpec((seq, dv), lambda n: (n % batch, n // batch)),
            pl.BlockSpec((L, W), lambda n: (n, 0)),
        ] + cast_specs(),
        out_shape=[
            jax.ShapeDtypeStruct((batch * seq, DIFF_WIDTH), BF16),
            jax.ShapeDtypeStruct((batch * seq, W), BF16),
        ] + [jax.ShapeDtypeStruct(w.shape, BF16) for w in next_weights],
        scratch_shapes=[
            pltpu.VMEM((nq, 2 * T, dv), BF16),
            pltpu.VMEM((dv + ATT_SUM_ROWS, seq), BF16),
            pltpu.VMEM((2, ATT_FAR_TILES * T, 2 * T), F32),
            pltpu.VMEM((2, ATT_FAR_TILES * T, 2 * T), BF16),
            pltpu.VMEM((dv + ATT_SUM_ROWS, 2 * T), F32),
            pltpu.VMEM((MLSTM_HEADS, MLSTM_HEAD_DIM + MLSTM_AUG, MLSTM_HEAD_DIM), F32),
            pltpu.VMEM((MLSTM_HEADS, 1, 1), F32),
            pltpu.VMEM((W, L), F32),
        ],
        compiler_params=pltpu.CompilerParams(
            dimension_semantics=("arbitrary",), vmem_limit_bytes=VMEM_LIMIT),
        name="token_mixers",
    )(proj, proj_t, bias_tiles, lq1, lk1, lq2, lk2, dnorm_w, proj_t, proj, gcol, grow, mnorm_w, *next_weights)


def _layer_norm(y, g, b):
    mu = jnp.mean(y, axis=-1, keepdims=True)
    cen = y - mu
    var = jnp.mean(cen * cen, axis=-1, keepdims=True)
    return cen * lax.rsqrt(var + LN_EPS) * g + b


def _merge_kernel(hm_ref, hd_ref, gates_ref, x_ref, wbm_ref, wbd_ref, wo_ref, g_ref, b_ref, out_ref):
    tm = hm_ref.shape[0]
    gm_ref = gates_ref.at[:, 0:D_MODEL]
    gd_ref = gates_ref.at[:, D_MODEL:2 * D_MODEL]
    nb = tm // MERGE_ROWS
    halves = [slice(r * MERGE_ROWS, (r + 1) * MERGE_ROWS) for r in range(nb)]
    merged = []
    for rows in halves:
        pm = jnp.dot(hm_ref[rows, :], wbm_ref[...], preferred_element_type=F32)
        pd = jnp.dot(hd_ref[rows, :], wbd_ref[...], preferred_element_type=F32)
        merged.append((gm_ref[rows, :].astype(F32) * pm + gd_ref[rows, :].astype(F32) * pd).astype(BF16))
    for rows, m in zip(halves, merged):
        mix = jnp.dot(m, wo_ref[...], preferred_element_type=F32)
        out_ref[rows, :] = _layer_norm(DEEPNORM_ALPHA * x_ref[rows, :] + mix, g_ref[...], b_ref[...])


def _ffn_kernel(tiles_per_seq, h_ref, wu_ref, cw_ref, cb_ref, wd_ref, g_ref, b_ref, out_ref,
                conv_ref, halo_ref, act_ref):
    i = pl.program_id(0)
    tm = h_ref.shape[0]
    tc = FFN_TC

    @pl.when(i % tiles_per_seq == 0)
    def _():
        halo_ref[...] = jnp.zeros_like(halo_ref)

    h = h_ref[...]
    hb = h.astype(BF16)
    for c0 in range(0, D_FF, tc):
        cols = slice(c0, c0 + tc)
        a = jnp.dot(hb, wu_ref[:, cols], preferred_element_type=F32)
        gate = jnp.dot(hb, wu_ref[:, D_FF + c0:D_FF + c0 + tc], preferred_element_type=F32)
        conv_ref[0:HALO, cols] = halo_ref[:, cols]
        conv_ref[HALO:HALO + tm, cols] = a
        y = cb_ref[:, cols] + cw_ref[FFN_CONV - 1:FFN_CONV, cols] * a
        for kk in range(FFN_CONV - 1):
            off = HALO - (FFN_CONV - 1) + kk
            y = y + cw_ref[kk:kk + 1, cols] * conv_ref[off:off + tm, cols]
        halo_ref[:, cols] = a[tm - HALO:tm, :]
        act_ref[:, cols] = (y * _sigmoid(y) * gate).astype(BF16)

    for r0 in range(0, tm, tm // 2):
        rows = slice(r0, r0 + tm // 2)
        ffn = jnp.dot(act_ref[rows, :], wd_ref[...], preferred_element_type=F32)
        out_ref[rows, :] = _layer_norm(DEEPNORM_ALPHA * h[rows, :] + ffn, g_ref[...], b_ref[...])


def _merge_ffn_kernel(tiles_per_seq, hm_ref, hd_ref, gates_ref, x_ref, wbm_ref, wbd_ref, wo_ref, g1_ref,
                      b1_ref, wu_ref, cw_ref, cb_ref, wd_ref, g2_ref, b2_ref, out_ref,
                      h_ref, conv_ref, halo_ref, act_ref):
    _merge_kernel(hm_ref, hd_ref, gates_ref, x_ref, wbm_ref, wbd_ref, wo_ref, g1_ref, b1_ref, h_ref)
    _ffn_kernel(tiles_per_seq, h_ref, wu_ref, cw_ref, cb_ref, wd_ref, g2_ref, b2_ref, out_ref,
                conv_ref, halo_ref, act_ref)


def _merge_ffn(hm, hd, proj, x2, w_bm, w_bd, w_o, ln1_g, ln1_b, w_up, conv_w, conv_b, w_down, ln2_g, ln2_b, seq):
    T = x2.shape[0]
    tm = FFN_TM
    D = D_MODEL
    rows = lambda col: pl.BlockSpec((tm, D), lambda i: (i, col))
    const = _const_spec
    return pl.pallas_call(
        functools.partial(_merge_ffn_kernel, seq // tm),
        grid=(T // tm,),
        in_specs=[
            rows(0), rows(0), pl.BlockSpec((tm, 2 * D), lambda i: (i, COL_GM // (2 * D))), rows(0),
            const((MLSTM_WIDTH, D)), const((DIFF_WIDTH, D)), const((D, D)), const((1, D)), const((1, D)),
            const((D, 2 * D_FF)), const((FFN_CONV, D_FF)), const((1, D_FF)), const((D_FF, D)),
            const((1, D)), const((1, D)),
        ],
        out_specs=rows(0),
        out_shape=jax.ShapeDtypeStruct((T, D), F32),
        scratch_shapes=[
            pltpu.VMEM((tm, D), F32),
            pltpu.VMEM((HALO + tm, D_FF), F32),
            pltpu.VMEM((HALO, D_FF), F32),
            pltpu.VMEM((tm, D_FF), BF16),
        ],
        compiler_params=pltpu.CompilerParams(
            dimension_semantics=("arbitrary",), vmem_limit_bytes=VMEM_LIMIT),
        name="merge_ffn",
    )(hm, hd, proj, x2, w_bm, w_bd, w_o, ln1_g, ln1_b, w_up, conv_w, conv_b, w_down, ln2_g, ln2_b)


def _layer(h2, batch, seq, l, w_in, b_in, mconv_w, mconv_b, mnorm_w, lq1, lk1, lq2, lk2, dnorm_w, bias_tiles,
           w_bm, w_bd, w_o, ln1_g, ln1_b, w_up, fconv_w, fconv_b, w_down, ln2_g, ln2_b):
    lam_init = 0.8 - 0.6 * math.exp(-0.3 * l)
    n_m = 4 * MLSTM_WIDTH
    n_gate = 2 * MLSTM_HEADS
    n_qk = 2 * MLSTM_WIDTH
    w_all, w_b = _cast_w_in(w_in.T.astype(F32), n_m + n_gate)
    conv_params = jnp.concatenate(
        [mconv_w.astype(F32).T, mconv_b.astype(F32)[:, None], b_in[:n_qk].astype(F32)[:, None],
         jnp.zeros((n_qk, CONV_PARAMS - MLSTM_CONV - 2), F32)], axis=1)
    b_b = b_in[n_m + n_gate:][None, :].astype(F32)
    dv0 = n_m + n_gate + TAIL_DV
    b_t = jnp.concatenate([b_in[n_qk:n_m], b_in[dv0:dv0 + DIFF_WIDTH]])[:, None].astype(F32)
    b_g = b_in[n_m:n_m + GATE_PAD][None, :].astype(F32)
    b_gt = b_in[n_m:n_m + GATE_ROWS][:, None].astype(F32)

    proj, proj_t, gcol, grow = _in_proj(h2, w_all, conv_params, w_b, b_b, b_t, b_g, b_gt, seq)
    hd, hm, w_bm16, w_bd16, w_o16, w_up16, w_down16 = _mixers(
        proj, proj_t, gcol, grow, bias_tiles, lq1[None, :].astype(F32), lk1[None, :].astype(F32),
        lq2[None, :].astype(F32), lk2[None, :].astype(F32), dnorm_w[:, None].astype(F32),
        mnorm_w[:, None].astype(F32), lam_init, batch, seq,
        [w.astype(F32) for w in (w_bm, w_bd, w_o, w_up, w_down)])
    return _merge_ffn(hm, hd, proj, h2, w_bm16, w_bd16, w_o16,
                      ln1_g[None, :].astype(F32), ln1_b[None, :].astype(F32),
                      w_up16, fconv_w.astype(F32), fconv_b[None, :].astype(F32),
                      w_down16, ln2_g[None, :].astype(F32), ln2_b[None, :].astype(F32), seq)


def kernel(x, w_in, b_in, mlstm_conv_w, mlstm_conv_b, mlstm_norm_w, lambda_q1, lambda_k1, lambda_q2, lambda_k2,
           diff_norm_w, rel_bias, w_branch_mlstm, w_branch_diff, w_out, ln1_g, ln1_b, w_ffn_up, ffn_conv_w,
           ffn_conv_b, w_ffn_down, ln2_g, ln2_b):
    batch, seq, d_model = x.shape
    assert d_model == D_MODEL and seq % max(PROJ_TM, FFN_TM, ATT_T, MLSTM_L) == 0
    bias_tiles = _rel_bias_tiles(rel_bias)
    h2 = x.reshape(batch * seq, d_model)
    for l in range(w_in.shape[0]):
        h2 = _layer(h2, batch, seq, l, w_in[l], b_in[l], mlstm_conv_w[l], mlstm_conv_b[l], mlstm_norm_w[l],
                    lambda_q1[l], lambda_k1[l], lambda_q2[l], lambda_k2[l], diff_norm_w[l], bias_tiles,
                    w_branch_mlstm[l], w_branch_diff[l], w_out[l], ln1_g[l], ln1_b[l], w_ffn_up[l],
                    ffn_conv_w[l], ffn_conv_b[l], w_ffn_down[l], ln2_g[l], ln2_b[l])
    return h2.reshape(batch, seq, d_model).astype(x.dtype)
```

```python
import functools
import math

import jax
import jax.numpy as jnp
from jax import lax
from jax.experimental import pallas as pl
from jax.experimental.pallas import tpu as pltpu

F32 = jnp.float32
BF16 = jnp.bfloat16

D_MODEL = 1024
MLSTM_HEADS = 4
MLSTM_HEAD_DIM = 256
MLSTM_WIDTH = MLSTM_HEADS * MLSTM_HEAD_DIM
MLSTM_CONV = 4
DIFF_HEADS = 8
DIFF_HEAD_DIM = 64
DIFF_V_DIM = 2 * DIFF_HEAD_DIM
DIFF_WIDTH = DIFF_HEADS * DIFF_V_DIM
REL_BUCKETS = 32
REL_MAX_DIST = 128
D_FF = 2816
FFN_CONV = 3
DEPTH = 1
DEEPNORM_ALPHA = (2.0 * DEPTH) ** 0.25
LN_EPS = 1e-5
LOG2E = math.log2(math.e)

COL_GM = 0
COL_GD = 1024
COL_MK = 2048
COL_QK = 3072
N_MAIN = 5120
ROW_MV = 0
ROW_MO = 1024
ROW_MQ = 2048
ROW_DV = 3072
N_TRANS = 4096
TAIL_DQ = 0
TAIL_DK = 1024
TAIL_DV = 2048
TAIL_GM = 3072
N_TAIL = 5120
GATE_PAD = 128
GATE_ROWS = 16
MLSTM_AUG = 16

HALO = 8
LANES = 128
CONV_PARAMS = 8
PROJ_TM = 512
PROJ_TN = 512
MLSTM_L = 256
ATT_T = 256
ATT_FAR_TILES = 1
ATT_SUM_ROWS = 16
INTERLEAVE_EVERY = 8
W_IN_CAST_ROWS = 512
CAST_STEPS = 4
MERGE_ROWS = 256
FFN_TM = 512
FFN_TC = 256
VMEM_LIMIT = 58 * 1024 * 1024


def _sigmoid(v):
    return 1.0 / (1.0 + jnp.exp(-v))


def _log_sigmoid(v):
    return jnp.minimum(v, 0.0) - jnp.log(1.0 + jnp.exp(-jnp.abs(v)))


def _split3(v):
    hi = v.astype(BF16)
    rest = v - hi.astype(F32)
    mid = rest.astype(BF16)
    lo = (rest - mid.astype(F32)).astype(BF16)
    return hi, mid, lo


def _rel_bias_kernel(table_ref, out_ref):
    h = pl.program_id(0)
    T = out_ref.shape[-1]
    max_exact = REL_BUCKETS // 2
    far = table_ref[REL_BUCKETS - 1, h]
    n = lax.broadcasted_iota(jnp.int32, (8, T), 1)
    nf = jnp.maximum(n, 1).astype(F32)
    large = max_exact + (jnp.log(nf / max_exact) / math.log(REL_MAX_DIST / max_exact)
                         * (REL_BUCKETS - max_exact)).astype(jnp.int32)
    large = jnp.minimum(large, REL_BUCKETS - 1)
    bucket = jnp.where(n < max_exact, n, large)
    by_dist = jnp.zeros((8, T), F32)
    for kk in range(REL_BUCKETS):
        by_dist = jnp.where(bucket == kk, table_ref[kk, h], by_dist)
    by_dist = (by_dist - far) * LOG2E
    rolled = pltpu.roll(jnp.broadcast_to(by_dist[0:1, :], (T, T)), 0, 1, stride=1, stride_axis=0)
    kpos = lax.broadcasted_iota(jnp.int32, (T, T), 0)
    qpos = lax.broadcasted_iota(jnp.int32, (T, T), 1)
    out_ref[0, 0] = jnp.where(qpos < kpos, rolled, 0.0)
    out_ref[0, 1] = jnp.where(qpos >= kpos, rolled, -jnp.inf)


def _rel_bias_tiles(rel_bias):
    T = ATT_T
    return pl.pallas_call(
        _rel_bias_kernel,
        grid=(DIFF_HEADS,),
        in_specs=[pl.BlockSpec(memory_space=pltpu.SMEM)],
        out_specs=pl.BlockSpec((1, 2, T, T), lambda h: (h, 0, 0, 0)),
        out_shape=jax.ShapeDtypeStruct((DIFF_HEADS, 2, T, T), F32),
        name="rel_bias_tiles",
    )(rel_bias.astype(F32))


def _in_proj_kernel(tiles_per_seq, x_ref, wa_ref, cp_ref, wb_ref, bb_ref, wt_ref, bt_ref, wg_ref, bg_ref,
                    bgt_ref, o_ref, ot_ref, gcol_ref, grow_ref, halo_ref):
    i = pl.program_id(0)
    tm = x_ref.shape[0]
    tn = PROJ_TN
    n_conv_cols = 2 * MLSTM_WIDTH
    nt_dims = (((1,), (1,)), ((), ()))

    @pl.when(i % tiles_per_seq == 0)
    def _():
        halo_ref[...] = jnp.zeros_like(halo_ref)

    xb = x_ref[...].astype(BF16)
    gcol = lax.dot_general(xb, wg_ref[...], nt_dims, preferred_element_type=F32) + bg_ref[...]
    grow = lax.dot_general(wg_ref[0:GATE_ROWS, :], xb, nt_dims,
                           preferred_element_type=F32) + bgt_ref[...]
    L = MLSTM_L
    H = MLSTM_HEADS
    tri = jnp.where(lax.broadcasted_iota(jnp.int32, (L, L), 0) >= lax.broadcasted_iota(jnp.int32, (L, L), 1),
                    1.0, 0.0).astype(BF16)
    gate_row = lax.broadcasted_iota(jnp.int32, (GATE_ROWS, L), 0)
    for r0 in range(0, tm, L):
        g_c = gcol[r0:r0 + L, :]
        g_r = grow[:, r0:r0 + L]
        bc3 = jnp.dot(tri, jnp.concatenate(_split3(_log_sigmoid(g_c)), axis=1), preferred_element_type=F32)
        bcol = bc3[:, 0:GATE_PAD] + bc3[:, GATE_PAD:2 * GATE_PAD] + bc3[:, 2 * GATE_PAD:3 * GATE_PAD]
        br3 = lax.dot_general(jnp.concatenate(_split3(_log_sigmoid(g_r)), axis=0), tri, nt_dims,
                              preferred_element_type=F32)
        brow = br3[0:GATE_ROWS] + br3[GATE_ROWS:2 * GATE_ROWS] + br3[2 * GATE_ROWS:3 * GATE_ROWS]
        gcol_ref[r0:r0 + L, :] = g_c - pltpu.roll(bcol, GATE_PAD - H, 1)
        grow_ref[:, r0:r0 + L] = jnp.where(gate_row < H, g_r, brow)

    n_vo = 2 * MLSTM_WIDTH
    for r0 in list(range(ROW_MV, ROW_MV + n_vo, tn)) + list(range(ROW_DV, ROW_DV + DIFF_WIDTH, tn)):
        if r0 < ROW_DV:
            src = slice(r0 - ROW_MV, r0 - ROW_MV + tn)
            w_rows, bias_col = wt_ref[src, :], bt_ref[src, :]
        else:
            w_rows = wb_ref[TAIL_DV + r0 - ROW_DV:TAIL_DV + r0 - ROW_DV + tn, :]
            bias_col = bt_ref[n_vo + r0 - ROW_DV:n_vo + r0 - ROW_DV + tn, :]
        acc_t = lax.dot_general(w_rows, xb, nt_dims, preferred_element_type=F32) + bias_col
        if ROW_MO <= r0 < ROW_MO + MLSTM_WIDTH:
            acc_t = _sigmoid(acc_t)
        ot_ref[r0:r0 + tn, :] = acc_t.astype(BF16)

    lane = lax.broadcasted_iota(jnp.int32, (tn, LANES), 1)
    for c0 in range(0, n_conv_cols, tn):
        cols = slice(c0, c0 + tn)
        par = cp_ref[cols, :]
        acc_t = lax.dot_general(wa_ref[cols, :], xb, nt_dims, preferred_element_type=F32) \
            + par[:, MLSTM_CONV + 1:MLSTM_CONV + 2]
        prev = halo_ref[cols, :]
        y = par[:, MLSTM_CONV:MLSTM_CONV + 1] + par[:, MLSTM_CONV - 1:MLSTM_CONV] * acc_t
        for kk in range(MLSTM_CONV - 1):
            shift = MLSTM_CONV - 1 - kk
            rolled = pltpu.roll(acc_t, shift, 1)
            head = jnp.where(lane < shift, pltpu.roll(prev, shift, 1), rolled[:, 0:LANES])
            y = y + par[:, kk:kk + 1] * jnp.concatenate([head, rolled[:, LANES:]], axis=1)
        halo_ref[cols, :] = acc_t[:, tm - LANES:tm]
        y = y * _sigmoid(y)
        if c0 < MLSTM_WIDTH:
            ot_ref[ROW_MQ + c0:ROW_MQ + c0 + tn, :] = y.astype(BF16)
        else:
            k0 = COL_MK + c0 - MLSTM_WIDTH
            o_ref[:, k0:k0 + tn] = (y * (MLSTM_HEAD_DIM ** -0.5)).T.astype(BF16)

    for g0 in range(0, 2 * D_MODEL, tn):
        wrows = slice(TAIL_GM + g0, TAIL_GM + g0 + tn)
        acc = lax.dot_general(xb, wb_ref[wrows, :], nt_dims, preferred_element_type=F32) + bb_ref[:, wrows]
        o_ref[:, COL_GM + g0:COL_GM + g0 + tn] = _sigmoid(acc).astype(BF16)

    dv = DIFF_V_DIM
    heads_per_chunk = tn // (2 * dv)
    q_scale = jnp.full((1, dv), DIFF_HEAD_DIM ** -0.5 * LOG2E, F32)
    scale_row = jnp.concatenate([q_scale, jnp.ones((1, dv), F32)] * heads_per_chunk, axis=1)
    for h0 in range(0, DIFF_HEADS, heads_per_chunk):
        starts = []
        for h in range(h0, h0 + heads_per_chunk):
            starts += [TAIL_DQ + h * dv, TAIL_DK + h * dv]
        w_rows = jnp.concatenate([wb_ref[r:r + dv, :] for r in starts], axis=0)
        b_row = jnp.concatenate([bb_ref[:, r:r + dv] for r in starts], axis=1)
        acc = lax.dot_general(xb, w_rows, nt_dims, preferred_element_type=F32) + b_row
        c0 = COL_QK + h0 * 2 * dv
        o_ref[:, c0:c0 + tn] = (acc * scale_row).astype(BF16)


def _cast_w_in_kernel(head_ref, tail_ref, head_out, tail_out):
    head_out[...] = head_ref[...].astype(BF16)
    tail_out[...] = tail_ref[...].astype(BF16)


def _cast_w_in(w_in_t, tail_start):
    k = w_in_t.shape[1]
    blk = W_IN_CAST_ROWS
    n_steps = N_TAIL // blk
    n_head_blocks = -(-(tail_start - 2 * MLSTM_HEADS + GATE_PAD) // blk)
    assert N_TAIL % blk == 0 and n_head_blocks <= n_steps and tail_start % 8 == 0
    head_idx = lambda j: (jnp.minimum(j, n_head_blocks - 1), 0)
    return pl.pallas_call(
        _cast_w_in_kernel,
        grid=(n_steps,),
        in_specs=[
            pl.BlockSpec((blk, k), head_idx),
            pl.BlockSpec((pl.Element(blk), pl.Element(k)), lambda j: (pl.multiple_of(tail_start + j * blk, 8), 0)),
        ],
        out_specs=[
            pl.BlockSpec((blk, k), head_idx),
            pl.BlockSpec((blk, k), lambda j: (j, 0)),
        ],
        out_shape=[
            jax.ShapeDtypeStruct((n_head_blocks * blk, k), BF16),
            jax.ShapeDtypeStruct((N_TAIL, k), BF16),
        ],
        name="cast_w_in",
    )(w_in_t, w_in_t)


def _const_spec(shape):
    return pl.BlockSpec(shape, lambda i: (0, 0), pipeline_mode=pl.Buffered(1))


def _in_proj(x2, w_all, conv_params, w_b, b_b, b_t, b_g, b_gt, seq):
    T = x2.shape[0]
    tm = PROJ_TM
    n_conv_cols = 2 * MLSTM_WIDTH
    n_vo = 2 * MLSTM_WIDTH
    assert n_vo == n_conv_cols and (n_conv_cols + n_vo) % GATE_PAD == 0
    const = _const_spec
    row_block = lambda shape, idx: pl.BlockSpec(shape, lambda i: (idx, 0), pipeline_mode=pl.Buffered(1))
    return pl.pallas_call(
        functools.partial(_in_proj_kernel, seq // tm),
        grid=(T // tm,),
        in_specs=[
            pl.BlockSpec((tm, D_MODEL), lambda i: (i, 0)),
            row_block((n_conv_cols, D_MODEL), 0),
            const((n_conv_cols, CONV_PARAMS)),
            const((N_TAIL, D_MODEL)),
            const((1, N_TAIL)),
            row_block((n_vo, D_MODEL), 1),
            const((n_vo + DIFF_WIDTH, 1)),
            row_block((GATE_PAD, D_MODEL), (n_conv_cols + n_vo) // GATE_PAD),
            const((1, GATE_PAD)),
            const((GATE_ROWS, 1)),
        ],
        out_specs=[
            pl.BlockSpec((tm, N_MAIN), lambda i: (i, 0)),
            pl.BlockSpec((N_TRANS, tm), lambda i: (0, i)),
            pl.BlockSpec((tm, GATE_PAD), lambda i: (i, 0)),
            pl.BlockSpec((GATE_ROWS, tm), lambda i: (0, i)),
        ],
        out_shape=[
            jax.ShapeDtypeStruct((T, N_MAIN), BF16),
            jax.ShapeDtypeStruct((N_TRANS, T), BF16),
            jax.ShapeDtypeStruct((T, GATE_PAD), F32),
            jax.ShapeDtypeStruct((GATE_ROWS, T), F32),
        ],
        scratch_shapes=[
            pltpu.VMEM((n_conv_cols, LANES), F32),
        ],
        compiler_params=pltpu.CompilerParams(
            dimension_semantics=("arbitrary",), vmem_limit_bytes=VMEM_LIMIT),
        name="in_proj",
    )(x2, w_all, conv_params, w_b, b_b, w_all, b_t, w_all, b_g, b_gt)


def _mlstm_chunk(c, qt_ref, k_ref, vt_ref, ot_ref, gcol_ref, grow_ref, nw_ref, out_ref, ct_ref, m_ref, nwb_ref):
    L = k_ref.shape[0]
    d = MLSTM_HEAD_DIM
    H = MLSTM_HEADS
    nt_dims = (((1,), (1,)), ((), ()))

    @pl.when(c == 0)
    def _():
        ct_ref[...] = jnp.zeros_like(ct_ref)
        m_ref[...] = jnp.zeros_like(m_ref)
        nwb_ref[...] = jnp.broadcast_to(nw_ref[...], nwb_ref.shape)

    key = lax.broadcasted_iota(jnp.int32, (L, L), 0)
    qry = lax.broadcasted_iota(jnp.int32, (L, L), 1)
    causal_t = key <= qry

    gdiff = gcol_ref[...]
    grow = grow_ref[...]
    ones = jnp.ones((MLSTM_AUG, L), BF16)

    def head(h):
        cols = slice(h * d, (h + 1) * d)
        q_t = qt_ref[cols, :]
        k = k_ref[:, cols]
        v_aug = jnp.concatenate([vt_ref[cols, :], ones], axis=0)
        b_r = grow[H + h:H + h + 1, :]
        i_r = grow[h:h + 1, :]
        g_c = gdiff[:, h:h + 1]
        m_prev = m_ref[h]
        ct = ct_ref[h]

        dmat = jnp.where(causal_t, b_r + g_c, -jnp.inf)
        a = b_r + m_prev
        m_row = jnp.maximum(a, jnp.max(dmat, axis=0, keepdims=True))
        w = jnp.exp(dmat - m_row)
        inter = jnp.exp(a - m_row)
        kq = jnp.dot(k, q_t, preferred_element_type=F32)
        sqk = (kq * w).astype(BF16)
        nd = inter * jnp.dot(ct.astype(BF16), q_t, preferred_element_type=F32) \
            + jnp.dot(v_aug, sqk, preferred_element_type=F32)
        rinv = 1.0 / jnp.maximum(jnp.abs(nd[d:d + 1, :]), jnp.exp(-m_row))
        hid = nd[0:d, :] * rinv * ot_ref[cols, :].astype(F32)
        mu = jnp.mean(hid, axis=0, keepdims=True)
        cen = hid - mu
        var = jnp.mean(cen * cen, axis=0, keepdims=True)
        out_ref[:, cols] = (cen * lax.rsqrt(var + LN_EPS) * nwb_ref[cols, :]).T.astype(BF16)

        b_last = b_r[:, L - 1:L]
        g_r = b_last - b_r + i_r
        m_new = jnp.maximum(b_last + m_prev, jnp.max(g_r, axis=-1, keepdims=True))
        decay = jnp.exp(b_last + m_prev - m_new)
        ws = jnp.exp(g_r - m_new)
        vw = (v_aug.astype(F32) * ws).astype(BF16)
        ct_ref[h] = decay * ct + jnp.dot(vw, k, preferred_element_type=F32)
        m_ref[h] = m_new

    return [functools.partial(head, h) for h in range(H)]


def _diff_attn_head(lam_init, interleaved, q_ref, k_ref, v_ref, bias_ref, lq1_ref, lk1_ref, lq2_ref, lk2_ref,
                    nw_ref, out_ref, qs_ref, vt_ref, s_ref, p_ref, acc_ref):
    T = ATT_T
    dh = DIFF_HEAD_DIM
    dv = DIFF_V_DIM
    nq = q_ref.shape[0] // T
    R = 2 * T

    vt_ref[0:dv, :] = v_ref[...]
    vt_ref[dv:dv + ATT_SUM_ROWS, :] = jnp.ones((ATT_SUM_ROWS, v_ref.shape[1]), BF16)

    lane = lax.broadcasted_iota(jnp.int32, (T, dv), 1)
    for ii in range(nq):
        qt = q_ref[ii * T:(ii + 1) * T, :]
        zero = jnp.zeros_like(qt)
        qs_ref[ii, 0:T, :] = jnp.where(lane < dh, qt, zero)
        qs_ref[ii, T:R, :] = jnp.where(lane >= dh, qt, zero)

    lam = (jnp.exp(jnp.sum(lq1_ref[...] * lk1_ref[...], axis=-1, keepdims=True))
           - jnp.exp(jnp.sum(lq2_ref[...] * lk2_ref[...], axis=-1, keepdims=True)) + lam_init)

    pairs = []
    for i in range(nq):
        k0 = 0
        while k0 < (i - 1) * T:
            klen = min(ATT_FAR_TILES * T, (i - 1) * T - k0)
            pairs.append((i, k0, klen, None))
            k0 += klen
        if i >= 1:
            pairs.append((i, (i - 1) * T, T, 0))
        pairs.append((i, i * T, T, 1))

    def scores(n):
        i, k0, klen, bias_idx = pairs[n]
        s = lax.dot_general(k_ref[k0:k0 + klen, :], qs_ref[i], (((1,), (1,)), ((), ())),
                            preferred_element_type=F32)
        if bias_idx is not None:
            bias = bias_ref[0, bias_idx]
            s = s + jnp.concatenate([bias, bias], axis=1)
        s_ref[n % 2, 0:klen, :] = s
        return jnp.max(s, axis=0, keepdims=True)

    def softmax(n, m_tile, m_prev):
        i, k0, klen, _ = pairs[n]
        m_new = m_tile if k0 == 0 else jnp.maximum(m_prev, m_tile)
        p_ref[n % 2, 0:klen, :] = jnp.exp2(s_ref[n % 2, 0:klen, :] - m_new).astype(BF16)
        alpha = None if k0 == 0 else jnp.exp2(m_prev - m_new)
        return m_new, alpha

    def values(n, alpha):
        i, k0, klen, _ = pairs[n]
        pv = jnp.dot(vt_ref[:, k0:k0 + klen], p_ref[n % 2, 0:klen, :],
                     preferred_element_type=F32)
        if k0 == 0:
            acc_ref[...] = pv
        else:
            acc_ref[...] = alpha * acc_ref[...] + pv
        if k0 == i * T:
            acc = acc_ref[...]
            out = acc[0:dv, :] / acc[dv:dv + 1, :]
            hd = out[:, 0:T] - lam * out[:, T:R]
            hd = hd * lax.rsqrt(jnp.mean(hd * hd, axis=0, keepdims=True) + LN_EPS) * nw_ref[...]
            out_ref[i * T:(i + 1) * T, :] = (hd * (1.0 - lam_init)).T.astype(BF16)

    spacing = INTERLEAVE_EVERY if interleaved else 0
    m_tiles = {0: scores(0)}
    m_run = None
    alphas = {}
    for n in range(len(pairs)):
        if n + 1 < len(pairs):
            m_tiles[n + 1] = scores(n + 1)
        m_run, alphas[n] = softmax(n, m_tiles.pop(n), m_run)
        if n >= 1:
            values(n - 1, alphas.pop(n - 1))
        if spacing and (n + 1) % spacing == 0 and (n + 1) // spacing <= len(interleaved):
            interleaved[(n + 1) // spacing - 1]()
    values(len(pairs) - 1, alphas.pop(len(pairs) - 1))


def _mixers_kernel(lam_init, chunks_per_seq, n_attn_refs, n_cast, *refs):
    (qk_ref, v_ref, bias_ref, lq1_ref, lk1_ref, lq2_ref, lk2_ref, anw_ref) = refs[:n_attn_refs]
    q_ref, k_ref = qk_ref.at[:, 0:DIFF_V_DIM], qk_ref.at[:, DIFF_V_DIM:2 * DIFF_V_DIM]
    (voq_ref, mk_ref, gcol_ref, grow_ref, mnw_ref) = refs[n_attn_refs:n_attn_refs + 5]
    n_in = n_attn_refs + 5 + n_cast
    cast_in = refs[n_attn_refs + 5:n_in]
    hd_ref, hm_ref = refs[n_in:n_in + 2]
    cast_out = refs[n_in + 2:n_in + 2 + n_cast]
    (qs_ref, vt_ref, s_ref, p_ref, acc_ref, ct_ref, m_ref, nwb_ref) = refs[n_in + 2 + n_cast:]
    W = MLSTM_WIDTH
    mvt_ref, mot_ref, mq_ref = (voq_ref.at[r:r + W] for r in (ROW_MV, ROW_MO, ROW_MQ))
    c = pl.program_id(0) % chunks_per_seq
    heads = _mlstm_chunk(c, mq_ref, mk_ref, mvt_ref, mot_ref, gcol_ref, grow_ref, mnw_ref, hm_ref,
                         ct_ref, m_ref, nwb_ref)
    _diff_attn_head(lam_init, heads, q_ref, k_ref, v_ref, bias_ref, lq1_ref, lk1_ref, lq2_ref, lk2_ref,
                    anw_ref, hd_ref, qs_ref, vt_ref, s_ref, p_ref, acc_ref)

    @pl.when(pl.program_id(0) < CAST_STEPS)
    def _():
        for w_ref, o_ref in zip(cast_in, cast_out):
            o_ref[...] = w_ref[...].astype(BF16)


def _mixers(proj, proj_t, gcol, grow, bias_tiles, lq1, lk1, lq2, lk2, dnorm_w, mnorm_w, lam_init, batch, seq,
            next_weights):
    T = ATT_T
    nq = seq // T
    dv = DIFF_V_DIM
    L = MLSTM_L
    nc = seq // L
    W = MLSTM_WIDTH
    assert DIFF_HEADS * batch == batch * nc, "attention (head, batch) steps must pair 1:1 with mLSTM chunks"
    small = lambda shape: pl.BlockSpec(shape, lambda n: (0, 0))
    attn_specs = [
        pl.BlockSpec((seq, 2 * dv), lambda n: (n % batch, COL_QK // (2 * dv) + n // batch)),
        pl.BlockSpec((dv, seq), lambda n: (ROW_DV // dv + n // batch, n % batch)),
        pl.BlockSpec((1, 2, T, T), lambda n: (n // batch, 0, 0, 0)),
        small((1, DIFF_HEAD_DIM)), small((1, DIFF_HEAD_DIM)),
        small((1, DIFF_HEAD_DIM)), small((1, DIFF_HEAD_DIM)),
        small((dv, 1)),
    ]
    mlstm_specs = [
        pl.BlockSpec((ROW_DV, L), lambda n: (0, n)),
        pl.BlockSpec((L, W), lambda n: (n, COL_MK // W)),
        pl.BlockSpec((L, GATE_PAD), lambda n: (n, 0)),
        pl.BlockSpec((GATE_ROWS, L), lambda n: (0, n)),
        small((W, 1)),
    ]
    def cast_specs():
        return [pl.BlockSpec((w.shape[0] // CAST_STEPS, w.shape[1]),
                             lambda n: (jnp.minimum(n, CAST_STEPS - 1), 0)) for w in next_weights]

    assert all(w.shape[0] % (16 * CAST_STEPS) == 0 for w in next_weights)
    return pl.pallas_call(
        functools.partial(_mixers_kernel, lam_init, nc, len(attn_specs), len(next_weights)),
        grid=(batch * nc,),
        in_specs=attn_specs + mlstm_specs + cast_specs(),
        out_specs=[
            pl.BlockSpec((seq, dv), lambda n: (n % batch, n // batch)),
            pl.BlockSpec((L, W), lambda n: (n, 0)),
        ] + cast_specs(),
        out_shape=[
            jax.ShapeDtypeStruct((batch * seq, DIFF_WIDTH), BF16),
            jax.ShapeDtypeStruct((batch * seq, W), BF16),
        ] + [jax.ShapeDtypeStruct(w.shape, BF16) for w in next_weights],
        scratch_shapes=[
            pltpu.VMEM((nq, 2 * T, dv), BF16),
            pltpu.VMEM((dv + ATT_SUM_ROWS, seq), BF16),
            pltpu.VMEM((2, ATT_FAR_TILES * T, 2 * T), F32),
            pltpu.VMEM((2, ATT_FAR_TILES * T, 2 * T), BF16),
            pltpu.VMEM((dv + ATT_SUM_ROWS, 2 * T), F32),
            pltpu.VMEM((MLSTM_HEADS, MLSTM_HEAD_DIM + MLSTM_AUG, MLSTM_HEAD_DIM), F32),
            pltpu.VMEM((MLSTM_HEADS, 1, 1), F32),
            pltpu.VMEM((W, L), F32),
        ],
        compiler_params=pltpu.CompilerParams(
            dimension_semantics=("arbitrary",), vmem_limit_bytes=VMEM_LIMIT),
        name="token_mixers",
    )(proj, proj_t, bias_tiles, lq1, lk1, lq2, lk2, dnorm_w, proj_t, proj, gcol, grow, mnorm_w, *next_weights)


def _layer_norm(y, g, b):
    mu = jnp.mean(y, axis=-1, keepdims=True)
    cen = y - mu
    var = jnp.mean(cen * cen, axis=-1, keepdims=True)
    return cen * lax.rsqrt(var + LN_EPS) * g + b


def _merge_kernel(hm_ref, hd_ref, gates_ref, x_ref, wbm_ref, wbd_ref, wo_ref, g_ref, b_ref, out_ref):
    tm = hm_ref.shape[0]
    gm_ref = gates_ref.at[:, 0:D_MODEL]
    gd_ref = gates_ref.at[:, D_MODEL:2 * D_MODEL]
    nb = tm // MERGE_ROWS
    halves = [slice(r * MERGE_ROWS, (r + 1) * MERGE_ROWS) for r in range(nb)]
    merged = []
    for rows in halves:
        pm = jnp.dot(hm_ref[rows, :], wbm_ref[...], preferred_element_type=F32)
        pd = jnp.dot(hd_ref[rows, :], wbd_ref[...], preferred_element_type=F32)
        merged.append((gm_ref[rows, :].astype(F32) * pm + gd_ref[rows, :].astype(F32) * pd).astype(BF16))
    for rows, m in zip(halves, merged):
        mix = jnp.dot(m, wo_ref[...], preferred_element_type=F32)
        out_ref[rows, :] = _layer_norm(DEEPNORM_ALPHA * x_ref[rows, :] + mix, g_ref[...], b_ref[...])


def _ffn_kernel(tiles_per_seq, h_ref, wu_ref, cw_ref, cb_ref, wd_ref, g_ref, b_ref, out_ref,
                conv_ref, halo_ref, act_ref):
    i = pl.program_id(0)
    tm = h_ref.shape[0]
    tc = FFN_TC

    @pl.when(i % tiles_per_seq == 0)
    def _():
        halo_ref[...] = jnp.zeros_like(halo_ref)

    h = h_ref[...]
    hb = h.astype(BF16)
    for c0 in range(0, D_FF, tc):
        cols = slice(c0, c0 + tc)
        a = jnp.dot(hb, wu_ref[:, cols], preferred_element_type=F32)
        gate = jnp.dot(hb, wu_ref[:, D_FF + c0:D_FF + c0 + tc], preferred_element_type=F32)
        conv_ref[0:HALO, cols] = halo_ref[:, cols]
        conv_ref[HALO:HALO + tm, cols] = a
        y = cb_ref[:, cols] + cw_ref[FFN_CONV - 1:FFN_CONV, cols] * a
        for kk in range(FFN_CONV - 1):
            off = HALO - (FFN_CONV - 1) + kk
            y = y + cw_ref[kk:kk + 1, cols] * conv_ref[off:off + tm, cols]
        halo_ref[:, cols] = a[tm - HALO:tm, :]
        act_ref[:, cols] = (y * _sigmoid(y) * gate).astype(BF16)

    for r0 in range(0, tm, tm // 2):
        rows = slice(r0, r0 + tm // 2)
        ffn = jnp.dot(act_ref[rows, :], wd_ref[...], preferred_element_type=F32)
        out_ref[rows, :] = _layer_norm(DEEPNORM_ALPHA * h[rows, :] + ffn, g_ref[...], b_ref[...])


def _merge_ffn_kernel(tiles_per_seq, hm_ref, hd_ref, gates_ref, x_ref, wbm_ref, wbd_ref, wo_ref, g1_ref,
                      b1_ref, wu_ref, cw_ref, cb_ref, wd_ref, g2_ref, b2_ref, out_ref,
                      h_ref, conv_ref, halo_ref, act_ref):
    _merge_kernel(hm_ref, hd_ref, gates_ref, x_ref, wbm_ref, wbd_ref, wo_ref, g1_ref, b1_ref, h_ref)
    _ffn_kernel(tiles_per_seq, h_ref, wu_ref, cw_ref, cb_ref, wd_ref, g2_ref, b2_ref, out_ref,
                conv_ref, halo_ref, act_ref)


def _merge_ffn(hm, hd, proj, x2, w_bm, w_bd, w_o, ln1_g, ln1_b, w_up, conv_w, conv_b, w_down, ln2_g, ln2_b, seq):
    T = x2.shape[0]
    tm = FFN_TM
    D = D_MODEL
    rows = lambda col: pl.BlockSpec((tm, D), lambda i: (i, col))
    const = _const_spec
    return pl.pallas_call(
        functools.partial(_merge_ffn_kernel, seq // tm),
        grid=(T // tm,),
        in_specs=[
            rows(0), rows(0), pl.BlockSpec((tm, 2 * D), lambda i: (i, COL_GM // (2 * D))), rows(0),
            const((MLSTM_WIDTH, D)), const((DIFF_WIDTH, D)), const((D, D)), const((1, D)), const((1, D)),
            const((D, 2 * D_FF)), const((FFN_CONV, D_FF)), const((1, D_FF)), const((D_FF, D)),
            const((1, D)), const((1, D)),
        ],
        out_specs=rows(0),
        out_shape=jax.ShapeDtypeStruct((T, D), F32),
        scratch_shapes=[
            pltpu.VMEM((tm, D), F32),
            pltpu.VMEM((HALO + tm, D_FF), F32),
            pltpu.VMEM((HALO, D_FF), F32),
            pltpu.VMEM((tm, D_FF), BF16),
        ],
        compiler_params=pltpu.CompilerParams(
            dimension_semantics=("arbitrary",), vmem_limit_bytes=VMEM_LIMIT),
        name="merge_ffn",
    )(hm, hd, proj, x2, w_bm, w_bd, w_o, ln1_g, ln1_b, w_up, conv_w, conv_b, w_down, ln2_g, ln2_b)


def _layer(h2, batch, seq, l, w_in, b_in, mconv_w, mconv_b, mnorm_w, lq1, lk1, lq2, lk2, dnorm_w, bias_tiles,
           w_bm, w_bd, w_o, ln1_g, ln1_b, w_up, fconv_w, fconv_b, w_down, ln2_g, ln2_b):
    lam_init = 0.8 - 0.6 * math.exp(-0.3 * l)
    n_m = 4 * MLSTM_WIDTH
    n_gate = 2 * MLSTM_HEADS
    n_qk = 2 * MLSTM_WIDTH
    w_all, w_b = _cast_w_in(w_in.T.astype(F32), n_m + n_gate)
    conv_params = jnp.concatenate(
        [mconv_w.astype(F32).T, mconv_b.astype(F32)[:, None], b_in[:n_qk].astype(F32)[:, None],
         jnp.zeros((n_qk, CONV_PARAMS - MLSTM_CONV - 2), F32)], axis=1)
    b_b = b_in[n_m + n_gate:][None, :].astype(F32)
    dv0 = n_m + n_gate + TAIL_DV
    b_t = jnp.concatenate([b_in[n_qk:n_m], b_in[dv0:dv0 + DIFF_WIDTH]])[:, None].astype(F32)
    b_g = b_in[n_m:n_m + GATE_PAD][None, :].astype(F32)
    b_gt = b_in[n_m:n_m + GATE_ROWS][:, None].astype(F32)

    proj, proj_t, gcol, grow = _in_proj(h2, w_all, conv_params, w_b, b_b, b_t, b_g, b_gt, seq)
    hd, hm, w_bm16, w_bd16, w_o16, w_up16, w_down16 = _mixers(
        proj, proj_t, gcol, grow, bias_tiles, lq1[None, :].astype(F32), lk1[None, :].astype(F32),
        lq2[None, :].astype(F32), lk2[None, :].astype(F32), dnorm_w[:, None].astype(F32),
        mnorm_w[:, None].astype(F32), lam_init, batch, seq,
        [w.astype(F32) for w in (w_bm, w_bd, w_o, w_up, w_down)])
    return _merge_ffn(hm, hd, proj, h2, w_bm16, w_bd16, w_o16,
                      ln1_g[None, :].astype(F32), ln1_b[None, :].astype(F32),
                      w_up16, fconv_w.astype(F32), fconv_b[None, :].astype(F32),
                      w_down16, ln2_g[None, :].astype(F32), ln2_b[None, :].astype(F32), seq)


def kernel(x, w_in, b_in, mlstm_conv_w, mlstm_conv_b, mlstm_norm_w, lambda_q1, lambda_k1, lambda_q2, lambda_k2,
           diff_norm_w, rel_bias, w_branch_mlstm, w_branch_diff, w_out, ln1_g, ln1_b, w_ffn_up, ffn_conv_w,
           ffn_conv_b, w_ffn_down, ln2_g, ln2_b):
    batch, seq, d_model = x.shape
    assert d_model == D_MODEL and seq % max(PROJ_TM, FFN_TM, ATT_T, MLSTM_L) == 0
    bias_tiles = _rel_bias_tiles(rel_bias)
    h2 = x.reshape(batch * seq, d_model)
    for l in range(w_in.shape[0]):
        h2 = _layer(h2, batch, seq, l, w_in[l], b_in[l], mlstm_conv_w[l], mlstm_conv_b[l], mlstm_norm_w[l],
                    lambda_q1[l], lambda_k1[l], lambda_q2[l], lambda_k2[l], diff_norm_w[l], bias_tiles,
                    w_branch_mlstm[l], w_branch_diff[l], w_out[l], ln1_g[l], ln1_b[l], w_ffn_up[l],
                    ffn_conv_w[l], ffn_conv_b[l], w_ffn_down[l], ln2_g[l], ln2_b[l])
    return h2.reshape(batch, seq, d_model).astype(x.dtype)
```

```python
import functools
import math

import jax
import jax.numpy as jnp
from jax import lax
from jax.experimental import pallas as pl
from jax.experimental.pallas import tpu as pltpu

F32 = jnp.float32
BF16 = jnp.bfloat16

D_MODEL = 1024
MLSTM_HEADS = 4
MLSTM_HEAD_DIM = 256
MLSTM_WIDTH = MLSTM_HEADS * MLSTM_HEAD_DIM
MLSTM_CONV = 4
DIFF_HEADS = 8
DIFF_HEAD_DIM = 64
DIFF_V_DIM = 2 * DIFF_HEAD_DIM
DIFF_WIDTH = DIFF_HEADS * DIFF_V_DIM
REL_BUCKETS = 32
REL_MAX_DIST = 128
D_FF = 2816
FFN_CONV = 3
DEPTH = 1
DEEPNORM_ALPHA = (2.0 * DEPTH) ** 0.25
LN_EPS = 1e-5
LOG2E = math.log2(math.e)

COL_GM = 0
COL_GD = 1024
COL_MK = 2048
COL_QK = 3072
N_MAIN = 5120
ROW_MV = 0
ROW_MO = 1024
ROW_MQ = 2048
ROW_DV = 3072
N_TRANS = 4096
TAIL_DQ = 0
TAIL_DK = 1024
TAIL_DV = 2048
TAIL_GM = 3072
N_TAIL = 5120
GATE_PAD = 128
GATE_ROWS = 16
MLSTM_AUG = 16

HALO = 8
LANES = 128
CONV_PARAMS = 8
PROJ_TM = 512
PROJ_TN = 512
MLSTM_L = 256
ATT_T = 256
ATT_FAR_TILES = 1
ATT_SUM_ROWS = 16
INTERLEAVE_EVERY = 2
W_IN_CAST_ROWS = 512
CAST_STEPS = 4
MERGE_ROWS = 256
FFN_TM = 512
FFN_TC = 256
VMEM_LIMIT = 58 * 1024 * 1024


def _sigmoid(v):
    return 1.0 / (1.0 + jnp.exp(-v))


def _log_sigmoid(v):
    return jnp.minimum(v, 0.0) - jnp.log(1.0 + jnp.exp(-jnp.abs(v)))


def _split3(v):
    hi = v.astype(BF16)
    rest = v - hi.astype(F32)
    mid = rest.astype(BF16)
    lo = (rest - mid.astype(F32)).astype(BF16)
    return hi, mid, lo


def _rel_bias_kernel(table_ref, out_ref):
    h = pl.program_id(0)
    T = out_ref.shape[-1]
    max_exact = REL_BUCKETS // 2
    far = table_ref[REL_BUCKETS - 1, h]
    n = lax.broadcasted_iota(jnp.int32, (8, T), 1)
    nf = jnp.maximum(n, 1).astype(F32)
    large = max_exact + (jnp.log(nf / max_exact) / math.log(REL_MAX_DIST / max_exact)
                         * (REL_BUCKETS - max_exact)).astype(jnp.int32)
    large = jnp.minimum(large, REL_BUCKETS - 1)
    bucket = jnp.where(n < max_exact, n, large)
    by_dist = jnp.zeros((8, T), F32)
    for kk in range(REL_BUCKETS):
        by_dist = jnp.where(bucket == kk, table_ref[kk, h], by_dist)
    by_dist = (by_dist - far) * LOG2E
    rolled = pltpu.roll(jnp.broadcast_to(by_dist[0:1, :], (T, T)), 0, 1, stride=1, stride_axis=0)
    kpos = lax.broadcasted_iota(jnp.int32, (T, T), 0)
    qpos = lax.broadcasted_iota(jnp.int32, (T, T), 1)
    out_ref[0, 0] = jnp.where(qpos < kpos, rolled, 0.0)
    out_ref[0, 1] = jnp.where(qpos >= kpos, rolled, -jnp.inf)


def _rel_bias_tiles(rel_bias):
    T = ATT_T
    return pl.pallas_call(
        _rel_bias_kernel,
        grid=(DIFF_HEADS,),
        in_specs=[pl.BlockSpec(memory_space=pltpu.SMEM)],
        out_specs=pl.BlockSpec((1, 2, T, T), lambda h: (h, 0, 0, 0)),
        out_shape=jax.ShapeDtypeStruct((DIFF_HEADS, 2, T, T), F32),
        name="rel_bias_tiles",
    )(rel_bias.astype(F32))


def _in_proj_kernel(tiles_per_seq, x_ref, wa_ref, cp_ref, wb_ref, bb_ref, wt_ref, bt_ref, wg_ref, bg_ref,
                    bgt_ref, o_ref, ot_ref, gcol_ref, grow_ref, halo_ref):
    i = pl.program_id(0)
    tm = x_ref.shape[0]
    tn = PROJ_TN
    n_conv_cols = 2 * MLSTM_WIDTH
    nt_dims = (((1,), (1,)), ((), ()))

    @pl.when(i % tiles_per_seq == 0)
    def _():
        halo_ref[...] = jnp.zeros_like(halo_ref)

    xb = x_ref[...].astype(BF16)
    gcol = lax.dot_general(xb, wg_ref[...], nt_dims, preferred_element_type=F32) + bg_ref[...]
    grow = lax.dot_general(wg_ref[0:GATE_ROWS, :], xb, nt_dims,
                           preferred_element_type=F32) + bgt_ref[...]
    L = MLSTM_L
    H = MLSTM_HEADS
    tri = jnp.where(lax.broadcasted_iota(jnp.int32, (L, L), 0) >= lax.broadcasted_iota(jnp.int32, (L, L), 1),
                    1.0, 0.0).astype(BF16)
    gate_row = lax.broadcasted_iota(jnp.int32, (GATE_ROWS, L), 0)
    for r0 in range(0, tm, L):
        g_c = gcol[r0:r0 + L, :]
        g_r = grow[:, r0:r0 + L]
        bc3 = jnp.dot(tri, jnp.concatenate(_split3(_log_sigmoid(g_c)), axis=1), preferred_element_type=F32)
        bcol = bc3[:, 0:GATE_PAD] + bc3[:, GATE_PAD:2 * GATE_PAD] + bc3[:, 2 * GATE_PAD:3 * GATE_PAD]
        br3 = lax.dot_general(jnp.concatenate(_split3(_log_sigmoid(g_r)), axis=0), tri, nt_dims,
                              preferred_element_type=F32)
        brow = br3[0:GATE_ROWS] + br3[GATE_ROWS:2 * GATE_ROWS] + br3[2 * GATE_ROWS:3 * GATE_ROWS]
        gcol_ref[r0:r0 + L, :] = g_c - pltpu.roll(bcol, GATE_PAD - H, 1)
        grow_ref[:, r0:r0 + L] = jnp.where(gate_row < H, g_r, brow)

    n_vo = 2 * MLSTM_WIDTH
    for r0 in list(range(ROW_MV, ROW_MV + n_vo, tn)) + list(range(ROW_DV, ROW_DV + DIFF_WIDTH, tn)):
        if r0 < ROW_DV:
            src = slice(r0 - ROW_MV, r0 - ROW_MV + tn)
            w_rows, bias_col = wt_ref[src, :], bt_ref[src, :]
        else:
            w_rows = wb_ref[TAIL_DV + r0 - ROW_DV:TAIL_DV + r0 - ROW_DV + tn, :]
            bias_col = bt_ref[n_vo + r0 - ROW_DV:n_vo + r0 - ROW_DV + tn, :]
        acc_t = lax.dot_general(w_rows, xb, nt_dims, preferred_element_type=F32) + bias_col
        if ROW_MO <= r0 < ROW_MO + MLSTM_WIDTH:
            acc_t = _sigmoid(acc_t)
        ot_ref[r0:r0 + tn, :] = acc_t.astype(BF16)

    lane = lax.broadcasted_iota(jnp.int32, (tn, LANES), 1)
    for c0 in range(0, n_conv_cols, tn):
        cols = slice(c0, c0 + tn)
        par = cp_ref[cols, :]
        acc_t = lax.dot_general(wa_ref[cols, :], xb, nt_dims, preferred_element_type=F32) \
            + par[:, MLSTM_CONV + 1:MLSTM_CONV + 2]
        prev = halo_ref[cols, :]
        y = par[:, MLSTM_CONV:MLSTM_CONV + 1] + par[:, MLSTM_CONV - 1:MLSTM_CONV] * acc_t
        for kk in range(MLSTM_CONV - 1):
            shift = MLSTM_CONV - 1 - kk
            rolled = pltpu.roll(acc_t, shift, 1)
            head = jnp.where(lane < shift, pltpu.roll(prev, shift, 1), rolled[:, 0:LANES])
            y = y + par[:, kk:kk + 1] * jnp.concatenate([head, rolled[:, LANES:]], axis=1)
        halo_ref[cols, :] = acc_t[:, tm - LANES:tm]
        y = y * _sigmoid(y)
        if c0 < MLSTM_WIDTH:
            ot_ref[ROW_MQ + c0:ROW_MQ + c0 + tn, :] = y.astype(BF16)
        else:
            k0 = COL_MK + c0 - MLSTM_WIDTH
            o_ref[:, k0:k0 + tn] = (y * (MLSTM_HEAD_DIM ** -0.5)).T.astype(BF16)

    for g0 in range(0, 2 * D_MODEL, tn):
        wrows = slice(TAIL_GM + g0, TAIL_GM + g0 + tn)
        acc = lax.dot_general(xb, wb_ref[wrows, :], nt_dims, preferred_element_type=F32) + bb_ref[:, wrows]
        o_ref[:, COL_GM + g0:COL_GM + g0 + tn] = _sigmoid(acc).astype(BF16)

    dv = DIFF_V_DIM
    heads_per_chunk = tn // (2 * dv)
    q_scale = jnp.full((1, dv), DIFF_HEAD_DIM ** -0.5 * LOG2E, F32)
    scale_row = jnp.concatenate([q_scale, jnp.ones((1, dv), F32)] * heads_per_chunk, axis=1)
    for h0 in range(0, DIFF_HEADS, heads_per_chunk):
        starts = []
        for h in range(h0, h0 + heads_per_chunk):
            starts += [TAIL_DQ + h * dv, TAIL_DK + h * dv]
        w_rows = jnp.concatenate([wb_ref[r:r + dv, :] for r in starts], axis=0)
        b_row = jnp.concatenate([bb_ref[:, r:r + dv] for r in starts], axis=1)
        acc = lax.dot_general(xb, w_rows, nt_dims, preferred_element_type=F32) + b_row
        c0 = COL_QK + h0 * 2 * dv
        o_ref[:, c0:c0 + tn] = (acc * scale_row).astype(BF16)


def _cast_w_in_kernel(head_ref, tail_ref, head_out, tail_out):
    head_out[...] = head_ref[...].astype(BF16)
    tail_out[...] = tail_ref[...].astype(BF16)


def _cast_w_in(w_in_t, tail_start):
    k = w_in_t.shape[1]
    blk = W_IN_CAST_ROWS
    n_steps = N_TAIL // blk
    n_head_blocks = -(-(tail_start - 2 * MLSTM_HEADS + GATE_PAD) // blk)
    assert N_TAIL % blk == 0 and n_head_blocks <= n_steps and tail_start % 8 == 0
    head_idx = lambda j: (jnp.minimum(j, n_head_blocks - 1), 0)
    return pl.pallas_call(
        _cast_w_in_kernel,
        grid=(n_steps,),
        in_specs=[
            pl.BlockSpec((blk, k), head_idx),
            pl.BlockSpec((pl.Element(blk), pl.Element(k)), lambda j: (pl.multiple_of(tail_start + j * blk, 8), 0)),
        ],
        out_specs=[
            pl.BlockSpec((blk, k), head_idx),
            pl.BlockSpec((blk, k), lambda j: (j, 0)),
        ],
        out_shape=[
            jax.ShapeDtypeStruct((n_head_blocks * blk, k), BF16),
            jax.ShapeDtypeStruct((N_TAIL, k), BF16),
        ],
        name="cast_w_in",
    )(w_in_t, w_in_t)


def _const_spec(shape):
    return pl.BlockSpec(shape, lambda i: (0, 0), pipeline_mode=pl.Buffered(1))


def _in_proj(x2, w_all, conv_params, w_b, b_b, b_t, b_g, b_gt, seq):
    T = x2.shape[0]
    tm = PROJ_TM
    n_conv_cols = 2 * MLSTM_WIDTH
    n_vo = 2 * MLSTM_WIDTH
    assert n_vo == n_conv_cols and (n_conv_cols + n_vo) % GATE_PAD == 0
    const = _const_spec
    row_block = lambda shape, idx: pl.BlockSpec(shape, lambda i: (idx, 0), pipeline_mode=pl.Buffered(1))
    return pl.pallas_call(
        functools.partial(_in_proj_kernel, seq // tm),
        grid=(T // tm,),
        in_specs=[
            pl.BlockSpec((tm, D_MODEL), lambda i: (i, 0)),
            row_block((n_conv_cols, D_MODEL), 0),
            const((n_conv_cols, CONV_PARAMS)),
            const((N_TAIL, D_MODEL)),
            const((1, N_TAIL)),
            row_block((n_vo, D_MODEL), 1),
            const((n_vo + DIFF_WIDTH, 1)),
            row_block((GATE_PAD, D_MODEL), (n_conv_cols + n_vo) // GATE_PAD),
            const((1, GATE_PAD)),
            const((GATE_ROWS, 1)),
        ],
        out_specs=[
            pl.BlockSpec((tm, N_MAIN), lambda i: (i, 0)),
            pl.BlockSpec((N_TRANS, tm), lambda i: (0, i)),
            pl.BlockSpec((tm, GATE_PAD), lambda i: (i, 0)),
            pl.BlockSpec((GATE_ROWS, tm), lambda i: (0, i)),
        ],
        out_shape=[
            jax.ShapeDtypeStruct((T, N_MAIN), BF16),
            jax.ShapeDtypeStruct((N_TRANS, T), BF16),
            jax.ShapeDtypeStruct((T, GATE_PAD), F32),
            jax.ShapeDtypeStruct((GATE_ROWS, T), F32),
        ],
        scratch_shapes=[
            pltpu.VMEM((n_conv_cols, LANES), F32),
        ],
        compiler_params=pltpu.CompilerParams(
            dimension_semantics=("arbitrary",), vmem_limit_bytes=VMEM_LIMIT),
        name="in_proj",
    )(x2, w_all, conv_params, w_b, b_b, w_all, b_t, w_all, b_g, b_gt)


def _mlstm_chunk(c, qt_ref, k_ref, vt_ref, ot_ref, gcol_ref, grow_ref, nw_ref, out_ref, ct_ref, m_ref, nwb_ref):
    L = k_ref.shape[0]
    d = MLSTM_HEAD_DIM
    H = MLSTM_HEADS
    nt_dims = (((1,), (1,)), ((), ()))

    @pl.when(c == 0)
    def _():
        ct_ref[...] = jnp.zeros_like(ct_ref)
        m_ref[...] = jnp.zeros_like(m_ref)
        nwb_ref[...] = jnp.broadcast_to(nw_ref[...], nwb_ref.shape)

    key = lax.broadcasted_iota(jnp.int32, (L, L), 0)
    qry = lax.broadcasted_iota(jnp.int32, (L, L), 1)
    causal_t = key <= qry

    gdiff = gcol_ref[...]
    grow = grow_ref[...]
    ones = jnp.ones((MLSTM_AUG, L), BF16)

    def head(h):
        cols = slice(h * d, (h + 1) * d)
        q_t = qt_ref[cols, :]
        k = k_ref[:, cols]
        v_aug = jnp.concatenate([vt_ref[cols, :], ones], axis=0)
        b_r = grow[H + h:H + h + 1, :]
        i_r = grow[h:h + 1, :]
        g_c = gdiff[:, h:h + 1]
        m_prev = m_ref[h]
        ct = ct_ref[h]

        dmat = jnp.where(causal_t, b_r + g_c, -jnp.inf)
        a = b_r + m_prev
        m_row = jnp.maximum(a, jnp.max(dmat, axis=0, keepdims=True))
        w = jnp.exp(dmat - m_row)
        inter = jnp.exp(a - m_row)
        kq = jnp.dot(k, q_t, preferred_element_type=F32)
        sqk = (kq * w).astype(BF16)
        nd = inter * jnp.dot(ct.astype(BF16), q_t, preferred_element_type=F32) \
            + jnp.dot(v_aug, sqk, preferred_element_type=F32)
        rinv = 1.0 / jnp.maximum(jnp.abs(nd[d:d + 1, :]), jnp.exp(-m_row))
        hid = nd[0:d, :] * rinv * ot_ref[cols, :].astype(F32)
        mu = jnp.mean(hid, axis=0, keepdims=True)
        cen = hid - mu
        var = jnp.mean(cen * cen, axis=0, keepdims=True)
        out_ref[:, cols] = (cen * lax.rsqrt(var + LN_EPS) * nwb_ref[cols, :]).T.astype(BF16)

        b_last = b_r[:, L - 1:L]
        g_r = b_last - b_r + i_r
        m_new = jnp.maximum(b_last + m_prev, jnp.max(g_r, axis=-1, keepdims=True))
        decay = jnp.exp(b_last + m_prev - m_new)
        ws = jnp.exp(g_r - m_new)
        vw = (v_aug.astype(F32) * ws).astype(BF16)
        ct_ref[h] = decay * ct + jnp.dot(vw, k, preferred_element_type=F32)
        m_ref[h] = m_new

    return [functools.partial(head, h) for h in range(H)]


def _diff_attn_head(lam_init, interleaved, q_ref, k_ref, v_ref, bias_ref, lq1_ref, lk1_ref, lq2_ref, lk2_ref,
                    nw_ref, out_ref, qs_ref, vt_ref, s_ref, p_ref, acc_ref):
    T = ATT_T
    dh = DIFF_HEAD_DIM
    dv = DIFF_V_DIM
    nq = q_ref.shape[0] // T
    R = 2 * T

    vt_ref[0:dv, :] = v_ref[...]
    vt_ref[dv:dv + ATT_SUM_ROWS, :] = jnp.ones((ATT_SUM_ROWS, v_ref.shape[1]), BF16)

    lane = lax.broadcasted_iota(jnp.int32, (T, dv), 1)
    for ii in range(nq):
        qt = q_ref[ii * T:(ii + 1) * T, :]
        zero = jnp.zeros_like(qt)
        qs_ref[ii, 0:T, :] = jnp.where(lane < dh, qt, zero)
        qs_ref[ii, T:R, :] = jnp.where(lane >= dh, qt, zero)

    lam = (jnp.exp(jnp.sum(lq1_ref[...] * lk1_ref[...], axis=-1, keepdims=True))
           - jnp.exp(jnp.sum(lq2_ref[...] * lk2_ref[...], axis=-1, keepdims=True)) + lam_init)

    pairs = []
    for i in range(nq):
        k0 = 0
        while k0 < (i - 1) * T:
            klen = min(ATT_FAR_TILES * T, (i - 1) * T - k0)
            pairs.append((i, k0, klen, None))
            k0 += klen
        if i >= 1:
            pairs.append((i, (i - 1) * T, T, 0))
        pairs.append((i, i * T, T, 1))

    def scores(n):
        i, k0, klen, bias_idx = pairs[n]
        s = lax.dot_general(k_ref[k0:k0 + klen, :], qs_ref[i], (((1,), (1,)), ((), ())),
                            preferred_element_type=F32)
        if bias_idx is not None:
            bias = bias_ref[0, bias_idx]
            s = s + jnp.concatenate([bias, bias], axis=1)
        s_ref[n % 2, 0:klen, :] = s
        return jnp.max(s, axis=0, keepdims=True)

    def softmax(n, m_tile, m_prev):
        i, k0, klen, _ = pairs[n]
        m_new = m_tile if k0 == 0 else jnp.maximum(m_prev, m_tile)
        p_ref[n % 2, 0:klen, :] = jnp.exp2(s_ref[n % 2, 0:klen, :] - m_new).astype(BF16)
        alpha = None if k0 == 0 else jnp.exp2(m_prev - m_new)
        return m_new, alpha

    def values(n, alpha):
        i, k0, klen, _ = pairs[n]
        pv = jnp.dot(vt_ref[:, k0:k0 + klen], p_ref[n % 2, 0:klen, :],
                     preferred_element_type=F32)
        if k0 == 0:
            acc_ref[...] = pv
        else:
            acc_ref[...] = alpha * acc_ref[...] + pv
        if k0 == i * T:
            acc = acc_ref[...]
            out = acc[0:dv, :] / acc[dv:dv + 1, :]
            hd = out[:, 0:T] - lam * out[:, T:R]
            hd = hd * lax.rsqrt(jnp.mean(hd * hd, axis=0, keepdims=True) + LN_EPS) * nw_ref[...]
            out_ref[i * T:(i + 1) * T, :] = (hd * (1.0 - lam_init)).T.astype(BF16)

    spacing = INTERLEAVE_EVERY if interleaved else 0
    m_tiles = {0: scores(0)}
    m_run = None
    alphas = {}
    for n in range(len(pairs)):
        if n + 1 < len(pairs):
            m_tiles[n + 1] = scores(n + 1)
        m_run, alphas[n] = softmax(n, m_tiles.pop(n), m_run)
        if n >= 1:
            values(n - 1, alphas.pop(n - 1))
        if spacing and (n + 1) % spacing == 0 and (n + 1) // spacing <= len(interleaved):
            interleaved[(n + 1) // spacing - 1]()
    values(len(pairs) - 1, alphas.pop(len(pairs) - 1))


def _mixers_kernel(lam_init, chunks_per_seq, n_attn_refs, n_cast, *refs):
    (qk_ref, v_ref, bias_ref, lq1_ref, lk1_ref, lq2_ref, lk2_ref, anw_ref) = refs[:n_attn_refs]
    q_ref, k_ref = qk_ref.at[:, 0:DIFF_V_DIM], qk_ref.at[:, DIFF_V_DIM:2 * DIFF_V_DIM]
    (voq_ref, mk_ref, gcol_ref, grow_ref, mnw_ref) = refs[n_attn_refs:n_attn_refs + 5]
    n_in = n_attn_refs + 5 + n_cast
    cast_in = refs[n_attn_refs + 5:n_in]
    hd_ref, hm_ref = refs[n_in:n_in + 2]
    cast_out = refs[n_in + 2:n_in + 2 + n_cast]
    (qs_ref, vt_ref, s_ref, p_ref, acc_ref, ct_ref, m_ref, nwb_ref) = refs[n_in + 2 + n_cast:]
    W = MLSTM_WIDTH
    mvt_ref, mot_ref, mq_ref = (voq_ref.at[r:r + W] for r in (ROW_MV, ROW_MO, ROW_MQ))
    c = pl.program_id(0) % chunks_per_seq
    heads = _mlstm_chunk(c, mq_ref, mk_ref, mvt_ref, mot_ref, gcol_ref, grow_ref, mnw_ref, hm_ref,
                         ct_ref, m_ref, nwb_ref)
    _diff_attn_head(lam_init, heads, q_ref, k_ref, v_ref, bias_ref, lq1_ref, lk1_ref, lq2_ref, lk2_ref,
                    anw_ref, hd_ref, qs_ref, vt_ref, s_ref, p_ref, acc_ref)

    @pl.when(pl.program_id(0) < CAST_STEPS)
    def _():
        for w_ref, o_ref in zip(cast_in, cast_out):
            o_ref[...] = w_ref[...].astype(BF16)


def _mixers(proj, proj_t, gcol, grow, bias_tiles, lq1, lk1, lq2, lk2, dnorm_w, mnorm_w, lam_init, batch, seq,
            next_weights):
    T = ATT_T
    nq = seq // T
    dv = DIFF_V_DIM
    L = MLSTM_L
    nc = seq // L
    W = MLSTM_WIDTH
    assert DIFF_HEADS * batch == batch * nc, "attention (head, batch) steps must pair 1:1 with mLSTM chunks"
    small = lambda shape: pl.BlockSpec(shape, lambda n: (0, 0))
    attn_specs = [
        pl.BlockSpec((seq, 2 * dv), lambda n: (n % batch, COL_QK // (2 * dv) + n // batch)),
        pl.BlockSpec((dv, seq), lambda n: (ROW_DV // dv + n // batch, n % batch)),
        pl.BlockSpec((1, 2, T, T), lambda n: (n // batch, 0, 0, 0)),
        small((1, DIFF_HEAD_DIM)), small((1, DIFF_HEAD_DIM)),
        small((1, DIFF_HEAD_DIM)), small((1, DIFF_HEAD_DIM)),
        small((dv, 1)),
    ]
    mlstm_specs = [
        pl.BlockSpec((ROW_DV, L), lambda n: (0, n)),
        pl.BlockSpec((L, W), lambda n: (n, COL_MK // W)),
        pl.BlockSpec((L, GATE_PAD), lambda n: (n, 0)),
        pl.BlockSpec((GATE_ROWS, L), lambda n: (0, n)),
        small((W, 1)),
    ]
    def cast_specs():
        return [pl.BlockSpec((w.shape[0] // CAST_STEPS, w.shape[1]),
                             lambda n: (jnp.minimum(n, CAST_STEPS - 1), 0)) for w in next_weights]

    assert all(w.shape[0] % (16 * CAST_STEPS) == 0 for w in next_weights)
    return pl.pallas_call(
        functools.partial(_mixers_kernel, lam_init, nc, len(attn_specs), len(next_weights)),
        grid=(batch * nc,),
        in_specs=attn_specs + mlstm_specs + cast_specs(),
        out_specs=[
            pl.BlockSpec((seq, dv), lambda n: (n % batch, n // batch)),
            pl.BlockSpec((L, W), lambda n: (n, 0)),
        ] + cast_specs(),
        out_shape=[
            jax.ShapeDtypeStruct((batch * seq, DIFF_WIDTH), BF16),
            jax.ShapeDtypeStruct((batch * seq, W), BF16),
        ] + [jax.ShapeDtypeStruct(w.shape, BF16) for w in next_weights],
        scratch_shapes=[
            pltpu.VMEM((nq, 2 * T, dv), BF16),
            pltpu.VMEM((dv + ATT_SUM_ROWS, seq), BF16),
            pltpu.VMEM((2, ATT_FAR_TILES * T, 2 * T), F32),
            pltpu.VMEM((2, ATT_FAR_TILES * T, 2 * T), BF16),
            pltpu.VMEM((dv + ATT_SUM_ROWS, 2 * T), F32),
            pltpu.VMEM((MLSTM_HEADS, MLSTM_HEAD_DIM + MLSTM_AUG, MLSTM_HEAD_DIM), F32),
            pltpu.VMEM((MLSTM_HEADS, 1, 1), F32),
            pltpu.VMEM((W, L), F32),
        ],
        compiler_params=pltpu.CompilerParams(
            dimension_semantics=("arbitrary",), vmem_limit_bytes=VMEM_LIMIT),
        name="token_mixers",
    )(proj, proj_t, bias_tiles, lq1, lk1, lq2, lk2, dnorm_w, proj_t, proj, gcol, grow, mnorm_w, *next_weights)


def _layer_norm(y, g, b):
    mu = jnp.mean(y, axis=-1, keepdims=True)
    cen = y - mu
    var = jnp.mean(cen * cen, axis=-1, keepdims=True)
    return cen * lax.rsqrt(var + LN_EPS) * g + b


def _merge_kernel(hm_ref, hd_ref, gates_ref, x_ref, wbm_ref, wbd_ref, wo_ref, g_ref, b_ref, out_ref):
    tm = hm_ref.shape[0]
    gm_ref = gates_ref.at[:, 0:D_MODEL]
    gd_ref = gates_ref.at[:, D_MODEL:2 * D_MODEL]
    nb = tm // MERGE_ROWS
    halves = [slice(r * MERGE_ROWS, (r + 1) * MERGE_ROWS) for r in range(nb)]
    merged = []
    for rows in halves:
        pm = jnp.dot(hm_ref[rows, :], wbm_ref[...], preferred_element_type=F32)
        pd = jnp.dot(hd_ref[rows, :], wbd_ref[...], preferred_element_type=F32)
        merged.append((gm_ref[rows, :].astype(F32) * pm + gd_ref[rows, :].astype(F32) * pd).astype(BF16))
    for rows, m in zip(halves, merged):
        mix = jnp.dot(m, wo_ref[...], preferred_element_type=F32)
        out_ref[rows, :] = _layer_norm(DEEPNORM_ALPHA * x_ref[rows, :] + mix, g_ref[...], b_ref[...])


def _ffn_kernel(tiles_per_seq, h_ref, wu_ref, cw_ref, cb_ref, wd_ref, g_ref, b_ref, out_ref,
                conv_ref, halo_ref, act_ref):
    i = pl.program_id(0)
    tm = h_ref.shape[0]
    tc = FFN_TC

    @pl.when(i % tiles_per_seq == 0)
    def _():
        halo_ref[...] = jnp.zeros_like(halo_ref)

    h = h_ref[...]
    hb = h.astype(BF16)
    for c0 in range(0, D_FF, tc):
        cols = slice(c0, c0 + tc)
        a = jnp.dot(hb, wu_ref[:, cols], preferred_element_type=F32)
        gate = jnp.dot(hb, wu_ref[:, D_FF + c0:D_FF + c0 + tc], preferred_element_type=F32)
        conv_ref[0:HALO, cols] = halo_ref[:, cols]
        conv_ref[HALO:HALO + tm, cols] = a
        y = cb_ref[:, cols] + cw_ref[FFN_CONV - 1:FFN_CONV, cols] * a
        for kk in range(FFN_CONV - 1):
            off = HALO - (FFN_CONV - 1) + kk
            y = y + cw_ref[kk:kk + 1, cols] * conv_ref[off:off + tm, cols]
        halo_ref[:, cols] = a[tm - HALO:tm, :]
        act_ref[:, cols] = (y * _sigmoid(y) * gate).astype(BF16)

    for r0 in range(0, tm, tm // 2):
        rows = slice(r0, r0 + tm // 2)
        ffn = jnp.dot(act_ref[rows, :], wd_ref[...], preferred_element_type=F32)
        out_ref[rows, :] = _layer_norm(DEEPNORM_ALPHA * h[rows, :] + ffn, g_ref[...], b_ref[...])


def _merge_ffn_kernel(tiles_per_seq, hm_ref, hd_ref, gates_ref, x_ref, wbm_ref, wbd_ref, wo_ref, g1_ref,
                      b1_ref, wu_ref, cw_ref, cb_ref, wd_ref, g2_ref, b2_ref, out_ref,
                      h_ref, conv_ref, halo_ref, act_ref):
    _merge_kernel(hm_ref, hd_ref, gates_ref, x_ref, wbm_ref, wbd_ref, wo_ref, g1_ref, b1_ref, h_ref)
    _ffn_kernel(tiles_per_seq, h_ref, wu_ref, cw_ref, cb_ref, wd_ref, g2_ref, b2_ref, out_ref,
                conv_ref, halo_ref, act_ref)


def _merge_ffn(hm, hd, proj, x2, w_bm, w_bd, w_o, ln1_g, ln1_b, w_up, conv_w, conv_b, w_down, ln2_g, ln2_b, seq):
    T = x2.shape[0]
    tm = FFN_TM
    D = D_MODEL
    rows = lambda col: pl.BlockSpec((tm, D), lambda i: (i, col))
    const = _const_spec
    return pl.pallas_call(
        functools.partial(_merge_ffn_kernel, seq // tm),
        grid=(T // tm,),
        in_specs=[
            rows(0), rows(0), pl.BlockSpec((tm, 2 * D), lambda i: (i, COL_GM // (2 * D))), rows(0),
            const((MLSTM_WIDTH, D)), const((DIFF_WIDTH, D)), const((D, D)), const((1, D)), const((1, D)),
            const((D, 2 * D_FF)), const((FFN_CONV, D_FF)), const((1, D_FF)), const((D_FF, D)),
            const((1, D)), const((1, D)),
        ],
        out_specs=rows(0),
        out_shape=jax.ShapeDtypeStruct((T, D), F32),
        scratch_shapes=[
            pltpu.VMEM((tm, D), F32),
            pltpu.VMEM((HALO + tm, D_FF), F32),
            pltpu.VMEM((HALO, D_FF), F32),
            pltpu.VMEM((tm, D_FF), BF16),
        ],
        compiler_params=pltpu.CompilerParams(
            dimension_semantics=("arbitrary",), vmem_limit_bytes=VMEM_LIMIT),
        name="merge_ffn",
    )(hm, hd, proj, x2, w_bm, w_bd, w_o, ln1_g, ln1_b, w_up, conv_w, conv_b, w_down, ln2_g, ln2_b)


def _layer(h2, batch, seq, l, w_in, b_in, mconv_w, mconv_b, mnorm_w, lq1, lk1, lq2, lk2, dnorm_w, bias_tiles,
           w_bm, w_bd, w_o, ln1_g, ln1_b, w_up, fconv_w, fconv_b, w_down, ln2_g, ln2_b):
    lam_init = 0.8 - 0.6 * math.exp(-0.3 * l)
    n_m = 4 * MLSTM_WIDTH
    n_gate = 2 * MLSTM_HEADS
    n_qk = 2 * MLSTM_WIDTH
    w_all, w_b = _cast_w_in(w_in.T.astype(F32), n_m + n_gate)
    conv_params = jnp.concatenate(
        [mconv_w.astype(F32).T, mconv_b.astype(F32)[:, None], b_in[:n_qk].astype(F32)[:, None],
         jnp.zeros((n_qk, CONV_PARAMS - MLSTM_CONV - 2), F32)], axis=1)
    b_b = b_in[n_m + n_gate:][None, :].astype(F32)
    dv0 = n_m + n_gate + TAIL_DV
    b_t = jnp.concatenate([b_in[n_qk:n_m], b_in[dv0:dv0 + DIFF_WIDTH]])[:, None].astype(F32)
    b_g = b_in[n_m:n_m + GATE_PAD][None, :].astype(F32)
    b_gt = b_in[n_m:n_m + GATE_ROWS][:, None].astype(F32)

    proj, proj_t, gcol, grow = _in_proj(h2, w_all, conv_params, w_b, b_b, b_t, b_g, b_gt, seq)
    hd, hm, w_bm16, w_bd16, w_o16, w_up16, w_down16 = _mixers(
        proj, proj_t, gcol, grow, bias_tiles, lq1[None, :].astype(F32), lk1[None, :].astype(F32),
        lq2[None, :].astype(F32), lk2[None, :].astype(F32), dnorm_w[:, None].astype(F32),
        mnorm_w[:, None].astype(F32), lam_init, batch, seq,
        [w.astype(F32) for w in (w_bm, w_bd, w_o, w_up, w_down)])
    return _merge_ffn(hm, hd, proj, h2, w_bm16, w_bd16, w_o16,
                      ln1_g[None, :].astype(F32), ln1_b[None, :].astype(F32),
                      w_up16, fconv_w.astype(F32), fconv_b[None, :].astype(F32),
                      w_down16, ln2_g[None, :].astype(F32), ln2_b[None, :].astype(F32), seq)


def kernel(x, w_in, b_in, mlstm_conv_w, mlstm_conv_b, mlstm_norm_w, lambda_q1, lambda_k1, lambda_q2, lambda_k2,
           diff_norm_w, rel_bias, w_branch_mlstm, w_branch_diff, w_out, ln1_g, ln1_b, w_ffn_up, ffn_conv_w,
           ffn_conv_b, w_ffn_down, ln2_g, ln2_b):
    batch, seq, d_model = x.shape
    assert d_model == D_MODEL and seq % max(PROJ_TM, FFN_TM, ATT_T, MLSTM_L) == 0
    bias_tiles = _rel_bias_tiles(rel_bias)
    h2 = x.reshape(batch * seq, d_model)
    for l in range(w_in.shape[0]):
        h2 = _layer(h2, batch, seq, l, w_in[l], b_in[l], mlstm_conv_w[l], mlstm_conv_b[l], mlstm_norm_w[l],
                    lambda_q1[l], lambda_k1[l], lambda_q2[l], lambda_k2[l], diff_norm_w[l], bias_tiles,
                    w_branch_mlstm[l], w_branch_diff[l], w_out[l], ln1_g[l], ln1_b[l], w_ffn_up[l],
                    ffn_conv_w[l], ffn_conv_b[l], w_ffn_down[l], ln2_g[l], ln2_b[l])
    return h2.reshape(batch, seq, d_model).astype(x.dtype)
```
